```python
import jax, jax.numpy as jnp
from jax import lax
import numpy as np

D_MODEL = 1024
BATCH = 2
SEQ = 8192
DEPTH = 1

CHUNK = 64
Q_BLOCK = 128

D_A = D_MODEL // 2
N_A = 64
H_A = D_A // N_A
LORA_W = max(32, int(round(1.8 * D_A ** 0.5 / 32)) * 32)
LORA_A = max(32, int(round(1.8 * D_A ** 0.5 / 32)) * 32)
LORA_G = max(32, int(round(0.6 * D_A ** 0.8 / 32)) * 32)
LNX_EPS = 64e-5

D_B = D_MODEL // 2
HD_B = 64
H_B = D_B // HD_B

D_FF = ((8 * D_MODEL // 3 + 255) // 256) * 256
RMS_EPS = 1e-6

SHIFT_COLS = 3 * D_A + LORA_W + LORA_A + LORA_G
FOX_COLS = 3 * D_B + H_B
IN_COLS = SHIFT_COLS + FOX_COLS + 2 * D_MODEL

kernel_name = "rwkv7_fox_gated_macaron_block"


def rms_norm(x, g):
    xf = x.astype(jnp.float32)
    y = xf * lax.rsqrt(jnp.mean(xf * xf, axis=-1, keepdims=True) + RMS_EPS)
    return (y * g.astype(jnp.float32)).astype(x.dtype)


def swiglu(h, w1, w3, w2):
    return (jax.nn.silu(h @ w1) * (h @ w3)) @ w2


def token_shift(u, mu):
    prev = jnp.pad(u, ((0, 0), (1, 0), (0, 0)))[:, :-1]
    return u + (prev - u) * mu


def rwkv7_recurrence(r, w, k, v, a, b):
    B, S, H, N = r.shape
    n_chunks = S // CHUNK

    def to_chunks(t):
        return jnp.moveaxis(t, 1, 0).reshape(n_chunks, CHUNK, B, H, N)

    def step(state, inp):
        r_t, w_t, k_t, v_t, a_t, b_t = inp
        sa = jnp.einsum('bhvk,bhk->bhv', state, a_t)
        state = (state * w_t[:, :, None, :]
                 + sa[..., :, None] * b_t[:, :, None, :]
                 + v_t[..., :, None] * k_t[:, :, None, :])
        y_t = jnp.einsum('bhvk,bhk->bhv', state, r_t)
        return state, y_t

    def chunk_step(state, chunk_inp):
        return lax.scan(step, state, chunk_inp)

    xs = tuple(to_chunks(t) for t in (r, w, k, v, a, b))
    state0 = jnp.zeros((B, H, N, N), jnp.float32)
    _, ys = lax.scan(chunk_step, state0, xs)
    return jnp.moveaxis(ys.reshape(S, B, H, N), 0, 1)


def rwkv7_mixer(p, w0, w_up, a0, a_up, g_up, k_k, k_a, r_k, lnx_g, lnx_b):
    B, S, _ = p.shape
    f32 = jnp.float32
    r, k, v, w_lo, a_lo, g_lo = jnp.split(
        p, [D_A, 2 * D_A, 3 * D_A, 3 * D_A + LORA_W, 3 * D_A + LORA_W + LORA_A], axis=-1)
    w_log = -jax.nn.softplus(-(w0 + jnp.tanh(w_lo) @ w_up)) - 0.5
    decay = jnp.exp(-jnp.exp(w_log.astype(f32)))
    a = jax.nn.sigmoid(a0 + a_lo @ a_up)
    g = jax.nn.sigmoid(g_lo) @ g_up

    def heads(t):
        return t.astype(f32).reshape(B, S, H_A, N_A)

    r, k, v, a, decay = heads(r), heads(k), heads(v), heads(a), heads(decay)
    kk = k * k_k.astype(f32).reshape(H_A, N_A)
    kk = kk / jnp.maximum(jnp.sqrt(jnp.sum(kk * kk, axis=-1, keepdims=True)), 1e-12)
    k = k * (1.0 + (a - 1.0) * k_a.astype(f32).reshape(H_A, N_A))
    y = rwkv7_recurrence(r, decay, k, v, -kk, kk * a)
    mu = jnp.mean(y, axis=-1, keepdims=True)
    var = jnp.mean(jnp.square(y - mu), axis=-1, keepdims=True)
    y = ((y - mu) * lax.rsqrt(var + LNX_EPS)).reshape(B, S, D_A)
    y = y * lnx_g.astype(f32) + lnx_b.astype(f32)
    bonus = jnp.sum(r * k * r_k.astype(f32), axis=-1, keepdims=True) * v
    y = (y + bonus.reshape(B, S, D_A)) * g.astype(f32)
    return y.astype(p.dtype)


def forgetting_attention(q, k, v, log_f):
    B, H, S, hd = q.shape
    nb = S // Q_BLOCK
    scale = 1.0 / np.sqrt(hd).astype(np.float32)
    cum = jnp.cumsum(log_f, axis=-1)
    q_blocks = q.reshape(B, H, nb, Q_BLOCK, hd).transpose(2, 0, 1, 3, 4)
    c_blocks = cum.reshape(B, H, nb, Q_BLOCK).transpose(2, 0, 1, 3)
    pos_k = jnp.arange(S)

    def one_block(args):
        i, q_i, c_i = args
        s = jnp.einsum('bhqd,bhkd->bhqk', q_i, k, preferred_element_type=jnp.float32) * scale
        s = s + c_i[..., :, None] - cum[..., None, :]
        pos_q = i * Q_BLOCK + jnp.arange(Q_BLOCK)
        mask = pos_k[None, :] <= pos_q[:, None]
        s = jnp.where(mask, s, -jnp.inf)
        pr = jax.nn.softmax(s, axis=-1)
        return jnp.einsum('bhqk,bhkd->bhqd', pr.astype(v.dtype), v)

    out = lax.map(one_block, (jnp.arange(nb), q_blocks, c_blocks))
    return out.transpose(1, 2, 0, 3, 4).reshape(B, H, S, hd)


def fox_mixer(p, f_bias, q_norm, k_norm):
    B, S, _ = p.shape
    q, k, v, f_lo = jnp.split(p, [D_B, 2 * D_B, 3 * D_B], axis=-1)

    def heads(t):
        return t.reshape(B, S, H_B, HD_B).transpose(0, 2, 1, 3)

    q = rms_norm(heads(q), q_norm)
    k = rms_norm(heads(k), k_norm)
    v = heads(v)
    log_f = jax.nn.log_sigmoid((f_lo + f_bias).astype(jnp.float32)).transpose(0, 2, 1)
    o = forgetting_attention(q, k, v, log_f)
    return o.transpose(0, 2, 1, 3).reshape(B, S, D_B)


def setup_inputs(seed: int = 0) -> dict:
    key = jax.random.key(seed)
    ks = jax.random.split(key, 28)
    L = DEPTH
    f32 = jnp.float32

    def nrm(k, shape, scale):
        return jax.random.normal(k, shape, f32) * scale

    def gain(k, shape):
        return 1.0 + 0.05 * jax.random.normal(k, shape, f32)

    return {
        "x": nrm(ks[0], (BATCH, SEQ, D_MODEL), 1.0),
        "ffn1_norm": gain(ks[1], (L, D_MODEL)),
        "ffn1_w1": nrm(ks[2], (L, D_MODEL, D_FF), D_MODEL ** -0.5),
        "ffn1_w3": nrm(ks[3], (L, D_MODEL, D_FF), D_MODEL ** -0.5),
        "ffn1_w2": nrm(ks[4], (L, D_FF, D_MODEL), D_FF ** -0.5),
        "mix_norm": gain(ks[5], (L, D_MODEL)),
        "w_in": nrm(ks[6], (L, D_MODEL, IN_COLS), D_MODEL ** -0.5),
        "shift_mu": jax.random.uniform(ks[7], (L, SHIFT_COLS), f32),
        "rwkv_w0": jax.random.uniform(ks[8], (L, D_A), f32, -6.5, -1.5),
        "rwkv_w_up": nrm(ks[9], (L, LORA_W, D_A), 0.5 * LORA_W ** -0.5),
        "rwkv_a0": nrm(ks[10], (L, D_A), 0.1),
        "rwkv_a_up": nrm(ks[11], (L, LORA_A, D_A), LORA_A ** -0.5),
        "rwkv_g_up": nrm(ks[12], (L, LORA_G, D_A), LORA_G ** -0.5),
        "rwkv_k_k": 0.85 + 0.05 * jax.random.normal(ks[13], (L, D_A), f32),
        "rwkv_k_a": gain(ks[14], (L, D_A)),
        "rwkv_r_k": nrm(ks[15], (L, H_A, N_A), 0.1),
        "rwkv_lnx_g": gain(ks[16], (L, D_A)),
        "rwkv_lnx_b": nrm(ks[17], (L, D_A), 0.02),
        "rwkv_proj": nrm(ks[18], (L, D_A, D_MODEL), D_A ** -0.5),
        "fox_f_bias": 2.0 + 0.5 * jax.random.normal(ks[19], (L, H_B), f32),
        "fox_q_norm": gain(ks[20], (L, HD_B)),
        "fox_k_norm": gain(ks[21], (L, HD_B)),
        "fox_proj": nrm(ks[22], (L, D_B, D_MODEL), D_B ** -0.5),
        "w_out": nrm(ks[23], (L, D_MODEL, D_MODEL), D_MODEL ** -0.5),
        "ffn2_norm": gain(ks[24], (L, D_MODEL)),
        "ffn2_w1": nrm(ks[25], (L, D_MODEL, D_FF), D_MODEL ** -0.5),
        "ffn2_w3": nrm(ks[26], (L, D_MODEL, D_FF), D_MODEL ** -0.5),
        "ffn2_w2": nrm(ks[27], (L, D_FF, D_MODEL), D_FF ** -0.5),
        "final_norm": gain(jax.random.fold_in(key, 99), (D_MODEL,)),
    }


def reference(x, ffn1_norm, ffn1_w1, ffn1_w3, ffn1_w2, mix_norm, w_in, shift_mu,
              rwkv_w0, rwkv_w_up, rwkv_a0, rwkv_a_up, rwkv_g_up, rwkv_k_k, rwkv_k_a, rwkv_r_k,
              rwkv_lnx_g, rwkv_lnx_b, rwkv_proj, fox_f_bias, fox_q_norm, fox_k_norm, fox_proj,
              w_out, ffn2_norm, ffn2_w1, ffn2_w3, ffn2_w2, final_norm):
    for l in range(DEPTH):
        x = x + 0.5 * swiglu(rms_norm(x, ffn1_norm[l]), ffn1_w1[l], ffn1_w3[l], ffn1_w2[l])

        h = rms_norm(x, mix_norm[l])
        p = h @ w_in[l]
        p_rwkv = token_shift(p[..., :SHIFT_COLS], shift_mu[l])
        p_fox = p[..., SHIFT_COLS:SHIFT_COLS + FOX_COLS]
        gates = jax.nn.sigmoid(p[..., SHIFT_COLS + FOX_COLS:])
        gate_a, gate_b = gates[..., :D_MODEL], gates[..., D_MODEL:]

        y_a = rwkv7_mixer(p_rwkv, rwkv_w0[l], rwkv_w_up[l], rwkv_a0[l], rwkv_a_up[l], rwkv_g_up[l],
                          rwkv_k_k[l], rwkv_k_a[l], rwkv_r_k[l], rwkv_lnx_g[l], rwkv_lnx_b[l]) @ rwkv_proj[l]
        y_b = fox_mixer(p_fox, fox_f_bias[l], fox_q_norm[l], fox_k_norm[l]) @ fox_proj[l]
        x = x + (gate_a * y_a + gate_b * y_b) @ w_out[l]

        x = x + 0.5 * swiglu(rms_norm(x, ffn2_norm[l]), ffn2_w1[l], ffn2_w3[l], ffn2_w2[l])
    return rms_norm(x, final_norm)
```

```python
import functools

import jax
import jax.numpy as jnp
from jax import lax
from jax.experimental import pallas as pl
from jax.experimental.pallas import tpu as pltpu

F32 = jnp.float32
BF16 = jnp.bfloat16

HEAD = 64
N_HEADS = 8
D_MIX = HEAD * N_HEADS
LANE = 128
CHUNK = 64
RMS_EPS = 1e-6
LNX_EPS = 64e-5
NEG_BIG = -1e30
VMEM_LIMIT = 56 * 1024 * 1024

_NT = (((1,), (1,)), ((), ()))
_TN = (((0,), (0,)), ((), ()))


def _dot(a, b, dims=None, precision=None):
    if dims is None:
        return jnp.dot(a, b, preferred_element_type=F32, precision=precision)
    return lax.dot_general(a, b, dims, preferred_element_type=F32, precision=precision)


def _bdot(a, b, dims=None):
    return _dot(a.astype(BF16), b.astype(BF16), dims)


def _rms(x, g):
    return x * lax.rsqrt(jnp.mean(x * x, axis=-1, keepdims=True) + RMS_EPS) * g


def _softplus(z):
    return jnp.maximum(z, 0.0) + jnp.log(1.0 + jnp.exp(-jnp.abs(z)))


def _sigmoid(z):
    return 1.0 / (1.0 + jnp.exp(-z))


def _head_sum(x, e):
    hi = x.astype(BF16)
    lo = (x - hi.astype(F32)).astype(BF16)
    return _dot(hi, e) + _dot(lo, e)


def _ffn_kernel(x_ref, g_ref, w1_ref, w3_ref, w2_ref, gf_ref, o_ref, h_scr, acc_scr, *, final_norm):
    f = pl.program_id(1)

    @pl.when(f == 0)
    def _():
        h_scr[...] = _rms(x_ref[...], g_ref[...]).astype(BF16)
        acc_scr[...] = jnp.zeros_like(acc_scr)

    h = h_scr[...]
    a = _dot(h, w1_ref[...])
    b = _dot(h, w3_ref[...])
    u = (a * _sigmoid(a) * b).astype(BF16)
    acc_scr[...] += _dot(u, w2_ref[...])

    @pl.when(f == pl.num_programs(1) - 1)
    def _():
        y = x_ref[...] + 0.5 * acc_scr[...]
        if final_norm:
            y = _rms(y, gf_ref[...])
        o_ref[...] = y


def _ffn(x2d, g, w1, w3, w2, gf, *, final_norm, tm=512, n_f=2):
    m, d = x2d.shape
    f_dim = w1.shape[1]
    tf = f_dim // n_f
    assert m % tm == 0 and tf * n_f == f_dim and tf % LANE == 0
    return pl.pallas_call(
        functools.partial(_ffn_kernel, final_norm=final_norm),
        out_shape=jax.ShapeDtypeStruct((m, d), F32),
        grid=(m // tm, n_f),
        in_specs=[
            pl.BlockSpec((tm, d), lambda i, f: (i, 0)),
            pl.BlockSpec((1, d), lambda i, f: (0, 0)),
            pl.BlockSpec((d, tf), lambda i, f: (0, f)),
            pl.BlockSpec((d, tf), lambda i, f: (0, f)),
            pl.BlockSpec((tf, d), lambda i, f: (f, 0)),
            pl.BlockSpec((1, d), lambda i, f: (0, 0)),
        ],
        out_specs=pl.BlockSpec((tm, d), lambda i, f: (i, 0)),
        scratch_shapes=[pltpu.VMEM((tm, d), BF16), pltpu.VMEM((tm, d), F32)],
        compiler_params=pltpu.CompilerParams(
            dimension_semantics=("parallel", "arbitrary"), vmem_limit_bytes=VMEM_LIMIT),
        name="ffn",
    )(x2d, g.reshape(1, d), w1.astype(BF16), w3.astype(BF16), w2.astype(BF16), gf.reshape(1, d))


N_SHIFT = 3 * D_MIX + 3 * LANE
N_PROJ = N_SHIFT + 3 * D_MIX + LANE


def _in_proj_kernel(x_ref, g_ref, w_ref, mu_ref, w0_ref, wup_ref, a0_ref, aup_ref, gup_ref, kk_ref, ka_ref,
                    rk_ref, fb_ref, qn_ref, kn_ref, e_ref,
                    r_out, lw_out, k_out, v_out, kk_out, b_out, g_out, bonus_out, fq_out, fk_out, fv_out, cum_out,
                    carry_p, carry_c, *, tm):
    i = pl.program_id(1)

    @pl.when(i == 0)
    def _():
        carry_p[...] = jnp.zeros_like(carry_p)
        carry_c[...] = jnp.zeros_like(carry_c)

    h = _rms(x_ref[0], g_ref[...]).astype(BF16)
    p = _dot(h, w_ref[...])

    ps = p[:, :N_SHIFT]
    row = lax.broadcasted_iota(jnp.int32, (tm, 1), 0)
    prev = jnp.where(row == 0, carry_p[...], pltpu.roll(ps, 1, axis=0))
    carry_p[...] = ps[tm - 1:tm, :]
    sh = ps + (prev - ps) * mu_ref[...]

    e = e_ref[...]
    r = sh[:, 0:D_MIX]
    k = sh[:, D_MIX:2 * D_MIX]
    v = sh[:, 2 * D_MIX:3 * D_MIX]
    w_lo = sh[:, 3 * D_MIX:3 * D_MIX + LANE]
    a_lo = sh[:, 3 * D_MIX + LANE:3 * D_MIX + 2 * LANE]
    g_lo = sh[:, 3 * D_MIX + 2 * LANE:N_SHIFT]

    w_log = -_softplus(-(w0_ref[...] + _bdot(jnp.tanh(w_lo), wup_ref[...]))) - 0.5
    lw = -jnp.exp(w_log)
    a = _sigmoid(a0_ref[...] + _bdot(a_lo, aup_ref[...]))
    g = _bdot(_sigmoid(g_lo), gup_ref[...])
    kk = k * kk_ref[...]
    kk = kk / jnp.maximum(jnp.sqrt(_head_sum(kk * kk, e)), 1e-12)
    k_mod = k * (1.0 + (a - 1.0) * ka_ref[...])
    bonus = _head_sum(r * k_mod * rk_ref[...], e) * v

    r_out[0] = r
    lw_out[0] = lw
    k_out[0] = k_mod
    v_out[0] = v
    kk_out[0] = kk
    b_out[0] = kk * a
    g_out[0] = g
    bonus_out[0] = bonus

    fq = p[:, N_SHIFT:N_SHIFT + D_MIX]
    fk = p[:, N_SHIFT + D_MIX:N_SHIFT + 2 * D_MIX]
    fv = p[:, N_SHIFT + 2 * D_MIX:N_SHIFT + 3 * D_MIX]
    f_lo = p[:, N_SHIFT + 3 * D_MIX:N_PROJ]
    inv_head = 1.0 / HEAD
    qn = fq * lax.rsqrt(_head_sum(fq * fq, e) * inv_head + RMS_EPS) * qn_ref[...] * (HEAD ** -0.5)
    kn = fk * lax.rsqrt(_head_sum(fk * fk, e) * inv_head + RMS_EPS) * kn_ref[...]
    for hh in range(N_HEADS):
        sl = slice(hh * HEAD, (hh + 1) * HEAD)
        fq_out[0, hh] = qn[:, sl].astype(BF16)
        fk_out[0, hh] = kn[:, sl].astype(BF16)
        fv_out[0, hh] = fv[:, sl].astype(BF16)
    z = f_lo + fb_ref[...]
    log_f = jnp.minimum(z, 0.0) - jnp.log(1.0 + jnp.exp(-jnp.abs(z)))
    tri = (lax.broadcasted_iota(jnp.int32, (tm, tm), 1) <= lax.broadcasted_iota(jnp.int32, (tm, tm), 0)).astype(F32)
    cum = _dot(tri, log_f, precision=lax.Precision.HIGHEST) + carry_c[...]
    carry_c[...] = cum[tm - 1:tm, :]
    cum_out[0] = cum


def _pad_cols(w, n):
    return jnp.pad(w, ((0, 0), (0, n - w.shape[1])))


def _pad_rows(w, n):
    return jnp.pad(w, ((0, n - w.shape[0]), (0, 0)))


def _in_proj(x, mix_norm, w_in, shift_mu, w0, w_up, a0, a_up, g_up, k_k, k_a, r_k, f_bias, q_norm, k_norm, *, tm=256):
    bsz, s, d = x.shape
    lw_, la_, lg_ = w_up.shape[0], a_up.shape[0], g_up.shape[0]
    c0 = 3 * D_MIX
    c1 = c0 + lw_ + la_ + lg_
    c2 = c1 + 3 * D_MIX
    c3 = c2 + N_HEADS

    def seg(t, pad):
        return jnp.concatenate([
            t[..., :c0],
            pad(t[..., c0:c0 + lw_], LANE), pad(t[..., c0 + lw_:c0 + lw_ + la_], LANE), pad(t[..., c0 + lw_ + la_:c1], LANE),
        ], axis=-1)

    w_cat = jnp.concatenate([seg(w_in, _pad_cols), w_in[:, c1:c2], _pad_cols(w_in[:, c2:c3], LANE)], axis=-1).astype(BF16)
    mu_cat = seg(shift_mu.reshape(1, -1), _pad_cols)
    row = lambda t: t.reshape(1, -1).astype(F32)
    head_id = jnp.arange(D_MIX) // HEAD
    e = (head_id[:, None] == head_id[None, :]).astype(BF16)
    tile = lambda t: jnp.tile(t.reshape(1, HEAD), (1, N_HEADS)).astype(F32)
    consts = [
        row(mix_norm), w_cat, mu_cat, row(w0), _pad_rows(w_up, LANE).astype(BF16), row(a0),
        _pad_rows(a_up, LANE).astype(BF16), _pad_rows(g_up, LANE).astype(BF16), row(k_k), row(k_a), row(r_k),
        _pad_cols(row(f_bias), LANE), tile(q_norm), tile(k_norm), e,
    ]
    const_specs = [pl.BlockSpec(c.shape, lambda b, i: (0, 0)) for c in consts]
    tok = lambda: pl.BlockSpec((1, tm, D_MIX), lambda b, i: (b, i, 0))
    headed = lambda: pl.BlockSpec((1, N_HEADS, tm, HEAD), lambda b, i: (b, 0, i, 0))
    f32_tok = jax.ShapeDtypeStruct((bsz, s, D_MIX), F32)
    bf_head = jax.ShapeDtypeStruct((bsz, N_HEADS, s, HEAD), BF16)
    assert s % tm == 0
    return pl.pallas_call(
        functools.partial(_in_proj_kernel, tm=tm),
        out_shape=[f32_tok] * 8 + [bf_head] * 3 + [jax.ShapeDtypeStruct((bsz, s, LANE), F32)],
        grid=(bsz, s // tm),
        in_specs=[pl.BlockSpec((1, tm, d), lambda b, i: (b, i, 0))] + const_specs,
        out_specs=[tok() for _ in range(8)] + [headed() for _ in range(3)]
        + [pl.BlockSpec((1, tm, LANE), lambda b, i: (b, i, 0))],
        scratch_shapes=[pltpu.VMEM((1, N_SHIFT), F32), pltpu.VMEM((1, LANE), F32)],
        compiler_params=pltpu.CompilerParams(
            dimension_semantics=("parallel", "arbitrary"), vmem_limit_bytes=VMEM_LIMIT),
        name="in_proj",
    )(x, *consts)


def _rwkv_kernel(r_ref, lw_ref, k_ref, v_ref, kk_ref, b_ref, g_ref, bonus_ref, lng_ref, lnb_ref, y_ref, s_scr, *, tb):
    i = pl.program_id(1)

    @pl.when(i == 0)
    def _():
        s_scr[...] = jnp.zeros_like(s_scr)

    c = CHUNK
    rr = lax.broadcasted_iota(jnp.int32, (c, c), 0)
    cc = lax.broadcasted_iota(jnp.int32, (c, c), 1)
    strict = cc < rr
    incl = cc <= rr
    tri = incl.astype(F32)
    eye = (cc == rr).astype(F32)
    zeros = jnp.zeros((c, c), F32)

    def level_mask(n):
        return ((rr // (2 * n)) == (cc // (2 * n))) & (((rr // n) % 2) == 1) & (((cc // n) % 2) == 0)

    def chunk_body(ci, _):
        sl = pl.ds(pl.multiple_of(ci * c, c), c)
        r = r_ref[0, sl, :]
        lw = lw_ref[0, sl, :]
        k = k_ref[0, sl, :]
        v = v_ref[0, sl, :]
        kk = kk_ref[0, sl, :]
        b = b_ref[0, sl, :]

        cs = _dot(tri, lw, precision=lax.Precision.HIGHEST)
        w_inv = jnp.exp(-cs)
        w_tot = jnp.exp(cs[c - 1:c, :])
        rt = r * jnp.exp(cs)
        at = -kk * jnp.exp(cs - lw)
        bi = b * w_inv
        ki = k * w_inv
        bh = bi * w_tot
        kh = ki * w_tot

        ys = []
        for h in range(N_HEADS):
            hs = slice(h * HEAD, (h + 1) * HEAD)
            at_h, rt_h, v_h = at[:, hs], rt[:, hs], v[:, hs]
            gm = _bdot(jnp.concatenate([at_h, rt_h], axis=0), jnp.concatenate([bi[:, hs], ki[:, hs]], axis=0), _NT)
            a_ab = jnp.where(strict, gm[:c, :c], 0.0)
            a_ak = jnp.where(strict, gm[:c, c:], 0.0)
            a_rb = jnp.where(incl, gm[c:, :c], 0.0)
            a_rk = jnp.where(incl, gm[c:, c:], 0.0)

            t = eye + jnp.where(level_mask(1), a_ab, 0.0)
            for n in (2, 4, 8, 16, 32):
                t = t + _bdot(t, _bdot(jnp.where(level_mask(n), a_ab, 0.0), t))

            x = _bdot(t, jnp.concatenate([at_h, _bdot(a_ak, v_h)], axis=1))
            z = jnp.concatenate([x, jnp.concatenate([zeros, v_h], axis=1)], axis=0).astype(BF16)
            d1 = _dot(jnp.concatenate([a_rb, a_rk], axis=1).astype(BF16), z)
            d2 = _dot(z, jnp.concatenate([bh[:, hs], kh[:, hs]], axis=0).astype(BF16), _TN)
            s0 = s_scr[h]
            s0b = s0.astype(BF16)
            ys.append(d1[:, c:] + _dot((rt_h + d1[:, :c]).astype(BF16), s0b, _NT))
            s_scr[h] = s0 * w_tot[:, hs] + _dot(s0b, d2[:c].astype(BF16)) + d2[c:]

        outs = []
        for h in range(N_HEADS):
            y = ys[h]
            mu = jnp.mean(y, axis=-1, keepdims=True)
            yc = y - mu
            var = jnp.mean(yc * yc, axis=-1, keepdims=True)
            outs.append(yc * lax.rsqrt(var + LNX_EPS))
        yn = jnp.concatenate(outs, axis=-1)
        y_ref[0, sl, :] = ((yn * lng_ref[...] + lnb_ref[...] + bonus_ref[0, sl, :]) * g_ref[0, sl, :]).astype(BF16)
        return 0

    lax.fori_loop(0, tb // c, chunk_body, 0)


def _rwkv(r, lw, k, v, kk, b, g, bonus, lnx_g, lnx_b, *, tb=256):
    bsz, s, _ = r.shape
    assert s % tb == 0 and tb % CHUNK == 0
    tok = pl.BlockSpec((1, tb, D_MIX), lambda bb, i: (bb, i, 0))
    vec = pl.BlockSpec((1, D_MIX), lambda bb, i: (0, 0))
    return pl.pallas_call(
        functools.partial(_rwkv_kernel, tb=tb),
        out_shape=jax.ShapeDtypeStruct((bsz, s, D_MIX), BF16),
        grid=(bsz, s // tb),
        in_specs=[tok] * 8 + [vec, vec],
        out_specs=tok,
        scratch_shapes=[pltpu.VMEM((N_HEADS, HEAD, HEAD), F32)],
        compiler_params=pltpu.CompilerParams(
            dimension_semantics=("parallel", "arbitrary"), vmem_limit_bytes=VMEM_LIMIT),
        name="rwkv",
    )(r, lw, k, v, kk, b, g, bonus, lnx_g.reshape(1, -1).astype(F32), lnx_b.reshape(1, -1).astype(F32))


def _fox_kernel(q_ref, k_ref, v_ref, c_ref, o_ref, *, tq):
    qb = pl.program_id(2)
    row = lax.broadcasted_iota(jnp.int32, (tq, tq), 0)
    col = lax.broadcasted_iota(jnp.int32, (tq, tq), 1)
    causal = col <= row
    outs = []
    for hh in range(2):
        q = q_ref[0, hh]
        c_start = c_ref[0, hh, :, pl.ds(pl.multiple_of(qb * tq, tq), LANE)][:, 0:1]

        def step(kb, carry, masked, hh=hh, q=q, c_start=c_start):
            m, l, acc = carry
            ksl = pl.ds(pl.multiple_of(kb * tq, tq), tq)
            s = _dot(q, k_ref[0, hh, ksl, :], _NT) - (c_ref[0, hh, :, ksl] - c_start)
            if masked:
                s = jnp.where(causal, s, NEG_BIG)
            m_new = jnp.maximum(m, jnp.max(s, axis=-1, keepdims=True))
            alpha = jnp.exp(m - m_new)
            p = jnp.exp(s - m_new)
            l = alpha * l + jnp.sum(p, axis=-1, keepdims=True)
            acc = alpha * acc + _dot(p.astype(BF16), v_ref[0, hh, ksl, :])
            return m_new, l, acc

        init = (jnp.full((tq, 1), NEG_BIG, F32), jnp.zeros((tq, 1), F32), jnp.zeros((tq, HEAD), F32))
        carry = lax.fori_loop(0, qb, functools.partial(step, masked=False), init)
        _, l, acc = step(qb, carry, True)
        outs.append(acc / l)
    o_ref[0] = jnp.concatenate(outs, axis=-1).astype(BF16)


def _fox(q, k, v, cum_t, *, tq=256):
    bsz, nh, s, _ = q.shape
    assert s % tq == 0 and nh % 2 == 0
    kv = pl.BlockSpec((1, 2, s, HEAD), lambda b, hp, i: (b, hp, 0, 0))
    return pl.pallas_call(
        functools.partial(_fox_kernel, tq=tq),
        out_shape=jax.ShapeDtypeStruct((bsz, s, nh * HEAD), BF16),
        grid=(bsz, nh // 2, s // tq),
        in_specs=[
            pl.BlockSpec((1, 2, tq, HEAD), lambda b, hp, i: (b, hp, i, 0)), kv, kv,
            pl.BlockSpec((1, 2, 1, s), lambda b, hp, i: (b, hp, 0, 0)),
        ],
        out_specs=pl.BlockSpec((1, tq, 2 * HEAD), lambda b, hp, i: (b, i, hp)),
        compiler_params=pltpu.CompilerParams(
            dimension_semantics=("parallel", "parallel", "arbitrary"), vmem_limit_bytes=VMEM_LIMIT),
        name="fox",
    )(q, k, v, cum_t)


def _merge_kernel(x_ref, g_ref, wg_ref, ya_ref, yb_ref, pa_ref, pb_ref, wo_ref, o_ref):
    x = x_ref[...]
    d = x.shape[-1]
    h = _rms(x, g_ref[...]).astype(BF16)
    gates = _sigmoid(_dot(h, wg_ref[...]))
    ya = _dot(ya_ref[...], pa_ref[...])
    yb = _dot(yb_ref[...], pb_ref[...])
    mix = (gates[:, :d] * ya + gates[:, d:] * yb).astype(BF16)
    o_ref[...] = x + _dot(mix, wo_ref[...])


def _merge(x2d, mix_norm, w_gates, ya, yb, p_a, p_b, w_out, *, tm=512):
    m, d = x2d.shape
    assert m % tm == 0
    full = lambda t: pl.BlockSpec(t.shape, lambda i: (0, 0))
    tok = lambda n: pl.BlockSpec((tm, n), lambda i: (i, 0))
    consts = [mix_norm.reshape(1, d).astype(F32), w_gates.astype(BF16)]
    mats = [p_a.astype(BF16), p_b.astype(BF16), w_out.astype(BF16)]
    return pl.pallas_call(
        _merge_kernel,
        out_shape=jax.ShapeDtypeStruct((m, d), F32),
        grid=(m // tm,),
        in_specs=[tok(d), full(consts[0]), full(consts[1]), tok(D_MIX), tok(D_MIX)] + [full(t) for t in mats],
        out_specs=tok(d),
        compiler_params=pltpu.CompilerParams(dimension_semantics=("parallel",), vmem_limit_bytes=VMEM_LIMIT),
        name="merge",
    )(x2d, consts[0], consts[1], ya, yb, *mats)


def kernel(x, ffn1_norm, ffn1_w1, ffn1_w3, ffn1_w2, mix_norm, w_in, shift_mu, rwkv_w0, rwkv_w_up, rwkv_a0, rwkv_a_up, rwkv_g_up, rwkv_k_k, rwkv_k_a, rwkv_r_k, rwkv_lnx_g, rwkv_lnx_b, rwkv_proj, fox_f_bias, fox_q_norm, fox_k_norm, fox_proj, w_out, ffn2_norm, ffn2_w1, ffn2_w3, ffn2_w2, final_norm):
    bsz, s, d = x.shape
    depth = ffn1_norm.shape[0]
    n_gate0 = w_in.shape[-1] - 2 * d
    x2d = x.reshape(bsz * s, d)
    for l in range(depth):
        x2d = _ffn(x2d, ffn1_norm[l], ffn1_w1[l], ffn1_w3[l], ffn1_w2[l], final_norm, final_norm=False)
        (r, lw, k, v, kk, b, g, bonus, fq, fk, fv, cum) = _in_proj(
            x2d.reshape(bsz, s, d), mix_norm[l], w_in[l], shift_mu[l], rwkv_w0[l], rwkv_w_up[l], rwkv_a0[l],
            rwkv_a_up[l], rwkv_g_up[l], rwkv_k_k[l], rwkv_k_a[l], rwkv_r_k[l], fox_f_bias[l], fox_q_norm[l],
            fox_k_norm[l])
        ya = _rwkv(r, lw, k, v, kk, b, g, bonus, rwkv_lnx_g[l], rwkv_lnx_b[l])
        cum_t = jnp.transpose(cum[:, :, :N_HEADS], (0, 2, 1)).reshape(bsz, N_HEADS, 1, s)
        yb = _fox(fq, fk, fv, cum_t)
        x2d = _merge(x2d, mix_norm[l], w_in[l][:, n_gate0:], ya.reshape(bsz * s, D_MIX), yb.reshape(bsz * s, D_MIX),
                     rwkv_proj[l], fox_proj[l], w_out[l])
        x2d = _ffn(x2d, ffn2_norm[l], ffn2_w1[l], ffn2_w3[l], ffn2_w2[l], final_norm, final_norm=(l == depth - 1))
    return x2d.reshape(bsz, s, d)
```

```python
import functools

import jax
import jax.numpy as jnp
from jax import lax
from jax.experimental import pallas as pl
from jax.experimental.pallas import tpu as pltpu

F32 = jnp.float32
BF16 = jnp.bfloat16

HEAD = 64
N_HEADS = 8
D_MIX = HEAD * N_HEADS
LANE = 128
CHUNK = 64
RMS_EPS = 1e-6
LNX_EPS = 64e-5
NEG_BIG = -1e30
VMEM_LIMIT = 56 * 1024 * 1024

_NT = (((1,), (1,)), ((), ()))
_TN = (((0,), (0,)), ((), ()))


def _dot(a, b, dims=None, precision=None):
    if dims is None:
        return jnp.dot(a, b, preferred_element_type=F32, precision=precision)
    return lax.dot_general(a, b, dims, preferred_element_type=F32, precision=precision)


def _bdot(a, b, dims=None):
    return _dot(a.astype(BF16), b.astype(BF16), dims)


def _rms(x, g):
    return x * lax.rsqrt(jnp.mean(x * x, axis=-1, keepdims=True) + RMS_EPS) * g


def _softplus(z):
    return jnp.maximum(z, 0.0) + jnp.log(1.0 + jnp.exp(-jnp.abs(z)))


def _sigmoid(z):
    return 1.0 / (1.0 + jnp.exp(-z))


def _head_sum(x, e):
    hi = x.astype(BF16)
    lo = (x - hi.astype(F32)).astype(BF16)
    return _dot(hi, e) + _dot(lo, e)


def _ffn_kernel(x_ref, g_ref, w1_ref, w3_ref, w2_ref, gf_ref, o_ref, h_scr, acc_scr, *, final_norm):
    f = pl.program_id(1)

    @pl.when(f == 0)
    def _():
        h_scr[...] = _rms(x_ref[...], g_ref[...]).astype(BF16)
        acc_scr[...] = jnp.zeros_like(acc_scr)

    h = h_scr[...]
    a = _dot(h, w1_ref[...])
    b = _dot(h, w3_ref[...])
    u = (a * _sigmoid(a) * b).astype(BF16)
    acc_scr[...] += _dot(u, w2_ref[...])

    @pl.when(f == pl.num_programs(1) - 1)
    def _():
        y = x_ref[...] + 0.5 * acc_scr[...]
        if final_norm:
            y = _rms(y, gf_ref[...])
        o_ref[...] = y


def _ffn(x2d, g, w1, w3, w2, gf, *, final_norm, tm=512, n_f=2):
    m, d = x2d.shape
    f_dim = w1.shape[1]
    tf = f_dim // n_f
    assert m % tm == 0 and tf * n_f == f_dim and tf % LANE == 0
    return pl.pallas_call(
        functools.partial(_ffn_kernel, final_norm=final_norm),
        out_shape=jax.ShapeDtypeStruct((m, d), F32),
        grid=(m // tm, n_f),
        in_specs=[
            pl.BlockSpec((tm, d), lambda i, f: (i, 0)),
            pl.BlockSpec((1, d), lambda i, f: (0, 0)),
            pl.BlockSpec((d, tf), lambda i, f: (0, f)),
            pl.BlockSpec((d, tf), lambda i, f: (0, f)),
            pl.BlockSpec((tf, d), lambda i, f: (f, 0)),
            pl.BlockSpec((1, d), lambda i, f: (0, 0)),
        ],
        out_specs=pl.BlockSpec((tm, d), lambda i, f: (i, 0)),
        scratch_shapes=[pltpu.VMEM((tm, d), BF16), pltpu.VMEM((tm, d), F32)],
        compiler_params=pltpu.CompilerParams(
            dimension_semantics=("parallel", "arbitrary"), vmem_limit_bytes=VMEM_LIMIT),
        name="ffn",
    )(x2d, g.reshape(1, d), w1.astype(BF16), w3.astype(BF16), w2.astype(BF16), gf.reshape(1, d))


N_SHIFT = 3 * D_MIX + 3 * LANE
N_PROJ = N_SHIFT + 3 * D_MIX + LANE


def _in_proj_kernel(x_ref, g_ref, w_ref, mu_ref, w0_ref, wup_ref, a0_ref, aup_ref, gup_ref, kk_ref, ka_ref,
                    rk_ref, fb_ref, qn_ref, kn_ref, e_ref,
                    r_out, lw_out, k_out, v_out, kk_out, b_out, g_out, bonus_out, fq_out, fk_out, fv_out, cum_out,
                    carry_p, carry_c, *, tm):
    i = pl.program_id(1)

    @pl.when(i == 0)
    def _():
        carry_p[...] = jnp.zeros_like(carry_p)
        carry_c[...] = jnp.zeros_like(carry_c)

    h = _rms(x_ref[0], g_ref[...]).astype(BF16)
    p = _dot(h, w_ref[...])

    ps = p[:, :N_SHIFT]
    row = lax.broadcasted_iota(jnp.int32, (tm, 1), 0)
    prev = jnp.where(row == 0, carry_p[...], pltpu.roll(ps, 1, axis=0))
    carry_p[...] = ps[tm - 1:tm, :]
    sh = ps + (prev - ps) * mu_ref[...]

    e = e_ref[...]
    r = sh[:, 0:D_MIX]
    k = sh[:, D_MIX:2 * D_MIX]
    v = sh[:, 2 * D_MIX:3 * D_MIX]
    w_lo = sh[:, 3 * D_MIX:3 * D_MIX + LANE]
    a_lo = sh[:, 3 * D_MIX + LANE:3 * D_MIX + 2 * LANE]
    g_lo = sh[:, 3 * D_MIX + 2 * LANE:N_SHIFT]

    w_log = -_softplus(-(w0_ref[...] + _bdot(jnp.tanh(w_lo), wup_ref[...]))) - 0.5
    lw = -jnp.exp(w_log)
    a = _sigmoid(a0_ref[...] + _bdot(a_lo, aup_ref[...]))
    g = _bdot(_sigmoid(g_lo), gup_ref[...])
    kk = k * kk_ref[...]
    kk = kk / jnp.maximum(jnp.sqrt(_head_sum(kk * kk, e)), 1e-12)
    k_mod = k * (1.0 + (a - 1.0) * ka_ref[...])
    bonus = _head_sum(r * k_mod * rk_ref[...], e) * v

    r_out[0] = r
    lw_out[0] = lw
    k_out[0] = k_mod
    v_out[0] = v
    kk_out[0] = kk
    b_out[0] = kk * a
    g_out[0] = g
    bonus_out[0] = bonus

    fq = p[:, N_SHIFT:N_SHIFT + D_MIX]
    fk = p[:, N_SHIFT + D_MIX:N_SHIFT + 2 * D_MIX]
    fv = p[:, N_SHIFT + 2 * D_MIX:N_SHIFT + 3 * D_MIX]
    f_lo = p[:, N_SHIFT + 3 * D_MIX:N_PROJ]
    inv_head = 1.0 / HEAD
    qn = fq * lax.rsqrt(_head_sum(fq * fq, e) * inv_head + RMS_EPS) * qn_ref[...] * (HEAD ** -0.5)
    kn = fk * lax.rsqrt(_head_sum(fk * fk, e) * inv_head + RMS_EPS) * kn_ref[...]
    for hh in range(N_HEADS):
        sl = slice(hh * HEAD, (hh + 1) * HEAD)
        fq_out[0, hh] = qn[:, sl].astype(BF16)
        fk_out[0, hh] = kn[:, sl].astype(BF16)
        fv_out[0, hh] = fv[:, sl].astype(BF16)
    z = f_lo + fb_ref[...]
    log_f = jnp.minimum(z, 0.0) - jnp.log(1.0 + jnp.exp(-jnp.abs(z)))
    tri = (lax.broadcasted_iota(jnp.int32, (tm, tm), 1) <= lax.broadcasted_iota(jnp.int32, (tm, tm), 0)).astype(F32)
    cum = _dot(tri, log_f, precision=lax.Precision.HIGHEST) + carry_c[...]
    carry_c[...] = cum[tm - 1:tm, :]
    cum_out[0] = cum


def _pad_cols(w, n):
    return jnp.pad(w, ((0, 0), (0, n - w.shape[1])))


def _pad_rows(w, n):
    return jnp.pad(w, ((0, n - w.shape[0]), (0, 0)))


def _in_proj(x, mix_norm, w_in, shift_mu, w0, w_up, a0, a_up, g_up, k_k, k_a, r_k, f_bias, q_norm, k_norm, *, tm=256):
    bsz, s, d = x.shape
    lw_, la_, lg_ = w_up.shape[0], a_up.shape[0], g_up.shape[0]
    c0 = 3 * D_MIX
    c1 = c0 + lw_ + la_ + lg_
    c2 = c1 + 3 * D_MIX
    c3 = c2 + N_HEADS

    def seg(t, pad):
        return jnp.concatenate([
            t[..., :c0],
            pad(t[..., c0:c0 + lw_], LANE), pad(t[..., c0 + lw_:c0 + lw_ + la_], LANE), pad(t[..., c0 + lw_ + la_:c1], LANE),
        ], axis=-1)

    w_cat = jnp.concatenate([seg(w_in, _pad_cols), w_in[:, c1:c2], _pad_cols(w_in[:, c2:c3], LANE)], axis=-1).astype(BF16)
    mu_cat = seg(shift_mu.reshape(1, -1), _pad_cols)
    row = lambda t: t.reshape(1, -1).astype(F32)
    e = _head_mask()
    tile = lambda t: jnp.tile(t.reshape(1, HEAD), (1, N_HEADS)).astype(F32)
    consts = [
        row(mix_norm), w_cat, mu_cat, row(w0), _pad_rows(w_up, LANE).astype(BF16), row(a0),
        _pad_rows(a_up, LANE).astype(BF16), _pad_rows(g_up, LANE).astype(BF16), row(k_k), row(k_a), row(r_k),
        _pad_cols(row(f_bias), LANE), tile(q_norm), tile(k_norm), e,
    ]
    const_specs = [pl.BlockSpec(c.shape, lambda b, i: (0, 0)) for c in consts]
    tok = lambda: pl.BlockSpec((1, tm, D_MIX), lambda b, i: (b, i, 0))
    headed = lambda: pl.BlockSpec((1, N_HEADS, tm, HEAD), lambda b, i: (b, 0, i, 0))
    f32_tok = jax.ShapeDtypeStruct((bsz, s, D_MIX), F32)
    bf_head = jax.ShapeDtypeStruct((bsz, N_HEADS, s, HEAD), BF16)
    assert s % tm == 0
    return pl.pallas_call(
        functools.partial(_in_proj_kernel, tm=tm),
        out_shape=[f32_tok] * 8 + [bf_head] * 3 + [jax.ShapeDtypeStruct((bsz, s, LANE), F32)],
        grid=(bsz, s // tm),
        in_specs=[pl.BlockSpec((1, tm, d), lambda b, i: (b, i, 0))] + const_specs,
        out_specs=[tok() for _ in range(8)] + [headed() for _ in range(3)]
        + [pl.BlockSpec((1, tm, LANE), lambda b, i: (b, i, 0))],
        scratch_shapes=[pltpu.VMEM((1, N_SHIFT), F32), pltpu.VMEM((1, LANE), F32)],
        compiler_params=pltpu.CompilerParams(
            dimension_semantics=("parallel", "arbitrary"), vmem_limit_bytes=VMEM_LIMIT),
        name="in_proj",
    )(x, *consts)


def _rwkv_kernel(r_ref, lw_ref, k_ref, v_ref, kk_ref, b_ref, g_ref, bonus_ref, lng_ref, lnb_ref, e_ref, y_ref, s_scr, *, tb):
    i = pl.program_id(1)

    @pl.when(i == 0)
    def _():
        s_scr[...] = jnp.zeros_like(s_scr)

    c = CHUNK
    pw = 2 * HEAD
    n_chunk = tb // c
    n_pair = N_HEADS // 2
    rr = lax.broadcasted_iota(jnp.int32, (c, pw), 0)
    ll = lax.broadcasted_iota(jnp.int32, (c, pw), 1)
    cc = ll % c
    lo = ll < c
    strict = cc < rr
    incl = cc <= rr
    eye = (cc == rr).astype(F32)
    r2 = lax.broadcasted_iota(jnp.int32, (2 * c, pw), 0)
    l2 = lax.broadcasted_iota(jnp.int32, (2 * c, pw), 1)
    lo2 = l2 < c
    diag2 = (r2 < c) == lo2
    zeros = jnp.zeros((c, pw), F32)

    def level_mask(n):
        return ((rr // (2 * n)) == (cc // (2 * n))) & (((rr // n) % 2) == 1) & (((cc // n) % 2) == 0)

    def bd(x):
        return jnp.where(diag2, jnp.concatenate([x, x], axis=0), 0.0).astype(BF16)

    def bd_sw(x):
        return jnp.where(diag2, 0.0, jnp.concatenate([x, x], axis=0)).astype(BF16)

    rb = lax.broadcasted_iota(jnp.int32, (tb, tb), 0)
    cb = lax.broadcasted_iota(jnp.int32, (tb, tb), 1)
    tri = ((cb <= rb) & ((rb // c) == (cb // c))).astype(F32)
    lw = lw_ref[0]
    cs = _dot(tri, lw, precision=lax.Precision.HIGHEST)
    w_inv = jnp.exp(-cs)
    rt = r_ref[0] * jnp.exp(cs)
    at = -kk_ref[0] * jnp.exp(cs - lw)
    bi = b_ref[0] * w_inv
    ki = k_ref[0] * w_inv
    v = v_ref[0]

    chains = [(j, p) for j in range(n_chunk) for p in range(n_pair)]
    cut = lambda x, j, p: x[j * c:(j + 1) * c, p * pw:(p + 1) * pw]
    w_tot = [jnp.exp(cs[(j + 1) * c - 1:(j + 1) * c, :]) for j in range(n_chunk)]
    a_t = [cut(at, j, p) for j, p in chains]
    r_t = [cut(rt, j, p) for j, p in chains]
    b_i = [cut(bi, j, p) for j, p in chains]
    k_i = [cut(ki, j, p) for j, p in chains]
    v_p = [cut(v, j, p) for j, p in chains]
    n = len(chains)

    a_ab, a_ak, a_rb, a_rk = [], [], [], []
    for q in range(n):
        ar = jnp.concatenate([a_t[q], r_t[q]], axis=0)
        gx = _dot(jnp.where(lo2, ar, 0.0).astype(BF16), jnp.concatenate([b_i[q], k_i[q]], axis=0).astype(BF16), _NT)
        gy = _dot(jnp.where(lo2, 0.0, ar).astype(BF16), jnp.concatenate([k_i[q], b_i[q]], axis=0).astype(BF16), _NT)
        a_ab.append(jnp.where(strict, jnp.where(lo, gx[:c], gy[:c]), 0.0))
        a_ak.append(jnp.where(strict, jnp.where(lo, gy[:c], gx[:c]), 0.0))
        a_rb.append(jnp.where(incl, jnp.where(lo, gx[c:], gy[c:]), 0.0))
        a_rk.append(jnp.where(incl, jnp.where(lo, gy[c:], gx[c:]), 0.0))

    t = [eye + jnp.where(level_mask(1), a_ab[q], 0.0) for q in range(n)]
    for m in (2, 4, 8, 16, 32):
        x = [_dot(jnp.where(level_mask(m), a_ab[q], 0.0).astype(BF16), bd(t[q])) for q in range(n)]
        t = [t[q] + _dot(t[q].astype(BF16), bd(x[q])) for q in range(n)]

    v_sw = [bd_sw(v_p[q]) for q in range(n)]
    akv = [_dot(a_ak[q].astype(BF16), v_sw[q]) for q in range(n)]
    x = [_dot(t[q].astype(BF16), jnp.concatenate([bd(a_t[q]), bd(akv[q])], axis=1)) for q in range(n)]
    a_p = [x[q][:, :pw] for q in range(n)]
    u_0 = [x[q][:, pw:] for q in range(n)]
    zero_bd = jnp.zeros((2 * c, pw), BF16)
    d1 = [_dot(jnp.concatenate([a_rb[q], a_rk[q]], axis=1).astype(BF16),
               jnp.concatenate([jnp.concatenate([bd(a_p[q]), bd(u_0[q])], axis=1),
                                jnp.concatenate([zero_bd, v_sw[q]], axis=1)], axis=0)) for q in range(n)]
    d2 = []
    for q, (j, p) in enumerate(chains):
        wt = w_tot[j][:, p * pw:(p + 1) * pw]
        lhs = jnp.concatenate([jnp.concatenate([a_p[q], u_0[q]], axis=1),
                               jnp.concatenate([zeros, v_p[q]], axis=1)], axis=0).astype(BF16)
        rhs = jnp.concatenate([b_i[q] * wt, k_i[q] * wt], axis=0).astype(BF16)
        d2.append(_dot(lhs, rhs, _TN))

    s = [s_scr[p] for p in range(n_pair)]
    ys = [None] * n
    for j in range(n_chunk):
        for p in range(n_pair):
            q = j * n_pair + p
            mt = jnp.where(diag2, d2[q][:pw], 0.0).astype(BF16)
            nt = jnp.where(lo, d2[q][pw:pw + c], d2[q][pw + c:])
            ys[q] = d1[q][:, pw:] + _dot((r_t[q] + d1[q][:, :pw]).astype(BF16), bd(s[p]), _NT)
            s[p] = s[p] * w_tot[j][:, p * pw:(p + 1) * pw] + _dot(s[p].astype(BF16), mt) + nt
    for p in range(n_pair):
        s_scr[p] = s[p]

    y = jnp.concatenate([jnp.concatenate([ys[j * n_pair + p] for p in range(n_pair)], axis=1) for j in range(n_chunk)], axis=0)
    e = e_ref[...]
    inv_head = 1.0 / HEAD
    yc = y - _head_sum(y, e) * inv_head
    yn = yc * lax.rsqrt(_head_sum(yc * yc, e) * inv_head + LNX_EPS)
    y_ref[0] = ((yn * lng_ref[...] + lnb_ref[...] + bonus_ref[0]) * g_ref[0]).astype(BF16)


def _head_mask():
    head_id = jnp.arange(D_MIX) // HEAD
    return (head_id[:, None] == head_id[None, :]).astype(BF16)


def _rwkv(r, lw, k, v, kk, b, g, bonus, lnx_g, lnx_b, *, tb=256):
    bsz, s, _ = r.shape
    assert s % tb == 0 and tb % CHUNK == 0
    tok = pl.BlockSpec((1, tb, D_MIX), lambda bb, i: (bb, i, 0))
    vec = pl.BlockSpec((1, D_MIX), lambda bb, i: (0, 0))
    return pl.pallas_call(
        functools.partial(_rwkv_kernel, tb=tb),
        out_shape=jax.ShapeDtypeStruct((bsz, s, D_MIX), BF16),
        grid=(bsz, s // tb),
        in_specs=[tok] * 8 + [vec, vec, pl.BlockSpec((D_MIX, D_MIX), lambda bb, i: (0, 0))],
        out_specs=tok,
        scratch_shapes=[pltpu.VMEM((N_HEADS // 2, HEAD, 2 * HEAD), F32)],
        compiler_params=pltpu.CompilerParams(
            dimension_semantics=("parallel", "arbitrary"), vmem_limit_bytes=VMEM_LIMIT),
        name="rwkv",
    )(r, lw, k, v, kk, b, g, bonus, lnx_g.reshape(1, -1).astype(F32), lnx_b.reshape(1, -1).astype(F32), _head_mask())


def _fox_kernel(q_ref, k_ref, v_ref, c_ref, o_ref, *, tq):
    qb = pl.program_id(2)
    row = lax.broadcasted_iota(jnp.int32, (tq, tq), 0)
    col = lax.broadcasted_iota(jnp.int32, (tq, tq), 1)
    causal = col <= row
    outs = []
    for hh in range(2):
        q = q_ref[0, hh]
        c_start = c_ref[0, hh, :, pl.ds(pl.multiple_of(qb * tq, tq), LANE)][:, 0:1]

        def step(kb, carry, masked, hh=hh, q=q, c_start=c_start):
            m, l, acc = carry
            ksl = pl.ds(pl.multiple_of(kb * tq, tq), tq)
            s = _dot(q, k_ref[0, hh, ksl, :], _NT) - (c_ref[0, hh, :, ksl] - c_start)
            if masked:
                s = jnp.where(causal, s, NEG_BIG)
            m_new = jnp.maximum(m, jnp.max(s, axis=-1, keepdims=True))
            alpha = jnp.exp(m - m_new)
            p = jnp.exp(s - m_new)
            l = alpha * l + jnp.sum(p, axis=-1, keepdims=True)
            acc = alpha * acc + _dot(p.astype(BF16), v_ref[0, hh, ksl, :])
            return m_new, l, acc

        init = (jnp.full((tq, 1), NEG_BIG, F32), jnp.zeros((tq, 1), F32), jnp.zeros((tq, HEAD), F32))
        carry = lax.fori_loop(0, qb, functools.partial(step, masked=False), init)
        _, l, acc = step(qb, carry, True)
        outs.append(acc / l)
    o_ref[0] = jnp.concatenate(outs, axis=-1).astype(BF16)


def _fox(q, k, v, cum_t, *, tq=256):
    bsz, nh, s, _ = q.shape
    assert s % tq == 0 and nh % 2 == 0
    kv = pl.BlockSpec((1, 2, s, HEAD), lambda b, hp, i: (b, hp, 0, 0))
    return pl.pallas_call(
        functools.partial(_fox_kernel, tq=tq),
        out_shape=jax.ShapeDtypeStruct((bsz, s, nh * HEAD), BF16),
        grid=(bsz, nh // 2, s // tq),
        in_specs=[
            pl.BlockSpec((1, 2, tq, HEAD), lambda b, hp, i: (b, hp, i, 0)), kv, kv,
            pl.BlockSpec((1, 2, 1, s), lambda b, hp, i: (b, hp, 0, 0)),
        ],
        out_specs=pl.BlockSpec((1, tq, 2 * HEAD), lambda b, hp, i: (b, i, hp)),
        compiler_params=pltpu.CompilerParams(
            dimension_semantics=("parallel", "parallel", "arbitrary"), vmem_limit_bytes=VMEM_LIMIT),
        name="fox",
    )(q, k, v, cum_t)


def _merge_kernel(x_ref, g_ref, wg_ref, ya_ref, yb_ref, pa_ref, pb_ref, wo_ref, o_ref):
    x = x_ref[...]
    d = x.shape[-1]
    h = _rms(x, g_ref[...]).astype(BF16)
    gates = _sigmoid(_dot(h, wg_ref[...]))
    ya = _dot(ya_ref[...], pa_ref[...])
    yb = _dot(yb_ref[...], pb_ref[...])
    mix = (gates[:, :d] * ya + gates[:, d:] * yb).astype(BF16)
    o_ref[...] = x + _dot(mix, wo_ref[...])


def _merge(x2d, mix_norm, w_gates, ya, yb, p_a, p_b, w_out, *, tm=512):
    m, d = x2d.shape
    assert m % tm == 0
    full = lambda t: pl.BlockSpec(t.shape, lambda i: (0, 0))
    tok = lambda n: pl.BlockSpec((tm, n), lambda i: (i, 0))
    consts = [mix_norm.reshape(1, d).astype(F32), w_gates.astype(BF16)]
    mats = [p_a.astype(BF16), p_b.astype(BF16), w_out.astype(BF16)]
    return pl.pallas_call(
        _merge_kernel,
        out_shape=jax.ShapeDtypeStruct((m, d), F32),
        grid=(m // tm,),
        in_specs=[tok(d), full(consts[0]), full(consts[1]), tok(D_MIX), tok(D_MIX)] + [full(t) for t in mats],
        out_specs=tok(d),
        compiler_params=pltpu.CompilerParams(dimension_semantics=("parallel",), vmem_limit_bytes=VMEM_LIMIT),
        name="merge",
    )(x2d, consts[0], consts[1], ya, yb, *mats)


def kernel(x, ffn1_norm, ffn1_w1, ffn1_w3, ffn1_w2, mix_norm, w_in, shift_mu, rwkv_w0, rwkv_w_up, rwkv_a0, rwkv_a_up, rwkv_g_up, rwkv_k_k, rwkv_k_a, rwkv_r_k, rwkv_lnx_g, rwkv_lnx_b, rwkv_proj, fox_f_bias, fox_q_norm, fox_k_norm, fox_proj, w_out, ffn2_norm, ffn2_w1, ffn2_w3, ffn2_w2, final_norm):
    bsz, s, d = x.shape
    depth = ffn1_norm.shape[0]
    n_gate0 = w_in.shape[-1] - 2 * d
    x2d = x.reshape(bsz * s, d)
    for l in range(depth):
        x2d = _ffn(x2d, ffn1_norm[l], ffn1_w1[l], ffn1_w3[l], ffn1_w2[l], final_norm, final_norm=False)
        (r, lw, k, v, kk, b, g, bonus, fq, fk, fv, cum) = _in_proj(
            x2d.reshape(bsz, s, d), mix_norm[l], w_in[l], shift_mu[l], rwkv_w0[l], rwkv_w_up[l], rwkv_a0[l],
            rwkv_a_up[l], rwkv_g_up[l], rwkv_k_k[l], rwkv_k_a[l], rwkv_r_k[l], fox_f_bias[l], fox_q_norm[l],
            fox_k_norm[l])
        ya = _rwkv(r, lw, k, v, kk, b, g, bonus, rwkv_lnx_g[l], rwkv_lnx_b[l])
        cum_t = jnp.transpose(cum[:, :, :N_HEADS], (0, 2, 1)).reshape(bsz, N_HEADS, 1, s)
        yb = _fox(fq, fk, fv, cum_t)
        x2d = _merge(x2d, mix_norm[l], w_in[l][:, n_gate0:], ya.reshape(bsz * s, D_MIX), yb.reshape(bsz * s, D_MIX),
                     rwkv_proj[l], fox_proj[l], w_out[l])
        x2d = _ffn(x2d, ffn2_norm[l], ffn2_w1[l], ffn2_w3[l], ffn2_w2[l], final_norm, final_norm=(l == depth - 1))
    return x2d.reshape(bsz, s, d)
```

```python
import functools

import jax
import jax.numpy as jnp
from jax import lax
from jax.experimental import pallas as pl
from jax.experimental.pallas import tpu as pltpu

F32 = jnp.float32
BF16 = jnp.bfloat16

HEAD = 64
N_HEADS = 8
D_MIX = HEAD * N_HEADS
LANE = 128
CHUNK = 64
RMS_EPS = 1e-6
LNX_EPS = 64e-5
NEG_BIG = -1e30
VMEM_LIMIT = 56 * 1024 * 1024

_NT = (((1,), (1,)), ((), ()))
_TN = (((0,), (0,)), ((), ()))


def _dot(a, b, dims=None, precision=None):
    if dims is None:
        return jnp.dot(a, b, preferred_element_type=F32, precision=precision)
    return lax.dot_general(a, b, dims, preferred_element_type=F32, precision=precision)


def _bdot(a, b, dims=None):
    return _dot(a.astype(BF16), b.astype(BF16), dims)


def _rms(x, g):
    return x * lax.rsqrt(jnp.mean(x * x, axis=-1, keepdims=True) + RMS_EPS) * g


def _softplus(z):
    return jnp.maximum(z, 0.0) + jnp.log(1.0 + jnp.exp(-jnp.abs(z)))


def _sigmoid(z):
    return 1.0 / (1.0 + jnp.exp(-z))


def _head_sum(x, e):
    hi = x.astype(BF16)
    lo = (x - hi.astype(F32)).astype(BF16)
    return _dot(hi, e) + _dot(lo, e)


def _ffn_kernel(x_ref, g_ref, w1_ref, w3_ref, w2_ref, gf_ref, o_ref, h_scr, acc_scr, *, final_norm):
    f = pl.program_id(1)

    @pl.when(f == 0)
    def _():
        h_scr[...] = _rms(x_ref[...], g_ref[...]).astype(BF16)
        acc_scr[...] = jnp.zeros_like(acc_scr)

    h = h_scr[...]
    a = _dot(h, w1_ref[...])
    b = _dot(h, w3_ref[...])
    u = (a * _sigmoid(a) * b).astype(BF16)
    acc_scr[...] += _dot(u, w2_ref[...])

    @pl.when(f == pl.num_programs(1) - 1)
    def _():
        y = x_ref[...] + 0.5 * acc_scr[...]
        if final_norm:
            y = _rms(y, gf_ref[...])
        o_ref[...] = y


def _ffn(x2d, g, w1, w3, w2, gf, *, final_norm, tm=512, n_f=2):
    m, d = x2d.shape
    f_dim = w1.shape[1]
    tf = f_dim // n_f
    assert m % tm == 0 and tf * n_f == f_dim and tf % LANE == 0
    return pl.pallas_call(
        functools.partial(_ffn_kernel, final_norm=final_norm),
        out_shape=jax.ShapeDtypeStruct((m, d), F32),
        grid=(m // tm, n_f),
        in_specs=[
            pl.BlockSpec((tm, d), lambda i, f: (i, 0)),
            pl.BlockSpec((1, d), lambda i, f: (0, 0)),
            pl.BlockSpec((d, tf), lambda i, f: (0, f)),
            pl.BlockSpec((d, tf), lambda i, f: (0, f)),
            pl.BlockSpec((tf, d), lambda i, f: (f, 0)),
            pl.BlockSpec((1, d), lambda i, f: (0, 0)),
        ],
        out_specs=pl.BlockSpec((tm, d), lambda i, f: (i, 0)),
        scratch_shapes=[pltpu.VMEM((tm, d), BF16), pltpu.VMEM((tm, d), F32)],
        compiler_params=pltpu.CompilerParams(
            dimension_semantics=("parallel", "arbitrary"), vmem_limit_bytes=VMEM_LIMIT),
        name="ffn",
    )(x2d, g.reshape(1, d), w1.astype(BF16), w3.astype(BF16), w2.astype(BF16), gf.reshape(1, d))


N_SHIFT = 3 * D_MIX + 3 * LANE
N_PROJ = N_SHIFT + 3 * D_MIX + LANE
LOG2E = 1.4426950408889634
V_ROWS = HEAD + 16
FOX_KBLOCK = 512


def _in_proj_kernel(x_ref, g_ref, w_ref, mu_ref, w0_ref, wup_ref, a0_ref, aup_ref, gup_ref, kk_ref, ka_ref,
                    rk_ref, fb_ref, qn_ref, kn_ref, e_ref,
                    r_out, lw_out, k_out, v_out, kk_out, b_out, g_out, bonus_out, fq_out, fk_out, fv_out, off_out,
                    carry_p, carry_c, carry_b, *, tm, tiles_per_kblock):
    i = pl.program_id(1)

    @pl.when(i == 0)
    def _():
        carry_p[...] = jnp.zeros_like(carry_p)
        carry_c[...] = jnp.zeros_like(carry_c)
        carry_b[...] = jnp.zeros_like(carry_b)

    h = _rms(x_ref[0], g_ref[...]).astype(BF16)
    p = _dot(h, w_ref[...])

    ps = p[:, :N_SHIFT]
    row = lax.broadcasted_iota(jnp.int32, (tm, 1), 0)
    prev = jnp.where(row == 0, carry_p[...], pltpu.roll(ps, 1, axis=0))
    carry_p[...] = ps[tm - 1:tm, :]
    sh = ps + (prev - ps) * mu_ref[...]

    e = e_ref[...]
    r = sh[:, 0:D_MIX]
    k = sh[:, D_MIX:2 * D_MIX]
    v = sh[:, 2 * D_MIX:3 * D_MIX]
    w_lo = sh[:, 3 * D_MIX:3 * D_MIX + LANE]
    a_lo = sh[:, 3 * D_MIX + LANE:3 * D_MIX + 2 * LANE]
    g_lo = sh[:, 3 * D_MIX + 2 * LANE:N_SHIFT]

    w_log = -_softplus(-(w0_ref[...] + _bdot(jnp.tanh(w_lo), wup_ref[...]))) - 0.5
    lw = -jnp.exp(w_log)
    a = _sigmoid(a0_ref[...] + _bdot(a_lo, aup_ref[...]))
    g = _bdot(_sigmoid(g_lo), gup_ref[...])
    kk = k * kk_ref[...]
    kk = kk / jnp.maximum(jnp.sqrt(_head_sum(kk * kk, e)), 1e-12)
    k_mod = k * (1.0 + (a - 1.0) * ka_ref[...])
    bonus = _head_sum(r * k_mod * rk_ref[...], e) * v

    r_out[0] = r
    lw_out[0] = lw
    k_out[0] = k_mod
    v_out[0] = v
    kk_out[0] = kk
    b_out[0] = kk * a
    g_out[0] = g
    bonus_out[0] = bonus

    fq = p[:, N_SHIFT:N_SHIFT + D_MIX]
    fk = p[:, N_SHIFT + D_MIX:N_SHIFT + 2 * D_MIX]
    fv = p[:, N_SHIFT + 2 * D_MIX:N_SHIFT + 3 * D_MIX]
    f_lo = p[:, N_SHIFT + 3 * D_MIX:N_PROJ]
    inv_head = 1.0 / HEAD
    qn = fq * lax.rsqrt(_head_sum(fq * fq, e) * inv_head + RMS_EPS) * qn_ref[...] * (LOG2E * HEAD ** -0.5)
    kn = fk * lax.rsqrt(_head_sum(fk * fk, e) * inv_head + RMS_EPS) * kn_ref[...]
    z = f_lo + fb_ref[...]
    log_f = jnp.minimum(z, 0.0) - jnp.log(1.0 + jnp.exp(-jnp.abs(z)))
    tri = (lax.broadcasted_iota(jnp.int32, (tm, tm), 1) <= lax.broadcasted_iota(jnp.int32, (tm, tm), 0)).astype(F32)
    c_loc = _dot(tri, log_f, precision=lax.Precision.HIGHEST)
    off_out[0, 0] = carry_c[...] * LOG2E
    in_blk = jnp.where(i % tiles_per_kblock == 0, 0.0, carry_b[...])
    carry_c[...] = carry_c[...] + c_loc[tm - 1:tm, :]
    carry_b[...] = in_blk + c_loc[tm - 1:tm, :]
    c2 = (c_loc + in_blk) * LOG2E

    lane = lax.broadcasted_iota(jnp.int32, (tm, HEAD), 1)
    sub = lax.broadcasted_iota(jnp.int32, (HEAD, tm), 0)
    q_tail = jnp.where(sub < 3, -1.0, 0.0)
    v_tail = jnp.where(lax.broadcasted_iota(jnp.int32, (V_ROWS - HEAD, tm), 0) == 0, 1.0, 0.0)
    for pr in range(N_HEADS // 2):
        ps2 = slice(pr * 2 * HEAD, (pr + 1) * 2 * HEAD)
        q_t = qn[:, ps2].T
        v_t = fv[:, ps2].T
        for half in range(2):
            hh = 2 * pr + half
            rows = slice(half * HEAD, (half + 1) * HEAD)
            fq_out[0, hh] = jnp.concatenate([q_t[rows], q_tail], axis=0).astype(BF16)
            fv_out[0, hh] = jnp.concatenate([v_t[rows], v_tail], axis=0).astype(BF16)
            c_h = c2[:, hh:hh + 1]
            c_hi = c_h.astype(BF16).astype(F32)
            c_mid = (c_h - c_hi).astype(BF16).astype(F32)
            c_lo = c_h - c_hi - c_mid
            c_cols = jnp.where(lane == 0, c_hi, jnp.where(lane == 1, c_mid, jnp.where(lane == 2, c_lo, 0.0)))
            fk_out[0, hh] = jnp.concatenate([kn[:, hh * HEAD:(hh + 1) * HEAD], c_cols], axis=1).astype(BF16)


def _pad_cols(w, n):
    return jnp.pad(w, ((0, 0), (0, n - w.shape[1])))


def _pad_rows(w, n):
    return jnp.pad(w, ((0, n - w.shape[0]), (0, 0)))


def _in_proj(x, mix_norm, w_in, shift_mu, w0, w_up, a0, a_up, g_up, k_k, k_a, r_k, f_bias, q_norm, k_norm, *, tm=256):
    bsz, s, d = x.shape
    lw_, la_, lg_ = w_up.shape[0], a_up.shape[0], g_up.shape[0]
    c0 = 3 * D_MIX
    c1 = c0 + lw_ + la_ + lg_
    c2 = c1 + 3 * D_MIX
    c3 = c2 + N_HEADS

    def seg(t, pad):
        return jnp.concatenate([
            t[..., :c0],
            pad(t[..., c0:c0 + lw_], LANE), pad(t[..., c0 + lw_:c0 + lw_ + la_], LANE), pad(t[..., c0 + lw_ + la_:c1], LANE),
        ], axis=-1)

    w_cat = jnp.concatenate([seg(w_in, _pad_cols), w_in[:, c1:c2], _pad_cols(w_in[:, c2:c3], LANE)], axis=-1).astype(BF16)
    mu_cat = seg(shift_mu.reshape(1, -1), _pad_cols)
    row = lambda t: t.reshape(1, -1).astype(F32)
    e = _head_mask()
    tile = lambda t: jnp.tile(t.reshape(1, HEAD), (1, N_HEADS)).astype(F32)
    consts = [
        row(mix_norm), w_cat, mu_cat, row(w0), _pad_rows(w_up, LANE).astype(BF16), row(a0),
        _pad_rows(a_up, LANE).astype(BF16), _pad_rows(g_up, LANE).astype(BF16), row(k_k), row(k_a), row(r_k),
        _pad_cols(row(f_bias), LANE), tile(q_norm), tile(k_norm), e,
    ]
    const_specs = [pl.BlockSpec(c.shape, lambda b, i: (0, 0)) for c in consts]
    tok = lambda: pl.BlockSpec((1, tm, D_MIX), lambda b, i: (b, i, 0))
    f32_tok = jax.ShapeDtypeStruct((bsz, s, D_MIX), F32)
    assert s % tm == 0
    return pl.pallas_call(
        functools.partial(_in_proj_kernel, tm=tm, tiles_per_kblock=FOX_KBLOCK // tm),
        out_shape=[f32_tok] * 8 + [
            jax.ShapeDtypeStruct((bsz, N_HEADS, 2 * HEAD, s), BF16),
            jax.ShapeDtypeStruct((bsz, N_HEADS, s, 2 * HEAD), BF16),
            jax.ShapeDtypeStruct((bsz, N_HEADS, V_ROWS, s), BF16),
            jax.ShapeDtypeStruct((bsz, s // tm, 1, LANE), F32),
        ],
        grid=(bsz, s // tm),
        in_specs=[pl.BlockSpec((1, tm, d), lambda b, i: (b, i, 0))] + const_specs,
        out_specs=[tok() for _ in range(8)] + [
            pl.BlockSpec((1, N_HEADS, 2 * HEAD, tm), lambda b, i: (b, 0, 0, i)),
            pl.BlockSpec((1, N_HEADS, tm, 2 * HEAD), lambda b, i: (b, 0, i, 0)),
            pl.BlockSpec((1, N_HEADS, V_ROWS, tm), lambda b, i: (b, 0, 0, i)),
            pl.BlockSpec((1, 1, 1, LANE), lambda b, i: (b, i, 0, 0)),
        ],
        scratch_shapes=[pltpu.VMEM((1, N_SHIFT), F32), pltpu.VMEM((1, LANE), F32), pltpu.VMEM((1, LANE), F32)],
        compiler_params=pltpu.CompilerParams(
            dimension_semantics=("parallel", "arbitrary"), vmem_limit_bytes=VMEM_LIMIT),
        name="in_proj",
    )(x, *consts)


def _rwkv_kernel(r_ref, lw_ref, k_ref, v_ref, kk_ref, b_ref, g_ref, bonus_ref, lng_ref, lnb_ref, e_ref, y_ref, s_scr, *, tb):
    i = pl.program_id(1)

    @pl.when(i == 0)
    def _():
        s_scr[...] = jnp.zeros_like(s_scr)

    c = CHUNK
    pw = 2 * HEAD
    n_chunk = tb // c
    n_pair = N_HEADS // 2
    rr = lax.broadcasted_iota(jnp.int32, (c, pw), 0)
    ll = lax.broadcasted_iota(jnp.int32, (c, pw), 1)
    cc = ll % c
    lo = ll < c
    strict = cc < rr
    incl = cc <= rr
    eye = (cc == rr).astype(F32)
    r2 = lax.broadcasted_iota(jnp.int32, (2 * c, pw), 0)
    l2 = lax.broadcasted_iota(jnp.int32, (2 * c, pw), 1)
    lo2 = l2 < c
    diag2 = (r2 < c) == lo2
    zeros = jnp.zeros((c, pw), F32)

    def level_mask(n):
        return ((rr // (2 * n)) == (cc // (2 * n))) & (((rr // n) % 2) == 1) & (((cc // n) % 2) == 0)

    def bd(x):
        return jnp.where(diag2, jnp.concatenate([x, x], axis=0), 0.0).astype(BF16)

    def bd_sw(x):
        return jnp.where(diag2, 0.0, jnp.concatenate([x, x], axis=0)).astype(BF16)

    rb = lax.broadcasted_iota(jnp.int32, (tb, tb), 0)
    cb = lax.broadcasted_iota(jnp.int32, (tb, tb), 1)
    tri = ((cb <= rb) & ((rb // c) == (cb // c))).astype(F32)
    lw = lw_ref[0]
    cs = _dot(tri, lw, precision=lax.Precision.HIGHEST)
    w_inv = jnp.exp(-cs)
    rt = r_ref[0] * jnp.exp(cs)
    at = -kk_ref[0] * jnp.exp(cs - lw)
    bi = b_ref[0] * w_inv
    ki = k_ref[0] * w_inv
    v = v_ref[0]

    chains = [(j, p) for j in range(n_chunk) for p in range(n_pair)]
    cut = lambda x, j, p: x[j * c:(j + 1) * c, p * pw:(p + 1) * pw]
    w_tot = [jnp.exp(cs[(j + 1) * c - 1:(j + 1) * c, :]) for j in range(n_chunk)]
    a_t = [cut(at, j, p) for j, p in chains]
    r_t = [cut(rt, j, p) for j, p in chains]
    b_i = [cut(bi, j, p) for j, p in chains]
    k_i = [cut(ki, j, p) for j, p in chains]
    v_p = [cut(v, j, p) for j, p in chains]
    n = len(chains)

    a_ab, a_ak, a_rb, a_rk = [], [], [], []
    for q in range(n):
        ar = jnp.concatenate([a_t[q], r_t[q]], axis=0)
        gx = _dot(jnp.where(lo2, ar, 0.0).astype(BF16), jnp.concatenate([b_i[q], k_i[q]], axis=0).astype(BF16), _NT)
        gy = _dot(jnp.where(lo2, 0.0, ar).astype(BF16), jnp.concatenate([k_i[q], b_i[q]], axis=0).astype(BF16), _NT)
        a_ab.append(jnp.where(strict, jnp.where(lo, gx[:c], gy[:c]), 0.0))
        a_ak.append(jnp.where(strict, jnp.where(lo, gy[:c], gx[:c]), 0.0))
        a_rb.append(jnp.where(incl, jnp.where(lo, gx[c:], gy[c:]), 0.0))
        a_rk.append(jnp.where(incl, jnp.where(lo, gy[c:], gx[c:]), 0.0))

    t = [eye + jnp.where(level_mask(1), a_ab[q], 0.0) for q in range(n)]
    for m in (2, 4, 8, 16, 32):
        x = [_dot(jnp.where(level_mask(m), a_ab[q], 0.0).astype(BF16), bd(t[q])) for q in range(n)]
        t = [t[q] + _dot(t[q].astype(BF16), bd(x[q])) for q in range(n)]

    v_sw = [bd_sw(v_p[q]) for q in range(n)]
    akv = [_dot(a_ak[q].astype(BF16), v_sw[q]) for q in range(n)]
    x = [_dot(t[q].astype(BF16), jnp.concatenate([bd(a_t[q]), bd(akv[q])], axis=1)) for q in range(n)]
    a_p = [x[q][:, :pw] for q in range(n)]
    u_0 = [x[q][:, pw:] for q in range(n)]
    zero_bd = jnp.zeros((2 * c, pw), BF16)
    d1 = [_dot(jnp.concatenate([a_rb[q], a_rk[q]], axis=1).astype(BF16),
               jnp.concatenate([jnp.concatenate([bd(a_p[q]), bd(u_0[q])], axis=1),
                                jnp.concatenate([zero_bd, v_sw[q]], axis=1)], axis=0)) for q in range(n)]
    d2 = []
    for q, (j, p) in enumerate(chains):
        wt = w_tot[j][:, p * pw:(p + 1) * pw]
        lhs = jnp.concatenate([jnp.concatenate([a_p[q], u_0[q]], axis=1),
                               jnp.concatenate([zeros, v_p[q]], axis=1)], axis=0).astype(BF16)
        rhs = jnp.concatenate([b_i[q] * wt, k_i[q] * wt], axis=0).astype(BF16)
        d2.append(_dot(lhs, rhs, _TN))

    s = [s_scr[p] for p in range(n_pair)]
    ys = [None] * n
    for j in range(n_chunk):
        for p in range(n_pair):
            q = j * n_pair + p
            mt = jnp.where(diag2, d2[q][:pw], 0.0).astype(BF16)
            nt = jnp.where(lo, d2[q][pw:pw + c], d2[q][pw + c:])
            ys[q] = d1[q][:, pw:] + _dot((r_t[q] + d1[q][:, :pw]).astype(BF16), bd(s[p]), _NT)
            s[p] = s[p] * w_tot[j][:, p * pw:(p + 1) * pw] + _dot(s[p].astype(BF16), mt) + nt
    for p in range(n_pair):
        s_scr[p] = s[p]

    y = jnp.concatenate([jnp.concatenate([ys[j * n_pair + p] for p in range(n_pair)], axis=1) for j in range(n_chunk)], axis=0)
    e = e_ref[...]
    inv_head = 1.0 / HEAD
    yc = y - _head_sum(y, e) * inv_head
    yn = yc * lax.rsqrt(_head_sum(yc * yc, e) * inv_head + LNX_EPS)
    y_ref[0] = ((yn * lng_ref[...] + lnb_ref[...] + bonus_ref[0]) * g_ref[0]).astype(BF16)


def _head_mask():
    head_id = jnp.arange(D_MIX) // HEAD
    return (head_id[:, None] == head_id[None, :]).astype(BF16)


def _rwkv(r, lw, k, v, kk, b, g, bonus, lnx_g, lnx_b, *, tb=256):
    bsz, s, _ = r.shape
    assert s % tb == 0 and tb % CHUNK == 0
    tok = pl.BlockSpec((1, tb, D_MIX), lambda bb, i: (bb, i, 0))
    vec = pl.BlockSpec((1, D_MIX), lambda bb, i: (0, 0))
    return pl.pallas_call(
        functools.partial(_rwkv_kernel, tb=tb),
        out_shape=jax.ShapeDtypeStruct((bsz, s, D_MIX), BF16),
        grid=(bsz, s // tb),
        in_specs=[tok] * 8 + [vec, vec, pl.BlockSpec((D_MIX, D_MIX), lambda bb, i: (0, 0))],
        out_specs=tok,
        scratch_shapes=[pltpu.VMEM((N_HEADS // 2, HEAD, 2 * HEAD), F32)],
        compiler_params=pltpu.CompilerParams(
            dimension_semantics=("parallel", "arbitrary"), vmem_limit_bytes=VMEM_LIMIT),
        name="rwkv",
    )(r, lw, k, v, kk, b, g, bonus, lnx_g.reshape(1, -1).astype(F32), lnx_b.reshape(1, -1).astype(F32), _head_mask())


def _fox_kernel(off_ref, q_ref, k_ref, v_ref, o_ref, s00, s01, s10, s11, m_scr, acc_scr, *, tq, tk, n_kblk):
    b, hp, qb = pl.program_id(0), pl.program_id(1), pl.program_id(2)
    kb_last = (qb * tq) // tk
    delta = qb * tq - kb_last * tk
    s_scr = ((s00, s01), (s10, s11))
    base = [(b * N_HEADS + 2 * hp + hh) * n_kblk for hh in range(2)]
    off_d = [off_ref[base[hh] + kb_last] for hh in range(2)]

    def qk(hh, kb, slot):
        ksl = pl.ds(pl.multiple_of(kb * tk, tk), tk)
        s_scr[slot][hh][...] = _dot(k_ref[0, hh, ksl, :], q_ref[0, hh])

    def soft_pv(hh, kb, slot, masked):
        s = s_scr[slot][hh][...]
        if masked:
            key = lax.broadcasted_iota(jnp.int32, (tk, tq), 0)
            qry = lax.broadcasted_iota(jnp.int32, (tk, tq), 1)
            s = jnp.where(key - qry <= delta, s, NEG_BIG)
        off = off_ref[base[hh] + kb] - off_d[hh]
        m = m_scr[hh]
        m_new = jnp.maximum(m, jnp.max(s, axis=0, keepdims=True) - off)
        p = jnp.exp2(s - (m_new + off)).astype(BF16)
        ksl = pl.ds(pl.multiple_of(kb * tk, tk), tk)
        acc_scr[hh] = jnp.exp2(m - m_new) * acc_scr[hh] + _dot(v_ref[0, hh, :, ksl], p)
        m_scr[hh] = m_new

    m_scr[...] = jnp.full(m_scr.shape, NEG_BIG, F32)
    acc_scr[...] = jnp.zeros(acc_scr.shape, F32)
    qk(0, 0, 0)
    qk(1, 0, 0)

    def pair(i, _):
        kb = 2 * i
        qk(0, kb + 1, 1)
        qk(1, kb + 1, 1)
        soft_pv(0, kb, 0, False)
        soft_pv(1, kb, 0, False)
        qk(0, kb + 2, 0)
        qk(1, kb + 2, 0)
        soft_pv(0, kb + 1, 1, False)
        soft_pv(1, kb + 1, 1, False)
        return 0

    lax.fori_loop(0, kb_last // 2, pair, 0)

    @pl.when(kb_last % 2 == 1)
    def _():
        qk(0, kb_last, 1)
        qk(1, kb_last, 1)
        soft_pv(0, kb_last - 1, 0, False)
        soft_pv(1, kb_last - 1, 0, False)
        soft_pv(0, kb_last, 1, True)
        soft_pv(1, kb_last, 1, True)

    @pl.when(kb_last % 2 == 0)
    def _():
        soft_pv(0, kb_last, 0, True)
        soft_pv(1, kb_last, 0, True)

    a0, a1 = acc_scr[0], acc_scr[1]
    o_t = jnp.concatenate([a0[:HEAD] / a0[HEAD:HEAD + 1], a1[:HEAD] / a1[HEAD:HEAD + 1]], axis=0)
    o_ref[0] = o_t.T.astype(BF16)


def _fox(q_t, k_aug, v_t, off, *, tq=256, tk=FOX_KBLOCK):
    bsz, nh, s, _ = k_aug.shape
    n_kblk = s // tk
    assert s % tq == 0 and s % tk == 0 and tk % tq == 0 and nh % 2 == 0 and off.shape == (bsz * nh * n_kblk,)
    return pl.pallas_call(
        functools.partial(_fox_kernel, tq=tq, tk=tk, n_kblk=n_kblk),
        out_shape=jax.ShapeDtypeStruct((bsz, s, nh * HEAD), BF16),
        grid=(bsz, nh // 2, s // tq),
        in_specs=[
            pl.BlockSpec(memory_space=pltpu.SMEM),
            pl.BlockSpec((1, 2, 2 * HEAD, tq), lambda b, hp, i: (b, hp, 0, i)),
            pl.BlockSpec((1, 2, s, 2 * HEAD), lambda b, hp, i: (b, hp, 0, 0)),
            pl.BlockSpec((1, 2, V_ROWS, s), lambda b, hp, i: (b, hp, 0, 0)),
        ],
        out_specs=pl.BlockSpec((1, tq, 2 * HEAD), lambda b, hp, i: (b, i, hp)),
        scratch_shapes=[pltpu.VMEM((tk, tq), F32)] * 4 + [pltpu.VMEM((2, 1, tq), F32), pltpu.VMEM((2, V_ROWS, tq), F32)],
        compiler_params=pltpu.CompilerParams(
            dimension_semantics=("parallel", "parallel", "arbitrary"), vmem_limit_bytes=VMEM_LIMIT),
        name="fox",
    )(off, q_t, k_aug, v_t)


def _merge_kernel(x_ref, g_ref, wg_ref, ya_ref, yb_ref, pa_ref, pb_ref, wo_ref, o_ref):
    x = x_ref[...]
    d = x.shape[-1]
    h = _rms(x, g_ref[...]).astype(BF16)
    gates = _sigmoid(_dot(h, wg_ref[...]))
    ya = _dot(ya_ref[...], pa_ref[...])
    yb = _dot(yb_ref[...], pb_ref[...])
    mix = (gates[:, :d] * ya + gates[:, d:] * yb).astype(BF16)
    o_ref[...] = x + _dot(mix, wo_ref[...])


def _merge(x2d, mix_norm, w_gates, ya, yb, p_a, p_b, w_out, *, tm=512):
    m, d = x2d.shape
    assert m % tm == 0
    full = lambda t: pl.BlockSpec(t.shape, lambda i: (0, 0))
    tok = lambda n: pl.BlockSpec((tm, n), lambda i: (i, 0))
    consts = [mix_norm.reshape(1, d).astype(F32), w_gates.astype(BF16)]
    mats = [p_a.astype(BF16), p_b.astype(BF16), w_out.astype(BF16)]
    return pl.pallas_call(
        _merge_kernel,
        out_shape=jax.ShapeDtypeStruct((m, d), F32),
        grid=(m // tm,),
        in_specs=[tok(d), full(consts[0]), full(consts[1]), tok(D_MIX), tok(D_MIX)] + [full(t) for t in mats],
        out_specs=tok(d),
        compiler_params=pltpu.CompilerParams(dimension_semantics=("parallel",), vmem_limit_bytes=VMEM_LIMIT),
        name="merge",
    )(x2d, consts[0], consts[1], ya, yb, *mats)


def kernel(x, ffn1_norm, ffn1_w1, ffn1_w3, ffn1_w2, mix_norm, w_in, shift_mu, rwkv_w0, rwkv_w_up, rwkv_a0, rwkv_a_up, rwkv_g_up, rwkv_k_k, rwkv_k_a, rwkv_r_k, rwkv_lnx_g, rwkv_lnx_b, rwkv_proj, fox_f_bias, fox_q_norm, fox_k_norm, fox_proj, w_out, ffn2_norm, ffn2_w1, ffn2_w3, ffn2_w2, final_norm):
    bsz, s, d = x.shape
    depth = ffn1_norm.shape[0]
    n_gate0 = w_in.shape[-1] - 2 * d
    x2d = x.reshape(bsz * s, d)
    for l in range(depth):
        x2d = _ffn(x2d, ffn1_norm[l], ffn1_w1[l], ffn1_w3[l], ffn1_w2[l], final_norm, final_norm=False)
        (r, lw, k, v, kk, b, g, bonus, fq_t, fk_aug, fv_t, off) = _in_proj(
            x2d.reshape(bsz, s, d), mix_norm[l], w_in[l], shift_mu[l], rwkv_w0[l], rwkv_w_up[l], rwkv_a0[l],
            rwkv_a_up[l], rwkv_g_up[l], rwkv_k_k[l], rwkv_k_a[l], rwkv_r_k[l], fox_f_bias[l], fox_q_norm[l],
            fox_k_norm[l])
        ya = _rwkv(r, lw, k, v, kk, b, g, bonus, rwkv_lnx_g[l], rwkv_lnx_b[l])
        off = off[:, ::FOX_KBLOCK // (s // off.shape[1]), 0, :N_HEADS]
        off = jnp.transpose(off, (0, 2, 1)).reshape(-1)
        yb = _fox(fq_t, fk_aug, fv_t, off)
        x2d = _merge(x2d, mix_norm[l], w_in[l][:, n_gate0:], ya.reshape(bsz * s, D_MIX), yb.reshape(bsz * s, D_MIX),
                     rwkv_proj[l], fox_proj[l], w_out[l])
        x2d = _ffn(x2d, ffn2_norm[l], ffn2_w1[l], ffn2_w3[l], ffn2_w2[l], final_norm, final_norm=(l == depth - 1))
    return x2d.reshape(bsz, s, d)
```

```python
import functools

import jax
import jax.numpy as jnp
from jax import lax
from jax.experimental import pallas as pl
from jax.experimental.pallas import tpu as pltpu

F32 = jnp.float32
BF16 = jnp.bfloat16

HEAD = 64
N_HEADS = 8
D_MIX = HEAD * N_HEADS
LANE = 128
CHUNK = 64
RMS_EPS = 1e-6
LNX_EPS = 64e-5
NEG_BIG = -1e30
EXP2_UNDERFLOW = 160.0
VMEM_LIMIT = 56 * 1024 * 1024

_NT = (((1,), (1,)), ((), ()))
_TN = (((0,), (0,)), ((), ()))


def _dot(a, b, dims=None, precision=None):
    if dims is None:
        return jnp.dot(a, b, preferred_element_type=F32, precision=precision)
    return lax.dot_general(a, b, dims, preferred_element_type=F32, precision=precision)


def _bdot(a, b, dims=None):
    return _dot(a.astype(BF16), b.astype(BF16), dims)


def _rms(x, g):
    return x * lax.rsqrt(jnp.mean(x * x, axis=-1, keepdims=True) + RMS_EPS) * g


def _softplus(z):
    return jnp.maximum(z, 0.0) + jnp.log(1.0 + jnp.exp(-jnp.abs(z)))


def _sigmoid(z):
    return 1.0 / (1.0 + jnp.exp(-z))


def _head_sum(x, e):
    hi = x.astype(BF16)
    lo = (x - hi.astype(F32)).astype(BF16)
    return _dot(hi, e) + _dot(lo, e)


def _ffn_kernel(x_ref, g_ref, w1_ref, w3_ref, w2_ref, gf_ref, o_ref, h_scr, acc_scr, *, final_norm):
    f = pl.program_id(1)

    @pl.when(f == 0)
    def _():
        h_scr[...] = _rms(x_ref[...], g_ref[...]).astype(BF16)
        acc_scr[...] = jnp.zeros_like(acc_scr)

    h = h_scr[...]
    a = _dot(h, w1_ref[...])
    b = _dot(h, w3_ref[...])
    u = (a * _sigmoid(a) * b).astype(BF16)
    acc_scr[...] += _dot(u, w2_ref[...])

    @pl.when(f == pl.num_programs(1) - 1)
    def _():
        y = x_ref[...] + 0.5 * acc_scr[...]
        if final_norm:
            y = _rms(y, gf_ref[...])
        o_ref[...] = y


def _ffn(x2d, g, w1, w3, w2, gf, *, final_norm, tm=512, n_f=2):
    m, d = x2d.shape
    f_dim = w1.shape[1]
    tf = f_dim // n_f
    assert m % tm == 0 and tf * n_f == f_dim and tf % LANE == 0
    return pl.pallas_call(
        functools.partial(_ffn_kernel, final_norm=final_norm),
        out_shape=jax.ShapeDtypeStruct((m, d), F32),
        grid=(m // tm, n_f),
        in_specs=[
            pl.BlockSpec((tm, d), lambda i, f: (i, 0)),
            pl.BlockSpec((1, d), lambda i, f: (0, 0)),
            pl.BlockSpec((d, tf), lambda i, f: (0, f)),
            pl.BlockSpec((d, tf), lambda i, f: (0, f)),
            pl.BlockSpec((tf, d), lambda i, f: (f, 0)),
            pl.BlockSpec((1, d), lambda i, f: (0, 0)),
        ],
        out_specs=pl.BlockSpec((tm, d), lambda i, f: (i, 0)),
        scratch_shapes=[pltpu.VMEM((tm, d), BF16), pltpu.VMEM((tm, d), F32)],
        compiler_params=pltpu.CompilerParams(
            dimension_semantics=("parallel", "arbitrary"), vmem_limit_bytes=VMEM_LIMIT),
        name="ffn",
    )(x2d, g.reshape(1, d), w1.astype(BF16), w3.astype(BF16), w2.astype(BF16), gf.reshape(1, d))


N_SHIFT = 3 * D_MIX + 3 * LANE
N_PROJ = N_SHIFT + 3 * D_MIX + LANE
LOG2E = 1.4426950408889634
V_ROWS = HEAD + 16
FOX_KBLOCK = 512


def _in_proj_kernel(x_ref, g_ref, w_ref, mu_ref, w0_ref, wup_ref, a0_ref, aup_ref, gup_ref, kk_ref, ka_ref,
                    rk_ref, fb_ref, qn_ref, kn_ref, e_ref,
                    r_out, lw_out, k_out, v_out, kk_out, b_out, g_out, bonus_out, fq_out, fk_out, fv_out, off_out,
                    carry_p, carry_c, carry_b, *, tm, tiles_per_kblock):
    i = pl.program_id(1)

    @pl.when(i == 0)
    def _():
        carry_p[...] = jnp.zeros_like(carry_p)
        carry_c[...] = jnp.zeros_like(carry_c)
        carry_b[...] = jnp.zeros_like(carry_b)

    h = _rms(x_ref[0], g_ref[...]).astype(BF16)
    p = _dot(h, w_ref[...])

    ps = p[:, :N_SHIFT]
    row = lax.broadcasted_iota(jnp.int32, (tm, 1), 0)
    prev = jnp.where(row == 0, carry_p[...], pltpu.roll(ps, 1, axis=0))
    carry_p[...] = ps[tm - 1:tm, :]
    sh = ps + (prev - ps) * mu_ref[...]

    e = e_ref[...]
    r = sh[:, 0:D_MIX]
    k = sh[:, D_MIX:2 * D_MIX]
    v = sh[:, 2 * D_MIX:3 * D_MIX]
    w_lo = sh[:, 3 * D_MIX:3 * D_MIX + LANE]
    a_lo = sh[:, 3 * D_MIX + LANE:3 * D_MIX + 2 * LANE]
    g_lo = sh[:, 3 * D_MIX + 2 * LANE:N_SHIFT]

    w_log = -_softplus(-(w0_ref[...] + _bdot(jnp.tanh(w_lo), wup_ref[...]))) - 0.5
    lw = -jnp.exp(w_log)
    a = _sigmoid(a0_ref[...] + _bdot(a_lo, aup_ref[...]))
    g = _bdot(_sigmoid(g_lo), gup_ref[...])
    kk = k * kk_ref[...]
    kk = kk / jnp.maximum(jnp.sqrt(_head_sum(kk * kk, e)), 1e-12)
    k_mod = k * (1.0 + (a - 1.0) * ka_ref[...])
    bonus = _head_sum(r * k_mod * rk_ref[...], e) * v

    r_out[0] = r
    lw_out[0] = lw
    k_out[0] = k_mod
    v_out[0] = v
    kk_out[0] = kk
    b_out[0] = kk * a
    g_out[0] = g
    bonus_out[0] = bonus

    fq = p[:, N_SHIFT:N_SHIFT + D_MIX]
    fk = p[:, N_SHIFT + D_MIX:N_SHIFT + 2 * D_MIX]
    fv = p[:, N_SHIFT + 2 * D_MIX:N_SHIFT + 3 * D_MIX]
    f_lo = p[:, N_SHIFT + 3 * D_MIX:N_PROJ]
    inv_head = 1.0 / HEAD
    qn = fq * lax.rsqrt(_head_sum(fq * fq, e) * inv_head + RMS_EPS) * qn_ref[...] * (LOG2E * HEAD ** -0.5)
    kn = fk * lax.rsqrt(_head_sum(fk * fk, e) * inv_head + RMS_EPS) * kn_ref[...]
    z = f_lo + fb_ref[...]
    log_f = jnp.minimum(z, 0.0) - jnp.log(1.0 + jnp.exp(-jnp.abs(z)))
    tri = (lax.broadcasted_iota(jnp.int32, (tm, tm), 1) <= lax.broadcasted_iota(jnp.int32, (tm, tm), 0)).astype(F32)
    c_loc = _dot(tri, log_f, precision=lax.Precision.HIGHEST)
    off_out[0, 0] = carry_c[...] * LOG2E
    in_blk = jnp.where(i % tiles_per_kblock == 0, 0.0, carry_b[...])
    carry_c[...] = carry_c[...] + c_loc[tm - 1:tm, :]
    carry_b[...] = in_blk + c_loc[tm - 1:tm, :]
    c2 = (c_loc + in_blk) * LOG2E

    lane = lax.broadcasted_iota(jnp.int32, (tm, HEAD), 1)
    sub = lax.broadcasted_iota(jnp.int32, (HEAD, tm), 0)
    q_tail = jnp.where(sub < 3, -1.0, 0.0)
    v_tail = jnp.where(lax.broadcasted_iota(jnp.int32, (V_ROWS - HEAD, tm), 0) == 0, 1.0, 0.0)
    for pr in range(N_HEADS // 2):
        ps2 = slice(pr * 2 * HEAD, (pr + 1) * 2 * HEAD)
        q_t = qn[:, ps2].T
        v_t = fv[:, ps2].T
        for half in range(2):
            hh = 2 * pr + half
            rows = slice(half * HEAD, (half + 1) * HEAD)
            fq_out[0, hh] = jnp.concatenate([q_t[rows], q_tail], axis=0).astype(BF16)
            fv_out[0, hh] = jnp.concatenate([v_t[rows], v_tail], axis=0).astype(BF16)
            c_h = c2[:, hh:hh + 1]
            c_hi = c_h.astype(BF16).astype(F32)
            c_mid = (c_h - c_hi).astype(BF16).astype(F32)
            c_lo = c_h - c_hi - c_mid
            c_cols = jnp.where(lane == 0, c_hi, jnp.where(lane == 1, c_mid, jnp.where(lane == 2, c_lo, 0.0)))
            fk_out[0, hh] = jnp.concatenate([kn[:, hh * HEAD:(hh + 1) * HEAD], c_cols], axis=1).astype(BF16)


def _pad_cols(w, n):
    return jnp.pad(w, ((0, 0), (0, n - w.shape[1])))


def _pad_rows(w, n):
    return jnp.pad(w, ((0, n - w.shape[0]), (0, 0)))


def _in_proj(x, mix_norm, w_in, shift_mu, w0, w_up, a0, a_up, g_up, k_k, k_a, r_k, f_bias, q_norm, k_norm, *, tm=256):
    bsz, s, d = x.shape
    lw_, la_, lg_ = w_up.shape[0], a_up.shape[0], g_up.shape[0]
    c0 = 3 * D_MIX
    c1 = c0 + lw_ + la_ + lg_
    c2 = c1 + 3 * D_MIX
    c3 = c2 + N_HEADS

    def seg(t, pad):
        return jnp.concatenate([
            t[..., :c0],
            pad(t[..., c0:c0 + lw_], LANE), pad(t[..., c0 + lw_:c0 + lw_ + la_], LANE), pad(t[..., c0 + lw_ + la_:c1], LANE),
        ], axis=-1)

    w_cat = jnp.concatenate([seg(w_in, _pad_cols), w_in[:, c1:c2], _pad_cols(w_in[:, c2:c3], LANE)], axis=-1).astype(BF16)
    mu_cat = seg(shift_mu.reshape(1, -1), _pad_cols)
    row = lambda t: t.reshape(1, -1).astype(F32)
    e = _head_mask()
    tile = lambda t: jnp.tile(t.reshape(1, HEAD), (1, N_HEADS)).astype(F32)
    consts = [
        row(mix_norm), w_cat, mu_cat, row(w0), _pad_rows(w_up, LANE).astype(BF16), row(a0),
        _pad_rows(a_up, LANE).astype(BF16), _pad_rows(g_up, LANE).astype(BF16), row(k_k), row(k_a), row(r_k),
        _pad_cols(row(f_bias), LANE), tile(q_norm), tile(k_norm), e,
    ]
    const_specs = [pl.BlockSpec(c.shape, lambda b, i: (0, 0)) for c in consts]
    tok = lambda: pl.BlockSpec((1, tm, D_MIX), lambda b, i: (b, i, 0))
    f32_tok = jax.ShapeDtypeStruct((bsz, s, D_MIX), F32)
    assert s % tm == 0
    return pl.pallas_call(
        functools.partial(_in_proj_kernel, tm=tm, tiles_per_kblock=FOX_KBLOCK // tm),
        out_shape=[f32_tok] * 8 + [
            jax.ShapeDtypeStruct((bsz, N_HEADS, 2 * HEAD, s), BF16),
            jax.ShapeDtypeStruct((bsz, N_HEADS, s, 2 * HEAD), BF16),
            jax.ShapeDtypeStruct((bsz, N_HEADS, V_ROWS, s), BF16),
            jax.ShapeDtypeStruct((bsz, s // tm, 1, LANE), F32),
        ],
        grid=(bsz, s // tm),
        in_specs=[pl.BlockSpec((1, tm, d), lambda b, i: (b, i, 0))] + const_specs,
        out_specs=[tok() for _ in range(8)] + [
            pl.BlockSpec((1, N_HEADS, 2 * HEAD, tm), lambda b, i: (b, 0, 0, i)),
            pl.BlockSpec((1, N_HEADS, tm, 2 * HEAD), lambda b, i: (b, 0, i, 0)),
            pl.BlockSpec((1, N_HEADS, V_ROWS, tm), lambda b, i: (b, 0, 0, i)),
            pl.BlockSpec((1, 1, 1, LANE), lambda b, i: (b, i, 0, 0)),
        ],
        scratch_shapes=[pltpu.VMEM((1, N_SHIFT), F32), pltpu.VMEM((1, LANE), F32), pltpu.VMEM((1, LANE), F32)],
        compiler_params=pltpu.CompilerParams(
            dimension_semantics=("parallel", "arbitrary"), vmem_limit_bytes=VMEM_LIMIT),
        name="in_proj",
    )(x, *consts)


def _rwkv_kernel(r_ref, lw_ref, k_ref, v_ref, kk_ref, b_ref, g_ref, bonus_ref, lng_ref, lnb_ref, e_ref, y_ref, s_scr, *, tb):
    i = pl.program_id(1)

    @pl.when(i == 0)
    def _():
        s_scr[...] = jnp.zeros_like(s_scr)

    c = CHUNK
    pw = 2 * HEAD
    n_chunk = tb // c
    n_pair = N_HEADS // 2
    rr = lax.broadcasted_iota(jnp.int32, (c, pw), 0)
    ll = lax.broadcasted_iota(jnp.int32, (c, pw), 1)
    cc = ll % c
    lo = ll < c
    strict = cc < rr
    incl = cc <= rr
    eye = (cc == rr).astype(F32)
    r2 = lax.broadcasted_iota(jnp.int32, (2 * c, pw), 0)
    l2 = lax.broadcasted_iota(jnp.int32, (2 * c, pw), 1)
    lo2 = l2 < c
    diag2 = (r2 < c) == lo2
    zeros = jnp.zeros((c, pw), F32)

    def level_mask(n):
        return ((rr // (2 * n)) == (cc // (2 * n))) & (((rr // n) % 2) == 1) & (((cc // n) % 2) == 0)

    def bd(x):
        return jnp.where(diag2, jnp.concatenate([x, x], axis=0), 0.0).astype(BF16)

    def bd_sw(x):
        return jnp.where(diag2, 0.0, jnp.concatenate([x, x], axis=0)).astype(BF16)

    rb = lax.broadcasted_iota(jnp.int32, (tb, tb), 0)
    cb = lax.broadcasted_iota(jnp.int32, (tb, tb), 1)
    tri = ((cb <= rb) & ((rb // c) == (cb // c))).astype(F32)
    lw = lw_ref[0]
    cs = _dot(tri, lw, precision=lax.Precision.HIGHEST)
    w_inv = jnp.exp(-cs)
    rt = r_ref[0] * jnp.exp(cs)
    at = -kk_ref[0] * jnp.exp(cs - lw)
    bi = b_ref[0] * w_inv
    ki = k_ref[0] * w_inv
    v = v_ref[0]

    chains = [(j, p) for j in range(n_chunk) for p in range(n_pair)]
    cut = lambda x, j, p: x[j * c:(j + 1) * c, p * pw:(p + 1) * pw]
    w_tot = [jnp.exp(cs[(j + 1) * c - 1:(j + 1) * c, :]) for j in range(n_chunk)]
    a_t = [cut(at, j, p) for j, p in chains]
    r_t = [cut(rt, j, p) for j, p in chains]
    b_i = [cut(bi, j, p) for j, p in chains]
    k_i = [cut(ki, j, p) for j, p in chains]
    v_p = [cut(v, j, p) for j, p in chains]
    n = len(chains)

    a_ab, a_ak, a_rb, a_rk = [], [], [], []
    for q in range(n):
        ar = jnp.concatenate([a_t[q], r_t[q]], axis=0)
        gx = _dot(jnp.where(lo2, ar, 0.0).astype(BF16), jnp.concatenate([b_i[q], k_i[q]], axis=0).astype(BF16), _NT)
        gy = _dot(jnp.where(lo2, 0.0, ar).astype(BF16), jnp.concatenate([k_i[q], b_i[q]], axis=0).astype(BF16), _NT)
        a_ab.append(jnp.where(strict, jnp.where(lo, gx[:c], gy[:c]), 0.0))
        a_ak.append(jnp.where(strict, jnp.where(lo, gy[:c], gx[:c]), 0.0))
        a_rb.append(jnp.where(incl, jnp.where(lo, gx[c:], gy[c:]), 0.0))
        a_rk.append(jnp.where(incl, jnp.where(lo, gy[c:], gx[c:]), 0.0))

    t = [eye + jnp.where(level_mask(1), a_ab[q], 0.0) for q in range(n)]
    for m in (2, 4, 8, 16, 32):
        x = [_dot(jnp.where(level_mask(m), a_ab[q], 0.0).astype(BF16), bd(t[q])) for q in range(n)]
        t = [t[q] + _dot(t[q].astype(BF16), bd(x[q])) for q in range(n)]

    v_sw = [bd_sw(v_p[q]) for q in range(n)]
    akv = [_dot(a_ak[q].astype(BF16), v_sw[q]) for q in range(n)]
    x = [_dot(t[q].astype(BF16), jnp.concatenate([bd(a_t[q]), bd(akv[q])], axis=1)) for q in range(n)]
    a_p = [x[q][:, :pw] for q in range(n)]
    u_0 = [x[q][:, pw:] for q in range(n)]
    zero_bd = jnp.zeros((2 * c, pw), BF16)
    d1 = [_dot(jnp.concatenate([a_rb[q], a_rk[q]], axis=1).astype(BF16),
               jnp.concatenate([jnp.concatenate([bd(a_p[q]), bd(u_0[q])], axis=1),
                                jnp.concatenate([zero_bd, v_sw[q]], axis=1)], axis=0)) for q in range(n)]
    d2 = []
    for q, (j, p) in enumerate(chains):
        wt = w_tot[j][:, p * pw:(p + 1) * pw]
        lhs = jnp.concatenate([jnp.concatenate([a_p[q], u_0[q]], axis=1),
                               jnp.concatenate([zeros, v_p[q]], axis=1)], axis=0).astype(BF16)
        rhs = jnp.concatenate([b_i[q] * wt, k_i[q] * wt], axis=0).astype(BF16)
        d2.append(_dot(lhs, rhs, _TN))

    s = [s_scr[p] for p in range(n_pair)]
    ys = [None] * n
    for j in range(n_chunk):
        for p in range(n_pair):
            q = j * n_pair + p
            mt = jnp.where(diag2, d2[q][:pw], 0.0).astype(BF16)
            nt = jnp.where(lo, d2[q][pw:pw + c], d2[q][pw + c:])
            ys[q] = d1[q][:, pw:] + _dot((r_t[q] + d1[q][:, :pw]).astype(BF16), bd(s[p]), _NT)
            s[p] = s[p] * w_tot[j][:, p * pw:(p + 1) * pw] + _dot(s[p].astype(BF16), mt) + nt
    for p in range(n_pair):
        s_scr[p] = s[p]

    y = jnp.concatenate([jnp.concatenate([ys[j * n_pair + p] for p in range(n_pair)], axis=1) for j in range(n_chunk)], axis=0)
    e = e_ref[...]
    inv_head = 1.0 / HEAD
    yc = y - _head_sum(y, e) * inv_head
    yn = yc * lax.rsqrt(_head_sum(yc * yc, e) * inv_head + LNX_EPS)
    y_ref[0] = ((yn * lng_ref[...] + lnb_ref[...] + bonus_ref[0]) * g_ref[0]).astype(BF16)


def _head_mask():
    head_id = jnp.arange(D_MIX) // HEAD
    return (head_id[:, None] == head_id[None, :]).astype(BF16)


def _rwkv(r, lw, k, v, kk, b, g, bonus, lnx_g, lnx_b, *, tb=256):
    bsz, s, _ = r.shape
    assert s % tb == 0 and tb % CHUNK == 0
    tok = pl.BlockSpec((1, tb, D_MIX), lambda bb, i: (bb, i, 0))
    vec = pl.BlockSpec((1, D_MIX), lambda bb, i: (0, 0))
    return pl.pallas_call(
        functools.partial(_rwkv_kernel, tb=tb),
        out_shape=jax.ShapeDtypeStruct((bsz, s, D_MIX), BF16),
        grid=(bsz, s // tb),
        in_specs=[tok] * 8 + [vec, vec, pl.BlockSpec((D_MIX, D_MIX), lambda bb, i: (0, 0))],
        out_specs=tok,
        scratch_shapes=[pltpu.VMEM((N_HEADS // 2, HEAD, 2 * HEAD), F32)],
        compiler_params=pltpu.CompilerParams(
            dimension_semantics=("parallel", "arbitrary"), vmem_limit_bytes=VMEM_LIMIT),
        name="rwkv",
    )(r, lw, k, v, kk, b, g, bonus, lnx_g.reshape(1, -1).astype(F32), lnx_b.reshape(1, -1).astype(F32), _head_mask())


def _fox_kernel(off_ref, thr_ref, q_ref, k_ref, v_ref, o_ref, s00, s01, s10, s11, m_scr, acc_scr, *, tq, tk, n_kblk):
    b, hp, qb = pl.program_id(0), pl.program_id(1), pl.program_id(2)
    kb_last = (qb * tq) // tk
    delta = qb * tq - kb_last * tk
    s_scr = ((s00, s01), (s10, s11))
    base = [(b * N_HEADS + 2 * hp + hh) * n_kblk for hh in range(2)]
    off_d = [off_ref[base[hh] + kb_last] for hh in range(2)]

    thr = thr_ref[0]

    def needed(kb):
        return jnp.minimum(off_ref[base[0] + kb + 1] - off_d[0], off_ref[base[1] + kb + 1] - off_d[1]) <= thr

    kb_first = lax.while_loop(lambda kb: jnp.logical_and(kb > 0, needed(jnp.maximum(kb - 1, 0))),
                              lambda kb: kb - 1, kb_last)
    n_open = kb_last - kb_first

    def qk(hh, kb, slot):
        ksl = pl.ds(pl.multiple_of(kb * tk, tk), tk)
        s_scr[slot][hh][...] = _dot(k_ref[0, hh, ksl, :], q_ref[0, hh])

    def soft_pv(hh, kb, slot, masked):
        s = s_scr[slot][hh][...]
        if masked:
            key = lax.broadcasted_iota(jnp.int32, (tk, tq), 0)
            qry = lax.broadcasted_iota(jnp.int32, (tk, tq), 1)
            s = jnp.where(key - qry <= delta, s, NEG_BIG)
        off = off_ref[base[hh] + kb] - off_d[hh]
        m = m_scr[hh]
        m_new = jnp.maximum(m, jnp.max(s, axis=0, keepdims=True) - off)
        p = jnp.exp2(s - (m_new + off)).astype(BF16)
        ksl = pl.ds(pl.multiple_of(kb * tk, tk), tk)
        acc_scr[hh] = jnp.exp2(m - m_new) * acc_scr[hh] + _dot(v_ref[0, hh, :, ksl], p)
        m_scr[hh] = m_new

    m_scr[...] = jnp.full(m_scr.shape, NEG_BIG, F32)
    acc_scr[...] = jnp.zeros(acc_scr.shape, F32)
    qk(0, kb_first, 0)
    qk(1, kb_first, 0)

    def pair(i, _):
        kb = kb_first + 2 * i
        qk(0, kb + 1, 1)
        qk(1, kb + 1, 1)
        soft_pv(0, kb, 0, False)
        soft_pv(1, kb, 0, False)
        qk(0, kb + 2, 0)
        qk(1, kb + 2, 0)
        soft_pv(0, kb + 1, 1, False)
        soft_pv(1, kb + 1, 1, False)
        return 0

    lax.fori_loop(0, n_open // 2, pair, 0)

    @pl.when(n_open % 2 == 1)
    def _():
        qk(0, kb_last, 1)
        qk(1, kb_last, 1)
        soft_pv(0, kb_last - 1, 0, False)
        soft_pv(1, kb_last - 1, 0, False)
        soft_pv(0, kb_last, 1, True)
        soft_pv(1, kb_last, 1, True)

    @pl.when(n_open % 2 == 0)
    def _():
        soft_pv(0, kb_last, 0, True)
        soft_pv(1, kb_last, 0, True)

    a0, a1 = acc_scr[0], acc_scr[1]
    o_t = jnp.concatenate([a0[:HEAD] / a0[HEAD:HEAD + 1], a1[:HEAD] / a1[HEAD:HEAD + 1]], axis=0)
    o_ref[0] = o_t.T.astype(BF16)


def _fox(q_t, k_aug, v_t, off, q_norm, k_norm, *, tq=256, tk=FOX_KBLOCK):
    bsz, nh, s, _ = k_aug.shape
    n_kblk = s // tk
    assert s % tq == 0 and s % tk == 0 and tk % tq == 0 and nh % 2 == 0 and off.shape == (bsz * nh * n_kblk,)
    qk_bound = 1.02 * LOG2E * HEAD ** 0.5 * jnp.max(jnp.abs(q_norm)) * jnp.max(jnp.abs(k_norm))
    thr = (2.0 * qk_bound + EXP2_UNDERFLOW).reshape(1).astype(F32)
    return pl.pallas_call(
        functools.partial(_fox_kernel, tq=tq, tk=tk, n_kblk=n_kblk),
        out_shape=jax.ShapeDtypeStruct((bsz, s, nh * HEAD), BF16),
        grid=(bsz, nh // 2, s // tq),
        in_specs=[
            pl.BlockSpec(memory_space=pltpu.SMEM),
            pl.BlockSpec(memory_space=pltpu.SMEM),
            pl.BlockSpec((1, 2, 2 * HEAD, tq), lambda b, hp, i: (b, hp, 0, i)),
            pl.BlockSpec((1, 2, s, 2 * HEAD), lambda b, hp, i: (b, hp, 0, 0)),
            pl.BlockSpec((1, 2, V_ROWS, s), lambda b, hp, i: (b, hp, 0, 0)),
        ],
        out_specs=pl.BlockSpec((1, tq, 2 * HEAD), lambda b, hp, i: (b, i, hp)),
        scratch_shapes=[pltpu.VMEM((tk, tq), F32)] * 4 + [pltpu.VMEM((2, 1, tq), F32), pltpu.VMEM((2, V_ROWS, tq), F32)],
        compiler_params=pltpu.CompilerParams(
            dimension_semantics=("parallel", "parallel", "arbitrary"), vmem_limit_bytes=VMEM_LIMIT),
        name="fox",
    )(off, thr, q_t, k_aug, v_t)


def _merge_kernel(x_ref, g_ref, wg_ref, ya_ref, yb_ref, pa_ref, pb_ref, wo_ref, o_ref):
    x = x_ref[...]
    d = x.shape[-1]
    h = _rms(x, g_ref[...]).astype(BF16)
    gates = _sigmoid(_dot(h, wg_ref[...]))
    ya = _dot(ya_ref[...], pa_ref[...])
    yb = _dot(yb_ref[...], pb_ref[...])
    mix = (gates[:, :d] * ya + gates[:, d:] * yb).astype(BF16)
    o_ref[...] = x + _dot(mix, wo_ref[...])


def _merge(x2d, mix_norm, w_gates, ya, yb, p_a, p_b, w_out, *, tm=512):
    m, d = x2d.shape
    assert m % tm == 0
    full = lambda t: pl.BlockSpec(t.shape, lambda i: (0, 0))
    tok = lambda n: pl.BlockSpec((tm, n), lambda i: (i, 0))
    consts = [mix_norm.reshape(1, d).astype(F32), w_gates.astype(BF16)]
    mats = [p_a.astype(BF16), p_b.astype(BF16), w_out.astype(BF16)]
    return pl.pallas_call(
        _merge_kernel,
        out_shape=jax.ShapeDtypeStruct((m, d), F32),
        grid=(m // tm,),
        in_specs=[tok(d), full(consts[0]), full(consts[1]), tok(D_MIX), tok(D_MIX)] + [full(t) for t in mats],
        out_specs=tok(d),
        compiler_params=pltpu.CompilerParams(dimension_semantics=("parallel",), vmem_limit_bytes=VMEM_LIMIT),
        name="merge",
    )(x2d, consts[0], consts[1], ya, yb, *mats)


def kernel(x, ffn1_norm, ffn1_w1, ffn1_w3, ffn1_w2, mix_norm, w_in, shift_mu, rwkv_w0, rwkv_w_up, rwkv_a0, rwkv_a_up, rwkv_g_up, rwkv_k_k, rwkv_k_a, rwkv_r_k, rwkv_lnx_g, rwkv_lnx_b, rwkv_proj, fox_f_bias, fox_q_norm, fox_k_norm, fox_proj, w_out, ffn2_norm, ffn2_w1, ffn2_w3, ffn2_w2, final_norm):
    bsz, s, d = x.shape
    depth = ffn1_norm.shape[0]
    n_gate0 = w_in.shape[-1] - 2 * d
    x2d = x.reshape(bsz * s, d)
    for l in range(depth):
        x2d = _ffn(x2d, ffn1_norm[l], ffn1_w1[l], ffn1_w3[l], ffn1_w2[l], final_norm, final_norm=False)
        (r, lw, k, v, kk, b, g, bonus, fq_t, fk_aug, fv_t, off) = _in_proj(
            x2d.reshape(bsz, s, d), mix_norm[l], w_in[l], shift_mu[l], rwkv_w0[l], rwkv_w_up[l], rwkv_a0[l],
            rwkv_a_up[l], rwkv_g_up[l], rwkv_k_k[l], rwkv_k_a[l], rwkv_r_k[l], fox_f_bias[l], fox_q_norm[l],
            fox_k_norm[l])
        ya = _rwkv(r, lw, k, v, kk, b, g, bonus, rwkv_lnx_g[l], rwkv_lnx_b[l])
        off = off[:, ::FOX_KBLOCK // (s // off.shape[1]), 0, :N_HEADS]
        off = jnp.transpose(off, (0, 2, 1)).reshape(-1)
        yb = _fox(fq_t, fk_aug, fv_t, off, fox_q_norm[l], fox_k_norm[l])
        x2d = _merge(x2d, mix_norm[l], w_in[l][:, n_gate0:], ya.reshape(bsz * s, D_MIX), yb.reshape(bsz * s, D_MIX),
                     rwkv_proj[l], fox_proj[l], w_out[l])
        x2d = _ffn(x2d, ffn2_norm[l], ffn2_w1[l], ffn2_w3[l], ffn2_w2[l], final_norm, final_norm=(l == depth - 1))
    return x2d.reshape(bsz, s, d)
```

```python
import functools

import jax
import jax.numpy as jnp
from jax import lax
from jax.experimental import pallas as pl
from jax.experimental.pallas import tpu as pltpu

F32 = jnp.float32
BF16 = jnp.bfloat16

HEAD = 64
N_HEADS = 8
D_MIX = HEAD * N_HEADS
LANE = 128
CHUNK = 64
RMS_EPS = 1e-6
LNX_EPS = 64e-5
NEG_BIG = -1e30
EXP2_UNDERFLOW = 160.0
VMEM_LIMIT = 56 * 1024 * 1024

_NT = (((1,), (1,)), ((), ()))
_TN = (((0,), (0,)), ((), ()))


def _dot(a, b, dims=None, precision=None):
    if dims is None:
        return jnp.dot(a, b, preferred_element_type=F32, precision=precision)
    return lax.dot_general(a, b, dims, preferred_element_type=F32, precision=precision)


def _bdot(a, b, dims=None):
    return _dot(a.astype(BF16), b.astype(BF16), dims)


def _rms(x, g):
    return x * lax.rsqrt(jnp.mean(x * x, axis=-1, keepdims=True) + RMS_EPS) * g


def _softplus(z):
    return jnp.maximum(z, 0.0) + jnp.log(1.0 + jnp.exp(-jnp.abs(z)))


def _sigmoid(z):
    return 1.0 / (1.0 + jnp.exp(-z))


def _head_sum(x, e):
    hi = x.astype(BF16)
    lo = (x - hi.astype(F32)).astype(BF16)
    return _dot(hi, e) + _dot(lo, e)


def _ffn_kernel(x_ref, g_ref, w1_ref, w3_ref, w2_ref, gf_ref, o_ref, h_scr, acc_scr, *, final_norm):
    f = pl.program_id(1)

    @pl.when(f == 0)
    def _():
        h_scr[...] = _rms(x_ref[...], g_ref[...]).astype(BF16)
        acc_scr[...] = jnp.zeros_like(acc_scr)

    h = h_scr[...]
    a = _dot(h, w1_ref[...])
    b = _dot(h, w3_ref[...])
    u = (a * _sigmoid(a) * b).astype(BF16)
    acc_scr[...] += _dot(u, w2_ref[...])

    @pl.when(f == pl.num_programs(1) - 1)
    def _():
        y = x_ref[...] + 0.5 * acc_scr[...]
        if final_norm:
            y = _rms(y, gf_ref[...])
        o_ref[...] = y


def _ffn(x2d, g, w1, w3, w2, gf, *, final_norm, tm=512, n_f=2):
    m, d = x2d.shape
    f_dim = w1.shape[1]
    tf = f_dim // n_f
    assert m % tm == 0 and tf * n_f == f_dim and tf % LANE == 0
    return pl.pallas_call(
        functools.partial(_ffn_kernel, final_norm=final_norm),
        out_shape=jax.ShapeDtypeStruct((m, d), F32),
        grid=(m // tm, n_f),
        in_specs=[
            pl.BlockSpec((tm, d), lambda i, f: (i, 0)),
            pl.BlockSpec((1, d), lambda i, f: (0, 0)),
            pl.BlockSpec((d, tf), lambda i, f: (0, f)),
            pl.BlockSpec((d, tf), lambda i, f: (0, f)),
            pl.BlockSpec((tf, d), lambda i, f: (f, 0)),
            pl.BlockSpec((1, d), lambda i, f: (0, 0)),
        ],
        out_specs=pl.BlockSpec((tm, d), lambda i, f: (i, 0)),
        scratch_shapes=[pltpu.VMEM((tm, d), BF16), pltpu.VMEM((tm, d), F32)],
        compiler_params=pltpu.CompilerParams(
            dimension_semantics=("parallel", "arbitrary"), vmem_limit_bytes=VMEM_LIMIT),
        name="ffn",
    )(x2d, g.reshape(1, d), w1.astype(BF16), w3.astype(BF16), w2.astype(BF16), gf.reshape(1, d))


N_SHIFT = 3 * D_MIX + 3 * LANE
N_PROJ = N_SHIFT + 3 * D_MIX + LANE
LOG2E = 1.4426950408889634
V_ROWS = HEAD + 16
FOX_KBLOCK = 512


def _in_proj_kernel(x_ref, g_ref, w_ref, mu_ref, w0_ref, wup_ref, a0_ref, aup_ref, gup_ref, kk_ref, ka_ref,
                    rk_ref, fb_ref, qn_ref, kn_ref, e_ref,
                    r_out, lw_out, k_out, v_out, kk_out, b_out, g_out, bonus_out, fq_out, fk_out, fv_out, off_out,
                    carry_p, carry_c, carry_b, *, tm, tiles_per_kblock):
    i = pl.program_id(1)

    @pl.when(i == 0)
    def _():
        carry_p[...] = jnp.zeros_like(carry_p)
        carry_c[...] = jnp.zeros_like(carry_c)
        carry_b[...] = jnp.zeros_like(carry_b)

    h = _rms(x_ref[0], g_ref[...]).astype(BF16)
    p = _dot(h, w_ref[...])

    ps = p[:, :N_SHIFT]
    row = lax.broadcasted_iota(jnp.int32, (tm, 1), 0)
    prev = jnp.where(row == 0, carry_p[...], pltpu.roll(ps, 1, axis=0))
    carry_p[...] = ps[tm - 1:tm, :]
    sh = ps + (prev - ps) * mu_ref[...]

    e = e_ref[...]
    r = sh[:, 0:D_MIX]
    k = sh[:, D_MIX:2 * D_MIX]
    v = sh[:, 2 * D_MIX:3 * D_MIX]
    w_lo = sh[:, 3 * D_MIX:3 * D_MIX + LANE]
    a_lo = sh[:, 3 * D_MIX + LANE:3 * D_MIX + 2 * LANE]
    g_lo = sh[:, 3 * D_MIX + 2 * LANE:N_SHIFT]

    w_log = -_softplus(-(w0_ref[...] + _bdot(jnp.tanh(w_lo), wup_ref[...]))) - 0.5
    lw = -jnp.exp(w_log)
    a = _sigmoid(a0_ref[...] + _bdot(a_lo, aup_ref[...]))
    g = _bdot(_sigmoid(g_lo), gup_ref[...])
    kk = k * kk_ref[...]
    kk = kk / jnp.maximum(jnp.sqrt(_head_sum(kk * kk, e)), 1e-12)
    k_mod = k * (1.0 + (a - 1.0) * ka_ref[...])
    bonus = _head_sum(r * k_mod * rk_ref[...], e) * v

    r_out[0] = r
    lw_out[0] = lw
    k_out[0] = k_mod
    v_out[0] = v
    kk_out[0] = kk
    b_out[0] = kk * a
    g_out[0] = g
    bonus_out[0] = bonus

    fq = p[:, N_SHIFT:N_SHIFT + D_MIX]
    fk = p[:, N_SHIFT + D_MIX:N_SHIFT + 2 * D_MIX]
    fv = p[:, N_SHIFT + 2 * D_MIX:N_SHIFT + 3 * D_MIX]
    f_lo = p[:, N_SHIFT + 3 * D_MIX:N_PROJ]
    inv_head = 1.0 / HEAD
    qn = fq * lax.rsqrt(_head_sum(fq * fq, e) * inv_head + RMS_EPS) * qn_ref[...] * (LOG2E * HEAD ** -0.5)
    kn = fk * lax.rsqrt(_head_sum(fk * fk, e) * inv_head + RMS_EPS) * kn_ref[...]
    z = f_lo + fb_ref[...]
    log_f = jnp.minimum(z, 0.0) - jnp.log(1.0 + jnp.exp(-jnp.abs(z)))
    tri = (lax.broadcasted_iota(jnp.int32, (tm, tm), 1) <= lax.broadcasted_iota(jnp.int32, (tm, tm), 0)).astype(F32)
    c_loc = _dot(tri, log_f, precision=lax.Precision.HIGHEST)
    off_out[0, 0] = carry_c[...] * LOG2E
    in_blk = jnp.where(i % tiles_per_kblock == 0, 0.0, carry_b[...])
    carry_c[...] = carry_c[...] + c_loc[tm - 1:tm, :]
    carry_b[...] = in_blk + c_loc[tm - 1:tm, :]
    c2 = (c_loc + in_blk) * LOG2E

    lane = lax.broadcasted_iota(jnp.int32, (tm, HEAD), 1)
    sub = lax.broadcasted_iota(jnp.int32, (HEAD, tm), 0)
    q_tail = jnp.where(sub < 3, -1.0, 0.0)
    v_tail = jnp.where(lax.broadcasted_iota(jnp.int32, (V_ROWS - HEAD, tm), 0) == 0, 1.0, 0.0)
    for pr in range(N_HEADS // 2):
        ps2 = slice(pr * 2 * HEAD, (pr + 1) * 2 * HEAD)
        q_t = qn[:, ps2].T
        v_t = fv[:, ps2].T
        for half in range(2):
            hh = 2 * pr + half
            rows = slice(half * HEAD, (half + 1) * HEAD)
            fq_out[0, hh] = jnp.concatenate([q_t[rows], q_tail], axis=0).astype(BF16)
            fv_out[0, hh] = jnp.concatenate([v_t[rows], v_tail], axis=0).astype(BF16)
            c_h = c2[:, hh:hh + 1]
            c_hi = c_h.astype(BF16).astype(F32)
            c_mid = (c_h - c_hi).astype(BF16).astype(F32)
            c_lo = c_h - c_hi - c_mid
            c_cols = jnp.where(lane == 0, c_hi, jnp.where(lane == 1, c_mid, jnp.where(lane == 2, c_lo, 0.0)))
            fk_out[0, hh] = jnp.concatenate([kn[:, hh * HEAD:(hh + 1) * HEAD], c_cols], axis=1).astype(BF16)


def _pad_cols(w, n):
    return jnp.pad(w, ((0, 0), (0, n - w.shape[1])))


def _pad_rows(w, n):
    return jnp.pad(w, ((0, n - w.shape[0]), (0, 0)))


def _in_proj(x, mix_norm, w_in, shift_mu, w0, w_up, a0, a_up, g_up, k_k, k_a, r_k, f_bias, q_norm, k_norm, *, tm=256):
    bsz, s, d = x.shape
    lw_, la_, lg_ = w_up.shape[0], a_up.shape[0], g_up.shape[0]
    c0 = 3 * D_MIX
    c1 = c0 + lw_ + la_ + lg_
    c2 = c1 + 3 * D_MIX
    c3 = c2 + N_HEADS

    def seg(t, pad):
        return jnp.concatenate([
            t[..., :c0],
            pad(t[..., c0:c0 + lw_], LANE), pad(t[..., c0 + lw_:c0 + lw_ + la_], LANE), pad(t[..., c0 + lw_ + la_:c1], LANE),
        ], axis=-1)

    w_cat = jnp.concatenate([seg(w_in, _pad_cols), w_in[:, c1:c2], _pad_cols(w_in[:, c2:c3], LANE)], axis=-1).astype(BF16)
    mu_cat = seg(shift_mu.reshape(1, -1), _pad_cols)
    row = lambda t: t.reshape(1, -1).astype(F32)
    e = _head_mask()
    tile = lambda t: jnp.tile(t.reshape(1, HEAD), (1, N_HEADS)).astype(F32)
    consts = [
        row(mix_norm), w_cat, mu_cat, row(w0), _pad_rows(w_up, LANE).astype(BF16), row(a0),
        _pad_rows(a_up, LANE).astype(BF16), _pad_rows(g_up, LANE).astype(BF16), row(k_k), row(k_a), row(r_k),
        _pad_cols(row(f_bias), LANE), tile(q_norm), tile(k_norm), e,
    ]
    const_specs = [pl.BlockSpec(c.shape, lambda b, i: (0, 0)) for c in consts]
    tok = lambda: pl.BlockSpec((1, tm, D_MIX), lambda b, i: (b, i, 0))
    f32_tok = jax.ShapeDtypeStruct((bsz, s, D_MIX), F32)
    assert s % tm == 0
    return pl.pallas_call(
        functools.partial(_in_proj_kernel, tm=tm, tiles_per_kblock=FOX_KBLOCK // tm),
        out_shape=[f32_tok] * 8 + [
            jax.ShapeDtypeStruct((bsz, N_HEADS, 2 * HEAD, s), BF16),
            jax.ShapeDtypeStruct((bsz, N_HEADS, s, 2 * HEAD), BF16),
            jax.ShapeDtypeStruct((bsz, N_HEADS, V_ROWS, s), BF16),
            jax.ShapeDtypeStruct((bsz, s // tm, 1, LANE), F32),
        ],
        grid=(bsz, s // tm),
        in_specs=[pl.BlockSpec((1, tm, d), lambda b, i: (b, i, 0))] + const_specs,
        out_specs=[tok() for _ in range(8)] + [
            pl.BlockSpec((1, N_HEADS, 2 * HEAD, tm), lambda b, i: (b, 0, 0, i)),
            pl.BlockSpec((1, N_HEADS, tm, 2 * HEAD), lambda b, i: (b, 0, i, 0)),
            pl.BlockSpec((1, N_HEADS, V_ROWS, tm), lambda b, i: (b, 0, 0, i)),
            pl.BlockSpec((1, 1, 1, LANE), lambda b, i: (b, i, 0, 0)),
        ],
        scratch_shapes=[pltpu.VMEM((1, N_SHIFT), F32), pltpu.VMEM((1, LANE), F32), pltpu.VMEM((1, LANE), F32)],
        compiler_params=pltpu.CompilerParams(
            dimension_semantics=("parallel", "arbitrary"), vmem_limit_bytes=VMEM_LIMIT),
        name="in_proj",
    )(x, *consts)


def _rwkv_kernel(r_ref, lw_ref, k_ref, v_ref, kk_ref, b_ref, g_ref, bonus_ref, lng_ref, lnb_ref, e_ref, y_ref, s_scr, *, tb):
    i = pl.program_id(1)

    @pl.when(i == 0)
    def _():
        s_scr[...] = jnp.zeros_like(s_scr)

    c = CHUNK
    pw = 2 * HEAD
    n_chunk = tb // c
    n_pair = N_HEADS // 2
    rr = lax.broadcasted_iota(jnp.int32, (c, pw), 0)
    ll = lax.broadcasted_iota(jnp.int32, (c, pw), 1)
    cc = ll % c
    lo = ll < c
    strict = cc < rr
    incl = cc <= rr
    eye = (cc == rr).astype(F32)
    r2 = lax.broadcasted_iota(jnp.int32, (2 * c, pw), 0)
    l2 = lax.broadcasted_iota(jnp.int32, (2 * c, pw), 1)
    lo2 = l2 < c
    diag2 = (r2 < c) == lo2
    zeros = jnp.zeros((c, pw), F32)

    def level_mask(n):
        return ((rr // (2 * n)) == (cc // (2 * n))) & (((rr // n) % 2) == 1) & (((cc // n) % 2) == 0)

    def bd(x):
        return jnp.where(diag2, jnp.concatenate([x, x], axis=0), 0.0).astype(BF16)

    def bd_sw(x):
        return jnp.where(diag2, 0.0, jnp.concatenate([x, x], axis=0)).astype(BF16)

    rb = lax.broadcasted_iota(jnp.int32, (tb, tb), 0)
    cb = lax.broadcasted_iota(jnp.int32, (tb, tb), 1)
    tri = ((cb <= rb) & ((rb // c) == (cb // c))).astype(F32)
    lw = lw_ref[0]
    cs = _dot(tri, lw, precision=lax.Precision.HIGHEST)
    w_inv = jnp.exp(-cs)
    rt = r_ref[0] * jnp.exp(cs)
    at = -kk_ref[0] * jnp.exp(cs - lw)
    bi = b_ref[0] * w_inv
    ki = k_ref[0] * w_inv
    v = v_ref[0]

    chains = [(j, p) for j in range(n_chunk) for p in range(n_pair)]
    cut = lambda x, j, p: x[j * c:(j + 1) * c, p * pw:(p + 1) * pw]
    w_tot = [jnp.exp(cs[(j + 1) * c - 1:(j + 1) * c, :]) for j in range(n_chunk)]
    a_t = [cut(at, j, p) for j, p in chains]
    r_t = [cut(rt, j, p) for j, p in chains]
    b_i = [cut(bi, j, p) for j, p in chains]
    k_i = [cut(ki, j, p) for j, p in chains]
    v_p = [cut(v, j, p) for j, p in chains]
    n = len(chains)

    a_ab, a_ak, a_rb, a_rk = [], [], [], []
    for q in range(n):
        ar = jnp.concatenate([a_t[q], r_t[q]], axis=0)
        gx = _dot(jnp.where(lo2, ar, 0.0).astype(BF16), jnp.concatenate([b_i[q], k_i[q]], axis=0).astype(BF16), _NT)
        gy = _dot(jnp.where(lo2, 0.0, ar).astype(BF16), jnp.concatenate([k_i[q], b_i[q]], axis=0).astype(BF16), _NT)
        a_ab.append(jnp.where(strict, jnp.where(lo, gx[:c], gy[:c]), 0.0))
        a_ak.append(jnp.where(strict, jnp.where(lo, gy[:c], gx[:c]), 0.0))
        a_rb.append(jnp.where(incl, jnp.where(lo, gx[c:], gy[c:]), 0.0))
        a_rk.append(jnp.where(incl, jnp.where(lo, gy[c:], gx[c:]), 0.0))

    t = [eye + jnp.where(level_mask(1), a_ab[q], 0.0) for q in range(n)]
    for m in (2, 4, 8, 16, 32):
        x = [_dot(jnp.where(level_mask(m), a_ab[q], 0.0).astype(BF16), bd(t[q])) for q in range(n)]
        t = [t[q] + _dot(t[q].astype(BF16), bd(x[q])) for q in range(n)]

    v_sw = [bd_sw(v_p[q]) for q in range(n)]
    akv = [_dot(a_ak[q].astype(BF16), v_sw[q]) for q in range(n)]
    x = [_dot(t[q].astype(BF16), jnp.concatenate([bd(a_t[q]), bd(akv[q])], axis=1)) for q in range(n)]
    a_p = [x[q][:, :pw] for q in range(n)]
    u_0 = [x[q][:, pw:] for q in range(n)]
    zero_bd = jnp.zeros((2 * c, pw), BF16)
    d1 = [_dot(jnp.concatenate([a_rb[q], a_rk[q]], axis=1).astype(BF16),
               jnp.concatenate([jnp.concatenate([bd(a_p[q]), bd(u_0[q])], axis=1),
                                jnp.concatenate([zero_bd, v_sw[q]], axis=1)], axis=0)) for q in range(n)]
    d2 = []
    for q, (j, p) in enumerate(chains):
        wt = w_tot[j][:, p * pw:(p + 1) * pw]
        lhs = jnp.concatenate([jnp.concatenate([a_p[q], u_0[q]], axis=1),
                               jnp.concatenate([zeros, v_p[q]], axis=1)], axis=0).astype(BF16)
        rhs = jnp.concatenate([b_i[q] * wt, k_i[q] * wt], axis=0).astype(BF16)
        d2.append(_dot(lhs, rhs, _TN))

    s = [s_scr[p] for p in range(n_pair)]
    ys = [None] * n
    for j in range(n_chunk):
        for p in range(n_pair):
            q = j * n_pair + p
            mt = jnp.where(diag2, d2[q][:pw], 0.0).astype(BF16)
            nt = jnp.where(lo, d2[q][pw:pw + c], d2[q][pw + c:])
            ys[q] = d1[q][:, pw:] + _dot((r_t[q] + d1[q][:, :pw]).astype(BF16), bd(s[p]), _NT)
            s[p] = s[p] * w_tot[j][:, p * pw:(p + 1) * pw] + _dot(s[p].astype(BF16), mt) + nt
    for p in range(n_pair):
        s_scr[p] = s[p]

    y = jnp.concatenate([jnp.concatenate([ys[j * n_pair + p] for p in range(n_pair)], axis=1) for j in range(n_chunk)], axis=0)
    e = e_ref[...]
    inv_head = 1.0 / HEAD
    yc = y - _head_sum(y, e) * inv_head
    yn = yc * lax.rsqrt(_head_sum(yc * yc, e) * inv_head + LNX_EPS)
    y_ref[0] = ((yn * lng_ref[...] + lnb_ref[...] + bonus_ref[0]) * g_ref[0]).astype(BF16)


def _head_mask():
    head_id = jnp.arange(D_MIX) // HEAD
    return (head_id[:, None] == head_id[None, :]).astype(BF16)


def _rwkv(r, lw, k, v, kk, b, g, bonus, lnx_g, lnx_b, *, tb=256):
    bsz, s, _ = r.shape
    assert s % tb == 0 and tb % CHUNK == 0
    tok = pl.BlockSpec((1, tb, D_MIX), lambda bb, i: (bb, i, 0))
    vec = pl.BlockSpec((1, D_MIX), lambda bb, i: (0, 0))
    return pl.pallas_call(
        functools.partial(_rwkv_kernel, tb=tb),
        out_shape=jax.ShapeDtypeStruct((bsz, s, D_MIX), BF16),
        grid=(bsz, s // tb),
        in_specs=[tok] * 8 + [vec, vec, pl.BlockSpec((D_MIX, D_MIX), lambda bb, i: (0, 0))],
        out_specs=tok,
        scratch_shapes=[pltpu.VMEM((N_HEADS // 2, HEAD, 2 * HEAD), F32)],
        compiler_params=pltpu.CompilerParams(
            dimension_semantics=("parallel", "arbitrary"), vmem_limit_bytes=VMEM_LIMIT),
        name="rwkv",
    )(r, lw, k, v, kk, b, g, bonus, lnx_g.reshape(1, -1).astype(F32), lnx_b.reshape(1, -1).astype(F32), _head_mask())


def _fox_kernel(off_ref, thr_ref, q_ref, k_ref, v_ref, o_ref, s00, s01, s10, s11, m_scr, acc_scr, *, tq, tk, n_kblk):
    b, hp, qb = pl.program_id(0), pl.program_id(1), pl.program_id(2)
    kb_last = (qb * tq) // tk
    delta = qb * tq - kb_last * tk
    s_scr = ((s00, s01), (s10, s11))
    base = [(b * N_HEADS + 2 * hp + hh) * n_kblk for hh in range(2)]
    off_d = [off_ref[base[hh] + kb_last] for hh in range(2)]

    thr = thr_ref[0]

    def needed(kb):
        return jnp.minimum(off_ref[base[0] + kb + 1] - off_d[0], off_ref[base[1] + kb + 1] - off_d[1]) <= thr

    kb_first = lax.while_loop(lambda kb: jnp.logical_and(kb > 0, needed(jnp.maximum(kb - 1, 0))),
                              lambda kb: kb - 1, kb_last)
    n_open = kb_last - kb_first

    def qk(hh, kb, slot):
        ksl = pl.ds(pl.multiple_of(kb * tk, tk), tk)
        s_scr[slot][hh][...] = _dot(k_ref[0, hh, ksl, :], q_ref[0, hh])

    def soft_pv(hh, kb, slot, masked):
        s = s_scr[slot][hh][...]
        if masked:
            key = lax.broadcasted_iota(jnp.int32, (tk, tq), 0)
            qry = lax.broadcasted_iota(jnp.int32, (tk, tq), 1)
            s = jnp.where(key - qry <= delta, s, NEG_BIG)
        off = off_ref[base[hh] + kb] - off_d[hh]
        m = m_scr[hh]
        m_new = jnp.maximum(m, jnp.max(s, axis=0, keepdims=True) - off)
        p = jnp.exp2(s - (m_new + off)).astype(BF16)
        ksl = pl.ds(pl.multiple_of(kb * tk, tk), tk)
        acc_scr[hh] = jnp.exp2(m - m_new) * acc_scr[hh] + _dot(v_ref[0, hh, :, ksl], p)
        m_scr[hh] = m_new

    m_scr[...] = jnp.full(m_scr.shape, NEG_BIG, F32)
    acc_scr[...] = jnp.zeros(acc_scr.shape, F32)
    qk(0, kb_first, 0)
    qk(1, kb_first, 0)

    def pair(i, _):
        kb = kb_first + 2 * i
        qk(0, kb + 1, 1)
        qk(1, kb + 1, 1)
        soft_pv(0, kb, 0, False)
        soft_pv(1, kb, 0, False)
        qk(0, kb + 2, 0)
        qk(1, kb + 2, 0)
        soft_pv(0, kb + 1, 1, False)
        soft_pv(1, kb + 1, 1, False)
        return 0

    lax.fori_loop(0, n_open // 2, pair, 0)

    @pl.when(n_open % 2 == 1)
    def _():
        qk(0, kb_last, 1)
        qk(1, kb_last, 1)
        soft_pv(0, kb_last - 1, 0, False)
        soft_pv(1, kb_last - 1, 0, False)
        soft_pv(0, kb_last, 1, True)
        soft_pv(1, kb_last, 1, True)

    @pl.when(n_open % 2 == 0)
    def _():
        soft_pv(0, kb_last, 0, True)
        soft_pv(1, kb_last, 0, True)

    a0, a1 = acc_scr[0], acc_scr[1]
    o_t = jnp.concatenate([a0[:HEAD] / a0[HEAD:HEAD + 1], a1[:HEAD] / a1[HEAD:HEAD + 1]], axis=0)
    o_ref[0] = o_t.T.astype(BF16)


def _fox(q_t, k_aug, v_t, off, q_norm, k_norm, *, tq=512, tk=FOX_KBLOCK):
    bsz, nh, s, _ = k_aug.shape
    n_kblk = s // tk
    assert s % tq == 0 and s % tk == 0 and tk % tq == 0 and nh % 2 == 0 and off.shape == (bsz * nh * n_kblk,)
    qk_bound = 1.02 * LOG2E * HEAD ** 0.5 * jnp.max(jnp.abs(q_norm)) * jnp.max(jnp.abs(k_norm))
    thr = (2.0 * qk_bound + EXP2_UNDERFLOW).reshape(1).astype(F32)
    return pl.pallas_call(
        functools.partial(_fox_kernel, tq=tq, tk=tk, n_kblk=n_kblk),
        out_shape=jax.ShapeDtypeStruct((bsz, s, nh * HEAD), BF16),
        grid=(bsz, nh // 2, s // tq),
        in_specs=[
            pl.BlockSpec(memory_space=pltpu.SMEM),
            pl.BlockSpec(memory_space=pltpu.SMEM),
            pl.BlockSpec((1, 2, 2 * HEAD, tq), lambda b, hp, i: (b, hp, 0, i)),
            pl.BlockSpec((1, 2, s, 2 * HEAD), lambda b, hp, i: (b, hp, 0, 0)),
            pl.BlockSpec((1, 2, V_ROWS, s), lambda b, hp, i: (b, hp, 0, 0)),
        ],
        out_specs=pl.BlockSpec((1, tq, 2 * HEAD), lambda b, hp, i: (b, i, hp)),
        scratch_shapes=[pltpu.VMEM((tk, tq), F32)] * 4 + [pltpu.VMEM((2, 1, tq), F32), pltpu.VMEM((2, V_ROWS, tq), F32)],
        compiler_params=pltpu.CompilerParams(
            dimension_semantics=("parallel", "parallel", "arbitrary"), vmem_limit_bytes=VMEM_LIMIT),
        name="fox",
    )(off, thr, q_t, k_aug, v_t)


def _merge_kernel(x_ref, g_ref, wg_ref, ya_ref, yb_ref, pa_ref, pb_ref, wo_ref, o_ref):
    x = x_ref[...]
    d = x.shape[-1]
    h = _rms(x, g_ref[...]).astype(BF16)
    gates = _sigmoid(_dot(h, wg_ref[...]))
    ya = _dot(ya_ref[...], pa_ref[...])
    yb = _dot(yb_ref[...], pb_ref[...])
    mix = (gates[:, :d] * ya + gates[:, d:] * yb).astype(BF16)
    o_ref[...] = x + _dot(mix, wo_ref[...])


def _merge(x2d, mix_norm, w_gates, ya, yb, p_a, p_b, w_out, *, tm=512):
    m, d = x2d.shape
    assert m % tm == 0
    full = lambda t: pl.BlockSpec(t.shape, lambda i: (0, 0))
    tok = lambda n: pl.BlockSpec((tm, n), lambda i: (i, 0))
    consts = [mix_norm.reshape(1, d).astype(F32), w_gates.astype(BF16)]
    mats = [p_a.astype(BF16), p_b.astype(BF16), w_out.astype(BF16)]
    return pl.pallas_call(
        _merge_kernel,
        out_shape=jax.ShapeDtypeStruct((m, d), F32),
        grid=(m // tm,),
        in_specs=[tok(d), full(consts[0]), full(consts[1]), tok(D_MIX), tok(D_MIX)] + [full(t) for t in mats],
        out_specs=tok(d),
        compiler_params=pltpu.CompilerParams(dimension_semantics=("parallel",), vmem_limit_bytes=VMEM_LIMIT),
        name="merge",
    )(x2d, consts[0], consts[1], ya, yb, *mats)


def kernel(x, ffn1_norm, ffn1_w1, ffn1_w3, ffn1_w2, mix_norm, w_in, shift_mu, rwkv_w0, rwkv_w_up, rwkv_a0, rwkv_a_up, rwkv_g_up, rwkv_k_k, rwkv_k_a, rwkv_r_k, rwkv_lnx_g, rwkv_lnx_b, rwkv_proj, fox_f_bias, fox_q_norm, fox_k_norm, fox_proj, w_out, ffn2_norm, ffn2_w1, ffn2_w3, ffn2_w2, final_norm):
    bsz, s, d = x.shape
    depth = ffn1_norm.shape[0]
    n_gate0 = w_in.shape[-1] - 2 * d
    x2d = x.reshape(bsz * s, d)
    for l in range(depth):
        x2d = _ffn(x2d, ffn1_norm[l], ffn1_w1[l], ffn1_w3[l], ffn1_w2[l], final_norm, final_norm=False)
        (r, lw, k, v, kk, b, g, bonus, fq_t, fk_aug, fv_t, off) = _in_proj(
            x2d.reshape(bsz, s, d), mix_norm[l], w_in[l], shift_mu[l], rwkv_w0[l], rwkv_w_up[l], rwkv_a0[l],
            rwkv_a_up[l], rwkv_g_up[l], rwkv_k_k[l], rwkv_k_a[l], rwkv_r_k[l], fox_f_bias[l], fox_q_norm[l],
            fox_k_norm[l])
        ya = _rwkv(r, lw, k, v, kk, b, g, bonus, rwkv_lnx_g[l], rwkv_lnx_b[l])
        off = off[:, ::FOX_KBLOCK // (s // off.shape[1]), 0, :N_HEADS]
        off = jnp.transpose(off, (0, 2, 1)).reshape(-1)
        yb = _fox(fq_t, fk_aug, fv_t, off, fox_q_norm[l], fox_k_norm[l])
        x2d = _merge(x2d, mix_norm[l], w_in[l][:, n_gate0:], ya.reshape(bsz * s, D_MIX), yb.reshape(bsz * s, D_MIX),
                     rwkv_proj[l], fox_proj[l], w_out[l])
        x2d = _ffn(x2d, ffn2_norm[l], ffn2_w1[l], ffn2_w3[l], ffn2_w2[l], final_norm, final_norm=(l == depth - 1))
    return x2d.reshape(bsz, s, d)
```

```python
import functools

import jax
import jax.numpy as jnp
from jax import lax
from jax.experimental import pallas as pl
from jax.experimental.pallas import tpu as pltpu

F32 = jnp.float32
BF16 = jnp.bfloat16

HEAD = 64
N_HEADS = 8
D_MIX = HEAD * N_HEADS
LANE = 128
CHUNK = 64
RMS_EPS = 1e-6
LNX_EPS = 64e-5
NEG_BIG = -1e30
EXP2_UNDERFLOW = 160.0
VMEM_LIMIT = 56 * 1024 * 1024

_NT = (((1,), (1,)), ((), ()))
_TN = (((0,), (0,)), ((), ()))


def _dot(a, b, dims=None, precision=None):
    if dims is None:
        return jnp.dot(a, b, preferred_element_type=F32, precision=precision)
    return lax.dot_general(a, b, dims, preferred_element_type=F32, precision=precision)


def _bdot(a, b, dims=None):
    return _dot(a.astype(BF16), b.astype(BF16), dims)


def _rms(x, g):
    return x * lax.rsqrt(jnp.mean(x * x, axis=-1, keepdims=True) + RMS_EPS) * g


def _softplus(z):
    return jnp.maximum(z, 0.0) + jnp.log(1.0 + jnp.exp(-jnp.abs(z)))


def _sigmoid(z):
    return 1.0 / (1.0 + jnp.exp(-z))


def _head_sum(x, e):
    hi = x.astype(BF16)
    lo = (x - hi.astype(F32)).astype(BF16)
    return _dot(hi, e) + _dot(lo, e)


def _ffn_kernel(x_ref, g_ref, w1_ref, w3_ref, w2_ref, gf_ref, o_ref, *, final_norm):
    x = x_ref[...]
    h = _rms(x, g_ref[...]).astype(BF16)
    a = _dot(h, w1_ref[...])
    b = _dot(h, w3_ref[...])
    u = (a * _sigmoid(a) * b).astype(BF16)
    y = x + 0.5 * _dot(u, w2_ref[...])
    if final_norm:
        y = _rms(y, gf_ref[...])
    o_ref[...] = y


def _resident(shape):
    return pl.BlockSpec(shape, lambda *_: (0,) * len(shape), pipeline_mode=pl.Buffered(1))


def _ffn(x2d, g, w1, w3, w2, gf, *, final_norm, tm=512):
    m, d = x2d.shape
    f_dim = w1.shape[1]
    assert m % tm == 0
    return pl.pallas_call(
        functools.partial(_ffn_kernel, final_norm=final_norm),
        out_shape=jax.ShapeDtypeStruct((m, d), F32),
        grid=(m // tm,),
        in_specs=[
            pl.BlockSpec((tm, d), lambda i: (i, 0)),
            _resident((1, d)), _resident((d, f_dim)), _resident((d, f_dim)), _resident((f_dim, d)), _resident((1, d)),
        ],
        out_specs=pl.BlockSpec((tm, d), lambda i: (i, 0)),
        compiler_params=pltpu.CompilerParams(dimension_semantics=("parallel",), vmem_limit_bytes=VMEM_LIMIT),
        name="ffn",
    )(x2d, g.reshape(1, d), w1.astype(BF16), w3.astype(BF16), w2.astype(BF16), gf.reshape(1, d))


N_SHIFT = 3 * D_MIX + 3 * LANE
N_PROJ = N_SHIFT + 3 * D_MIX + LANE
LOG2E = 1.4426950408889634
V_ROWS = HEAD + 16
FOX_KBLOCK = 512


def _in_proj_kernel(x_ref, g_ref, w_ref, mu_ref, w0_ref, wup_ref, a0_ref, aup_ref, gup_ref, kk_ref, ka_ref,
                    rk_ref, fb_ref, qn_ref, kn_ref, e_ref,
                    r_out, lw_out, k_out, v_out, kk_out, b_out, g_out, bonus_out, fq_out, fk_out, fv_out, off_out,
                    carry_p, carry_c, carry_b, *, tm, tiles_per_kblock):
    i = pl.program_id(1)

    @pl.when(i == 0)
    def _():
        carry_p[...] = jnp.zeros_like(carry_p)
        carry_c[...] = jnp.zeros_like(carry_c)
        carry_b[...] = jnp.zeros_like(carry_b)

    h = _rms(x_ref[0], g_ref[...]).astype(BF16)
    p = _dot(h, w_ref[...])

    ps = p[:, :N_SHIFT]
    row = lax.broadcasted_iota(jnp.int32, (tm, 1), 0)
    prev = jnp.where(row == 0, carry_p[...], pltpu.roll(ps, 1, axis=0))
    carry_p[...] = ps[tm - 1:tm, :]
    sh = ps + (prev - ps) * mu_ref[...]

    e = e_ref[...]
    r = sh[:, 0:D_MIX]
    k = sh[:, D_MIX:2 * D_MIX]
    v = sh[:, 2 * D_MIX:3 * D_MIX]
    w_lo = sh[:, 3 * D_MIX:3 * D_MIX + LANE]
    a_lo = sh[:, 3 * D_MIX + LANE:3 * D_MIX + 2 * LANE]
    g_lo = sh[:, 3 * D_MIX + 2 * LANE:N_SHIFT]

    w_log = -_softplus(-(w0_ref[...] + _bdot(jnp.tanh(w_lo), wup_ref[...]))) - 0.5
    lw = -jnp.exp(w_log)
    a = _sigmoid(a0_ref[...] + _bdot(a_lo, aup_ref[...]))
    g = _bdot(_sigmoid(g_lo), gup_ref[...])
    kk = k * kk_ref[...]
    kk = kk / jnp.maximum(jnp.sqrt(_head_sum(kk * kk, e)), 1e-12)
    k_mod = k * (1.0 + (a - 1.0) * ka_ref[...])
    bonus = _head_sum(r * k_mod * rk_ref[...], e) * v

    r_out[0] = r
    lw_out[0] = lw
    k_out[0] = k_mod
    v_out[0] = v
    kk_out[0] = kk
    b_out[0] = kk * a
    g_out[0] = g
    bonus_out[0] = bonus

    fq = p[:, N_SHIFT:N_SHIFT + D_MIX]
    fk = p[:, N_SHIFT + D_MIX:N_SHIFT + 2 * D_MIX]
    fv = p[:, N_SHIFT + 2 * D_MIX:N_SHIFT + 3 * D_MIX]
    f_lo = p[:, N_SHIFT + 3 * D_MIX:N_PROJ]
    inv_head = 1.0 / HEAD
    qn = fq * lax.rsqrt(_head_sum(fq * fq, e) * inv_head + RMS_EPS) * qn_ref[...] * (LOG2E * HEAD ** -0.5)
    kn = fk * lax.rsqrt(_head_sum(fk * fk, e) * inv_head + RMS_EPS) * kn_ref[...]
    z = f_lo + fb_ref[...]
    log_f = jnp.minimum(z, 0.0) - jnp.log(1.0 + jnp.exp(-jnp.abs(z)))
    tri = (lax.broadcasted_iota(jnp.int32, (tm, tm), 1) <= lax.broadcasted_iota(jnp.int32, (tm, tm), 0)).astype(F32)
    c_loc = _dot(tri, log_f, precision=lax.Precision.HIGHEST)
    off_out[0, 0] = carry_c[...] * LOG2E
    in_blk = jnp.where(i % tiles_per_kblock == 0, 0.0, carry_b[...])
    carry_c[...] = carry_c[...] + c_loc[tm - 1:tm, :]
    carry_b[...] = in_blk + c_loc[tm - 1:tm, :]
    c2 = (c_loc + in_blk) * LOG2E

    lane = lax.broadcasted_iota(jnp.int32, (tm, HEAD), 1)
    sub = lax.broadcasted_iota(jnp.int32, (HEAD, tm), 0)
    q_tail = jnp.where(sub < 3, -1.0, 0.0)
    v_tail = jnp.where(lax.broadcasted_iota(jnp.int32, (V_ROWS - HEAD, tm), 0) == 0, 1.0, 0.0)
    for pr in range(N_HEADS // 2):
        ps2 = slice(pr * 2 * HEAD, (pr + 1) * 2 * HEAD)
        q_t = qn[:, ps2].T
        v_t = fv[:, ps2].T
        for half in range(2):
            hh = 2 * pr + half
            rows = slice(half * HEAD, (half + 1) * HEAD)
            fq_out[0, hh] = jnp.concatenate([q_t[rows], q_tail], axis=0).astype(BF16)
            fv_out[0, hh] = jnp.concatenate([v_t[rows], v_tail], axis=0).astype(BF16)
            c_h = c2[:, hh:hh + 1]
            c_hi = c_h.astype(BF16).astype(F32)
            c_mid = (c_h - c_hi).astype(BF16).astype(F32)
            c_lo = c_h - c_hi - c_mid
            c_cols = jnp.where(lane == 0, c_hi, jnp.where(lane == 1, c_mid, jnp.where(lane == 2, c_lo, 0.0)))
            fk_out[0, hh] = jnp.concatenate([kn[:, hh * HEAD:(hh + 1) * HEAD], c_cols], axis=1).astype(BF16)


def _pad_cols(w, n):
    return jnp.pad(w, ((0, 0), (0, n - w.shape[1])))


def _pad_rows(w, n):
    return jnp.pad(w, ((0, n - w.shape[0]), (0, 0)))


def _in_proj(x, mix_norm, w_in, shift_mu, w0, w_up, a0, a_up, g_up, k_k, k_a, r_k, f_bias, q_norm, k_norm, *, tm=256):
    bsz, s, d = x.shape
    lw_, la_, lg_ = w_up.shape[0], a_up.shape[0], g_up.shape[0]
    c0 = 3 * D_MIX
    c1 = c0 + lw_ + la_ + lg_
    c2 = c1 + 3 * D_MIX
    c3 = c2 + N_HEADS

    def seg(t, pad):
        return jnp.concatenate([
            t[..., :c0],
            pad(t[..., c0:c0 + lw_], LANE), pad(t[..., c0 + lw_:c0 + lw_ + la_], LANE), pad(t[..., c0 + lw_ + la_:c1], LANE),
        ], axis=-1)

    w_cat = jnp.concatenate([seg(w_in, _pad_cols), w_in[:, c1:c2], _pad_cols(w_in[:, c2:c3], LANE)], axis=-1).astype(BF16)
    mu_cat = seg(shift_mu.reshape(1, -1), _pad_cols)
    row = lambda t: t.reshape(1, -1).astype(F32)
    e = _head_mask()
    tile = lambda t: jnp.tile(t.reshape(1, HEAD), (1, N_HEADS)).astype(F32)
    consts = [
        row(mix_norm), w_cat, mu_cat, row(w0), _pad_rows(w_up, LANE).astype(BF16), row(a0),
        _pad_rows(a_up, LANE).astype(BF16), _pad_rows(g_up, LANE).astype(BF16), row(k_k), row(k_a), row(r_k),
        _pad_cols(row(f_bias), LANE), tile(q_norm), tile(k_norm), e,
    ]
    const_specs = [pl.BlockSpec(c.shape, lambda b, i: (0, 0)) for c in consts]
    tok = lambda: pl.BlockSpec((1, tm, D_MIX), lambda b, i: (b, i, 0))
    f32_tok = jax.ShapeDtypeStruct((bsz, s, D_MIX), F32)
    assert s % tm == 0
    return pl.pallas_call(
        functools.partial(_in_proj_kernel, tm=tm, tiles_per_kblock=FOX_KBLOCK // tm),
        out_shape=[f32_tok] * 8 + [
            jax.ShapeDtypeStruct((bsz, N_HEADS, 2 * HEAD, s), BF16),
            jax.ShapeDtypeStruct((bsz, N_HEADS, s, 2 * HEAD), BF16),
            jax.ShapeDtypeStruct((bsz, N_HEADS, V_ROWS, s), BF16),
            jax.ShapeDtypeStruct((bsz, s // tm, 1, LANE), F32),
        ],
        grid=(bsz, s // tm),
        in_specs=[pl.BlockSpec((1, tm, d), lambda b, i: (b, i, 0))] + const_specs,
        out_specs=[tok() for _ in range(8)] + [
            pl.BlockSpec((1, N_HEADS, 2 * HEAD, tm), lambda b, i: (b, 0, 0, i)),
            pl.BlockSpec((1, N_HEADS, tm, 2 * HEAD), lambda b, i: (b, 0, i, 0)),
            pl.BlockSpec((1, N_HEADS, V_ROWS, tm), lambda b, i: (b, 0, 0, i)),
            pl.BlockSpec((1, 1, 1, LANE), lambda b, i: (b, i, 0, 0)),
        ],
        scratch_shapes=[pltpu.VMEM((1, N_SHIFT), F32), pltpu.VMEM((1, LANE), F32), pltpu.VMEM((1, LANE), F32)],
        compiler_params=pltpu.CompilerParams(
            dimension_semantics=("parallel", "arbitrary"), vmem_limit_bytes=VMEM_LIMIT),
        name="in_proj",
    )(x, *consts)


def _rwkv_kernel(r_ref, lw_ref, k_ref, v_ref, kk_ref, b_ref, g_ref, bonus_ref, lng_ref, lnb_ref, e_ref, y_ref, s_scr, *, tb):
    i = pl.program_id(1)

    @pl.when(i == 0)
    def _():
        s_scr[...] = jnp.zeros_like(s_scr)

    c = CHUNK
    pw = 2 * HEAD
    n_chunk = tb // c
    n_pair = N_HEADS // 2
    rr = lax.broadcasted_iota(jnp.int32, (c, pw), 0)
    ll = lax.broadcasted_iota(jnp.int32, (c, pw), 1)
    cc = ll % c
    lo = ll < c
    strict = cc < rr
    incl = cc <= rr
    eye = (cc == rr).astype(F32)
    r2 = lax.broadcasted_iota(jnp.int32, (2 * c, pw), 0)
    l2 = lax.broadcasted_iota(jnp.int32, (2 * c, pw), 1)
    lo2 = l2 < c
    diag2 = (r2 < c) == lo2
    zeros = jnp.zeros((c, pw), F32)

    def level_mask(n):
        return ((rr // (2 * n)) == (cc // (2 * n))) & (((rr // n) % 2) == 1) & (((cc // n) % 2) == 0)

    def bd(x):
        return jnp.where(diag2, jnp.concatenate([x, x], axis=0), 0.0).astype(BF16)

    def bd_sw(x):
        return jnp.where(diag2, 0.0, jnp.concatenate([x, x], axis=0)).astype(BF16)

    rb = lax.broadcasted_iota(jnp.int32, (tb, tb), 0)
    cb = lax.broadcasted_iota(jnp.int32, (tb, tb), 1)
    tri = ((cb <= rb) & ((rb // c) == (cb // c))).astype(F32)
    lw = lw_ref[0]
    cs = _dot(tri, lw, precision=lax.Precision.HIGHEST)
    w_inv = jnp.exp(-cs)
    rt = r_ref[0] * jnp.exp(cs)
    at = -kk_ref[0] * jnp.exp(cs - lw)
    bi = b_ref[0] * w_inv
    ki = k_ref[0] * w_inv
    v = v_ref[0]

    chains = [(j, p) for j in range(n_chunk) for p in range(n_pair)]
    cut = lambda x, j, p: x[j * c:(j + 1) * c, p * pw:(p + 1) * pw]
    w_tot = [jnp.exp(cs[(j + 1) * c - 1:(j + 1) * c, :]) for j in range(n_chunk)]
    a_t = [cut(at, j, p) for j, p in chains]
    r_t = [cut(rt, j, p) for j, p in chains]
    b_i = [cut(bi, j, p) for j, p in chains]
    k_i = [cut(ki, j, p) for j, p in chains]
    v_p = [cut(v, j, p) for j, p in chains]
    n = len(chains)

    a_ab, a_ak, a_rb, a_rk = [], [], [], []
    for q in range(n):
        ar = jnp.concatenate([a_t[q], r_t[q]], axis=0)
        gx = _dot(jnp.where(lo2, ar, 0.0).astype(BF16), jnp.concatenate([b_i[q], k_i[q]], axis=0).astype(BF16), _NT)
        gy = _dot(jnp.where(lo2, 0.0, ar).astype(BF16), jnp.concatenate([k_i[q], b_i[q]], axis=0).astype(BF16), _NT)
        a_ab.append(jnp.where(strict, jnp.where(lo, gx[:c], gy[:c]), 0.0))
        a_ak.append(jnp.where(strict, jnp.where(lo, gy[:c], gx[:c]), 0.0))
        a_rb.append(jnp.where(incl, jnp.where(lo, gx[c:], gy[c:]), 0.0))
        a_rk.append(jnp.where(incl, jnp.where(lo, gy[c:], gx[c:]), 0.0))

    t = [eye + jnp.where(level_mask(1), a_ab[q], 0.0) for q in range(n)]
    for m in (2, 4, 8, 16, 32):
        x = [_dot(jnp.where(level_mask(m), a_ab[q], 0.0).astype(BF16), bd(t[q])) for q in range(n)]
        t = [t[q] + _dot(t[q].astype(BF16), bd(x[q])) for q in range(n)]

    v_sw = [bd_sw(v_p[q]) for q in range(n)]
    akv = [_dot(a_ak[q].astype(BF16), v_sw[q]) for q in range(n)]
    x = [_dot(t[q].astype(BF16), jnp.concatenate([bd(a_t[q]), bd(akv[q])], axis=1)) for q in range(n)]
    a_p = [x[q][:, :pw] for q in range(n)]
    u_0 = [x[q][:, pw:] for q in range(n)]
    zero_bd = jnp.zeros((2 * c, pw), BF16)
    d1 = [_dot(jnp.concatenate([a_rb[q], a_rk[q]], axis=1).astype(BF16),
               jnp.concatenate([jnp.concatenate([bd(a_p[q]), bd(u_0[q])], axis=1),
                                jnp.concatenate([zero_bd, v_sw[q]], axis=1)], axis=0)) for q in range(n)]
    d2 = []
    for q, (j, p) in enumerate(chains):
        wt = w_tot[j][:, p * pw:(p + 1) * pw]
        lhs = jnp.concatenate([jnp.concatenate([a_p[q], u_0[q]], axis=1),
                               jnp.concatenate([zeros, v_p[q]], axis=1)], axis=0).astype(BF16)
        rhs = jnp.concatenate([b_i[q] * wt, k_i[q] * wt], axis=0).astype(BF16)
        d2.append(_dot(lhs, rhs, _TN))

    s = [s_scr[p] for p in range(n_pair)]
    ys = [None] * n
    for j in range(n_chunk):
        for p in range(n_pair):
            q = j * n_pair + p
            mt = jnp.where(diag2, d2[q][:pw], 0.0).astype(BF16)
            nt = jnp.where(lo, d2[q][pw:pw + c], d2[q][pw + c:])
            ys[q] = d1[q][:, pw:] + _dot((r_t[q] + d1[q][:, :pw]).astype(BF16), bd(s[p]), _NT)
            s[p] = s[p] * w_tot[j][:, p * pw:(p + 1) * pw] + _dot(s[p].astype(BF16), mt) + nt
    for p in range(n_pair):
        s_scr[p] = s[p]

    y = jnp.concatenate([jnp.concatenate([ys[j * n_pair + p] for p in range(n_pair)], axis=1) for j in range(n_chunk)], axis=0)
    e = e_ref[...]
    inv_head = 1.0 / HEAD
    yc = y - _head_sum(y, e) * inv_head
    yn = yc * lax.rsqrt(_head_sum(yc * yc, e) * inv_head + LNX_EPS)
    y_ref[0] = ((yn * lng_ref[...] + lnb_ref[...] + bonus_ref[0]) * g_ref[0]).astype(BF16)


def _head_mask():
    head_id = jnp.arange(D_MIX) // HEAD
    return (head_id[:, None] == head_id[None, :]).astype(BF16)


def _rwkv(r, lw, k, v, kk, b, g, bonus, lnx_g, lnx_b, *, tb=256):
    bsz, s, _ = r.shape
    assert s % tb == 0 and tb % CHUNK == 0
    tok = pl.BlockSpec((1, tb, D_MIX), lambda bb, i: (bb, i, 0))
    vec = pl.BlockSpec((1, D_MIX), lambda bb, i: (0, 0))
    return pl.pallas_call(
        functools.partial(_rwkv_kernel, tb=tb),
        out_shape=jax.ShapeDtypeStruct((bsz, s, D_MIX), BF16),
        grid=(bsz, s // tb),
        in_specs=[tok] * 8 + [vec, vec, pl.BlockSpec((D_MIX, D_MIX), lambda bb, i: (0, 0))],
        out_specs=tok,
        scratch_shapes=[pltpu.VMEM((N_HEADS // 2, HEAD, 2 * HEAD), F32)],
        compiler_params=pltpu.CompilerParams(
            dimension_semantics=("parallel", "arbitrary"), vmem_limit_bytes=VMEM_LIMIT),
        name="rwkv",
    )(r, lw, k, v, kk, b, g, bonus, lnx_g.reshape(1, -1).astype(F32), lnx_b.reshape(1, -1).astype(F32), _head_mask())


def _fox_kernel(off_ref, thr_ref, q_ref, k_ref, v_ref, o_ref, s00, s01, s10, s11, m_scr, acc_scr, *, tq, tk, n_kblk):
    b, hp, qb = pl.program_id(0), pl.program_id(1), pl.program_id(2)
    kb_last = (qb * tq) // tk
    delta = qb * tq - kb_last * tk
    s_scr = ((s00, s01), (s10, s11))
    base = [(b * N_HEADS + 2 * hp + hh) * n_kblk for hh in range(2)]
    off_d = [off_ref[base[hh] + kb_last] for hh in range(2)]

    thr = thr_ref[0]

    def needed(kb):
        return jnp.minimum(off_ref[base[0] + kb + 1] - off_d[0], off_ref[base[1] + kb + 1] - off_d[1]) <= thr

    kb_first = lax.while_loop(lambda kb: jnp.logical_and(kb > 0, needed(jnp.maximum(kb - 1, 0))),
                              lambda kb: kb - 1, kb_last)
    n_open = kb_last - kb_first

    def qk(hh, kb, slot):
        ksl = pl.ds(pl.multiple_of(kb * tk, tk), tk)
        s_scr[slot][hh][...] = _dot(k_ref[0, hh, ksl, :], q_ref[0, hh])

    def soft_pv(hh, kb, slot, masked):
        s = s_scr[slot][hh][...]
        if masked:
            key = lax.broadcasted_iota(jnp.int32, (tk, tq), 0)
            qry = lax.broadcasted_iota(jnp.int32, (tk, tq), 1)
            s = jnp.where(key - qry <= delta, s, NEG_BIG)
        off = off_ref[base[hh] + kb] - off_d[hh]
        m = m_scr[hh]
        m_new = jnp.maximum(m, jnp.max(s, axis=0, keepdims=True) - off)
        p = jnp.exp2(s - (m_new + off)).astype(BF16)
        ksl = pl.ds(pl.multiple_of(kb * tk, tk), tk)
        acc_scr[hh] = jnp.exp2(m - m_new) * acc_scr[hh] + _dot(v_ref[0, hh, :, ksl], p)
        m_scr[hh] = m_new

    m_scr[...] = jnp.full(m_scr.shape, NEG_BIG, F32)
    acc_scr[...] = jnp.zeros(acc_scr.shape, F32)
    qk(0, kb_first, 0)
    qk(1, kb_first, 0)

    def pair(i, _):
        kb = kb_first + 2 * i
        qk(0, kb + 1, 1)
        qk(1, kb + 1, 1)
        soft_pv(0, kb, 0, False)
        soft_pv(1, kb, 0, False)
        qk(0, kb + 2, 0)
        qk(1, kb + 2, 0)
        soft_pv(0, kb + 1, 1, False)
        soft_pv(1, kb + 1, 1, False)
        return 0

    lax.fori_loop(0, n_open // 2, pair, 0)

    @pl.when(n_open % 2 == 1)
    def _():
        qk(0, kb_last, 1)
        qk(1, kb_last, 1)
        soft_pv(0, kb_last - 1, 0, False)
        soft_pv(1, kb_last - 1, 0, False)
        soft_pv(0, kb_last, 1, True)
        soft_pv(1, kb_last, 1, True)

    @pl.when(n_open % 2 == 0)
    def _():
        soft_pv(0, kb_last, 0, True)
        soft_pv(1, kb_last, 0, True)

    a0, a1 = acc_scr[0], acc_scr[1]
    o_t = jnp.concatenate([a0[:HEAD] / a0[HEAD:HEAD + 1], a1[:HEAD] / a1[HEAD:HEAD + 1]], axis=0)
    o_ref[0] = o_t.T.astype(BF16)


def _fox(q_t, k_aug, v_t, off, q_norm, k_norm, *, tq=512, tk=FOX_KBLOCK):
    bsz, nh, s, _ = k_aug.shape
    n_kblk = s // tk
    assert s % tq == 0 and s % tk == 0 and tk % tq == 0 and nh % 2 == 0 and off.shape == (bsz * nh * n_kblk,)
    qk_bound = 1.02 * LOG2E * HEAD ** 0.5 * jnp.max(jnp.abs(q_norm)) * jnp.max(jnp.abs(k_norm))
    thr = (2.0 * qk_bound + EXP2_UNDERFLOW).reshape(1).astype(F32)
    return pl.pallas_call(
        functools.partial(_fox_kernel, tq=tq, tk=tk, n_kblk=n_kblk),
        out_shape=jax.ShapeDtypeStruct((bsz, s, nh * HEAD), BF16),
        grid=(bsz, nh // 2, s // tq),
        in_specs=[
            pl.BlockSpec(memory_space=pltpu.SMEM),
            pl.BlockSpec(memory_space=pltpu.SMEM),
            pl.BlockSpec((1, 2, 2 * HEAD, tq), lambda b, hp, i: (b, hp, 0, i)),
            pl.BlockSpec((1, 2, s, 2 * HEAD), lambda b, hp, i: (b, hp, 0, 0)),
            pl.BlockSpec((1, 2, V_ROWS, s), lambda b, hp, i: (b, hp, 0, 0)),
        ],
        out_specs=pl.BlockSpec((1, tq, 2 * HEAD), lambda b, hp, i: (b, i, hp)),
        scratch_shapes=[pltpu.VMEM((tk, tq), F32)] * 4 + [pltpu.VMEM((2, 1, tq), F32), pltpu.VMEM((2, V_ROWS, tq), F32)],
        compiler_params=pltpu.CompilerParams(
            dimension_semantics=("parallel", "parallel", "arbitrary"), vmem_limit_bytes=VMEM_LIMIT),
        name="fox",
    )(off, thr, q_t, k_aug, v_t)


def _merge_kernel(x_ref, g_ref, wg_ref, ya_ref, yb_ref, pa_ref, pb_ref, wo_ref, o_ref):
    x = x_ref[...]
    d = x.shape[-1]
    h = _rms(x, g_ref[...]).astype(BF16)
    gates = _sigmoid(_dot(h, wg_ref[...]))
    ya = _dot(ya_ref[...], pa_ref[...])
    yb = _dot(yb_ref[...], pb_ref[...])
    mix = (gates[:, :d] * ya + gates[:, d:] * yb).astype(BF16)
    o_ref[...] = x + _dot(mix, wo_ref[...])


def _merge(x2d, mix_norm, w_gates, ya, yb, p_a, p_b, w_out, *, tm=512):
    m, d = x2d.shape
    assert m % tm == 0
    full = lambda t: pl.BlockSpec(t.shape, lambda i: (0, 0))
    tok = lambda n: pl.BlockSpec((tm, n), lambda i: (i, 0))
    consts = [mix_norm.reshape(1, d).astype(F32), w_gates.astype(BF16)]
    mats = [p_a.astype(BF16), p_b.astype(BF16), w_out.astype(BF16)]
    return pl.pallas_call(
        _merge_kernel,
        out_shape=jax.ShapeDtypeStruct((m, d), F32),
        grid=(m // tm,),
        in_specs=[tok(d), full(consts[0]), full(consts[1]), tok(D_MIX), tok(D_MIX)] + [full(t) for t in mats],
        out_specs=tok(d),
        compiler_params=pltpu.CompilerParams(dimension_semantics=("parallel",), vmem_limit_bytes=VMEM_LIMIT),
        name="merge",
    )(x2d, consts[0], consts[1], ya, yb, *mats)


def kernel(x, ffn1_norm, ffn1_w1, ffn1_w3, ffn1_w2, mix_norm, w_in, shift_mu, rwkv_w0, rwkv_w_up, rwkv_a0, rwkv_a_up, rwkv_g_up, rwkv_k_k, rwkv_k_a, rwkv_r_k, rwkv_lnx_g, rwkv_lnx_b, rwkv_proj, fox_f_bias, fox_q_norm, fox_k_norm, fox_proj, w_out, ffn2_norm, ffn2_w1, ffn2_w3, ffn2_w2, final_norm):
    bsz, s, d = x.shape
    depth = ffn1_norm.shape[0]
    n_gate0 = w_in.shape[-1] - 2 * d
    x2d = x.reshape(bsz * s, d)
    for l in range(depth):
        x2d = _ffn(x2d, ffn1_norm[l], ffn1_w1[l], ffn1_w3[l], ffn1_w2[l], final_norm, final_norm=False)
        (r, lw, k, v, kk, b, g, bonus, fq_t, fk_aug, fv_t, off) = _in_proj(
            x2d.reshape(bsz, s, d), mix_norm[l], w_in[l], shift_mu[l], rwkv_w0[l], rwkv_w_up[l], rwkv_a0[l],
            rwkv_a_up[l], rwkv_g_up[l], rwkv_k_k[l], rwkv_k_a[l], rwkv_r_k[l], fox_f_bias[l], fox_q_norm[l],
            fox_k_norm[l])
        ya = _rwkv(r, lw, k, v, kk, b, g, bonus, rwkv_lnx_g[l], rwkv_lnx_b[l])
        off = off[:, ::FOX_KBLOCK // (s // off.shape[1]), 0, :N_HEADS]
        off = jnp.transpose(off, (0, 2, 1)).reshape(-1)
        yb = _fox(fq_t, fk_aug, fv_t, off, fox_q_norm[l], fox_k_norm[l])
        x2d = _merge(x2d, mix_norm[l], w_in[l][:, n_gate0:], ya.reshape(bsz * s, D_MIX), yb.reshape(bsz * s, D_MIX),
                     rwkv_proj[l], fox_proj[l], w_out[l])
        x2d = _ffn(x2d, ffn2_norm[l], ffn2_w1[l], ffn2_w3[l], ffn2_w2[l], final_norm, final_norm=(l == depth - 1))
    return x2d.reshape(bsz, s, d)
```

```python
import functools

import jax
import jax.numpy as jnp
from jax import lax
from jax.experimental import pallas as pl
from jax.experimental.pallas import tpu as pltpu

F32 = jnp.float32
BF16 = jnp.bfloat16

HEAD = 64
N_HEADS = 8
D_MIX = HEAD * N_HEADS
LANE = 128
CHUNK = 64
RMS_EPS = 1e-6
LNX_EPS = 64e-5
NEG_BIG = -1e30
EXP2_UNDERFLOW = 160.0
VMEM_LIMIT = 56 * 1024 * 1024

_NT = (((1,), (1,)), ((), ()))
_TN = (((0,), (0,)), ((), ()))


def _dot(a, b, dims=None):
    if dims is None:
        return jnp.dot(a, b, preferred_element_type=F32)
    return lax.dot_general(a, b, dims, preferred_element_type=F32)


def _bdot(a, b, dims=None):
    return _dot(a.astype(BF16), b.astype(BF16), dims)


def _rms(x, g):
    return x * lax.rsqrt(jnp.mean(x * x, axis=-1, keepdims=True) + RMS_EPS) * g


def _softplus(z):
    return jnp.maximum(z, 0.0) + jnp.log(1.0 + jnp.exp(-jnp.abs(z)))


def _sigmoid(z):
    return 1.0 / (1.0 + jnp.exp(-z))


def _head_sum(x, e):
    return _dot(x.astype(BF16), e)


def _cumsum_rows(tri, x):
    hi = x.astype(BF16)
    r1 = x - hi.astype(F32)
    mid = r1.astype(BF16)
    lo = (r1 - mid.astype(F32)).astype(BF16)
    return _dot(tri, hi) + _dot(tri, mid) + _dot(tri, lo)


def _ffn_kernel(x_ref, g_ref, w1_ref, w3_ref, w2_ref, gf_ref, o_ref, *, final_norm):
    x = x_ref[...]
    h = _rms(x, g_ref[...]).astype(BF16)
    a = _dot(h, w1_ref[...])
    b = _dot(h, w3_ref[...])
    u = (a * _sigmoid(a) * b).astype(BF16)
    y = x + 0.5 * _dot(u, w2_ref[...])
    if final_norm:
        y = _rms(y, gf_ref[...])
    o_ref[...] = y


def _resident(shape):
    return pl.BlockSpec(shape, lambda *_: (0,) * len(shape), pipeline_mode=pl.Buffered(1))


def _ffn(x2d, g, w1, w3, w2, gf, *, final_norm, tm=512):
    m, d = x2d.shape
    f_dim = w1.shape[1]
    assert m % tm == 0
    return pl.pallas_call(
        functools.partial(_ffn_kernel, final_norm=final_norm),
        out_shape=jax.ShapeDtypeStruct((m, d), F32),
        grid=(m // tm,),
        in_specs=[
            pl.BlockSpec((tm, d), lambda i: (i, 0)),
            _resident((1, d)), _resident((d, f_dim)), _resident((d, f_dim)), _resident((f_dim, d)), _resident((1, d)),
        ],
        out_specs=pl.BlockSpec((tm, d), lambda i: (i, 0)),
        compiler_params=pltpu.CompilerParams(dimension_semantics=("parallel",), vmem_limit_bytes=VMEM_LIMIT),
        name="ffn",
    )(x2d, g.reshape(1, d), w1.astype(BF16), w3.astype(BF16), w2.astype(BF16), gf.reshape(1, d))


N_SHIFT = 3 * D_MIX + 3 * LANE
N_PROJ = N_SHIFT + 3 * D_MIX + LANE
LOG2E = 1.4426950408889634
V_ROWS = HEAD + 16
FOX_KBLOCK = 512


def _in_proj_kernel(x_ref, g_ref, w_ref, mu_ref, w0_ref, wup_ref, a0_ref, aup_ref, gup_ref, kk_ref, ka_ref,
                    rk_ref, fb_ref, qn_ref, kn_ref, e_ref,
                    r_out, lw_out, k_out, v_out, kk_out, b_out, g_out, bonus_out, fq_out, fk_out, fv_out, off_out,
                    carry_p, carry_c, carry_b, *, tm, tiles_per_kblock):
    i = pl.program_id(1)

    @pl.when(i == 0)
    def _():
        carry_p[...] = jnp.zeros_like(carry_p)
        carry_c[...] = jnp.zeros_like(carry_c)
        carry_b[...] = jnp.zeros_like(carry_b)

    h = _rms(x_ref[0], g_ref[...]).astype(BF16)
    p = _dot(h, w_ref[...])

    ps = p[:, :N_SHIFT]
    row = lax.broadcasted_iota(jnp.int32, (tm, 1), 0)
    prev = jnp.where(row == 0, carry_p[...], pltpu.roll(ps, 1, axis=0))
    carry_p[...] = ps[tm - 1:tm, :]
    sh = ps + (prev - ps) * mu_ref[...]

    e = e_ref[...]
    r = sh[:, 0:D_MIX]
    k = sh[:, D_MIX:2 * D_MIX]
    v = sh[:, 2 * D_MIX:3 * D_MIX]
    w_lo = sh[:, 3 * D_MIX:3 * D_MIX + LANE]
    a_lo = sh[:, 3 * D_MIX + LANE:3 * D_MIX + 2 * LANE]
    g_lo = sh[:, 3 * D_MIX + 2 * LANE:N_SHIFT]

    w_log = -_softplus(-(w0_ref[...] + _bdot(jnp.tanh(w_lo), wup_ref[...]))) - 0.5
    lw = -jnp.exp(w_log)
    a = _sigmoid(a0_ref[...] + _bdot(a_lo, aup_ref[...]))
    g = _bdot(_sigmoid(g_lo), gup_ref[...])
    kk = k * kk_ref[...]
    kk = kk * lax.rsqrt(jnp.maximum(_head_sum(kk * kk, e), 1e-24))
    k_mod = k * (1.0 + (a - 1.0) * ka_ref[...])
    bonus = _head_sum(r * k_mod * rk_ref[...], e) * v

    r_out[0] = r
    lw_out[0] = lw
    k_out[0] = k_mod
    v_out[0] = v
    kk_out[0] = kk
    b_out[0] = kk * a
    g_out[0] = g
    bonus_out[0] = bonus

    fq = p[:, N_SHIFT:N_SHIFT + D_MIX]
    fk = p[:, N_SHIFT + D_MIX:N_SHIFT + 2 * D_MIX]
    fv = p[:, N_SHIFT + 2 * D_MIX:N_SHIFT + 3 * D_MIX]
    f_lo = p[:, N_SHIFT + 3 * D_MIX:N_PROJ]
    inv_head = 1.0 / HEAD
    qn = fq * lax.rsqrt(_head_sum(fq * fq, e) * inv_head + RMS_EPS) * qn_ref[...] * (LOG2E * HEAD ** -0.5)
    kn = fk * lax.rsqrt(_head_sum(fk * fk, e) * inv_head + RMS_EPS) * kn_ref[...]
    z = f_lo + fb_ref[...]
    log_f = jnp.minimum(z, 0.0) - jnp.log(1.0 + jnp.exp(-jnp.abs(z)))
    tri = jnp.where(lax.broadcasted_iota(jnp.int32, (tm, tm), 1) <= lax.broadcasted_iota(jnp.int32, (tm, tm), 0),
                    1.0, 0.0).astype(BF16)
    c_loc = _cumsum_rows(tri, log_f)
    off_out[0, 0] = carry_c[...] * LOG2E
    in_blk = jnp.where(i % tiles_per_kblock == 0, 0.0, carry_b[...])
    carry_c[...] = carry_c[...] + c_loc[tm - 1:tm, :]
    carry_b[...] = in_blk + c_loc[tm - 1:tm, :]
    c2 = (c_loc + in_blk) * LOG2E

    lane = lax.broadcasted_iota(jnp.int32, (tm, HEAD), 1)
    sub = lax.broadcasted_iota(jnp.int32, (HEAD, tm), 0)
    q_tail = jnp.where(sub < 3, -1.0, 0.0)
    v_tail = jnp.where(lax.broadcasted_iota(jnp.int32, (V_ROWS - HEAD, tm), 0) == 0, 1.0, 0.0)
    for pr in range(N_HEADS // 2):
        ps2 = slice(pr * 2 * HEAD, (pr + 1) * 2 * HEAD)
        q_t = qn[:, ps2].T
        v_t = fv[:, ps2].T
        for half in range(2):
            hh = 2 * pr + half
            rows = slice(half * HEAD, (half + 1) * HEAD)
            fq_out[0, hh] = jnp.concatenate([q_t[rows], q_tail], axis=0).astype(BF16)
            fv_out[0, hh] = jnp.concatenate([v_t[rows], v_tail], axis=0).astype(BF16)
            c_h = c2[:, hh:hh + 1]
            c_hi = c_h.astype(BF16).astype(F32)
            c_mid = (c_h - c_hi).astype(BF16).astype(F32)
            c_lo = c_h - c_hi - c_mid
            c_cols = jnp.where(lane == 0, c_hi, jnp.where(lane == 1, c_mid, jnp.where(lane == 2, c_lo, 0.0)))
            fk_out[0, hh] = jnp.concatenate([kn[:, hh * HEAD:(hh + 1) * HEAD], c_cols], axis=1).astype(BF16)


def _pad_cols(w, n):
    return jnp.pad(w, ((0, 0), (0, n - w.shape[1])))


def _pad_rows(w, n):
    return jnp.pad(w, ((0, n - w.shape[0]), (0, 0)))


def _in_proj(x, mix_norm, w_in, shift_mu, w0, w_up, a0, a_up, g_up, k_k, k_a, r_k, f_bias, q_norm, k_norm, *, tm=256):
    bsz, s, d = x.shape
    lw_, la_, lg_ = w_up.shape[0], a_up.shape[0], g_up.shape[0]
    c0 = 3 * D_MIX
    c1 = c0 + lw_ + la_ + lg_
    c2 = c1 + 3 * D_MIX
    c3 = c2 + N_HEADS

    def seg(t, pad):
        return jnp.concatenate([
            t[..., :c0],
            pad(t[..., c0:c0 + lw_], LANE), pad(t[..., c0 + lw_:c0 + lw_ + la_], LANE), pad(t[..., c0 + lw_ + la_:c1], LANE),
        ], axis=-1)

    w_cat = jnp.concatenate([seg(w_in, _pad_cols), w_in[:, c1:c2], _pad_cols(w_in[:, c2:c3], LANE)], axis=-1).astype(BF16)
    mu_cat = seg(shift_mu.reshape(1, -1), _pad_cols)
    row = lambda t: t.reshape(1, -1).astype(F32)
    e = _head_mask()
    tile = lambda t: jnp.tile(t.reshape(1, HEAD), (1, N_HEADS)).astype(F32)
    consts = [
        row(mix_norm), w_cat, mu_cat, row(w0), _pad_rows(w_up, LANE).astype(BF16), row(a0),
        _pad_rows(a_up, LANE).astype(BF16), _pad_rows(g_up, LANE).astype(BF16), row(k_k), row(k_a), row(r_k),
        _pad_cols(row(f_bias), LANE), tile(q_norm), tile(k_norm), e,
    ]
    const_specs = [_resident(c.shape) for c in consts]
    tok = lambda: pl.BlockSpec((1, tm, D_MIX), lambda b, i: (b, i, 0))
    f32_tok = jax.ShapeDtypeStruct((bsz, s, D_MIX), F32)
    assert s % tm == 0
    return pl.pallas_call(
        functools.partial(_in_proj_kernel, tm=tm, tiles_per_kblock=FOX_KBLOCK // tm),
        out_shape=[f32_tok] * 8 + [
            jax.ShapeDtypeStruct((bsz, N_HEADS, 2 * HEAD, s), BF16),
            jax.ShapeDtypeStruct((bsz, N_HEADS, s, 2 * HEAD), BF16),
            jax.ShapeDtypeStruct((bsz, N_HEADS, V_ROWS, s), BF16),
            jax.ShapeDtypeStruct((bsz, s // tm, 1, LANE), F32),
        ],
        grid=(bsz, s // tm),
        in_specs=[pl.BlockSpec((1, tm, d), lambda b, i: (b, i, 0))] + const_specs,
        out_specs=[tok() for _ in range(8)] + [
            pl.BlockSpec((1, N_HEADS, 2 * HEAD, tm), lambda b, i: (b, 0, 0, i)),
            pl.BlockSpec((1, N_HEADS, tm, 2 * HEAD), lambda b, i: (b, 0, i, 0)),
            pl.BlockSpec((1, N_HEADS, V_ROWS, tm), lambda b, i: (b, 0, 0, i)),
            pl.BlockSpec((1, 1, 1, LANE), lambda b, i: (b, i, 0, 0)),
        ],
        scratch_shapes=[pltpu.VMEM((1, N_SHIFT), F32), pltpu.VMEM((1, LANE), F32), pltpu.VMEM((1, LANE), F32)],
        compiler_params=pltpu.CompilerParams(
            dimension_semantics=("parallel", "arbitrary"), vmem_limit_bytes=VMEM_LIMIT),
        name="in_proj",
    )(x, *consts)


def _rwkv_kernel(r_ref, lw_ref, k_ref, v_ref, kk_ref, b_ref, g_ref, bonus_ref, lng_ref, lnb_ref, e_ref, y_ref, s_scr, *, tb):
    i = pl.program_id(1)

    @pl.when(i == 0)
    def _():
        s_scr[...] = jnp.zeros_like(s_scr)

    c = CHUNK
    pw = 2 * HEAD
    n_chunk = tb // c
    n_pair = N_HEADS // 2
    rr = lax.broadcasted_iota(jnp.int32, (c, pw), 0)
    ll = lax.broadcasted_iota(jnp.int32, (c, pw), 1)
    cc = ll % c
    lo = ll < c
    strict = cc < rr
    incl = cc <= rr
    eye = (cc == rr).astype(F32)
    r2 = lax.broadcasted_iota(jnp.int32, (2 * c, pw), 0)
    l2 = lax.broadcasted_iota(jnp.int32, (2 * c, pw), 1)
    lo2 = l2 < c
    diag2 = (r2 < c) == lo2
    zeros = jnp.zeros((c, pw), F32)

    def level_mask(n):
        return ((rr // (2 * n)) == (cc // (2 * n))) & (((rr // n) % 2) == 1) & (((cc // n) % 2) == 0)

    def bd(x):
        return jnp.where(diag2, jnp.concatenate([x, x], axis=0), 0.0).astype(BF16)

    def bd_sw(x):
        return jnp.where(diag2, 0.0, jnp.concatenate([x, x], axis=0)).astype(BF16)

    rb = lax.broadcasted_iota(jnp.int32, (tb, tb), 0)
    cb = lax.broadcasted_iota(jnp.int32, (tb, tb), 1)
    tri = jnp.where((cb <= rb) & ((rb // c) == (cb // c)), 1.0, 0.0).astype(BF16)
    lw = lw_ref[0]
    cs = _cumsum_rows(tri, lw)
    w_inv = jnp.exp(-cs)
    rt = r_ref[0] * jnp.exp(cs)
    at = -kk_ref[0] * jnp.exp(cs - lw)
    bi = b_ref[0] * w_inv
    ki = k_ref[0] * w_inv
    v = v_ref[0]

    chains = [(j, p) for j in range(n_chunk) for p in range(n_pair)]
    cut = lambda x, j, p: x[j * c:(j + 1) * c, p * pw:(p + 1) * pw]
    w_tot = [jnp.exp(cs[(j + 1) * c - 1:(j + 1) * c, :]) for j in range(n_chunk)]
    a_t = [cut(at, j, p) for j, p in chains]
    r_t = [cut(rt, j, p) for j, p in chains]
    b_i = [cut(bi, j, p) for j, p in chains]
    k_i = [cut(ki, j, p) for j, p in chains]
    v_p = [cut(v, j, p) for j, p in chains]
    n = len(chains)

    a_ab, a_ak, a_rb, a_rk = [], [], [], []
    for q in range(n):
        ar = jnp.concatenate([a_t[q], r_t[q]], axis=0)
        gx = _dot(jnp.where(lo2, ar, 0.0).astype(BF16), jnp.concatenate([b_i[q], k_i[q]], axis=0).astype(BF16), _NT)
        gy = _dot(jnp.where(lo2, 0.0, ar).astype(BF16), jnp.concatenate([k_i[q], b_i[q]], axis=0).astype(BF16), _NT)
        a_ab.append(jnp.where(strict, jnp.where(lo, gx[:c], gy[:c]), 0.0))
        a_ak.append(jnp.where(strict, jnp.where(lo, gy[:c], gx[:c]), 0.0))
        a_rb.append(jnp.where(incl, jnp.where(lo, gx[c:], gy[c:]), 0.0))
        a_rk.append(jnp.where(incl, jnp.where(lo, gy[c:], gx[c:]), 0.0))

    t = [eye + jnp.where(level_mask(1), a_ab[q], 0.0) for q in range(n)]
    for m in (2, 4, 8, 16, 32):
        x = [_dot(jnp.where(level_mask(m), a_ab[q], 0.0).astype(BF16), bd(t[q])) for q in range(n)]
        t = [t[q] + _dot(t[q].astype(BF16), bd(x[q])) for q in range(n)]

    v_sw = [bd_sw(v_p[q]) for q in range(n)]
    akv = [_dot(a_ak[q].astype(BF16), v_sw[q]) for q in range(n)]
    x = [_dot(t[q].astype(BF16), jnp.concatenate([bd(a_t[q]), bd(akv[q])], axis=1)) for q in range(n)]
    a_p = [x[q][:, :pw] for q in range(n)]
    u_0 = [x[q][:, pw:] for q in range(n)]
    zero_bd = jnp.zeros((2 * c, pw), BF16)
    d1 = [_dot(jnp.concatenate([a_rb[q], a_rk[q]], axis=1).astype(BF16),
               jnp.concatenate([jnp.concatenate([bd(a_p[q]), bd(u_0[q])], axis=1),
                                jnp.concatenate([zero_bd, v_sw[q]], axis=1)], axis=0)) for q in range(n)]
    d2 = []
    for q, (j, p) in enumerate(chains):
        wt = w_tot[j][:, p * pw:(p + 1) * pw]
        lhs = jnp.concatenate([jnp.concatenate([a_p[q], u_0[q]], axis=1),
                               jnp.concatenate([zeros, v_p[q]], axis=1)], axis=0).astype(BF16)
        rhs = jnp.concatenate([b_i[q] * wt, k_i[q] * wt], axis=0).astype(BF16)
        d2.append(_dot(lhs, rhs, _TN))

    s = [s_scr[p] for p in range(n_pair)]
    ys = [None] * n
    for j in range(n_chunk):
        for p in range(n_pair):
            q = j * n_pair + p
            mt = jnp.where(diag2, d2[q][:pw], 0.0).astype(BF16)
            nt = jnp.where(lo, d2[q][pw:pw + c], d2[q][pw + c:])
            ys[q] = d1[q][:, pw:] + _dot((r_t[q] + d1[q][:, :pw]).astype(BF16), bd(s[p]), _NT)
            s[p] = s[p] * w_tot[j][:, p * pw:(p + 1) * pw] + _dot(s[p].astype(BF16), mt) + nt
    for p in range(n_pair):
        s_scr[p] = s[p]

    y = jnp.concatenate([jnp.concatenate([ys[j * n_pair + p] for p in range(n_pair)], axis=1) for j in range(n_chunk)], axis=0)
    e = e_ref[...]
    inv_head = 1.0 / HEAD
    yc = y - _head_sum(y, e) * inv_head
    yn = yc * lax.rsqrt(_head_sum(yc * yc, e) * inv_head + LNX_EPS)
    y_ref[0] = ((yn * lng_ref[...] + lnb_ref[...] + bonus_ref[0]) * g_ref[0]).astype(BF16)


def _head_mask():
    head_id = jnp.arange(D_MIX) // HEAD
    return (head_id[:, None] == head_id[None, :]).astype(BF16)


def _rwkv(r, lw, k, v, kk, b, g, bonus, lnx_g, lnx_b, *, tb=256):
    bsz, s, _ = r.shape
    assert s % tb == 0 and tb % CHUNK == 0
    tok = pl.BlockSpec((1, tb, D_MIX), lambda bb, i: (bb, i, 0))
    vec = pl.BlockSpec((1, D_MIX), lambda bb, i: (0, 0))
    return pl.pallas_call(
        functools.partial(_rwkv_kernel, tb=tb),
        out_shape=jax.ShapeDtypeStruct((bsz, s, D_MIX), BF16),
        grid=(bsz, s // tb),
        in_specs=[tok] * 8 + [vec, vec, pl.BlockSpec((D_MIX, D_MIX), lambda bb, i: (0, 0))],
        out_specs=tok,
        scratch_shapes=[pltpu.VMEM((N_HEADS // 2, HEAD, 2 * HEAD), F32)],
        compiler_params=pltpu.CompilerParams(
            dimension_semantics=("parallel", "arbitrary"), vmem_limit_bytes=VMEM_LIMIT),
        name="rwkv",
    )(r, lw, k, v, kk, b, g, bonus, lnx_g.reshape(1, -1).astype(F32), lnx_b.reshape(1, -1).astype(F32), _head_mask())


def _fox_kernel(off_ref, thr_ref, q_ref, k_ref, v_ref, o_ref, s00, s01, s10, s11, m_scr, acc_scr, *, tq, tk, n_kblk):
    b, hp, qb = pl.program_id(0), pl.program_id(1), pl.program_id(2)
    kb_last = (qb * tq) // tk
    delta = qb * tq - kb_last * tk
    s_scr = ((s00, s01), (s10, s11))
    base = [(b * N_HEADS + 2 * hp + hh) * n_kblk for hh in range(2)]
    off_d = [off_ref[base[hh] + kb_last] for hh in range(2)]

    thr = thr_ref[0]

    def needed(kb):
        nxt = jnp.minimum(kb + 1, kb_last)
        return jnp.minimum(off_ref[base[0] + nxt] - off_d[0], off_ref[base[1] + nxt] - off_d[1]) <= thr

    kb_first = lax.while_loop(lambda kb: jnp.logical_and(kb > 0, needed(kb - 1)), lambda kb: kb - 1, kb_last)
    n_open = kb_last - kb_first

    def qk(hh, kb, slot):
        ksl = pl.ds(pl.multiple_of(kb * tk, tk), tk)
        s_scr[slot][hh][...] = _dot(k_ref[0, hh, ksl, :], q_ref[0, hh])

    def soft_pv(hh, kb, slot, masked):
        s = s_scr[slot][hh][...]
        if masked:
            key = lax.broadcasted_iota(jnp.int32, (tk, tq), 0)
            qry = lax.broadcasted_iota(jnp.int32, (tk, tq), 1)
            s = jnp.where(key - qry <= delta, s, NEG_BIG)
        off = off_ref[base[hh] + kb] - off_d[hh]
        m = m_scr[hh]
        m_new = jnp.maximum(m, jnp.max(s, axis=0, keepdims=True) - off)
        p = jnp.exp2(s - (m_new + off)).astype(BF16)
        ksl = pl.ds(pl.multiple_of(kb * tk, tk), tk)
        acc_scr[hh] = jnp.exp2(m - m_new) * acc_scr[hh] + _dot(v_ref[0, hh, :, ksl], p)
        m_scr[hh] = m_new

    m_scr[...] = jnp.full(m_scr.shape, NEG_BIG, F32)
    acc_scr[...] = jnp.zeros(acc_scr.shape, F32)
    qk(0, kb_first, 0)
    qk(1, kb_first, 0)

    def pair(i, _):
        kb = kb_first + 2 * i
        qk(0, kb + 1, 1)
        qk(1, kb + 1, 1)
        soft_pv(0, kb, 0, False)
        soft_pv(1, kb, 0, False)
        qk(0, kb + 2, 0)
        qk(1, kb + 2, 0)
        soft_pv(0, kb + 1, 1, False)
        soft_pv(1, kb + 1, 1, False)
        return 0

    lax.fori_loop(0, n_open // 2, pair, 0)

    @pl.when(n_open % 2 == 1)
    def _():
        qk(0, kb_last, 1)
        qk(1, kb_last, 1)
        soft_pv(0, kb_last - 1, 0, False)
        soft_pv(1, kb_last - 1, 0, False)
        soft_pv(0, kb_last, 1, True)
        soft_pv(1, kb_last, 1, True)

    @pl.when(n_open % 2 == 0)
    def _():
        soft_pv(0, kb_last, 0, True)
        soft_pv(1, kb_last, 0, True)

    a0, a1 = acc_scr[0], acc_scr[1]
    o_t = jnp.concatenate([a0[:HEAD] / a0[HEAD:HEAD + 1], a1[:HEAD] / a1[HEAD:HEAD + 1]], axis=0)
    o_ref[0] = o_t.T.astype(BF16)


def _fox(q_t, k_aug, v_t, off, q_norm, k_norm, *, tq=512, tk=FOX_KBLOCK):
    bsz, nh, s, _ = k_aug.shape
    n_kblk = s // tk
    assert s % tq == 0 and s % tk == 0 and tk % tq == 0 and nh % 2 == 0 and off.shape == (bsz * nh * n_kblk,)
    qk_bound = 1.02 * LOG2E * HEAD ** 0.5 * jnp.max(jnp.abs(q_norm)) * jnp.max(jnp.abs(k_norm))
    thr = (2.0 * qk_bound + EXP2_UNDERFLOW).reshape(1).astype(F32)
    return pl.pallas_call(
        functools.partial(_fox_kernel, tq=tq, tk=tk, n_kblk=n_kblk),
        out_shape=jax.ShapeDtypeStruct((bsz, s, nh * HEAD), BF16),
        grid=(bsz, nh // 2, s // tq),
        in_specs=[
            pl.BlockSpec(memory_space=pltpu.SMEM),
            pl.BlockSpec(memory_space=pltpu.SMEM),
            pl.BlockSpec((1, 2, 2 * HEAD, tq), lambda b, hp, i: (b, hp, 0, i)),
            pl.BlockSpec((1, 2, s, 2 * HEAD), lambda b, hp, i: (b, hp, 0, 0)),
            pl.BlockSpec((1, 2, V_ROWS, s), lambda b, hp, i: (b, hp, 0, 0)),
        ],
        out_specs=pl.BlockSpec((1, tq, 2 * HEAD), lambda b, hp, i: (b, i, hp)),
        scratch_shapes=[pltpu.VMEM((tk, tq), F32)] * 4 + [pltpu.VMEM((2, 1, tq), F32), pltpu.VMEM((2, V_ROWS, tq), F32)],
        compiler_params=pltpu.CompilerParams(
            dimension_semantics=("parallel", "parallel", "arbitrary"), vmem_limit_bytes=VMEM_LIMIT),
        name="fox",
    )(off, thr, q_t, k_aug, v_t)


def _merge_kernel(x_ref, g_ref, wg_ref, ya_ref, yb_ref, pa_ref, pb_ref, wo_ref, o_ref):
    x = x_ref[...]
    d = x.shape[-1]
    h = _rms(x, g_ref[...]).astype(BF16)
    gates = _sigmoid(_dot(h, wg_ref[...]))
    ya = _dot(ya_ref[...], pa_ref[...])
    yb = _dot(yb_ref[...], pb_ref[...])
    mix = (gates[:, :d] * ya + gates[:, d:] * yb).astype(BF16)
    o_ref[...] = x + _dot(mix, wo_ref[...])


def _merge(x2d, mix_norm, w_gates, ya, yb, p_a, p_b, w_out, *, tm=512):
    m, d = x2d.shape
    assert m % tm == 0
    full = lambda t: pl.BlockSpec(t.shape, lambda i: (0, 0))
    tok = lambda n: pl.BlockSpec((tm, n), lambda i: (i, 0))
    consts = [mix_norm.reshape(1, d).astype(F32), w_gates.astype(BF16)]
    mats = [p_a.astype(BF16), p_b.astype(BF16), w_out.astype(BF16)]
    return pl.pallas_call(
        _merge_kernel,
        out_shape=jax.ShapeDtypeStruct((m, d), F32),
        grid=(m // tm,),
        in_specs=[tok(d), full(consts[0]), full(consts[1]), tok(D_MIX), tok(D_MIX)] + [full(t) for t in mats],
        out_specs=tok(d),
        compiler_params=pltpu.CompilerParams(dimension_semantics=("parallel",), vmem_limit_bytes=VMEM_LIMIT),
        name="merge",
    )(x2d, consts[0], consts[1], ya, yb, *mats)


def kernel(x, ffn1_norm, ffn1_w1, ffn1_w3, ffn1_w2, mix_norm, w_in, shift_mu, rwkv_w0, rwkv_w_up, rwkv_a0, rwkv_a_up, rwkv_g_up, rwkv_k_k, rwkv_k_a, rwkv_r_k, rwkv_lnx_g, rwkv_lnx_b, rwkv_proj, fox_f_bias, fox_q_norm, fox_k_norm, fox_proj, w_out, ffn2_norm, ffn2_w1, ffn2_w3, ffn2_w2, final_norm):
    bsz, s, d = x.shape
    depth = ffn1_norm.shape[0]
    n_gate0 = w_in.shape[-1] - 2 * d
    x2d = x.reshape(bsz * s, d)
    for l in range(depth):
        x2d = _ffn(x2d, ffn1_norm[l], ffn1_w1[l], ffn1_w3[l], ffn1_w2[l], final_norm, final_norm=False)
        (r, lw, k, v, kk, b, g, bonus, fq_t, fk_aug, fv_t, off) = _in_proj(
            x2d.reshape(bsz, s, d), mix_norm[l], w_in[l], shift_mu[l], rwkv_w0[l], rwkv_w_up[l], rwkv_a0[l],
            rwkv_a_up[l], rwkv_g_up[l], rwkv_k_k[l], rwkv_k_a[l], rwkv_r_k[l], fox_f_bias[l], fox_q_norm[l],
            fox_k_norm[l])
        ya = _rwkv(r, lw, k, v, kk, b, g, bonus, rwkv_lnx_g[l], rwkv_lnx_b[l])
        off = off[:, ::FOX_KBLOCK // (s // off.shape[1]), 0, :N_HEADS]
        off = jnp.transpose(off, (0, 2, 1)).reshape(-1)
        yb = _fox(fq_t, fk_aug, fv_t, off, fox_q_norm[l], fox_k_norm[l])
        x2d = _merge(x2d, mix_norm[l], w_in[l][:, n_gate0:], ya.reshape(bsz * s, D_MIX), yb.reshape(bsz * s, D_MIX),
                     rwkv_proj[l], fox_proj[l], w_out[l])
        x2d = _ffn(x2d, ffn2_norm[l], ffn2_w1[l], ffn2_w3[l], ffn2_w2[l], final_norm, final_norm=(l == depth - 1))
    return x2d.reshape(bsz, s, d)
```

```python
import functools

import jax
import jax.numpy as jnp
from jax import lax
from jax.experimental import pallas as pl
from jax.experimental.pallas import tpu as pltpu

F32 = jnp.float32
BF16 = jnp.bfloat16

HEAD = 64
N_HEADS = 8
D_MIX = HEAD * N_HEADS
LANE = 128
CHUNK = 64
RMS_EPS = 1e-6
LNX_EPS = 64e-5
NEG_BIG = -1e30
EXP2_UNDERFLOW = 160.0
VMEM_LIMIT = 56 * 1024 * 1024

_NT = (((1,), (1,)), ((), ()))
_TN = (((0,), (0,)), ((), ()))


def _dot(a, b, dims=None):
    if dims is None:
        return jnp.dot(a, b, preferred_element_type=F32)
    return lax.dot_general(a, b, dims, preferred_element_type=F32)


def _bdot(a, b, dims=None):
    return _dot(a.astype(BF16), b.astype(BF16), dims)


def _rms(x, g):
    return x * lax.rsqrt(jnp.mean(x * x, axis=-1, keepdims=True) + RMS_EPS) * g


def _softplus(z):
    return jnp.maximum(z, 0.0) + jnp.log(1.0 + jnp.exp(-jnp.abs(z)))


def _sigmoid(z):
    return 1.0 / (1.0 + jnp.exp(-z))


def _head_sum(x, e):
    return _dot(x.astype(BF16), e)


def _cumsum_rows(tri, x):
    hi = x.astype(BF16)
    r1 = x - hi.astype(F32)
    mid = r1.astype(BF16)
    lo = (r1 - mid.astype(F32)).astype(BF16)
    return _dot(tri, hi) + _dot(tri, mid) + _dot(tri, lo)


def _ffn_kernel(x_ref, g_ref, w1_ref, w3_ref, w2_ref, gf_ref, o_ref, *, final_norm):
    x = x_ref[...]
    h = _rms(x, g_ref[...]).astype(BF16)
    a = _dot(h, w1_ref[...])
    b = _dot(h, w3_ref[...])
    u = (a * _sigmoid(a) * b).astype(BF16)
    y = x + 0.5 * _dot(u, w2_ref[...])
    if final_norm:
        y = _rms(y, gf_ref[...])
    o_ref[...] = y


def _resident(shape):
    return pl.BlockSpec(shape, lambda *_: (0,) * len(shape), pipeline_mode=pl.Buffered(1))


def _ffn(x2d, g, w1, w3, w2, gf, *, final_norm, tm=512):
    m, d = x2d.shape
    f_dim = w1.shape[1]
    assert m % tm == 0
    return pl.pallas_call(
        functools.partial(_ffn_kernel, final_norm=final_norm),
        out_shape=jax.ShapeDtypeStruct((m, d), F32),
        grid=(m // tm,),
        in_specs=[
            pl.BlockSpec((tm, d), lambda i: (i, 0)),
            _resident((1, d)), _resident((d, f_dim)), _resident((d, f_dim)), _resident((f_dim, d)), _resident((1, d)),
        ],
        out_specs=pl.BlockSpec((tm, d), lambda i: (i, 0)),
        compiler_params=pltpu.CompilerParams(dimension_semantics=("parallel",), vmem_limit_bytes=VMEM_LIMIT),
        name="ffn",
    )(x2d, g.reshape(1, d), w1.astype(BF16), w3.astype(BF16), w2.astype(BF16), gf.reshape(1, d))


N_SHIFT = 3 * D_MIX + 3 * LANE
N_PROJ = N_SHIFT + 3 * D_MIX + LANE
LOG2E = 1.4426950408889634
V_ROWS = HEAD + 16
FOX_KBLOCK = 512


def _in_proj_kernel(x_ref, g_ref, w_ref, mu_ref, w0_ref, wup_ref, a0_ref, aup_ref, gup_ref, kk_ref, ka_ref,
                    rk_ref, fb_ref, qn_ref, kn_ref, e_ref,
                    r_out, lw_out, k_out, v_out, kk_out, b_out, g_out, bonus_out, fq_out, fk_out, fv_out, off_out,
                    carry_p, carry_c, carry_b, *, tm, n_sub, tiles_per_kblock):
    step = pl.program_id(1)

    @pl.when(step == 0)
    def _():
        carry_p[...] = jnp.zeros_like(carry_p)
        carry_c[...] = jnp.zeros_like(carry_c)
        carry_b[...] = jnp.zeros_like(carry_b)

    outs = (r_out, lw_out, k_out, v_out, kk_out, b_out, g_out, bonus_out)
    for t in range(n_sub):
        rows = pl.ds(t * tm, tm)
        _in_proj_tile(step * n_sub + t, x_ref.at[0, rows], g_ref, w_ref, mu_ref, w0_ref, wup_ref, a0_ref, aup_ref, gup_ref,
                      kk_ref, ka_ref, rk_ref, fb_ref, qn_ref, kn_ref, e_ref, *[o.at[0, rows] for o in outs],
                      fq_out.at[0, :, :, rows], fk_out.at[0, :, rows], fv_out.at[0, :, :, rows], off_out.at[0, t],
                      carry_p, carry_c, carry_b, tm=tm, tiles_per_kblock=tiles_per_kblock)


def _in_proj_tile(i, x_ref, g_ref, w_ref, mu_ref, w0_ref, wup_ref, a0_ref, aup_ref, gup_ref, kk_ref, ka_ref,
                  rk_ref, fb_ref, qn_ref, kn_ref, e_ref,
                  r_out, lw_out, k_out, v_out, kk_out, b_out, g_out, bonus_out, fq_out, fk_out, fv_out, off_out,
                  carry_p, carry_c, carry_b, *, tm, tiles_per_kblock):
    h = _rms(x_ref[...], g_ref[...]).astype(BF16)
    p = _dot(h, w_ref[...])

    ps = p[:, :N_SHIFT]
    row = lax.broadcasted_iota(jnp.int32, (tm, 1), 0)
    prev = jnp.where(row == 0, carry_p[...], pltpu.roll(ps, 1, axis=0))
    carry_p[...] = ps[tm - 1:tm, :]
    sh = ps + (prev - ps) * mu_ref[...]

    e = e_ref[...]
    r = sh[:, 0:D_MIX]
    k = sh[:, D_MIX:2 * D_MIX]
    v = sh[:, 2 * D_MIX:3 * D_MIX]
    w_lo = sh[:, 3 * D_MIX:3 * D_MIX + LANE]
    a_lo = sh[:, 3 * D_MIX + LANE:3 * D_MIX + 2 * LANE]
    g_lo = sh[:, 3 * D_MIX + 2 * LANE:N_SHIFT]

    w_log = -_softplus(-(w0_ref[...] + _bdot(jnp.tanh(w_lo), wup_ref[...]))) - 0.5
    lw = -jnp.exp(w_log)
    a = _sigmoid(a0_ref[...] + _bdot(a_lo, aup_ref[...]))
    g = _bdot(_sigmoid(g_lo), gup_ref[...])
    kk = k * kk_ref[...]
    kk = kk * lax.rsqrt(jnp.maximum(_head_sum(kk * kk, e), 1e-24))
    k_mod = k * (1.0 + (a - 1.0) * ka_ref[...])
    bonus = _head_sum(r * k_mod * rk_ref[...], e) * v

    r_out[...] = r
    lw_out[...] = lw
    k_out[...] = k_mod
    v_out[...] = v.astype(BF16)
    kk_out[...] = kk
    b_out[...] = kk * a
    g_out[...] = g.astype(BF16)
    bonus_out[...] = bonus.astype(BF16)

    fq = p[:, N_SHIFT:N_SHIFT + D_MIX]
    fk = p[:, N_SHIFT + D_MIX:N_SHIFT + 2 * D_MIX]
    fv = p[:, N_SHIFT + 2 * D_MIX:N_SHIFT + 3 * D_MIX]
    f_lo = p[:, N_SHIFT + 3 * D_MIX:N_PROJ]
    inv_head = 1.0 / HEAD
    qn = fq * lax.rsqrt(_head_sum(fq * fq, e) * inv_head + RMS_EPS) * qn_ref[...] * (LOG2E * HEAD ** -0.5)
    kn = fk * lax.rsqrt(_head_sum(fk * fk, e) * inv_head + RMS_EPS) * kn_ref[...]
    z = f_lo + fb_ref[...]
    log_f = jnp.minimum(z, 0.0) - jnp.log(1.0 + jnp.exp(-jnp.abs(z)))
    tri = jnp.where(lax.broadcasted_iota(jnp.int32, (tm, tm), 1) <= lax.broadcasted_iota(jnp.int32, (tm, tm), 0),
                    1.0, 0.0).astype(BF16)
    c_loc = _cumsum_rows(tri, log_f)
    off_out[...] = carry_c[...] * LOG2E
    in_blk = jnp.where(i % tiles_per_kblock == 0, 0.0, carry_b[...])
    carry_c[...] = carry_c[...] + c_loc[tm - 1:tm, :]
    carry_b[...] = in_blk + c_loc[tm - 1:tm, :]
    c2 = (c_loc + in_blk) * LOG2E

    lane = lax.broadcasted_iota(jnp.int32, (tm, HEAD), 1)
    sub = lax.broadcasted_iota(jnp.int32, (HEAD, tm), 0)
    q_tail = jnp.where(sub < 3, -1.0, 0.0)
    v_tail = jnp.where(lax.broadcasted_iota(jnp.int32, (V_ROWS - HEAD, tm), 0) == 0, 1.0, 0.0)
    for pr in range(N_HEADS // 2):
        ps2 = slice(pr * 2 * HEAD, (pr + 1) * 2 * HEAD)
        q_t = qn[:, ps2].T
        v_t = fv[:, ps2].T
        for half in range(2):
            hh = 2 * pr + half
            rows = slice(half * HEAD, (half + 1) * HEAD)
            fq_out[hh] = jnp.concatenate([q_t[rows], q_tail], axis=0).astype(BF16)
            fv_out[hh] = jnp.concatenate([v_t[rows], v_tail], axis=0).astype(BF16)
            c_h = c2[:, hh:hh + 1]
            c_hi = c_h.astype(BF16).astype(F32)
            c_mid = (c_h - c_hi).astype(BF16).astype(F32)
            c_lo = c_h - c_hi - c_mid
            c_cols = jnp.where(lane == 0, c_hi, jnp.where(lane == 1, c_mid, jnp.where(lane == 2, c_lo, 0.0)))
            fk_out[hh] = jnp.concatenate([kn[:, hh * HEAD:(hh + 1) * HEAD], c_cols], axis=1).astype(BF16)


def _pad_cols(w, n):
    return jnp.pad(w, ((0, 0), (0, n - w.shape[1])))


def _pad_rows(w, n):
    return jnp.pad(w, ((0, n - w.shape[0]), (0, 0)))


def _in_proj(x, mix_norm, w_in, shift_mu, w0, w_up, a0, a_up, g_up, k_k, k_a, r_k, f_bias, q_norm, k_norm, *, tm=256, n_sub=2):
    bsz, s, d = x.shape
    lw_, la_, lg_ = w_up.shape[0], a_up.shape[0], g_up.shape[0]
    c0 = 3 * D_MIX
    c1 = c0 + lw_ + la_ + lg_
    c2 = c1 + 3 * D_MIX
    c3 = c2 + N_HEADS

    def seg(t, pad):
        return jnp.concatenate([
            t[..., :c0],
            pad(t[..., c0:c0 + lw_], LANE), pad(t[..., c0 + lw_:c0 + lw_ + la_], LANE), pad(t[..., c0 + lw_ + la_:c1], LANE),
        ], axis=-1)

    w_cat = jnp.concatenate([seg(w_in, _pad_cols), w_in[:, c1:c2], _pad_cols(w_in[:, c2:c3], LANE)], axis=-1).astype(BF16)
    mu_cat = seg(shift_mu.reshape(1, -1), _pad_cols)
    row = lambda t: t.reshape(1, -1).astype(F32)
    e = _head_mask()
    tile = lambda t: jnp.tile(t.reshape(1, HEAD), (1, N_HEADS)).astype(F32)
    consts = [
        row(mix_norm), w_cat, mu_cat, row(w0), _pad_rows(w_up, LANE).astype(BF16), row(a0),
        _pad_rows(a_up, LANE).astype(BF16), _pad_rows(g_up, LANE).astype(BF16), row(k_k), row(k_a), row(r_k),
        _pad_cols(row(f_bias), LANE), tile(q_norm), tile(k_norm), e,
    ]
    const_specs = [_resident(c.shape) for c in consts]
    tm, tile_rows = tm * n_sub, tm
    tok = lambda: pl.BlockSpec((1, tm, D_MIX), lambda b, i: (b, i, 0))
    f32_tok = jax.ShapeDtypeStruct((bsz, s, D_MIX), F32)
    bf_tok = jax.ShapeDtypeStruct((bsz, s, D_MIX), BF16)
    assert s % tm == 0 and FOX_KBLOCK % tile_rows == 0
    return pl.pallas_call(
        functools.partial(_in_proj_kernel, tm=tile_rows, n_sub=n_sub, tiles_per_kblock=FOX_KBLOCK // tile_rows),
        out_shape=[f32_tok, f32_tok, f32_tok, bf_tok, f32_tok, f32_tok, bf_tok, bf_tok] + [
            jax.ShapeDtypeStruct((bsz, N_HEADS, 2 * HEAD, s), BF16),
            jax.ShapeDtypeStruct((bsz, N_HEADS, s, 2 * HEAD), BF16),
            jax.ShapeDtypeStruct((bsz, N_HEADS, V_ROWS, s), BF16),
            jax.ShapeDtypeStruct((bsz, s // tile_rows, 1, LANE), F32),
        ],
        grid=(bsz, s // tm),
        in_specs=[pl.BlockSpec((1, tm, d), lambda b, i: (b, i, 0))] + const_specs,
        out_specs=[tok() for _ in range(8)] + [
            pl.BlockSpec((1, N_HEADS, 2 * HEAD, tm), lambda b, i: (b, 0, 0, i)),
            pl.BlockSpec((1, N_HEADS, tm, 2 * HEAD), lambda b, i: (b, 0, i, 0)),
            pl.BlockSpec((1, N_HEADS, V_ROWS, tm), lambda b, i: (b, 0, 0, i)),
            pl.BlockSpec((1, n_sub, 1, LANE), lambda b, i: (b, i, 0, 0)),
        ],
        scratch_shapes=[pltpu.VMEM((1, N_SHIFT), F32), pltpu.VMEM((1, LANE), F32), pltpu.VMEM((1, LANE), F32)],
        compiler_params=pltpu.CompilerParams(
            dimension_semantics=("parallel", "arbitrary"), vmem_limit_bytes=VMEM_LIMIT),
        name="in_proj",
    )(x, *consts)


def _rwkv_kernel(r_ref, lw_ref, k_ref, v_ref, kk_ref, b_ref, g_ref, bonus_ref, lng_ref, lnb_ref, e_ref, y_ref, s_scr, *, tb):
    i = pl.program_id(1)

    @pl.when(i == 0)
    def _():
        s_scr[...] = jnp.zeros_like(s_scr)

    c = CHUNK
    pw = 2 * HEAD
    n_chunk = tb // c
    n_pair = N_HEADS // 2
    rr = lax.broadcasted_iota(jnp.int32, (c, pw), 0)
    ll = lax.broadcasted_iota(jnp.int32, (c, pw), 1)
    cc = ll % c
    lo = ll < c
    strict = cc < rr
    incl = cc <= rr
    eye = (cc == rr).astype(F32)
    r2 = lax.broadcasted_iota(jnp.int32, (2 * c, pw), 0)
    l2 = lax.broadcasted_iota(jnp.int32, (2 * c, pw), 1)
    lo2 = l2 < c
    diag2 = (r2 < c) == lo2
    zeros = jnp.zeros((c, pw), F32)

    def level_mask(n):
        return ((rr // (2 * n)) == (cc // (2 * n))) & (((rr // n) % 2) == 1) & (((cc // n) % 2) == 0)

    def bd(x):
        return jnp.where(diag2, jnp.concatenate([x, x], axis=0), 0.0).astype(BF16)

    def bd_sw(x):
        return jnp.where(diag2, 0.0, jnp.concatenate([x, x], axis=0)).astype(BF16)

    rb = lax.broadcasted_iota(jnp.int32, (tb, tb), 0)
    cb = lax.broadcasted_iota(jnp.int32, (tb, tb), 1)
    tri = jnp.where((cb <= rb) & ((rb // c) == (cb // c)), 1.0, 0.0).astype(BF16)
    lw = lw_ref[0]
    cs = _cumsum_rows(tri, lw)
    w_inv = jnp.exp(-cs)
    rt = r_ref[0] * jnp.exp(cs)
    at = -kk_ref[0] * jnp.exp(cs - lw)
    bi = b_ref[0] * w_inv
    ki = k_ref[0] * w_inv
    v = v_ref[0].astype(F32)

    chains = [(j, p) for j in range(n_chunk) for p in range(n_pair)]
    cut = lambda x, j, p: x[j * c:(j + 1) * c, p * pw:(p + 1) * pw]
    w_tot = [jnp.exp(cs[(j + 1) * c - 1:(j + 1) * c, :]) for j in range(n_chunk)]
    a_t = [cut(at, j, p) for j, p in chains]
    r_t = [cut(rt, j, p) for j, p in chains]
    b_i = [cut(bi, j, p) for j, p in chains]
    k_i = [cut(ki, j, p) for j, p in chains]
    v_p = [cut(v, j, p) for j, p in chains]
    n = len(chains)

    a_ab, a_ak, a_rb, a_rk = [], [], [], []
    for q in range(n):
        ar = jnp.concatenate([a_t[q], r_t[q]], axis=0)
        gx = _dot(jnp.where(lo2, ar, 0.0).astype(BF16), jnp.concatenate([b_i[q], k_i[q]], axis=0).astype(BF16), _NT)
        gy = _dot(jnp.where(lo2, 0.0, ar).astype(BF16), jnp.concatenate([k_i[q], b_i[q]], axis=0).astype(BF16), _NT)
        a_ab.append(jnp.where(strict, jnp.where(lo, gx[:c], gy[:c]), 0.0))
        a_ak.append(jnp.where(strict, jnp.where(lo, gy[:c], gx[:c]), 0.0))
        a_rb.append(jnp.where(incl, jnp.where(lo, gx[c:], gy[c:]), 0.0))
        a_rk.append(jnp.where(incl, jnp.where(lo, gy[c:], gx[c:]), 0.0))

    t = [eye + jnp.where(level_mask(1), a_ab[q], 0.0) for q in range(n)]
    for m in (2, 4, 8, 16, 32):
        x = [_dot(jnp.where(level_mask(m), a_ab[q], 0.0).astype(BF16), bd(t[q])) for q in range(n)]
        t = [t[q] + _dot(t[q].astype(BF16), bd(x[q])) for q in range(n)]

    v_sw = [bd_sw(v_p[q]) for q in range(n)]
    akv = [_dot(a_ak[q].astype(BF16), v_sw[q]) for q in range(n)]
    x = [_dot(t[q].astype(BF16), jnp.concatenate([bd(a_t[q]), bd(akv[q])], axis=1)) for q in range(n)]
    a_p = [x[q][:, :pw] for q in range(n)]
    u_0 = [x[q][:, pw:] for q in range(n)]
    zero_bd = jnp.zeros((2 * c, pw), BF16)
    d1 = [_dot(jnp.concatenate([a_rb[q], a_rk[q]], axis=1).astype(BF16),
               jnp.concatenate([jnp.concatenate([bd(a_p[q]), bd(u_0[q])], axis=1),
                                jnp.concatenate([zero_bd, v_sw[q]], axis=1)], axis=0)) for q in range(n)]
    d2 = []
    for q, (j, p) in enumerate(chains):
        wt = w_tot[j][:, p * pw:(p + 1) * pw]
        lhs = jnp.concatenate([jnp.concatenate([a_p[q], u_0[q]], axis=1),
                               jnp.concatenate([zeros, v_p[q]], axis=1)], axis=0).astype(BF16)
        rhs = jnp.concatenate([b_i[q] * wt, k_i[q] * wt], axis=0).astype(BF16)
        d2.append(_dot(lhs, rhs, _TN))

    s = [s_scr[p] for p in range(n_pair)]
    ys = [None] * n
    for j in range(n_chunk):
        for p in range(n_pair):
            q = j * n_pair + p
            mt = jnp.where(diag2, d2[q][:pw], 0.0).astype(BF16)
            nt = jnp.where(lo, d2[q][pw:pw + c], d2[q][pw + c:])
            ys[q] = d1[q][:, pw:] + _dot((r_t[q] + d1[q][:, :pw]).astype(BF16), bd(s[p]), _NT)
            s[p] = s[p] * w_tot[j][:, p * pw:(p + 1) * pw] + _dot(s[p].astype(BF16), mt) + nt
    for p in range(n_pair):
        s_scr[p] = s[p]

    y = jnp.concatenate([jnp.concatenate([ys[j * n_pair + p] for p in range(n_pair)], axis=1) for j in range(n_chunk)], axis=0)
    e = e_ref[...]
    inv_head = 1.0 / HEAD
    yc = y - _head_sum(y, e) * inv_head
    yn = yc * lax.rsqrt(_head_sum(yc * yc, e) * inv_head + LNX_EPS)
    y_ref[0] = ((yn * lng_ref[...] + lnb_ref[...] + bonus_ref[0]) * g_ref[0]).astype(BF16)


def _head_mask():
    head_id = jnp.arange(D_MIX) // HEAD
    return (head_id[:, None] == head_id[None, :]).astype(BF16)


def _rwkv(r, lw, k, v, kk, b, g, bonus, lnx_g, lnx_b, *, tb=256):
    bsz, s, _ = r.shape
    assert s % tb == 0 and tb % CHUNK == 0
    tok = pl.BlockSpec((1, tb, D_MIX), lambda bb, i: (bb, i, 0))
    vec = pl.BlockSpec((1, D_MIX), lambda bb, i: (0, 0))
    return pl.pallas_call(
        functools.partial(_rwkv_kernel, tb=tb),
        out_shape=jax.ShapeDtypeStruct((bsz, s, D_MIX), BF16),
        grid=(bsz, s // tb),
        in_specs=[tok] * 8 + [vec, vec, pl.BlockSpec((D_MIX, D_MIX), lambda bb, i: (0, 0))],
        out_specs=tok,
        scratch_shapes=[pltpu.VMEM((N_HEADS // 2, HEAD, 2 * HEAD), F32)],
        compiler_params=pltpu.CompilerParams(
            dimension_semantics=("parallel", "arbitrary"), vmem_limit_bytes=VMEM_LIMIT),
        name="rwkv",
    )(r, lw, k, v, kk, b, g, bonus, lnx_g.reshape(1, -1).astype(F32), lnx_b.reshape(1, -1).astype(F32), _head_mask())


def _fox_kernel(off_ref, thr_ref, q_ref, k_ref, v_ref, o_ref, s00, s01, s10, s11, m_scr, acc_scr, *, tq, tk, n_kblk):
    b, hp, qb = pl.program_id(0), pl.program_id(1), pl.program_id(2)
    kb_last = (qb * tq) // tk
    delta = qb * tq - kb_last * tk
    s_scr = ((s00, s01), (s10, s11))
    base = [(b * N_HEADS + 2 * hp + hh) * n_kblk for hh in range(2)]
    off_d = [off_ref[base[hh] + kb_last] for hh in range(2)]

    thr = thr_ref[0]

    def needed(kb):
        nxt = jnp.minimum(kb + 1, kb_last)
        return jnp.minimum(off_ref[base[0] + nxt] - off_d[0], off_ref[base[1] + nxt] - off_d[1]) <= thr

    kb_first = lax.while_loop(lambda kb: jnp.logical_and(kb > 0, needed(kb - 1)), lambda kb: kb - 1, kb_last)
    n_open = kb_last - kb_first

    def qk(hh, kb, slot):
        ksl = pl.ds(pl.multiple_of(kb * tk, tk), tk)
        s_scr[slot][hh][...] = _dot(k_ref[0, hh, ksl, :], q_ref[0, hh])

    def soft_pv(hh, kb, slot, masked):
        s = s_scr[slot][hh][...]
        if masked:
            key = lax.broadcasted_iota(jnp.int32, (tk, tq), 0)
            qry = lax.broadcasted_iota(jnp.int32, (tk, tq), 1)
            s = jnp.where(key - qry <= delta, s, NEG_BIG)
        off = off_ref[base[hh] + kb] - off_d[hh]
        m = m_scr[hh]
        m_new = jnp.maximum(m, jnp.max(s, axis=0, keepdims=True) - off)
        p = jnp.exp2(s - (m_new + off)).astype(BF16)
        ksl = pl.ds(pl.multiple_of(kb * tk, tk), tk)
        acc_scr[hh] = jnp.exp2(m - m_new) * acc_scr[hh] + _dot(v_ref[0, hh, :, ksl], p)
        m_scr[hh] = m_new

    m_scr[...] = jnp.full(m_scr.shape, NEG_BIG, F32)
    acc_scr[...] = jnp.zeros(acc_scr.shape, F32)
    qk(0, kb_first, 0)
    qk(1, kb_first, 0)

    def pair(i, _):
        kb = kb_first + 2 * i
        qk(0, kb + 1, 1)
        qk(1, kb + 1, 1)
        soft_pv(0, kb, 0, False)
        soft_pv(1, kb, 0, False)
        qk(0, kb + 2, 0)
        qk(1, kb + 2, 0)
        soft_pv(0, kb + 1, 1, False)
        soft_pv(1, kb + 1, 1, False)
        return 0

    lax.fori_loop(0, n_open // 2, pair, 0)

    @pl.when(n_open % 2 == 1)
    def _():
        qk(0, kb_last, 1)
        qk(1, kb_last, 1)
        soft_pv(0, kb_last - 1, 0, False)
        soft_pv(1, kb_last - 1, 0, False)
        soft_pv(0, kb_last, 1, True)
        soft_pv(1, kb_last, 1, True)

    @pl.when(n_open % 2 == 0)
    def _():
        soft_pv(0, kb_last, 0, True)
        soft_pv(1, kb_last, 0, True)

    a0, a1 = acc_scr[0], acc_scr[1]
    o_t = jnp.concatenate([a0[:HEAD] / a0[HEAD:HEAD + 1], a1[:HEAD] / a1[HEAD:HEAD + 1]], axis=0)
    o_ref[0] = o_t.T.astype(BF16)


def _fox(q_t, k_aug, v_t, off, q_norm, k_norm, *, tq=512, tk=FOX_KBLOCK):
    bsz, nh, s, _ = k_aug.shape
    n_kblk = s // tk
    assert s % tq == 0 and s % tk == 0 and tk % tq == 0 and nh % 2 == 0 and off.shape == (bsz * nh * n_kblk,)
    qk_bound = 1.02 * LOG2E * HEAD ** 0.5 * jnp.max(jnp.abs(q_norm)) * jnp.max(jnp.abs(k_norm))
    thr = (2.0 * qk_bound + EXP2_UNDERFLOW).reshape(1).astype(F32)
    return pl.pallas_call(
        functools.partial(_fox_kernel, tq=tq, tk=tk, n_kblk=n_kblk),
        out_shape=jax.ShapeDtypeStruct((bsz, s, nh * HEAD), BF16),
        grid=(bsz, nh // 2, s // tq),
        in_specs=[
            pl.BlockSpec(memory_space=pltpu.SMEM),
            pl.BlockSpec(memory_space=pltpu.SMEM),
            pl.BlockSpec((1, 2, 2 * HEAD, tq), lambda b, hp, i: (b, hp, 0, i)),
            pl.BlockSpec((1, 2, s, 2 * HEAD), lambda b, hp, i: (b, hp, 0, 0)),
            pl.BlockSpec((1, 2, V_ROWS, s), lambda b, hp, i: (b, hp, 0, 0)),
        ],
        out_specs=pl.BlockSpec((1, tq, 2 * HEAD), lambda b, hp, i: (b, i, hp)),
        scratch_shapes=[pltpu.VMEM((tk, tq), F32)] * 4 + [pltpu.VMEM((2, 1, tq), F32), pltpu.VMEM((2, V_ROWS, tq), F32)],
        compiler_params=pltpu.CompilerParams(
            dimension_semantics=("parallel", "parallel", "arbitrary"), vmem_limit_bytes=VMEM_LIMIT),
        name="fox",
    )(off, thr, q_t, k_aug, v_t)


def _merge_kernel(x_ref, g_ref, wg_ref, ya_ref, yb_ref, pa_ref, pb_ref, wo_ref, o_ref):
    x = x_ref[...]
    d = x.shape[-1]
    h = _rms(x, g_ref[...]).astype(BF16)
    gates = _sigmoid(_dot(h, wg_ref[...]))
    ya = _dot(ya_ref[...], pa_ref[...])
    yb = _dot(yb_ref[...], pb_ref[...])
    mix = (gates[:, :d] * ya + gates[:, d:] * yb).astype(BF16)
    o_ref[...] = x + _dot(mix, wo_ref[...])


def _merge(x2d, mix_norm, w_gates, ya, yb, p_a, p_b, w_out, *, tm=512):
    m, d = x2d.shape
    assert m % tm == 0
    full = lambda t: pl.BlockSpec(t.shape, lambda i: (0, 0))
    tok = lambda n: pl.BlockSpec((tm, n), lambda i: (i, 0))
    consts = [mix_norm.reshape(1, d).astype(F32), w_gates.astype(BF16)]
    mats = [p_a.astype(BF16), p_b.astype(BF16), w_out.astype(BF16)]
    return pl.pallas_call(
        _merge_kernel,
        out_shape=jax.ShapeDtypeStruct((m, d), F32),
        grid=(m // tm,),
        in_specs=[tok(d), full(consts[0]), full(consts[1]), tok(D_MIX), tok(D_MIX)] + [full(t) for t in mats],
        out_specs=tok(d),
        compiler_params=pltpu.CompilerParams(dimension_semantics=("parallel",), vmem_limit_bytes=VMEM_LIMIT),
        name="merge",
    )(x2d, consts[0], consts[1], ya, yb, *mats)


def kernel(x, ffn1_norm, ffn1_w1, ffn1_w3, ffn1_w2, mix_norm, w_in, shift_mu, rwkv_w0, rwkv_w_up, rwkv_a0, rwkv_a_up, rwkv_g_up, rwkv_k_k, rwkv_k_a, rwkv_r_k, rwkv_lnx_g, rwkv_lnx_b, rwkv_proj, fox_f_bias, fox_q_norm, fox_k_norm, fox_proj, w_out, ffn2_norm, ffn2_w1, ffn2_w3, ffn2_w2, final_norm):
    bsz, s, d = x.shape
    depth = ffn1_norm.shape[0]
    n_gate0 = w_in.shape[-1] - 2 * d
    x2d = x.reshape(bsz * s, d)
    for l in range(depth):
        x2d = _ffn(x2d, ffn1_norm[l], ffn1_w1[l], ffn1_w3[l], ffn1_w2[l], final_norm, final_norm=False)
        (r, lw, k, v, kk, b, g, bonus, fq_t, fk_aug, fv_t, off) = _in_proj(
            x2d.reshape(bsz, s, d), mix_norm[l], w_in[l], shift_mu[l], rwkv_w0[l], rwkv_w_up[l], rwkv_a0[l],
            rwkv_a_up[l], rwkv_g_up[l], rwkv_k_k[l], rwkv_k_a[l], rwkv_r_k[l], fox_f_bias[l], fox_q_norm[l],
            fox_k_norm[l])
        ya = _rwkv(r, lw, k, v, kk, b, g, bonus, rwkv_lnx_g[l], rwkv_lnx_b[l])
        off = off[:, ::FOX_KBLOCK // (s // off.shape[1]), 0, :N_HEADS]
        off = jnp.transpose(off, (0, 2, 1)).reshape(-1)
        yb = _fox(fq_t, fk_aug, fv_t, off, fox_q_norm[l], fox_k_norm[l])
        x2d = _merge(x2d, mix_norm[l], w_in[l][:, n_gate0:], ya.reshape(bsz * s, D_MIX), yb.reshape(bsz * s, D_MIX),
                     rwkv_proj[l], fox_proj[l], w_out[l])
        x2d = _ffn(x2d, ffn2_norm[l], ffn2_w1[l], ffn2_w3[l], ffn2_w2[l], final_norm, final_norm=(l == depth - 1))
    return x2d.reshape(bsz, s, d)
```

```python
import functools

import jax
import jax.numpy as jnp
from jax import lax
from jax.experimental import pallas as pl
from jax.experimental.pallas import tpu as pltpu

F32 = jnp.float32
BF16 = jnp.bfloat16

HEAD = 64
N_HEADS = 8
D_MIX = HEAD * N_HEADS
LANE = 128
MXU_DIM = 256
CHUNK = 64
RMS_EPS = 1e-6
LNX_EPS = 64e-5
NEG_BIG = -1e30
EXP2_UNDERFLOW = 160.0
VMEM_LIMIT = 56 * 1024 * 1024

_NT = (((1,), (1,)), ((), ()))
_TN = (((0,), (0,)), ((), ()))


def _dot(a, b, dims=None):
    if dims is None:
        return jnp.dot(a, b, preferred_element_type=F32)
    return lax.dot_general(a, b, dims, preferred_element_type=F32)


def _bdot(a, b, dims=None):
    return _dot(a.astype(BF16), b.astype(BF16), dims)


def _rms(x, g):
    return x * lax.rsqrt(jnp.mean(x * x, axis=-1, keepdims=True) + RMS_EPS) * g


def _softplus(z):
    return jnp.maximum(z, 0.0) + jnp.log(1.0 + jnp.exp(-jnp.abs(z)))


def _sigmoid(z):
    return 1.0 / (1.0 + jnp.exp(-z))


def _head_sum(x, e):
    xb = x.astype(BF16)
    return jnp.concatenate([_dot(xb[:, j:j + MXU_DIM], e) for j in range(0, x.shape[1], MXU_DIM)], axis=1)


def _cumsum_rows(tri, x):
    hi = x.astype(BF16)
    r1 = x - hi.astype(F32)
    mid = r1.astype(BF16)
    lo = (r1 - mid.astype(F32)).astype(BF16)
    return _dot(tri, hi) + _dot(tri, mid) + _dot(tri, lo)


def _ffn_kernel(x_ref, g_ref, w1_ref, w3_ref, w2_ref, gf_ref, o_ref, *, final_norm):
    x = x_ref[...]
    h = _rms(x, g_ref[...]).astype(BF16)
    a = _dot(h, w1_ref[...])
    b = _dot(h, w3_ref[...])
    u = (a * _sigmoid(a) * b).astype(BF16)
    y = x + 0.5 * _dot(u, w2_ref[...])
    if final_norm:
        y = _rms(y, gf_ref[...])
    o_ref[...] = y


def _resident(shape):
    return pl.BlockSpec(shape, lambda *_: (0,) * len(shape), pipeline_mode=pl.Buffered(1))


def _ffn(x2d, g, w1, w3, w2, gf, *, final_norm, tm=512):
    m, d = x2d.shape
    f_dim = w1.shape[1]
    assert m % tm == 0
    return pl.pallas_call(
        functools.partial(_ffn_kernel, final_norm=final_norm),
        out_shape=jax.ShapeDtypeStruct((m, d), F32),
        grid=(m // tm,),
        in_specs=[
            pl.BlockSpec((tm, d), lambda i: (i, 0)),
            _resident((1, d)), _resident((d, f_dim)), _resident((d, f_dim)), _resident((f_dim, d)), _resident((1, d)),
        ],
        out_specs=pl.BlockSpec((tm, d), lambda i: (i, 0)),
        compiler_params=pltpu.CompilerParams(dimension_semantics=("parallel",), vmem_limit_bytes=VMEM_LIMIT),
        name="ffn",
    )(x2d, g.reshape(1, d), w1.astype(BF16), w3.astype(BF16), w2.astype(BF16), gf.reshape(1, d))


N_SHIFT = 3 * D_MIX + 3 * LANE
N_PROJ = N_SHIFT + 3 * D_MIX + LANE
LOG2E = 1.4426950408889634
V_ROWS = HEAD + 16
FOX_KBLOCK = 512


def _in_proj_kernel(x_ref, g_ref, w_ref, mu_ref, w0_ref, wup_ref, a0_ref, aup_ref, gup_ref, kk_ref, ka_ref,
                    rk_ref, fb_ref, qn_ref, kn_ref, e_ref,
                    r_out, lw_out, k_out, v_out, kk_out, b_out, g_out, bonus_out, fq_out, fk_out, fv_out, off_out,
                    carry_p, carry_c, carry_b, *, tm, n_sub, tiles_per_kblock):
    step = pl.program_id(1)

    @pl.when(step == 0)
    def _():
        carry_p[...] = jnp.zeros_like(carry_p)
        carry_c[...] = jnp.zeros_like(carry_c)
        carry_b[...] = jnp.zeros_like(carry_b)

    outs = (r_out, lw_out, k_out, v_out, kk_out, b_out, g_out, bonus_out)
    for t in range(n_sub):
        rows = pl.ds(t * tm, tm)
        _in_proj_tile(step * n_sub + t, x_ref.at[0, rows], g_ref, w_ref, mu_ref, w0_ref, wup_ref, a0_ref, aup_ref, gup_ref,
                      kk_ref, ka_ref, rk_ref, fb_ref, qn_ref, kn_ref, e_ref, *[o.at[0, rows] for o in outs],
                      fq_out.at[0, :, :, rows], fk_out.at[0, :, rows], fv_out.at[0, :, :, rows], off_out.at[0, t],
                      carry_p, carry_c, carry_b, tm=tm, tiles_per_kblock=tiles_per_kblock)


def _in_proj_tile(i, x_ref, g_ref, w_ref, mu_ref, w0_ref, wup_ref, a0_ref, aup_ref, gup_ref, kk_ref, ka_ref,
                  rk_ref, fb_ref, qn_ref, kn_ref, e_ref,
                  r_out, lw_out, k_out, v_out, kk_out, b_out, g_out, bonus_out, fq_out, fk_out, fv_out, off_out,
                  carry_p, carry_c, carry_b, *, tm, tiles_per_kblock):
    h = _rms(x_ref[...], g_ref[...]).astype(BF16)
    p = _dot(h, w_ref[...])

    ps = p[:, :N_SHIFT]
    row = lax.broadcasted_iota(jnp.int32, (tm, 1), 0)
    prev = jnp.where(row == 0, carry_p[...], pltpu.roll(ps, 1, axis=0))
    carry_p[...] = ps[tm - 1:tm, :]
    sh = ps + (prev - ps) * mu_ref[...]

    e = e_ref[...]
    r = sh[:, 0:D_MIX]
    k = sh[:, D_MIX:2 * D_MIX]
    v = sh[:, 2 * D_MIX:3 * D_MIX]
    w_lo = sh[:, 3 * D_MIX:3 * D_MIX + LANE]
    a_lo = sh[:, 3 * D_MIX + LANE:3 * D_MIX + 2 * LANE]
    g_lo = sh[:, 3 * D_MIX + 2 * LANE:N_SHIFT]

    w_log = -_softplus(-(w0_ref[...] + _bdot(jnp.tanh(w_lo), wup_ref[...]))) - 0.5
    lw = -jnp.exp(w_log)
    a = _sigmoid(a0_ref[...] + _bdot(a_lo, aup_ref[...]))
    g = _bdot(_sigmoid(g_lo), gup_ref[...])
    kk = k * kk_ref[...]
    kk = kk * lax.rsqrt(jnp.maximum(_head_sum(kk * kk, e), 1e-24))
    k_mod = k * (1.0 + (a - 1.0) * ka_ref[...])
    bonus = _head_sum(r * k_mod * rk_ref[...], e) * v

    r_out[...] = r
    lw_out[...] = lw
    k_out[...] = k_mod
    v_out[...] = v.astype(BF16)
    kk_out[...] = kk
    b_out[...] = kk * a
    g_out[...] = g.astype(BF16)
    bonus_out[...] = bonus.astype(BF16)

    fq = p[:, N_SHIFT:N_SHIFT + D_MIX]
    fk = p[:, N_SHIFT + D_MIX:N_SHIFT + 2 * D_MIX]
    fv = p[:, N_SHIFT + 2 * D_MIX:N_SHIFT + 3 * D_MIX]
    f_lo = p[:, N_SHIFT + 3 * D_MIX:N_PROJ]
    inv_head = 1.0 / HEAD
    qn = fq * lax.rsqrt(_head_sum(fq * fq, e) * inv_head + RMS_EPS) * qn_ref[...] * (LOG2E * HEAD ** -0.5)
    kn = fk * lax.rsqrt(_head_sum(fk * fk, e) * inv_head + RMS_EPS) * kn_ref[...]
    z = f_lo + fb_ref[...]
    log_f = jnp.minimum(z, 0.0) - jnp.log(1.0 + jnp.exp(-jnp.abs(z)))
    tri = jnp.where(lax.broadcasted_iota(jnp.int32, (tm, tm), 1) <= lax.broadcasted_iota(jnp.int32, (tm, tm), 0),
                    1.0, 0.0).astype(BF16)
    c_loc = _cumsum_rows(tri, log_f)
    off_out[...] = carry_c[...] * LOG2E
    in_blk = jnp.where(i % tiles_per_kblock == 0, 0.0, carry_b[...])
    carry_c[...] = carry_c[...] + c_loc[tm - 1:tm, :]
    carry_b[...] = in_blk + c_loc[tm - 1:tm, :]
    c2 = (c_loc + in_blk) * LOG2E

    lane = lax.broadcasted_iota(jnp.int32, (tm, HEAD), 1)
    sub = lax.broadcasted_iota(jnp.int32, (HEAD, tm), 0)
    q_tail = jnp.where(sub < 3, -1.0, 0.0)
    v_tail = jnp.where(lax.broadcasted_iota(jnp.int32, (V_ROWS - HEAD, tm), 0) == 0, 1.0, 0.0)
    for pr in range(N_HEADS // 2):
        ps2 = slice(pr * 2 * HEAD, (pr + 1) * 2 * HEAD)
        q_t = qn[:, ps2].T
        v_t = fv[:, ps2].T
        for half in range(2):
            hh = 2 * pr + half
            rows = slice(half * HEAD, (half + 1) * HEAD)
            fq_out[hh] = jnp.concatenate([q_t[rows], q_tail], axis=0).astype(BF16)
            fv_out[hh] = jnp.concatenate([v_t[rows], v_tail], axis=0).astype(BF16)
            c_h = c2[:, hh:hh + 1]
            c_hi = c_h.astype(BF16).astype(F32)
            c_mid = (c_h - c_hi).astype(BF16).astype(F32)
            c_lo = c_h - c_hi - c_mid
            c_cols = jnp.where(lane == 0, c_hi, jnp.where(lane == 1, c_mid, jnp.where(lane == 2, c_lo, 0.0)))
            fk_out[hh] = jnp.concatenate([kn[:, hh * HEAD:(hh + 1) * HEAD], c_cols], axis=1).astype(BF16)


def _pad_cols(w, n):
    return jnp.pad(w, ((0, 0), (0, n - w.shape[1])))


def _pad_rows(w, n):
    return jnp.pad(w, ((0, n - w.shape[0]), (0, 0)))


def _in_proj(x, mix_norm, w_in, shift_mu, w0, w_up, a0, a_up, g_up, k_k, k_a, r_k, f_bias, q_norm, k_norm, *, tm=256, n_sub=2):
    bsz, s, d = x.shape
    lw_, la_, lg_ = w_up.shape[0], a_up.shape[0], g_up.shape[0]
    c0 = 3 * D_MIX
    c1 = c0 + lw_ + la_ + lg_
    c2 = c1 + 3 * D_MIX
    c3 = c2 + N_HEADS

    def seg(t, pad):
        return jnp.concatenate([
            t[..., :c0],
            pad(t[..., c0:c0 + lw_], LANE), pad(t[..., c0 + lw_:c0 + lw_ + la_], LANE), pad(t[..., c0 + lw_ + la_:c1], LANE),
        ], axis=-1)

    w_cat = jnp.concatenate([seg(w_in, _pad_cols), w_in[:, c1:c2], _pad_cols(w_in[:, c2:c3], LANE)], axis=-1).astype(BF16)
    mu_cat = seg(shift_mu.reshape(1, -1), _pad_cols)
    row = lambda t: t.reshape(1, -1).astype(F32)
    e = _head_mask()
    tile = lambda t: jnp.tile(t.reshape(1, HEAD), (1, N_HEADS)).astype(F32)
    consts = [
        row(mix_norm), w_cat, mu_cat, row(w0), _pad_rows(w_up, LANE).astype(BF16), row(a0),
        _pad_rows(a_up, LANE).astype(BF16), _pad_rows(g_up, LANE).astype(BF16), row(k_k), row(k_a), row(r_k),
        _pad_cols(row(f_bias), LANE), tile(q_norm), tile(k_norm), e,
    ]
    const_specs = [_resident(c.shape) for c in consts]
    tm, tile_rows = tm * n_sub, tm
    tok = lambda: pl.BlockSpec((1, tm, D_MIX), lambda b, i: (b, i, 0))
    f32_tok = jax.ShapeDtypeStruct((bsz, s, D_MIX), F32)
    bf_tok = jax.ShapeDtypeStruct((bsz, s, D_MIX), BF16)
    assert s % tm == 0 and FOX_KBLOCK % tile_rows == 0
    return pl.pallas_call(
        functools.partial(_in_proj_kernel, tm=tile_rows, n_sub=n_sub, tiles_per_kblock=FOX_KBLOCK // tile_rows),
        out_shape=[f32_tok, f32_tok, f32_tok, bf_tok, f32_tok, f32_tok, bf_tok, bf_tok] + [
            jax.ShapeDtypeStruct((bsz, N_HEADS, 2 * HEAD, s), BF16),
            jax.ShapeDtypeStruct((bsz, N_HEADS, s, 2 * HEAD), BF16),
            jax.ShapeDtypeStruct((bsz, N_HEADS, V_ROWS, s), BF16),
            jax.ShapeDtypeStruct((bsz, s // tile_rows, 1, LANE), F32),
        ],
        grid=(bsz, s // tm),
        in_specs=[pl.BlockSpec((1, tm, d), lambda b, i: (b, i, 0))] + const_specs,
        out_specs=[tok() for _ in range(8)] + [
            pl.BlockSpec((1, N_HEADS, 2 * HEAD, tm), lambda b, i: (b, 0, 0, i)),
            pl.BlockSpec((1, N_HEADS, tm, 2 * HEAD), lambda b, i: (b, 0, i, 0)),
            pl.BlockSpec((1, N_HEADS, V_ROWS, tm), lambda b, i: (b, 0, 0, i)),
            pl.BlockSpec((1, n_sub, 1, LANE), lambda b, i: (b, i, 0, 0)),
        ],
        scratch_shapes=[pltpu.VMEM((1, N_SHIFT), F32), pltpu.VMEM((1, LANE), F32), pltpu.VMEM((1, LANE), F32)],
        compiler_params=pltpu.CompilerParams(
            dimension_semantics=("parallel", "arbitrary"), vmem_limit_bytes=VMEM_LIMIT),
        name="in_proj",
    )(x, *consts)


def _rwkv_kernel(r_ref, lw_ref, k_ref, v_ref, kk_ref, b_ref, g_ref, bonus_ref, lng_ref, lnb_ref, e_ref, y_ref, s_scr, *, tb):
    i = pl.program_id(1)

    @pl.when(i == 0)
    def _():
        s_scr[...] = jnp.zeros_like(s_scr)

    c = CHUNK
    pw = 2 * HEAD
    n_chunk = tb // c
    n_pair = N_HEADS // 2
    rr = lax.broadcasted_iota(jnp.int32, (c, pw), 0)
    ll = lax.broadcasted_iota(jnp.int32, (c, pw), 1)
    cc = ll % c
    lo = ll < c
    strict = cc < rr
    incl = cc <= rr
    eye = (cc == rr).astype(F32)
    r2 = lax.broadcasted_iota(jnp.int32, (2 * c, pw), 0)
    l2 = lax.broadcasted_iota(jnp.int32, (2 * c, pw), 1)
    lo2 = l2 < c
    diag2 = (r2 < c) == lo2
    zeros = jnp.zeros((c, pw), F32)

    def level_mask(n):
        return ((rr // (2 * n)) == (cc // (2 * n))) & (((rr // n) % 2) == 1) & (((cc // n) % 2) == 0)

    def bd(x):
        return jnp.where(diag2, jnp.concatenate([x, x], axis=0), 0.0).astype(BF16)

    def bd_sw(x):
        return jnp.where(diag2, 0.0, jnp.concatenate([x, x], axis=0)).astype(BF16)

    rb = lax.broadcasted_iota(jnp.int32, (tb, tb), 0)
    cb = lax.broadcasted_iota(jnp.int32, (tb, tb), 1)
    tri = jnp.where((cb <= rb) & ((rb // c) == (cb // c)), 1.0, 0.0).astype(BF16)
    lw = lw_ref[0]
    cs = _cumsum_rows(tri, lw)
    w_inv = jnp.exp(-cs)
    rt = r_ref[0] * jnp.exp(cs)
    at = -kk_ref[0] * jnp.exp(cs - lw)
    bi = b_ref[0] * w_inv
    ki = k_ref[0] * w_inv
    v = v_ref[0].astype(F32)

    chains = [(j, p) for j in range(n_chunk) for p in range(n_pair)]
    cut = lambda x, j, p: x[j * c:(j + 1) * c, p * pw:(p + 1) * pw]
    w_tot = [jnp.exp(cs[(j + 1) * c - 1:(j + 1) * c, :]) for j in range(n_chunk)]
    a_t = [cut(at, j, p) for j, p in chains]
    r_t = [cut(rt, j, p) for j, p in chains]
    b_i = [cut(bi, j, p) for j, p in chains]
    k_i = [cut(ki, j, p) for j, p in chains]
    v_p = [cut(v, j, p) for j, p in chains]
    n = len(chains)

    a_ab, a_ak, a_rb, a_rk = [], [], [], []
    for q in range(n):
        ar = jnp.concatenate([a_t[q], r_t[q]], axis=0)
        gx = _dot(jnp.where(lo2, ar, 0.0).astype(BF16), jnp.concatenate([b_i[q], k_i[q]], axis=0).astype(BF16), _NT)
        gy = _dot(jnp.where(lo2, 0.0, ar).astype(BF16), jnp.concatenate([k_i[q], b_i[q]], axis=0).astype(BF16), _NT)
        a_ab.append(jnp.where(strict, jnp.where(lo, gx[:c], gy[:c]), 0.0))
        a_ak.append(jnp.where(strict, jnp.where(lo, gy[:c], gx[:c]), 0.0))
        a_rb.append(jnp.where(incl, jnp.where(lo, gx[c:], gy[c:]), 0.0))
        a_rk.append(jnp.where(incl, jnp.where(lo, gy[c:], gx[c:]), 0.0))

    t = [eye + jnp.where(level_mask(1), a_ab[q], 0.0) for q in range(n)]
    for m in (2, 4, 8, 16, 32):
        x = [_dot(jnp.where(level_mask(m), a_ab[q], 0.0).astype(BF16), bd(t[q])) for q in range(n)]
        t = [t[q] + _dot(t[q].astype(BF16), bd(x[q])) for q in range(n)]

    v_sw = [bd_sw(v_p[q]) for q in range(n)]
    akv = [_dot(a_ak[q].astype(BF16), v_sw[q]) for q in range(n)]
    x = [_dot(t[q].astype(BF16), jnp.concatenate([bd(a_t[q]), bd(akv[q])], axis=1)) for q in range(n)]
    a_p = [x[q][:, :pw] for q in range(n)]
    u_0 = [x[q][:, pw:] for q in range(n)]
    zero_bd = jnp.zeros((2 * c, pw), BF16)
    d1 = [_dot(jnp.concatenate([a_rb[q], a_rk[q]], axis=1).astype(BF16),
               jnp.concatenate([jnp.concatenate([bd(a_p[q]), bd(u_0[q])], axis=1),
                                jnp.concatenate([zero_bd, v_sw[q]], axis=1)], axis=0)) for q in range(n)]
    d2 = []
    for q, (j, p) in enumerate(chains):
        wt = w_tot[j][:, p * pw:(p + 1) * pw]
        lhs = jnp.concatenate([jnp.concatenate([a_p[q], u_0[q]], axis=1),
                               jnp.concatenate([zeros, v_p[q]], axis=1)], axis=0).astype(BF16)
        rhs = jnp.concatenate([b_i[q] * wt, k_i[q] * wt], axis=0).astype(BF16)
        d2.append(_dot(lhs, rhs, _TN))

    s = [s_scr[p] for p in range(n_pair)]
    ys = [None] * n
    for j in range(n_chunk):
        for p in range(n_pair):
            q = j * n_pair + p
            mt = jnp.where(diag2, d2[q][:pw], 0.0).astype(BF16)
            nt = jnp.where(lo, d2[q][pw:pw + c], d2[q][pw + c:])
            ys[q] = d1[q][:, pw:] + _dot((r_t[q] + d1[q][:, :pw]).astype(BF16), bd(s[p]), _NT)
            s[p] = s[p] * w_tot[j][:, p * pw:(p + 1) * pw] + _dot(s[p].astype(BF16), mt) + nt
    for p in range(n_pair):
        s_scr[p] = s[p]

    y = jnp.concatenate([jnp.concatenate([ys[j * n_pair + p] for p in range(n_pair)], axis=1) for j in range(n_chunk)], axis=0)
    e = e_ref[...]
    inv_head = 1.0 / HEAD
    yc = y - _head_sum(y, e) * inv_head
    yn = yc * lax.rsqrt(_head_sum(yc * yc, e) * inv_head + LNX_EPS)
    y_ref[0] = ((yn * lng_ref[...] + lnb_ref[...] + bonus_ref[0]) * g_ref[0]).astype(BF16)


def _head_mask():
    head_id = jnp.arange(MXU_DIM) // HEAD
    return (head_id[:, None] == head_id[None, :]).astype(BF16)


def _rwkv(r, lw, k, v, kk, b, g, bonus, lnx_g, lnx_b, *, tb=256):
    bsz, s, _ = r.shape
    assert s % tb == 0 and tb % CHUNK == 0
    tok = pl.BlockSpec((1, tb, D_MIX), lambda bb, i: (bb, i, 0))
    vec = pl.BlockSpec((1, D_MIX), lambda bb, i: (0, 0))
    return pl.pallas_call(
        functools.partial(_rwkv_kernel, tb=tb),
        out_shape=jax.ShapeDtypeStruct((bsz, s, D_MIX), BF16),
        grid=(bsz, s // tb),
        in_specs=[tok] * 8 + [vec, vec, pl.BlockSpec((MXU_DIM, MXU_DIM), lambda bb, i: (0, 0))],
        out_specs=tok,
        scratch_shapes=[pltpu.VMEM((N_HEADS // 2, HEAD, 2 * HEAD), F32)],
        compiler_params=pltpu.CompilerParams(
            dimension_semantics=("parallel", "arbitrary"), vmem_limit_bytes=VMEM_LIMIT),
        name="rwkv",
    )(r, lw, k, v, kk, b, g, bonus, lnx_g.reshape(1, -1).astype(F32), lnx_b.reshape(1, -1).astype(F32), _head_mask())


def _fox_kernel(off_ref, thr_ref, q_ref, k_ref, v_ref, o_ref, s00, s01, s10, s11, m_scr, acc_scr, *, tq, tk, n_kblk):
    b, hp, qb = pl.program_id(0), pl.program_id(1), pl.program_id(2)
    kb_last = (qb * tq) // tk
    delta = qb * tq - kb_last * tk
    s_scr = ((s00, s01), (s10, s11))
    base = [(b * N_HEADS + 2 * hp + hh) * n_kblk for hh in range(2)]
    off_d = [off_ref[base[hh] + kb_last] for hh in range(2)]

    thr = thr_ref[0]

    def needed(kb):
        nxt = jnp.minimum(kb + 1, kb_last)
        return jnp.minimum(off_ref[base[0] + nxt] - off_d[0], off_ref[base[1] + nxt] - off_d[1]) <= thr

    kb_first = lax.while_loop(lambda kb: jnp.logical_and(kb > 0, needed(kb - 1)), lambda kb: kb - 1, kb_last)
    n_open = kb_last - kb_first

    def qk(hh, kb, slot):
        ksl = pl.ds(pl.multiple_of(kb * tk, tk), tk)
        s_scr[slot][hh][...] = _dot(k_ref[0, hh, ksl, :], q_ref[0, hh])

    def soft_pv(hh, kb, slot, masked):
        s = s_scr[slot][hh][...]
        if masked:
            key = lax.broadcasted_iota(jnp.int32, (tk, tq), 0)
            qry = lax.broadcasted_iota(jnp.int32, (tk, tq), 1)
            s = jnp.where(key - qry <= delta, s, NEG_BIG)
        off = off_ref[base[hh] + kb] - off_d[hh]
        m = m_scr[hh]
        m_new = jnp.maximum(m, jnp.max(s, axis=0, keepdims=True) - off)
        p = jnp.exp2((s - (m_new + off)).astype(BF16))
        ksl = pl.ds(pl.multiple_of(kb * tk, tk), tk)
        acc_scr[hh] = jnp.exp2(m - m_new) * acc_scr[hh] + _dot(v_ref[0, hh, :, ksl], p)
        m_scr[hh] = m_new

    m_scr[...] = jnp.full(m_scr.shape, NEG_BIG, F32)
    acc_scr[...] = jnp.zeros(acc_scr.shape, F32)
    qk(0, kb_first, 0)
    qk(1, kb_first, 0)

    def pair(i, _):
        kb = kb_first + 2 * i
        qk(0, kb + 1, 1)
        qk(1, kb + 1, 1)
        soft_pv(0, kb, 0, False)
        soft_pv(1, kb, 0, False)
        qk(0, kb + 2, 0)
        qk(1, kb + 2, 0)
        soft_pv(0, kb + 1, 1, False)
        soft_pv(1, kb + 1, 1, False)
        return 0

    lax.fori_loop(0, n_open // 2, pair, 0)

    @pl.when(n_open % 2 == 1)
    def _():
        qk(0, kb_last, 1)
        qk(1, kb_last, 1)
        soft_pv(0, kb_last - 1, 0, False)
        soft_pv(1, kb_last - 1, 0, False)
        soft_pv(0, kb_last, 1, True)
        soft_pv(1, kb_last, 1, True)

    @pl.when(n_open % 2 == 0)
    def _():
        soft_pv(0, kb_last, 0, True)
        soft_pv(1, kb_last, 0, True)

    a0, a1 = acc_scr[0], acc_scr[1]
    o_t = jnp.concatenate([a0[:HEAD] / a0[HEAD:HEAD + 1], a1[:HEAD] / a1[HEAD:HEAD + 1]], axis=0)
    o_ref[0] = o_t.T.astype(BF16)


def _fox(q_t, k_aug, v_t, off, q_norm, k_norm, *, tq=512, tk=FOX_KBLOCK):
    bsz, nh, s, _ = k_aug.shape
    n_kblk = s // tk
    assert s % tq == 0 and s % tk == 0 and tk % tq == 0 and nh % 2 == 0 and off.shape == (bsz * nh * n_kblk,)
    qk_bound = 1.02 * LOG2E * HEAD ** 0.5 * jnp.max(jnp.abs(q_norm)) * jnp.max(jnp.abs(k_norm))
    thr = (2.0 * qk_bound + EXP2_UNDERFLOW).reshape(1).astype(F32)
    return pl.pallas_call(
        functools.partial(_fox_kernel, tq=tq, tk=tk, n_kblk=n_kblk),
        out_shape=jax.ShapeDtypeStruct((bsz, s, nh * HEAD), BF16),
        grid=(bsz, nh // 2, s // tq),
        in_specs=[
            pl.BlockSpec(memory_space=pltpu.SMEM),
            pl.BlockSpec(memory_space=pltpu.SMEM),
            pl.BlockSpec((1, 2, 2 * HEAD, tq), lambda b, hp, i: (b, hp, 0, i)),
            pl.BlockSpec((1, 2, s, 2 * HEAD), lambda b, hp, i: (b, hp, 0, 0)),
            pl.BlockSpec((1, 2, V_ROWS, s), lambda b, hp, i: (b, hp, 0, 0)),
        ],
        out_specs=pl.BlockSpec((1, tq, 2 * HEAD), lambda b, hp, i: (b, i, hp)),
        scratch_shapes=[pltpu.VMEM((tk, tq), F32)] * 4 + [pltpu.VMEM((2, 1, tq), F32), pltpu.VMEM((2, V_ROWS, tq), F32)],
        compiler_params=pltpu.CompilerParams(
            dimension_semantics=("parallel", "parallel", "arbitrary"), vmem_limit_bytes=VMEM_LIMIT),
        name="fox",
    )(off, thr, q_t, k_aug, v_t)


def _merge_kernel(x_ref, g_ref, wg_ref, ya_ref, yb_ref, pa_ref, pb_ref, wo_ref, o_ref):
    x = x_ref[...]
    d = x.shape[-1]
    h = _rms(x, g_ref[...]).astype(BF16)
    gates = _sigmoid(_dot(h, wg_ref[...]))
    ya = _dot(ya_ref[...], pa_ref[...])
    yb = _dot(yb_ref[...], pb_ref[...])
    mix = (gates[:, :d] * ya + gates[:, d:] * yb).astype(BF16)
    o_ref[...] = x + _dot(mix, wo_ref[...])


def _merge(x2d, mix_norm, w_gates, ya, yb, p_a, p_b, w_out, *, tm=512):
    m, d = x2d.shape
    assert m % tm == 0
    full = lambda t: pl.BlockSpec(t.shape, lambda i: (0, 0))
    tok = lambda n: pl.BlockSpec((tm, n), lambda i: (i, 0))
    consts = [mix_norm.reshape(1, d).astype(F32), w_gates.astype(BF16)]
    mats = [p_a.astype(BF16), p_b.astype(BF16), w_out.astype(BF16)]
    return pl.pallas_call(
        _merge_kernel,
        out_shape=jax.ShapeDtypeStruct((m, d), F32),
        grid=(m // tm,),
        in_specs=[tok(d), full(consts[0]), full(consts[1]), tok(D_MIX), tok(D_MIX)] + [full(t) for t in mats],
        out_specs=tok(d),
        compiler_params=pltpu.CompilerParams(dimension_semantics=("parallel",), vmem_limit_bytes=VMEM_LIMIT),
        name="merge",
    )(x2d, consts[0], consts[1], ya, yb, *mats)


def kernel(x, ffn1_norm, ffn1_w1, ffn1_w3, ffn1_w2, mix_norm, w_in, shift_mu, rwkv_w0, rwkv_w_up, rwkv_a0, rwkv_a_up, rwkv_g_up, rwkv_k_k, rwkv_k_a, rwkv_r_k, rwkv_lnx_g, rwkv_lnx_b, rwkv_proj, fox_f_bias, fox_q_norm, fox_k_norm, fox_proj, w_out, ffn2_norm, ffn2_w1, ffn2_w3, ffn2_w2, final_norm):
    bsz, s, d = x.shape
    depth = ffn1_norm.shape[0]
    n_gate0 = w_in.shape[-1] - 2 * d
    x2d = x.reshape(bsz * s, d)
    for l in range(depth):
        x2d = _ffn(x2d, ffn1_norm[l], ffn1_w1[l], ffn1_w3[l], ffn1_w2[l], final_norm, final_norm=False)
        (r, lw, k, v, kk, b, g, bonus, fq_t, fk_aug, fv_t, off) = _in_proj(
            x2d.reshape(bsz, s, d), mix_norm[l], w_in[l], shift_mu[l], rwkv_w0[l], rwkv_w_up[l], rwkv_a0[l],
            rwkv_a_up[l], rwkv_g_up[l], rwkv_k_k[l], rwkv_k_a[l], rwkv_r_k[l], fox_f_bias[l], fox_q_norm[l],
            fox_k_norm[l])
        ya = _rwkv(r, lw, k, v, kk, b, g, bonus, rwkv_lnx_g[l], rwkv_lnx_b[l])
        off = off[:, ::FOX_KBLOCK // (s // off.shape[1]), 0, :N_HEADS]
        off = jnp.transpose(off, (0, 2, 1)).reshape(-1)
        yb = _fox(fq_t, fk_aug, fv_t, off, fox_q_norm[l], fox_k_norm[l])
        x2d = _merge(x2d, mix_norm[l], w_in[l][:, n_gate0:], ya.reshape(bsz * s, D_MIX), yb.reshape(bsz * s, D_MIX),
                     rwkv_proj[l], fox_proj[l], w_out[l])
        x2d = _ffn(x2d, ffn2_norm[l], ffn2_w1[l], ffn2_w3[l], ffn2_w2[l], final_norm, final_norm=(l == depth - 1))
    return x2d.reshape(bsz, s, d)
```

```python
import functools

import jax
import jax.numpy as jnp
from jax import lax
from jax.experimental import pallas as pl
from jax.experimental.pallas import tpu as pltpu

F32 = jnp.float32
BF16 = jnp.bfloat16

HEAD = 64
N_HEADS = 8
D_MIX = HEAD * N_HEADS
LANE = 128
MXU_DIM = 256
CHUNK = 64
RMS_EPS = 1e-6
LNX_EPS = 64e-5
NEG_BIG = -1e30
EXP2_UNDERFLOW = 160.0
VMEM_LIMIT = 56 * 1024 * 1024

_NT = (((1,), (1,)), ((), ()))
_TN = (((0,), (0,)), ((), ()))


def _dot(a, b, dims=None):
    if dims is None:
        return jnp.dot(a, b, preferred_element_type=F32)
    return lax.dot_general(a, b, dims, preferred_element_type=F32)


def _bdot(a, b, dims=None):
    return _dot(a.astype(BF16), b.astype(BF16), dims)


def _rms(x, g):
    return x * lax.rsqrt(jnp.mean(x * x, axis=-1, keepdims=True) + RMS_EPS) * g


def _softplus(z):
    return jnp.maximum(z, 0.0) + jnp.log(1.0 + jnp.exp(-jnp.abs(z)))


def _sigmoid(z):
    return 1.0 / (1.0 + jnp.exp(-z))


def _head_sum(x, e):
    xb = x.astype(BF16)
    return jnp.concatenate([_dot(xb[:, j:j + MXU_DIM], e) for j in range(0, x.shape[1], MXU_DIM)], axis=1)


def _cumsum_rows(tri, x):
    hi = x.astype(BF16)
    r1 = x - hi.astype(F32)
    mid = r1.astype(BF16)
    lo = (r1 - mid.astype(F32)).astype(BF16)
    return _dot(tri, hi) + _dot(tri, mid) + _dot(tri, lo)


def _ffn_kernel(x_ref, g_ref, w1_ref, w3_ref, w2_ref, gf_ref, o_ref, *, final_norm):
    x = x_ref[...]
    h = _rms(x, g_ref[...]).astype(BF16)
    a = _dot(h, w1_ref[...])
    b = _dot(h, w3_ref[...])
    u = (a * _sigmoid(a) * b).astype(BF16)
    y = x + 0.5 * _dot(u, w2_ref[...])
    if final_norm:
        y = _rms(y, gf_ref[...])
    o_ref[...] = y


def _resident(shape):
    return pl.BlockSpec(shape, lambda *_: (0,) * len(shape), pipeline_mode=pl.Buffered(1))


def _ffn(x2d, g, w1, w3, w2, gf, *, final_norm, tm=512):
    m, d = x2d.shape
    f_dim = w1.shape[1]
    assert m % tm == 0
    return pl.pallas_call(
        functools.partial(_ffn_kernel, final_norm=final_norm),
        out_shape=jax.ShapeDtypeStruct((m, d), F32),
        grid=(m // tm,),
        in_specs=[
            pl.BlockSpec((tm, d), lambda i: (i, 0)),
            _resident((1, d)), _resident((d, f_dim)), _resident((d, f_dim)), _resident((f_dim, d)), _resident((1, d)),
        ],
        out_specs=pl.BlockSpec((tm, d), lambda i: (i, 0)),
        compiler_params=pltpu.CompilerParams(dimension_semantics=("parallel",), vmem_limit_bytes=VMEM_LIMIT),
        name="ffn",
    )(x2d, g.reshape(1, d), w1.astype(BF16), w3.astype(BF16), w2.astype(BF16), gf.reshape(1, d))


N_SHIFT = 3 * D_MIX + 3 * LANE
N_PROJ = N_SHIFT + 3 * D_MIX + LANE
LOG2E = 1.4426950408889634
V_ROWS = HEAD + 16
FOX_KBLOCK = 512


def _in_proj_kernel(x_ref, g_ref, w_ref, mu_ref, w0_ref, wup_ref, a0_ref, aup_ref, gup_ref, kk_ref, ka_ref,
                    rk_ref, fb_ref, qn_ref, kn_ref, e_ref,
                    r_out, lw_out, k_out, v_out, kk_out, b_out, g_out, bonus_out, fq_out, fk_out, fv_out, off_out,
                    carry_p, carry_c, carry_b, *, tm, n_sub, tiles_per_kblock):
    step = pl.program_id(1)

    @pl.when(step == 0)
    def _():
        carry_p[...] = jnp.zeros_like(carry_p)
        carry_c[...] = jnp.zeros_like(carry_c)
        carry_b[...] = jnp.zeros_like(carry_b)

    outs = (r_out, lw_out, k_out, v_out, kk_out, b_out, g_out, bonus_out)
    for t in range(n_sub):
        rows = pl.ds(t * tm, tm)
        _in_proj_tile(step * n_sub + t, x_ref.at[0, rows], g_ref, w_ref, mu_ref, w0_ref, wup_ref, a0_ref, aup_ref, gup_ref,
                      kk_ref, ka_ref, rk_ref, fb_ref, qn_ref, kn_ref, e_ref, *[o.at[0, rows] for o in outs],
                      fq_out.at[0, :, :, rows], fk_out.at[0, :, rows], fv_out.at[0, :, :, rows], off_out.at[0, t],
                      carry_p, carry_c, carry_b, tm=tm, tiles_per_kblock=tiles_per_kblock)


def _in_proj_tile(i, x_ref, g_ref, w_ref, mu_ref, w0_ref, wup_ref, a0_ref, aup_ref, gup_ref, kk_ref, ka_ref,
                  rk_ref, fb_ref, qn_ref, kn_ref, e_ref,
                  r_out, lw_out, k_out, v_out, kk_out, b_out, g_out, bonus_out, fq_out, fk_out, fv_out, off_out,
                  carry_p, carry_c, carry_b, *, tm, tiles_per_kblock):
    h = _rms(x_ref[...], g_ref[...]).astype(BF16)
    p = _dot(h, w_ref[...])

    ps = p[:, :N_SHIFT]
    row = lax.broadcasted_iota(jnp.int32, (tm, 1), 0)
    prev = jnp.where(row == 0, carry_p[...], pltpu.roll(ps, 1, axis=0))
    carry_p[...] = ps[tm - 1:tm, :]
    sh = ps + (prev - ps) * mu_ref[...]

    e = e_ref[...]
    r = sh[:, 0:D_MIX]
    k = sh[:, D_MIX:2 * D_MIX]
    v = sh[:, 2 * D_MIX:3 * D_MIX]
    w_lo = sh[:, 3 * D_MIX:3 * D_MIX + LANE]
    a_lo = sh[:, 3 * D_MIX + LANE:3 * D_MIX + 2 * LANE]
    g_lo = sh[:, 3 * D_MIX + 2 * LANE:N_SHIFT]

    w_log = -_softplus(-(w0_ref[...] + _bdot(jnp.tanh(w_lo), wup_ref[...]))) - 0.5
    lw = -jnp.exp(w_log)
    a = _sigmoid(a0_ref[...] + _bdot(a_lo, aup_ref[...]))
    g = _bdot(_sigmoid(g_lo), gup_ref[...])
    kk = k * kk_ref[...]
    kk = kk * lax.rsqrt(jnp.maximum(_head_sum(kk * kk, e), 1e-24))
    k_mod = k * (1.0 + (a - 1.0) * ka_ref[...])
    bonus = _head_sum(r * k_mod * rk_ref[...], e) * v

    r_out[...] = r
    lw_out[...] = lw
    k_out[...] = k_mod
    v_out[...] = v.astype(BF16)
    kk_out[...] = kk
    b_out[...] = kk * a
    g_out[...] = g.astype(BF16)
    bonus_out[...] = bonus.astype(BF16)

    fq = p[:, N_SHIFT:N_SHIFT + D_MIX]
    fk = p[:, N_SHIFT + D_MIX:N_SHIFT + 2 * D_MIX]
    fv = p[:, N_SHIFT + 2 * D_MIX:N_SHIFT + 3 * D_MIX]
    f_lo = p[:, N_SHIFT + 3 * D_MIX:N_PROJ]
    inv_head = 1.0 / HEAD
    qn = fq * lax.rsqrt(_head_sum(fq * fq, e) * inv_head + RMS_EPS) * qn_ref[...] * (LOG2E * HEAD ** -0.5)
    kn = fk * lax.rsqrt(_head_sum(fk * fk, e) * inv_head + RMS_EPS) * kn_ref[...]
    z = f_lo + fb_ref[...]
    log_f = jnp.minimum(z, 0.0) - jnp.log(1.0 + jnp.exp(-jnp.abs(z)))
    tri = jnp.where(lax.broadcasted_iota(jnp.int32, (tm, tm), 1) <= lax.broadcasted_iota(jnp.int32, (tm, tm), 0),
                    1.0, 0.0).astype(BF16)
    c_loc = _cumsum_rows(tri, log_f)
    off_out[...] = carry_c[...] * LOG2E
    in_blk = jnp.where(i % tiles_per_kblock == 0, 0.0, carry_b[...])
    carry_c[...] = carry_c[...] + c_loc[tm - 1:tm, :]
    carry_b[...] = in_blk + c_loc[tm - 1:tm, :]
    c2 = (c_loc + in_blk) * LOG2E

    lane = lax.broadcasted_iota(jnp.int32, (tm, HEAD), 1)
    sub = lax.broadcasted_iota(jnp.int32, (HEAD, tm), 0)
    q_tail = jnp.where(sub < 3, -1.0, 0.0)
    v_tail = jnp.where(lax.broadcasted_iota(jnp.int32, (V_ROWS - HEAD, tm), 0) == 0, 1.0, 0.0)
    for pr in range(N_HEADS // 2):
        ps2 = slice(pr * 2 * HEAD, (pr + 1) * 2 * HEAD)
        q_t = qn[:, ps2].T
        v_t = fv[:, ps2].T
        for half in range(2):
            hh = 2 * pr + half
            rows = slice(half * HEAD, (half + 1) * HEAD)
            fq_out[hh] = jnp.concatenate([q_t[rows], q_tail], axis=0).astype(BF16)
            fv_out[hh] = jnp.concatenate([v_t[rows], v_tail], axis=0).astype(BF16)
            c_h = c2[:, hh:hh + 1]
            c_hi = c_h.astype(BF16).astype(F32)
            c_mid = (c_h - c_hi).astype(BF16).astype(F32)
            c_lo = c_h - c_hi - c_mid
            c_cols = jnp.where(lane == 0, c_hi, jnp.where(lane == 1, c_mid, jnp.where(lane == 2, c_lo, 0.0)))
            fk_out[hh] = jnp.concatenate([kn[:, hh * HEAD:(hh + 1) * HEAD], c_cols], axis=1).astype(BF16)


def _pad_cols(w, n):
    return jnp.pad(w, ((0, 0), (0, n - w.shape[1])))


def _pad_rows(w, n):
    return jnp.pad(w, ((0, n - w.shape[0]), (0, 0)))


def _in_proj(x, mix_norm, w_in, shift_mu, w0, w_up, a0, a_up, g_up, k_k, k_a, r_k, f_bias, q_norm, k_norm, *, tm=256, n_sub=2):
    bsz, s, d = x.shape
    lw_, la_, lg_ = w_up.shape[0], a_up.shape[0], g_up.shape[0]
    c0 = 3 * D_MIX
    c1 = c0 + lw_ + la_ + lg_
    c2 = c1 + 3 * D_MIX
    c3 = c2 + N_HEADS

    def seg(t, pad):
        return jnp.concatenate([
            t[..., :c0],
            pad(t[..., c0:c0 + lw_], LANE), pad(t[..., c0 + lw_:c0 + lw_ + la_], LANE), pad(t[..., c0 + lw_ + la_:c1], LANE),
        ], axis=-1)

    w_cat = jnp.concatenate([seg(w_in, _pad_cols), w_in[:, c1:c2], _pad_cols(w_in[:, c2:c3], LANE)], axis=-1).astype(BF16)
    mu_cat = seg(shift_mu.reshape(1, -1), _pad_cols)
    row = lambda t: t.reshape(1, -1).astype(F32)
    e = _head_mask()
    tile = lambda t: jnp.tile(t.reshape(1, HEAD), (1, N_HEADS)).astype(F32)
    consts = [
        row(mix_norm), w_cat, mu_cat, row(w0), _pad_rows(w_up, LANE).astype(BF16), row(a0),
        _pad_rows(a_up, LANE).astype(BF16), _pad_rows(g_up, LANE).astype(BF16), row(k_k), row(k_a), row(r_k),
        _pad_cols(row(f_bias), LANE), tile(q_norm), tile(k_norm), e,
    ]
    const_specs = [_resident(c.shape) for c in consts]
    tm, tile_rows = tm * n_sub, tm
    tok = lambda: pl.BlockSpec((1, tm, D_MIX), lambda b, i: (b, i, 0))
    f32_tok = jax.ShapeDtypeStruct((bsz, s, D_MIX), F32)
    bf_tok = jax.ShapeDtypeStruct((bsz, s, D_MIX), BF16)
    assert s % tm == 0 and FOX_KBLOCK % tile_rows == 0
    return pl.pallas_call(
        functools.partial(_in_proj_kernel, tm=tile_rows, n_sub=n_sub, tiles_per_kblock=FOX_KBLOCK // tile_rows),
        out_shape=[f32_tok, f32_tok, f32_tok, bf_tok, f32_tok, f32_tok, bf_tok, bf_tok] + [
            jax.ShapeDtypeStruct((bsz, N_HEADS, 2 * HEAD, s), BF16),
            jax.ShapeDtypeStruct((bsz, N_HEADS, s, 2 * HEAD), BF16),
            jax.ShapeDtypeStruct((bsz, N_HEADS, V_ROWS, s), BF16),
            jax.ShapeDtypeStruct((bsz, s // tile_rows, 1, LANE), F32),
        ],
        grid=(bsz, s // tm),
        in_specs=[pl.BlockSpec((1, tm, d), lambda b, i: (b, i, 0))] + const_specs,
        out_specs=[tok() for _ in range(8)] + [
            pl.BlockSpec((1, N_HEADS, 2 * HEAD, tm), lambda b, i: (b, 0, 0, i)),
            pl.BlockSpec((1, N_HEADS, tm, 2 * HEAD), lambda b, i: (b, 0, i, 0)),
            pl.BlockSpec((1, N_HEADS, V_ROWS, tm), lambda b, i: (b, 0, 0, i)),
            pl.BlockSpec((1, n_sub, 1, LANE), lambda b, i: (b, i, 0, 0)),
        ],
        scratch_shapes=[pltpu.VMEM((1, N_SHIFT), F32), pltpu.VMEM((1, LANE), F32), pltpu.VMEM((1, LANE), F32)],
        compiler_params=pltpu.CompilerParams(
            dimension_semantics=("parallel", "arbitrary"), vmem_limit_bytes=VMEM_LIMIT),
        name="in_proj",
    )(x, *consts)


def _rwkv_kernel(r_ref, lw_ref, k_ref, v_ref, kk_ref, b_ref, g_ref, bonus_ref, lng_ref, lnb_ref, e_ref, y_ref, s_scr, *, tb):
    i = pl.program_id(1)

    @pl.when(i == 0)
    def _():
        s_scr[...] = jnp.zeros_like(s_scr)

    c = CHUNK
    pw = 2 * HEAD
    n_chunk = tb // c
    n_pair = N_HEADS // 2
    rr = lax.broadcasted_iota(jnp.int32, (c, pw), 0)
    ll = lax.broadcasted_iota(jnp.int32, (c, pw), 1)
    cc = ll % c
    lo = ll < c
    strict = cc < rr
    incl = cc <= rr
    eye = (cc == rr).astype(F32)
    r2 = lax.broadcasted_iota(jnp.int32, (2 * c, pw), 0)
    l2 = lax.broadcasted_iota(jnp.int32, (2 * c, pw), 1)
    lo2 = l2 < c
    diag2 = (r2 < c) == lo2
    zeros = jnp.zeros((c, pw), F32)

    def level_mask(n):
        return ((rr // (2 * n)) == (cc // (2 * n))) & (((rr // n) % 2) == 1) & (((cc // n) % 2) == 0)

    def bd(x):
        return jnp.where(diag2, jnp.concatenate([x, x], axis=0), 0.0).astype(BF16)

    def bd_sw(x):
        return jnp.where(diag2, 0.0, jnp.concatenate([x, x], axis=0)).astype(BF16)

    slab = min(tb, MXU_DIM)
    rb = lax.broadcasted_iota(jnp.int32, (slab, slab), 0)
    cb = lax.broadcasted_iota(jnp.int32, (slab, slab), 1)
    tri = jnp.where((cb <= rb) & ((rb // c) == (cb // c)), 1.0, 0.0).astype(BF16)
    lw = lw_ref[0]
    cs = jnp.concatenate([_cumsum_rows(tri, lw[j:j + slab]) for j in range(0, tb, slab)], axis=0)
    w_inv = jnp.exp(-cs)
    rt = r_ref[0] * jnp.exp(cs)
    at = -kk_ref[0] * jnp.exp(cs - lw)
    bi = b_ref[0] * w_inv
    ki = k_ref[0] * w_inv
    v = v_ref[0].astype(F32)

    chains = [(j, p) for j in range(n_chunk) for p in range(n_pair)]
    cut = lambda x, j, p: x[j * c:(j + 1) * c, p * pw:(p + 1) * pw]
    w_tot = [jnp.exp(cs[(j + 1) * c - 1:(j + 1) * c, :]) for j in range(n_chunk)]
    a_t = [cut(at, j, p) for j, p in chains]
    r_t = [cut(rt, j, p) for j, p in chains]
    b_i = [cut(bi, j, p) for j, p in chains]
    k_i = [cut(ki, j, p) for j, p in chains]
    v_p = [cut(v, j, p) for j, p in chains]
    n = len(chains)

    a_ab, a_ak, a_rb, a_rk = [], [], [], []
    for q in range(n):
        ar = jnp.concatenate([a_t[q], r_t[q]], axis=0)
        gx = _dot(jnp.where(lo2, ar, 0.0).astype(BF16), jnp.concatenate([b_i[q], k_i[q]], axis=0).astype(BF16), _NT)
        gy = _dot(jnp.where(lo2, 0.0, ar).astype(BF16), jnp.concatenate([k_i[q], b_i[q]], axis=0).astype(BF16), _NT)
        a_ab.append(jnp.where(strict, jnp.where(lo, gx[:c], gy[:c]), 0.0))
        a_ak.append(jnp.where(strict, jnp.where(lo, gy[:c], gx[:c]), 0.0))
        a_rb.append(jnp.where(incl, jnp.where(lo, gx[c:], gy[c:]), 0.0))
        a_rk.append(jnp.where(incl, jnp.where(lo, gy[c:], gx[c:]), 0.0))

    t = [eye + jnp.where(level_mask(1), a_ab[q], 0.0) for q in range(n)]
    for m in (2, 4, 8, 16, 32):
        x = [_dot(jnp.where(level_mask(m), a_ab[q], 0.0).astype(BF16), bd(t[q])) for q in range(n)]
        t = [t[q] + _dot(t[q].astype(BF16), bd(x[q])) for q in range(n)]

    v_sw = [bd_sw(v_p[q]) for q in range(n)]
    akv = [_dot(a_ak[q].astype(BF16), v_sw[q]) for q in range(n)]
    x = [_dot(t[q].astype(BF16), jnp.concatenate([bd(a_t[q]), bd(akv[q])], axis=1)) for q in range(n)]
    a_p = [x[q][:, :pw] for q in range(n)]
    u_0 = [x[q][:, pw:] for q in range(n)]
    zero_bd = jnp.zeros((2 * c, pw), BF16)
    d1 = [_dot(jnp.concatenate([a_rb[q], a_rk[q]], axis=1).astype(BF16),
               jnp.concatenate([jnp.concatenate([bd(a_p[q]), bd(u_0[q])], axis=1),
                                jnp.concatenate([zero_bd, v_sw[q]], axis=1)], axis=0)) for q in range(n)]
    d2 = []
    for q, (j, p) in enumerate(chains):
        wt = w_tot[j][:, p * pw:(p + 1) * pw]
        lhs = jnp.concatenate([jnp.concatenate([a_p[q], u_0[q]], axis=1),
                               jnp.concatenate([zeros, v_p[q]], axis=1)], axis=0).astype(BF16)
        rhs = jnp.concatenate([b_i[q] * wt, k_i[q] * wt], axis=0).astype(BF16)
        d2.append(_dot(lhs, rhs, _TN))

    s = [s_scr[p] for p in range(n_pair)]
    ys = [None] * n
    for j in range(n_chunk):
        for p in range(n_pair):
            q = j * n_pair + p
            mt = jnp.where(diag2, d2[q][:pw], 0.0).astype(BF16)
            nt = jnp.where(lo, d2[q][pw:pw + c], d2[q][pw + c:])
            ys[q] = d1[q][:, pw:] + _dot((r_t[q] + d1[q][:, :pw]).astype(BF16), bd(s[p]), _NT)
            s[p] = s[p] * w_tot[j][:, p * pw:(p + 1) * pw] + _dot(s[p].astype(BF16), mt) + nt
    for p in range(n_pair):
        s_scr[p] = s[p]

    y = jnp.concatenate([jnp.concatenate([ys[j * n_pair + p] for p in range(n_pair)], axis=1) for j in range(n_chunk)], axis=0)
    e = e_ref[...]
    inv_head = 1.0 / HEAD
    yc = y - _head_sum(y, e) * inv_head
    yn = yc * lax.rsqrt(_head_sum(yc * yc, e) * inv_head + LNX_EPS)
    y_ref[0] = ((yn * lng_ref[...] + lnb_ref[...] + bonus_ref[0]) * g_ref[0]).astype(BF16)


def _head_mask():
    head_id = jnp.arange(MXU_DIM) // HEAD
    return (head_id[:, None] == head_id[None, :]).astype(BF16)


def _rwkv(r, lw, k, v, kk, b, g, bonus, lnx_g, lnx_b, *, tb=512):
    bsz, s, _ = r.shape
    assert s % tb == 0 and tb % CHUNK == 0
    tok = pl.BlockSpec((1, tb, D_MIX), lambda bb, i: (bb, i, 0))
    vec = pl.BlockSpec((1, D_MIX), lambda bb, i: (0, 0))
    return pl.pallas_call(
        functools.partial(_rwkv_kernel, tb=tb),
        out_shape=jax.ShapeDtypeStruct((bsz, s, D_MIX), BF16),
        grid=(bsz, s // tb),
        in_specs=[tok] * 8 + [vec, vec, pl.BlockSpec((MXU_DIM, MXU_DIM), lambda bb, i: (0, 0))],
        out_specs=tok,
        scratch_shapes=[pltpu.VMEM((N_HEADS // 2, HEAD, 2 * HEAD), F32)],
        compiler_params=pltpu.CompilerParams(
            dimension_semantics=("parallel", "arbitrary"), vmem_limit_bytes=VMEM_LIMIT),
        name="rwkv",
    )(r, lw, k, v, kk, b, g, bonus, lnx_g.reshape(1, -1).astype(F32), lnx_b.reshape(1, -1).astype(F32), _head_mask())


def _fox_kernel(off_ref, thr_ref, q_ref, k_ref, v_ref, o_ref, s00, s01, s10, s11, m_scr, acc_scr, *, tq, tk, n_kblk):
    b, hp, qb = pl.program_id(0), pl.program_id(1), pl.program_id(2)
    kb_last = (qb * tq) // tk
    delta = qb * tq - kb_last * tk
    s_scr = ((s00, s01), (s10, s11))
    base = [(b * N_HEADS + 2 * hp + hh) * n_kblk for hh in range(2)]
    off_d = [off_ref[base[hh] + kb_last] for hh in range(2)]

    thr = thr_ref[0]

    def needed(kb):
        nxt = jnp.minimum(kb + 1, kb_last)
        return jnp.minimum(off_ref[base[0] + nxt] - off_d[0], off_ref[base[1] + nxt] - off_d[1]) <= thr

    kb_first = lax.while_loop(lambda kb: jnp.logical_and(kb > 0, needed(kb - 1)), lambda kb: kb - 1, kb_last)
    n_open = kb_last - kb_first

    def qk(hh, kb, slot):
        ksl = pl.ds(pl.multiple_of(kb * tk, tk), tk)
        s_scr[slot][hh][...] = _dot(k_ref[0, hh, ksl, :], q_ref[0, hh])

    def soft_pv(hh, kb, slot, masked):
        s = s_scr[slot][hh][...]
        if masked:
            key = lax.broadcasted_iota(jnp.int32, (tk, tq), 0)
            qry = lax.broadcasted_iota(jnp.int32, (tk, tq), 1)
            s = jnp.where(key - qry <= delta, s, NEG_BIG)
        off = off_ref[base[hh] + kb] - off_d[hh]
        m = m_scr[hh]
        m_new = jnp.maximum(m, jnp.max(s, axis=0, keepdims=True) - off)
        p = jnp.exp2(s - (m_new + off)).astype(BF16)
        ksl = pl.ds(pl.multiple_of(kb * tk, tk), tk)
        acc_scr[hh] = jnp.exp2(m - m_new) * acc_scr[hh] + _dot(v_ref[0, hh, :, ksl], p)
        m_scr[hh] = m_new

    m_scr[...] = jnp.full(m_scr.shape, NEG_BIG, F32)
    acc_scr[...] = jnp.zeros(acc_scr.shape, F32)
    qk(0, kb_first, 0)
    qk(1, kb_first, 0)

    def pair(i, _):
        kb = kb_first + 2 * i
        qk(0, kb + 1, 1)
        qk(1, kb + 1, 1)
        soft_pv(0, kb, 0, False)
        soft_pv(1, kb, 0, False)
        qk(0, kb + 2, 0)
        qk(1, kb + 2, 0)
        soft_pv(0, kb + 1, 1, False)
        soft_pv(1, kb + 1, 1, False)
        return 0

    lax.fori_loop(0, n_open // 2, pair, 0)

    @pl.when(n_open % 2 == 1)
    def _():
        qk(0, kb_last, 1)
        qk(1, kb_last, 1)
        soft_pv(0, kb_last - 1, 0, False)
        soft_pv(1, kb_last - 1, 0, False)
        soft_pv(0, kb_last, 1, True)
        soft_pv(1, kb_last, 1, True)

    @pl.when(n_open % 2 == 0)
    def _():
        soft_pv(0, kb_last, 0, True)
        soft_pv(1, kb_last, 0, True)

    a0, a1 = acc_scr[0], acc_scr[1]
    o_t = jnp.concatenate([a0[:HEAD] / a0[HEAD:HEAD + 1], a1[:HEAD] / a1[HEAD:HEAD + 1]], axis=0)
    o_ref[0] = o_t.T.astype(BF16)


def _fox(q_t, k_aug, v_t, off, q_norm, k_norm, *, tq=512, tk=FOX_KBLOCK):
    bsz, nh, s, _ = k_aug.shape
    n_kblk = s // tk
    assert s % tq == 0 and s % tk == 0 and tk % tq == 0 and nh % 2 == 0 and off.shape == (bsz * nh * n_kblk,)
    qk_bound = 1.02 * LOG2E * HEAD ** 0.5 * jnp.max(jnp.abs(q_norm)) * jnp.max(jnp.abs(k_norm))
    thr = (2.0 * qk_bound + EXP2_UNDERFLOW).reshape(1).astype(F32)
    return pl.pallas_call(
        functools.partial(_fox_kernel, tq=tq, tk=tk, n_kblk=n_kblk),
        out_shape=jax.ShapeDtypeStruct((bsz, s, nh * HEAD), BF16),
        grid=(bsz, nh // 2, s // tq),
        in_specs=[
            pl.BlockSpec(memory_space=pltpu.SMEM),
            pl.BlockSpec(memory_space=pltpu.SMEM),
            pl.BlockSpec((1, 2, 2 * HEAD, tq), lambda b, hp, i: (b, hp, 0, i)),
            pl.BlockSpec((1, 2, s, 2 * HEAD), lambda b, hp, i: (b, hp, 0, 0)),
            pl.BlockSpec((1, 2, V_ROWS, s), lambda b, hp, i: (b, hp, 0, 0)),
        ],
        out_specs=pl.BlockSpec((1, tq, 2 * HEAD), lambda b, hp, i: (b, i, hp)),
        scratch_shapes=[pltpu.VMEM((tk, tq), F32)] * 4 + [pltpu.VMEM((2, 1, tq), F32), pltpu.VMEM((2, V_ROWS, tq), F32)],
        compiler_params=pltpu.CompilerParams(
            dimension_semantics=("parallel", "parallel", "arbitrary"), vmem_limit_bytes=VMEM_LIMIT),
        name="fox",
    )(off, thr, q_t, k_aug, v_t)


def _merge_kernel(x_ref, g_ref, wg_ref, ya_ref, yb_ref, pa_ref, pb_ref, wo_ref, o_ref):
    x = x_ref[...]
    d = x.shape[-1]
    h = _rms(x, g_ref[...]).astype(BF16)
    gates = _sigmoid(_dot(h, wg_ref[...]))
    ya = _dot(ya_ref[...], pa_ref[...])
    yb = _dot(yb_ref[...], pb_ref[...])
    mix = (gates[:, :d] * ya + gates[:, d:] * yb).astype(BF16)
    o_ref[...] = x + _dot(mix, wo_ref[...])


def _merge(x2d, mix_norm, w_gates, ya, yb, p_a, p_b, w_out, *, tm=512):
    m, d = x2d.shape
    assert m % tm == 0
    full = lambda t: pl.BlockSpec(t.shape, lambda i: (0, 0))
    tok = lambda n: pl.BlockSpec((tm, n), lambda i: (i, 0))
    consts = [mix_norm.reshape(1, d).astype(F32), w_gates.astype(BF16)]
    mats = [p_a.astype(BF16), p_b.astype(BF16), w_out.astype(BF16)]
    return pl.pallas_call(
        _merge_kernel,
        out_shape=jax.ShapeDtypeStruct((m, d), F32),
        grid=(m // tm,),
        in_specs=[tok(d), full(consts[0]), full(consts[1]), tok(D_MIX), tok(D_MIX)] + [full(t) for t in mats],
        out_specs=tok(d),
        compiler_params=pltpu.CompilerParams(dimension_semantics=("parallel",), vmem_limit_bytes=VMEM_LIMIT),
        name="merge",
    )(x2d, consts[0], consts[1], ya, yb, *mats)


def kernel(x, ffn1_norm, ffn1_w1, ffn1_w3, ffn1_w2, mix_norm, w_in, shift_mu, rwkv_w0, rwkv_w_up, rwkv_a0, rwkv_a_up, rwkv_g_up, rwkv_k_k, rwkv_k_a, rwkv_r_k, rwkv_lnx_g, rwkv_lnx_b, rwkv_proj, fox_f_bias, fox_q_norm, fox_k_norm, fox_proj, w_out, ffn2_norm, ffn2_w1, ffn2_w3, ffn2_w2, final_norm):
    bsz, s, d = x.shape
    depth = ffn1_norm.shape[0]
    n_gate0 = w_in.shape[-1] - 2 * d
    x2d = x.reshape(bsz * s, d)
    for l in range(depth):
        x2d = _ffn(x2d, ffn1_norm[l], ffn1_w1[l], ffn1_w3[l], ffn1_w2[l], final_norm, final_norm=False)
        (r, lw, k, v, kk, b, g, bonus, fq_t, fk_aug, fv_t, off) = _in_proj(
            x2d.reshape(bsz, s, d), mix_norm[l], w_in[l], shift_mu[l], rwkv_w0[l], rwkv_w_up[l], rwkv_a0[l],
            rwkv_a_up[l], rwkv_g_up[l], rwkv_k_k[l], rwkv_k_a[l], rwkv_r_k[l], fox_f_bias[l], fox_q_norm[l],
            fox_k_norm[l])
        ya = _rwkv(r, lw, k, v, kk, b, g, bonus, rwkv_lnx_g[l], rwkv_lnx_b[l])
        off = off[:, ::FOX_KBLOCK // (s // off.shape[1]), 0, :N_HEADS]
        off = jnp.transpose(off, (0, 2, 1)).reshape(-1)
        yb = _fox(fq_t, fk_aug, fv_t, off, fox_q_norm[l], fox_k_norm[l])
        x2d = _merge(x2d, mix_norm[l], w_in[l][:, n_gate0:], ya.reshape(bsz * s, D_MIX), yb.reshape(bsz * s, D_MIX),
                     rwkv_proj[l], fox_proj[l], w_out[l])
        x2d = _ffn(x2d, ffn2_norm[l], ffn2_w1[l], ffn2_w3[l], ffn2_w2[l], final_norm, final_norm=(l == depth - 1))
    return x2d.reshape(bsz, s, d)
```

```python
import functools

import jax
import jax.numpy as jnp
from jax import lax
from jax.experimental import pallas as pl
from jax.experimental.pallas import tpu as pltpu

F32 = jnp.float32
BF16 = jnp.bfloat16

HEAD = 64
N_HEADS = 8
D_MIX = HEAD * N_HEADS
LANE = 128
MXU_DIM = 256
CHUNK = 64
RMS_EPS = 1e-6
LNX_EPS = 64e-5
NEG_BIG = -1e30
EXP2_UNDERFLOW = 160.0
VMEM_LIMIT = 56 * 1024 * 1024

_NT = (((1,), (1,)), ((), ()))
_TN = (((0,), (0,)), ((), ()))


def _dot(a, b, dims=None):
    if dims is None:
        return jnp.dot(a, b, preferred_element_type=F32)
    return lax.dot_general(a, b, dims, preferred_element_type=F32)


def _bdot(a, b, dims=None):
    return _dot(a.astype(BF16), b.astype(BF16), dims)


def _rms(x, g):
    return x * lax.rsqrt(jnp.mean(x * x, axis=-1, keepdims=True) + RMS_EPS) * g


def _softplus(z):
    return jnp.maximum(z, 0.0) + jnp.log(1.0 + jnp.exp(-jnp.abs(z)))


def _sigmoid(z):
    return 1.0 / (1.0 + jnp.exp(-z))


def _head_sum(x, e):
    xb = x.astype(BF16)
    return jnp.concatenate([_dot(xb[:, j:j + MXU_DIM], e) for j in range(0, x.shape[1], MXU_DIM)], axis=1)


def _cumsum_rows(tri, x):
    hi = x.astype(BF16)
    r1 = x - hi.astype(F32)
    mid = r1.astype(BF16)
    lo = (r1 - mid.astype(F32)).astype(BF16)
    return _dot(tri, hi) + _dot(tri, mid) + _dot(tri, lo)


def _ffn_kernel(x_ref, g_ref, w1_ref, w3_ref, w2_ref, gf_ref, o_ref, *, final_norm):
    x = x_ref[...]
    h = _rms(x, g_ref[...]).astype(BF16)
    a = _dot(h, w1_ref[...])
    b = _dot(h, w3_ref[...])
    u = (a * _sigmoid(a) * b).astype(BF16)
    y = x + 0.5 * _dot(u, w2_ref[...])
    if final_norm:
        y = _rms(y, gf_ref[...])
    o_ref[...] = y


def _resident(shape):
    return pl.BlockSpec(shape, lambda *_: (0,) * len(shape), pipeline_mode=pl.Buffered(1))


def _ffn(x2d, g, w1, w3, w2, gf, *, final_norm, tm=512):
    m, d = x2d.shape
    f_dim = w1.shape[1]
    assert m % tm == 0
    return pl.pallas_call(
        functools.partial(_ffn_kernel, final_norm=final_norm),
        out_shape=jax.ShapeDtypeStruct((m, d), F32),
        grid=(m // tm,),
        in_specs=[
            pl.BlockSpec((tm, d), lambda i: (i, 0)),
            _resident((1, d)), _resident((d, f_dim)), _resident((d, f_dim)), _resident((f_dim, d)), _resident((1, d)),
        ],
        out_specs=pl.BlockSpec((tm, d), lambda i: (i, 0)),
        compiler_params=pltpu.CompilerParams(dimension_semantics=("parallel",), vmem_limit_bytes=VMEM_LIMIT),
        name="ffn",
    )(x2d, g.reshape(1, d), w1.astype(BF16), w3.astype(BF16), w2.astype(BF16), gf.reshape(1, d))


N_SHIFT = 3 * D_MIX + 3 * LANE
N_PROJ = N_SHIFT + 3 * D_MIX + LANE
LOG2E = 1.4426950408889634
V_ROWS = HEAD + 16
FOX_KBLOCK = 512


def _in_proj_kernel(x_ref, g_ref, w_ref, mu_ref, w0_ref, wup_ref, a0_ref, aup_ref, gup_ref, kk_ref, ka_ref,
                    rk_ref, fb_ref, qn_ref, kn_ref, e_ref,
                    r_out, lw_out, k_out, v_out, kk_out, b_out, g_out, bonus_out, fq_out, fk_out, fv_out, off_out,
                    carry_p, carry_c, carry_b, *, tm, n_sub, tiles_per_kblock):
    step = pl.program_id(1)

    @pl.when(step == 0)
    def _():
        carry_p[...] = jnp.zeros_like(carry_p)
        carry_c[...] = jnp.zeros_like(carry_c)
        carry_b[...] = jnp.zeros_like(carry_b)

    outs = (r_out, lw_out, k_out, v_out, kk_out, b_out, g_out, bonus_out)
    for t in range(n_sub):
        rows = pl.ds(t * tm, tm)
        _in_proj_tile(step * n_sub + t, x_ref.at[0, rows], g_ref, w_ref, mu_ref, w0_ref, wup_ref, a0_ref, aup_ref, gup_ref,
                      kk_ref, ka_ref, rk_ref, fb_ref, qn_ref, kn_ref, e_ref, *[o.at[0, rows] for o in outs],
                      fq_out.at[0, :, :, rows], fk_out.at[0, :, rows], fv_out.at[0, :, :, rows], off_out.at[0, t],
                      carry_p, carry_c, carry_b, tm=tm, tiles_per_kblock=tiles_per_kblock)


def _in_proj_tile(i, x_ref, g_ref, w_ref, mu_ref, w0_ref, wup_ref, a0_ref, aup_ref, gup_ref, kk_ref, ka_ref,
                  rk_ref, fb_ref, qn_ref, kn_ref, e_ref,
                  r_out, lw_out, k_out, v_out, kk_out, b_out, g_out, bonus_out, fq_out, fk_out, fv_out, off_out,
                  carry_p, carry_c, carry_b, *, tm, tiles_per_kblock):
    h = _rms(x_ref[...], g_ref[...]).astype(BF16)
    p = _dot(h, w_ref[...])

    ps = p[:, :N_SHIFT]
    row = lax.broadcasted_iota(jnp.int32, (tm, 1), 0)
    prev = jnp.where(row == 0, carry_p[...], pltpu.roll(ps, 1, axis=0))
    carry_p[...] = ps[tm - 1:tm, :]
    sh = ps + (prev - ps) * mu_ref[...]

    e = e_ref[...]
    r = sh[:, 0:D_MIX]
    k = sh[:, D_MIX:2 * D_MIX]
    v = sh[:, 2 * D_MIX:3 * D_MIX]
    w_lo = sh[:, 3 * D_MIX:3 * D_MIX + LANE]
    a_lo = sh[:, 3 * D_MIX + LANE:3 * D_MIX + 2 * LANE]
    g_lo = sh[:, 3 * D_MIX + 2 * LANE:N_SHIFT]

    w_log = -_softplus(-(w0_ref[...] + _bdot(jnp.tanh(w_lo), wup_ref[...]))) - 0.5
    lw = -jnp.exp(w_log)
    a = _sigmoid(a0_ref[...] + _bdot(a_lo, aup_ref[...]))
    g = _bdot(_sigmoid(g_lo), gup_ref[...])
    kk = k * kk_ref[...]
    kk = kk * lax.rsqrt(jnp.maximum(_head_sum(kk * kk, e), 1e-24))
    k_mod = k * (1.0 + (a - 1.0) * ka_ref[...])
    bonus = _head_sum(r * k_mod * rk_ref[...], e) * v

    r_out[...] = r
    lw_out[...] = lw
    k_out[...] = k_mod
    v_out[...] = v.astype(BF16)
    kk_out[...] = kk
    b_out[...] = kk * a
    g_out[...] = g.astype(BF16)
    bonus_out[...] = bonus.astype(BF16)

    fq = p[:, N_SHIFT:N_SHIFT + D_MIX]
    fk = p[:, N_SHIFT + D_MIX:N_SHIFT + 2 * D_MIX]
    fv = p[:, N_SHIFT + 2 * D_MIX:N_SHIFT + 3 * D_MIX]
    f_lo = p[:, N_SHIFT + 3 * D_MIX:N_PROJ]
    inv_head = 1.0 / HEAD
    qn = fq * lax.rsqrt(_head_sum(fq * fq, e) * inv_head + RMS_EPS) * qn_ref[...] * (LOG2E * HEAD ** -0.5)
    kn = fk * lax.rsqrt(_head_sum(fk * fk, e) * inv_head + RMS_EPS) * kn_ref[...]
    z = f_lo + fb_ref[...]
    log_f = jnp.minimum(z, 0.0) - jnp.log(1.0 + jnp.exp(-jnp.abs(z)))
    tri = jnp.where(lax.broadcasted_iota(jnp.int32, (tm, tm), 1) <= lax.broadcasted_iota(jnp.int32, (tm, tm), 0),
                    1.0, 0.0).astype(BF16)
    c_loc = _cumsum_rows(tri, log_f)
    off_out[...] = carry_c[...] * LOG2E
    in_blk = jnp.where(i % tiles_per_kblock == 0, 0.0, carry_b[...])
    carry_c[...] = carry_c[...] + c_loc[tm - 1:tm, :]
    carry_b[...] = in_blk + c_loc[tm - 1:tm, :]
    c2 = (c_loc + in_blk) * LOG2E

    lane = lax.broadcasted_iota(jnp.int32, (tm, HEAD), 1)
    sub = lax.broadcasted_iota(jnp.int32, (HEAD, tm), 0)
    q_tail = jnp.where(sub < 3, -1.0, 0.0)
    v_tail = jnp.where(lax.broadcasted_iota(jnp.int32, (V_ROWS - HEAD, tm), 0) == 0, 1.0, 0.0)
    for pr in range(N_HEADS // 2):
        ps2 = slice(pr * 2 * HEAD, (pr + 1) * 2 * HEAD)
        q_t = qn[:, ps2].T
        v_t = fv[:, ps2].T
        for half in range(2):
            hh = 2 * pr + half
            rows = slice(half * HEAD, (half + 1) * HEAD)
            fq_out[hh] = jnp.concatenate([q_t[rows], q_tail], axis=0).astype(BF16)
            fv_out[hh] = jnp.concatenate([v_t[rows], v_tail], axis=0).astype(BF16)
            c_h = c2[:, hh:hh + 1]
            c_hi = c_h.astype(BF16).astype(F32)
            c_mid = (c_h - c_hi).astype(BF16).astype(F32)
            c_lo = c_h - c_hi - c_mid
            c_cols = jnp.where(lane == 0, c_hi, jnp.where(lane == 1, c_mid, jnp.where(lane == 2, c_lo, 0.0)))
            fk_out[hh] = jnp.concatenate([kn[:, hh * HEAD:(hh + 1) * HEAD], c_cols], axis=1).astype(BF16)


def _pad_cols(w, n):
    return jnp.pad(w, ((0, 0), (0, n - w.shape[1])))


def _pad_rows(w, n):
    return jnp.pad(w, ((0, n - w.shape[0]), (0, 0)))


def _in_proj(x, mix_norm, w_in, shift_mu, w0, w_up, a0, a_up, g_up, k_k, k_a, r_k, f_bias, q_norm, k_norm, *, tm=256, n_sub=2):
    bsz, s, d = x.shape
    lw_, la_, lg_ = w_up.shape[0], a_up.shape[0], g_up.shape[0]
    c0 = 3 * D_MIX
    c1 = c0 + lw_ + la_ + lg_
    c2 = c1 + 3 * D_MIX
    c3 = c2 + N_HEADS

    def seg(t, pad):
        return jnp.concatenate([
            t[..., :c0],
            pad(t[..., c0:c0 + lw_], LANE), pad(t[..., c0 + lw_:c0 + lw_ + la_], LANE), pad(t[..., c0 + lw_ + la_:c1], LANE),
        ], axis=-1)

    w_cat = jnp.concatenate([seg(w_in, _pad_cols), w_in[:, c1:c2], _pad_cols(w_in[:, c2:c3], LANE)], axis=-1).astype(BF16)
    mu_cat = seg(shift_mu.reshape(1, -1), _pad_cols)
    row = lambda t: t.reshape(1, -1).astype(F32)
    e = _head_mask()
    tile = lambda t: jnp.tile(t.reshape(1, HEAD), (1, N_HEADS)).astype(F32)
    consts = [
        row(mix_norm), w_cat, mu_cat, row(w0), _pad_rows(w_up, LANE).astype(BF16), row(a0),
        _pad_rows(a_up, LANE).astype(BF16), _pad_rows(g_up, LANE).astype(BF16), row(k_k), row(k_a), row(r_k),
        _pad_cols(row(f_bias), LANE), tile(q_norm), tile(k_norm), e,
    ]
    const_specs = [_resident(c.shape) for c in consts]
    tm, tile_rows = tm * n_sub, tm
    tok = lambda: pl.BlockSpec((1, tm, D_MIX), lambda b, i: (b, i, 0))
    f32_tok = jax.ShapeDtypeStruct((bsz, s, D_MIX), F32)
    bf_tok = jax.ShapeDtypeStruct((bsz, s, D_MIX), BF16)
    assert s % tm == 0 and FOX_KBLOCK % tile_rows == 0
    return pl.pallas_call(
        functools.partial(_in_proj_kernel, tm=tile_rows, n_sub=n_sub, tiles_per_kblock=FOX_KBLOCK // tile_rows),
        out_shape=[f32_tok, f32_tok, f32_tok, bf_tok, f32_tok, f32_tok, bf_tok, bf_tok] + [
            jax.ShapeDtypeStruct((bsz, N_HEADS, 2 * HEAD, s), BF16),
            jax.ShapeDtypeStruct((bsz, N_HEADS, s, 2 * HEAD), BF16),
            jax.ShapeDtypeStruct((bsz, N_HEADS, V_ROWS, s), BF16),
            jax.ShapeDtypeStruct((bsz, s // tile_rows, 1, LANE), F32),
        ],
        grid=(bsz, s // tm),
        in_specs=[pl.BlockSpec((1, tm, d), lambda b, i: (b, i, 0))] + const_specs,
        out_specs=[tok() for _ in range(8)] + [
            pl.BlockSpec((1, N_HEADS, 2 * HEAD, tm), lambda b, i: (b, 0, 0, i)),
            pl.BlockSpec((1, N_HEADS, tm, 2 * HEAD), lambda b, i: (b, 0, i, 0)),
            pl.BlockSpec((1, N_HEADS, V_ROWS, tm), lambda b, i: (b, 0, 0, i)),
            pl.BlockSpec((1, n_sub, 1, LANE), lambda b, i: (b, i, 0, 0)),
        ],
        scratch_shapes=[pltpu.VMEM((1, N_SHIFT), F32), pltpu.VMEM((1, LANE), F32), pltpu.VMEM((1, LANE), F32)],
        compiler_params=pltpu.CompilerParams(
            dimension_semantics=("parallel", "arbitrary"), vmem_limit_bytes=VMEM_LIMIT),
        name="in_proj",
    )(x, *consts)


def _rwkv_kernel(r_ref, lw_ref, k_ref, v_ref, kk_ref, b_ref, g_ref, bonus_ref, lng_ref, lnb_ref, e_ref, y_ref, s_scr, *, tb):
    i = pl.program_id(1)

    @pl.when(i == 0)
    def _():
        s_scr[...] = jnp.zeros_like(s_scr)

    c = CHUNK
    pw = 2 * HEAD
    n_chunk = tb // c
    n_pair = N_HEADS // 2
    rr = lax.broadcasted_iota(jnp.int32, (c, pw), 0)
    ll = lax.broadcasted_iota(jnp.int32, (c, pw), 1)
    cc = ll % c
    lo = ll < c
    strict = cc < rr
    incl = cc <= rr
    eye = (cc == rr).astype(F32)
    r2 = lax.broadcasted_iota(jnp.int32, (2 * c, pw), 0)
    l2 = lax.broadcasted_iota(jnp.int32, (2 * c, pw), 1)
    lo2 = l2 < c
    diag2 = (r2 < c) == lo2
    zeros = jnp.zeros((c, pw), F32)

    def level_mask(n):
        return ((rr // (2 * n)) == (cc // (2 * n))) & (((rr // n) % 2) == 1) & (((cc // n) % 2) == 0)

    def bd(x):
        return jnp.where(diag2, jnp.concatenate([x, x], axis=0), 0.0).astype(BF16)

    def bd_sw(x):
        return jnp.where(diag2, 0.0, jnp.concatenate([x, x], axis=0)).astype(BF16)

    slab = min(tb, MXU_DIM)
    rb = lax.broadcasted_iota(jnp.int32, (slab, slab), 0)
    cb = lax.broadcasted_iota(jnp.int32, (slab, slab), 1)
    tri = jnp.where((cb <= rb) & ((rb // c) == (cb // c)), 1.0, 0.0).astype(BF16)
    lw = lw_ref[0]
    cs = jnp.concatenate([_cumsum_rows(tri, lw[j:j + slab]) for j in range(0, tb, slab)], axis=0)
    w_inv = jnp.exp(-cs)
    rt = r_ref[0] * jnp.exp(cs)
    at = -kk_ref[0] * jnp.exp(cs - lw)
    bi = b_ref[0] * w_inv
    ki = k_ref[0] * w_inv
    v = v_ref[0].astype(F32)

    chains = [(j, p) for j in range(n_chunk) for p in range(n_pair)]
    cut = lambda x, j, p: x[j * c:(j + 1) * c, p * pw:(p + 1) * pw]
    w_tot = [jnp.exp(cs[(j + 1) * c - 1:(j + 1) * c, :]) for j in range(n_chunk)]
    a_t = [cut(at, j, p) for j, p in chains]
    r_t = [cut(rt, j, p) for j, p in chains]
    b_i = [cut(bi, j, p) for j, p in chains]
    k_i = [cut(ki, j, p) for j, p in chains]
    v_p = [cut(v, j, p) for j, p in chains]
    n = len(chains)

    a_ab, a_ak, a_rb, a_rk = [], [], [], []
    for q in range(n):
        ar = jnp.concatenate([a_t[q], r_t[q]], axis=0)
        gx = _dot(jnp.where(lo2, ar, 0.0).astype(BF16), jnp.concatenate([b_i[q], k_i[q]], axis=0).astype(BF16), _NT)
        gy = _dot(jnp.where(lo2, 0.0, ar).astype(BF16), jnp.concatenate([k_i[q], b_i[q]], axis=0).astype(BF16), _NT)
        a_ab.append(jnp.where(strict, jnp.where(lo, gx[:c], gy[:c]), 0.0))
        a_ak.append(jnp.where(strict, jnp.where(lo, gy[:c], gx[:c]), 0.0))
        a_rb.append(jnp.where(incl, jnp.where(lo, gx[c:], gy[c:]), 0.0))
        a_rk.append(jnp.where(incl, jnp.where(lo, gy[c:], gx[c:]), 0.0))

    t = [eye + jnp.where(level_mask(1), a_ab[q], 0.0) for q in range(n)]
    for m in (2, 4, 8, 16, 32):
        x = [_dot(jnp.where(level_mask(m), a_ab[q], 0.0).astype(BF16), bd(t[q])) for q in range(n)]
        t = [t[q] + _dot(t[q].astype(BF16), bd(x[q])) for q in range(n)]

    v_sw = [bd_sw(v_p[q]) for q in range(n)]
    akv = [_dot(a_ak[q].astype(BF16), v_sw[q]) for q in range(n)]
    x = [_dot(t[q].astype(BF16), jnp.concatenate([bd(a_t[q]), bd(akv[q])], axis=1)) for q in range(n)]
    a_p = [x[q][:, :pw] for q in range(n)]
    u_0 = [x[q][:, pw:] for q in range(n)]
    zero_bd = jnp.zeros((2 * c, pw), BF16)
    d1 = [_dot(jnp.concatenate([a_rb[q], a_rk[q]], axis=1).astype(BF16),
               jnp.concatenate([jnp.concatenate([bd(a_p[q]), bd(u_0[q])], axis=1),
                                jnp.concatenate([zero_bd, v_sw[q]], axis=1)], axis=0)) for q in range(n)]
    d2 = []
    for q, (j, p) in enumerate(chains):
        wt = w_tot[j][:, p * pw:(p + 1) * pw]
        lhs = jnp.concatenate([jnp.concatenate([a_p[q], u_0[q]], axis=1),
                               jnp.concatenate([zeros, v_p[q]], axis=1)], axis=0).astype(BF16)
        rhs = jnp.concatenate([b_i[q] * wt, k_i[q] * wt], axis=0).astype(BF16)
        d2.append(_dot(lhs, rhs, _TN))

    s = [s_scr[p] for p in range(n_pair)]
    ys = [None] * n
    for j in range(n_chunk):
        for p in range(n_pair):
            q = j * n_pair + p
            mt = jnp.where(diag2, d2[q][:pw], 0.0).astype(BF16)
            nt = jnp.where(lo, d2[q][pw:pw + c], d2[q][pw + c:])
            ys[q] = d1[q][:, pw:] + _dot((r_t[q] + d1[q][:, :pw]).astype(BF16), bd(s[p]), _NT)
            s[p] = s[p] * w_tot[j][:, p * pw:(p + 1) * pw] + _dot(s[p].astype(BF16), mt) + nt
    for p in range(n_pair):
        s_scr[p] = s[p]

    y = jnp.concatenate([jnp.concatenate([ys[j * n_pair + p] for p in range(n_pair)], axis=1) for j in range(n_chunk)], axis=0)
    e = e_ref[...]
    inv_head = 1.0 / HEAD
    yc = y - _head_sum(y, e) * inv_head
    yn = yc * lax.rsqrt(_head_sum(yc * yc, e) * inv_head + LNX_EPS)
    y_ref[0] = ((yn * lng_ref[...] + lnb_ref[...] + bonus_ref[0]) * g_ref[0]).astype(BF16)


def _head_mask():
    head_id = jnp.arange(MXU_DIM) // HEAD
    return (head_id[:, None] == head_id[None, :]).astype(BF16)


def _rwkv(r, lw, k, v, kk, b, g, bonus, lnx_g, lnx_b, *, tb=512):
    bsz, s, _ = r.shape
    assert s % tb == 0 and tb % CHUNK == 0
    tok = pl.BlockSpec((1, tb, D_MIX), lambda bb, i: (bb, i, 0))
    vec = pl.BlockSpec((1, D_MIX), lambda bb, i: (0, 0))
    return pl.pallas_call(
        functools.partial(_rwkv_kernel, tb=tb),
        out_shape=jax.ShapeDtypeStruct((bsz, s, D_MIX), BF16),
        grid=(bsz, s // tb),
        in_specs=[tok] * 8 + [vec, vec, pl.BlockSpec((MXU_DIM, MXU_DIM), lambda bb, i: (0, 0))],
        out_specs=tok,
        scratch_shapes=[pltpu.VMEM((N_HEADS // 2, HEAD, 2 * HEAD), F32)],
        compiler_params=pltpu.CompilerParams(
            dimension_semantics=("parallel", "arbitrary"), vmem_limit_bytes=VMEM_LIMIT),
        name="rwkv",
    )(r, lw, k, v, kk, b, g, bonus, lnx_g.reshape(1, -1).astype(F32), lnx_b.reshape(1, -1).astype(F32), _head_mask())


def _fox_kernel(off_ref, thr_ref, q_ref, k_ref, v_ref, o_ref, *scratch, tq, tk, qw, n_kblk):
    b, hp, qb = pl.program_id(0), pl.program_id(1), pl.program_id(2)
    kb_last = (qb * tq) // tk
    delta = qb * tq - kb_last * tk
    chains = [(hh, g) for hh in range(2) for g in range(tq // qw)]
    n_c = len(chains)
    s_scr = (scratch[:n_c], scratch[n_c:2 * n_c])
    m_scr, acc_scr = scratch[2 * n_c:]
    base = [(b * N_HEADS + 2 * hp + hh) * n_kblk for hh in range(2)]
    off_d = [off_ref[base[hh] + kb_last] for hh in range(2)]

    thr = thr_ref[0]

    def needed(kb):
        nxt = jnp.minimum(kb + 1, kb_last)
        return jnp.minimum(off_ref[base[0] + nxt] - off_d[0], off_ref[base[1] + nxt] - off_d[1]) <= thr

    kb_first = lax.while_loop(lambda kb: jnp.logical_and(kb > 0, needed(kb - 1)), lambda kb: kb - 1, kb_last)
    n_open = kb_last - kb_first

    def qk(kb, slot):
        ksl = pl.ds(pl.multiple_of(kb * tk, tk), tk)
        for c, (hh, g) in enumerate(chains):
            s_scr[slot][c][...] = _dot(k_ref[0, hh, ksl, :], q_ref[0, hh, :, g * qw:(g + 1) * qw])

    def soft_pv(kb, slot, masked):
        ksl = pl.ds(pl.multiple_of(kb * tk, tk), tk)
        for c, (hh, g) in enumerate(chains):
            s = s_scr[slot][c][...]
            if masked:
                key = lax.broadcasted_iota(jnp.int32, (tk, qw), 0)
                qry = lax.broadcasted_iota(jnp.int32, (tk, qw), 1)
                s = jnp.where(key - qry <= delta + g * qw, s, NEG_BIG)
            off = off_ref[base[hh] + kb] - off_d[hh]
            m = m_scr[c]
            m_new = jnp.maximum(m, jnp.max(s, axis=0, keepdims=True) - off)
            p = jnp.exp2(s - (m_new + off)).astype(BF16)
            acc_scr[c] = jnp.exp2(m - m_new) * acc_scr[c] + _dot(v_ref[0, hh, :, ksl], p)
            m_scr[c] = m_new

    m_scr[...] = jnp.full(m_scr.shape, NEG_BIG, F32)
    acc_scr[...] = jnp.zeros(acc_scr.shape, F32)
    qk(kb_first, 0)

    def pair(i, _):
        kb = kb_first + 2 * i
        qk(kb + 1, 1)
        soft_pv(kb, 0, False)
        qk(kb + 2, 0)
        soft_pv(kb + 1, 1, False)
        return 0

    lax.fori_loop(0, n_open // 2, pair, 0)

    @pl.when(n_open % 2 == 1)
    def _():
        qk(kb_last, 1)
        soft_pv(kb_last - 1, 0, False)
        soft_pv(kb_last, 1, True)

    @pl.when(n_open % 2 == 0)
    def _():
        soft_pv(kb_last, 0, True)

    acc = [acc_scr[c] for c in range(n_c)]
    o_t = jnp.concatenate([jnp.concatenate([acc[c][:HEAD] / acc[c][HEAD:HEAD + 1] for c, (h2, _) in enumerate(chains)
                                            if h2 == hh], axis=1) for hh in range(2)], axis=0)
    o_ref[0] = o_t.T.astype(BF16)


def _fox(q_t, k_aug, v_t, off, q_norm, k_norm, *, tq=512, tk=FOX_KBLOCK, qw=MXU_DIM):
    bsz, nh, s, _ = k_aug.shape
    n_kblk = s // tk
    n_chain = 2 * (tq // qw)
    assert tq % qw == 0
    assert s % tq == 0 and s % tk == 0 and tk % tq == 0 and nh % 2 == 0 and off.shape == (bsz * nh * n_kblk,)
    qk_bound = 1.02 * LOG2E * HEAD ** 0.5 * jnp.max(jnp.abs(q_norm)) * jnp.max(jnp.abs(k_norm))
    thr = (2.0 * qk_bound + EXP2_UNDERFLOW).reshape(1).astype(F32)
    return pl.pallas_call(
        functools.partial(_fox_kernel, tq=tq, tk=tk, qw=qw, n_kblk=n_kblk),
        out_shape=jax.ShapeDtypeStruct((bsz, s, nh * HEAD), BF16),
        grid=(bsz, nh // 2, s // tq),
        in_specs=[
            pl.BlockSpec(memory_space=pltpu.SMEM),
            pl.BlockSpec(memory_space=pltpu.SMEM),
            pl.BlockSpec((1, 2, 2 * HEAD, tq), lambda b, hp, i: (b, hp, 0, i)),
            pl.BlockSpec((1, 2, s, 2 * HEAD), lambda b, hp, i: (b, hp, 0, 0)),
            pl.BlockSpec((1, 2, V_ROWS, s), lambda b, hp, i: (b, hp, 0, 0)),
        ],
        out_specs=pl.BlockSpec((1, tq, 2 * HEAD), lambda b, hp, i: (b, i, hp)),
        scratch_shapes=[pltpu.VMEM((tk, qw), F32)] * (2 * n_chain)
        + [pltpu.VMEM((n_chain, 1, qw), F32), pltpu.VMEM((n_chain, V_ROWS, qw), F32)],
        compiler_params=pltpu.CompilerParams(
            dimension_semantics=("parallel", "parallel", "arbitrary"), vmem_limit_bytes=VMEM_LIMIT),
        name="fox",
    )(off, thr, q_t, k_aug, v_t)


def _merge_kernel(x_ref, g_ref, wg_ref, ya_ref, yb_ref, pa_ref, pb_ref, wo_ref, o_ref):
    x = x_ref[...]
    d = x.shape[-1]
    h = _rms(x, g_ref[...]).astype(BF16)
    gates = _sigmoid(_dot(h, wg_ref[...]))
    ya = _dot(ya_ref[...], pa_ref[...])
    yb = _dot(yb_ref[...], pb_ref[...])
    mix = (gates[:, :d] * ya + gates[:, d:] * yb).astype(BF16)
    o_ref[...] = x + _dot(mix, wo_ref[...])


def _merge(x2d, mix_norm, w_gates, ya, yb, p_a, p_b, w_out, *, tm=512):
    m, d = x2d.shape
    assert m % tm == 0
    full = lambda t: pl.BlockSpec(t.shape, lambda i: (0, 0))
    tok = lambda n: pl.BlockSpec((tm, n), lambda i: (i, 0))
    consts = [mix_norm.reshape(1, d).astype(F32), w_gates.astype(BF16)]
    mats = [p_a.astype(BF16), p_b.astype(BF16), w_out.astype(BF16)]
    return pl.pallas_call(
        _merge_kernel,
        out_shape=jax.ShapeDtypeStruct((m, d), F32),
        grid=(m // tm,),
        in_specs=[tok(d), full(consts[0]), full(consts[1]), tok(D_MIX), tok(D_MIX)] + [full(t) for t in mats],
        out_specs=tok(d),
        compiler_params=pltpu.CompilerParams(dimension_semantics=("parallel",), vmem_limit_bytes=VMEM_LIMIT),
        name="merge",
    )(x2d, consts[0], consts[1], ya, yb, *mats)


def kernel(x, ffn1_norm, ffn1_w1, ffn1_w3, ffn1_w2, mix_norm, w_in, shift_mu, rwkv_w0, rwkv_w_up, rwkv_a0, rwkv_a_up, rwkv_g_up, rwkv_k_k, rwkv_k_a, rwkv_r_k, rwkv_lnx_g, rwkv_lnx_b, rwkv_proj, fox_f_bias, fox_q_norm, fox_k_norm, fox_proj, w_out, ffn2_norm, ffn2_w1, ffn2_w3, ffn2_w2, final_norm):
    bsz, s, d = x.shape
    depth = ffn1_norm.shape[0]
    n_gate0 = w_in.shape[-1] - 2 * d
    x2d = x.reshape(bsz * s, d)
    for l in range(depth):
        x2d = _ffn(x2d, ffn1_norm[l], ffn1_w1[l], ffn1_w3[l], ffn1_w2[l], final_norm, final_norm=False)
        (r, lw, k, v, kk, b, g, bonus, fq_t, fk_aug, fv_t, off) = _in_proj(
            x2d.reshape(bsz, s, d), mix_norm[l], w_in[l], shift_mu[l], rwkv_w0[l], rwkv_w_up[l], rwkv_a0[l],
            rwkv_a_up[l], rwkv_g_up[l], rwkv_k_k[l], rwkv_k_a[l], rwkv_r_k[l], fox_f_bias[l], fox_q_norm[l],
            fox_k_norm[l])
        ya = _rwkv(r, lw, k, v, kk, b, g, bonus, rwkv_lnx_g[l], rwkv_lnx_b[l])
        off = off[:, ::FOX_KBLOCK // (s // off.shape[1]), 0, :N_HEADS]
        off = jnp.transpose(off, (0, 2, 1)).reshape(-1)
        yb = _fox(fq_t, fk_aug, fv_t, off, fox_q_norm[l], fox_k_norm[l])
        x2d = _merge(x2d, mix_norm[l], w_in[l][:, n_gate0:], ya.reshape(bsz * s, D_MIX), yb.reshape(bsz * s, D_MIX),
                     rwkv_proj[l], fox_proj[l], w_out[l])
        x2d = _ffn(x2d, ffn2_norm[l], ffn2_w1[l], ffn2_w3[l], ffn2_w2[l], final_norm, final_norm=(l == depth - 1))
    return x2d.reshape(bsz, s, d)
```

```python
import functools

import jax
import jax.numpy as jnp
from jax import lax
from jax.experimental import pallas as pl
from jax.experimental.pallas import tpu as pltpu

F32 = jnp.float32
BF16 = jnp.bfloat16

HEAD = 64
N_HEADS = 8
D_MIX = HEAD * N_HEADS
LANE = 128
MXU_DIM = 256
CHUNK = 64
RMS_EPS = 1e-6
LNX_EPS = 64e-5
NEG_BIG = -1e30
EXP2_UNDERFLOW = 160.0
VMEM_LIMIT = 56 * 1024 * 1024

_NT = (((1,), (1,)), ((), ()))
_TN = (((0,), (0,)), ((), ()))


def _dot(a, b, dims=None):
    if dims is None:
        return jnp.dot(a, b, preferred_element_type=F32)
    return lax.dot_general(a, b, dims, preferred_element_type=F32)


def _bdot(a, b, dims=None):
    return _dot(a.astype(BF16), b.astype(BF16), dims)


def _rms(x, g):
    return x * lax.rsqrt(jnp.mean(x * x, axis=-1, keepdims=True) + RMS_EPS) * g


def _softplus(z):
    return jnp.maximum(z, 0.0) + jnp.log(1.0 + jnp.exp(-jnp.abs(z)))


def _sigmoid(z):
    return 1.0 / (1.0 + jnp.exp(-z))


def _head_sum(x, e):
    xb = x.astype(BF16)
    return jnp.concatenate([_dot(xb[:, j:j + MXU_DIM], e) for j in range(0, x.shape[1], MXU_DIM)], axis=1)


def _cumsum_rows(tri, x):
    hi = x.astype(BF16)
    r1 = x - hi.astype(F32)
    mid = r1.astype(BF16)
    lo = (r1 - mid.astype(F32)).astype(BF16)
    return _dot(tri, hi) + _dot(tri, mid) + _dot(tri, lo)


WEIGHT_CHUNKS = 8


def _load_cast(w_hbm, w_ref, stage, sem):
    rows = w_hbm.shape[0] // WEIGHT_CHUNKS

    def copy(c):
        return pltpu.make_async_copy(w_hbm.at[pl.ds(c * rows, rows)], stage.at[c % 2], sem.at[c % 2])

    copy(0).start()
    for c in range(WEIGHT_CHUNKS):
        if c + 1 < WEIGHT_CHUNKS:
            copy(c + 1).start()
        copy(c).wait()
        w_ref[pl.ds(c * rows, rows), :] = stage[c % 2].astype(BF16)


def _ffn_kernel(x_ref, g_ref, w1_hbm, w3_hbm, w2_hbm, gf_ref, o_ref, w1_ref, w3_ref, w2_ref, stage_up, stage_down, sem,
                *, final_norm):
    @pl.when(pl.program_id(0) == 0)
    def _():
        _load_cast(w1_hbm, w1_ref, stage_up, sem)
        _load_cast(w3_hbm, w3_ref, stage_up, sem)
        _load_cast(w2_hbm, w2_ref, stage_down, sem)

    x = x_ref[...]
    h = _rms(x, g_ref[...]).astype(BF16)
    a = _dot(h, w1_ref[...])
    b = _dot(h, w3_ref[...])
    u = (a * _sigmoid(a) * b).astype(BF16)
    y = x + 0.5 * _dot(u, w2_ref[...])
    if final_norm:
        y = _rms(y, gf_ref[...])
    o_ref[...] = y


def _resident(shape):
    return pl.BlockSpec(shape, lambda *_: (0,) * len(shape), pipeline_mode=pl.Buffered(1))


def _ffn(x2d, g, w1, w3, w2, gf, *, final_norm, tm=512):
    m, d = x2d.shape
    f_dim = w1.shape[1]
    assert m % tm == 0 and d % WEIGHT_CHUNKS == 0 and f_dim % WEIGHT_CHUNKS == 0
    in_hbm = pl.BlockSpec(memory_space=pl.ANY)
    return pl.pallas_call(
        functools.partial(_ffn_kernel, final_norm=final_norm),
        out_shape=jax.ShapeDtypeStruct((m, d), F32),
        grid=(m // tm,),
        in_specs=[pl.BlockSpec((tm, d), lambda i: (i, 0)), _resident((1, d)), in_hbm, in_hbm, in_hbm, _resident((1, d))],
        out_specs=pl.BlockSpec((tm, d), lambda i: (i, 0)),
        scratch_shapes=[
            pltpu.VMEM((d, f_dim), BF16), pltpu.VMEM((d, f_dim), BF16), pltpu.VMEM((f_dim, d), BF16),
            pltpu.VMEM((2, d // WEIGHT_CHUNKS, f_dim), F32), pltpu.VMEM((2, f_dim // WEIGHT_CHUNKS, d), F32),
            pltpu.SemaphoreType.DMA((2,)),
        ],
        compiler_params=pltpu.CompilerParams(dimension_semantics=("arbitrary",), vmem_limit_bytes=VMEM_LIMIT),
        name="ffn",
    )(x2d, g.reshape(1, d), w1, w3, w2, gf.reshape(1, d))


N_SHIFT = 3 * D_MIX + 3 * LANE
N_PROJ = N_SHIFT + 3 * D_MIX + LANE
LOG2E = 1.4426950408889634
V_ROWS = HEAD + 16
FOX_KBLOCK = 512


def _in_proj_kernel(x_ref, g_ref, w_ref, mu_ref, w0_ref, wup_ref, a0_ref, aup_ref, gup_ref, kk_ref, ka_ref,
                    rk_ref, fb_ref, qn_ref, kn_ref, e_ref,
                    r_out, lw_out, k_out, v_out, kk_out, b_out, g_out, bonus_out, fq_out, fk_out, fv_out, off_out,
                    carry_p, carry_c, carry_b, *, tm, n_sub, tiles_per_kblock):
    step = pl.program_id(1)

    @pl.when(step == 0)
    def _():
        carry_p[...] = jnp.zeros_like(carry_p)
        carry_c[...] = jnp.zeros_like(carry_c)
        carry_b[...] = jnp.zeros_like(carry_b)

    outs = (r_out, lw_out, k_out, v_out, kk_out, b_out, g_out, bonus_out)
    for t in range(n_sub):
        rows = pl.ds(t * tm, tm)
        _in_proj_tile(step * n_sub + t, x_ref.at[0, rows], g_ref, w_ref, mu_ref, w0_ref, wup_ref, a0_ref, aup_ref, gup_ref,
                      kk_ref, ka_ref, rk_ref, fb_ref, qn_ref, kn_ref, e_ref, *[o.at[0, rows] for o in outs],
                      fq_out.at[0, :, :, rows], fk_out.at[0, :, rows], fv_out.at[0, :, :, rows], off_out.at[0, t],
                      carry_p, carry_c, carry_b, tm=tm, tiles_per_kblock=tiles_per_kblock)


def _in_proj_tile(i, x_ref, g_ref, w_ref, mu_ref, w0_ref, wup_ref, a0_ref, aup_ref, gup_ref, kk_ref, ka_ref,
                  rk_ref, fb_ref, qn_ref, kn_ref, e_ref,
                  r_out, lw_out, k_out, v_out, kk_out, b_out, g_out, bonus_out, fq_out, fk_out, fv_out, off_out,
                  carry_p, carry_c, carry_b, *, tm, tiles_per_kblock):
    h = _rms(x_ref[...], g_ref[...]).astype(BF16)
    p = _dot(h, w_ref[...])

    ps = p[:, :N_SHIFT]
    row = lax.broadcasted_iota(jnp.int32, (tm, 1), 0)
    prev = jnp.where(row == 0, carry_p[...], pltpu.roll(ps, 1, axis=0))
    carry_p[...] = ps[tm - 1:tm, :]
    sh = ps + (prev - ps) * mu_ref[...]

    e = e_ref[...]
    r = sh[:, 0:D_MIX]
    k = sh[:, D_MIX:2 * D_MIX]
    v = sh[:, 2 * D_MIX:3 * D_MIX]
    w_lo = sh[:, 3 * D_MIX:3 * D_MIX + LANE]
    a_lo = sh[:, 3 * D_MIX + LANE:3 * D_MIX + 2 * LANE]
    g_lo = sh[:, 3 * D_MIX + 2 * LANE:N_SHIFT]

    w_log = -_softplus(-(w0_ref[...] + _bdot(jnp.tanh(w_lo), wup_ref[...]))) - 0.5
    lw = -jnp.exp(w_log)
    a = _sigmoid(a0_ref[...] + _bdot(a_lo, aup_ref[...]))
    g = _bdot(_sigmoid(g_lo), gup_ref[...])
    kk = k * kk_ref[...]
    kk = kk * lax.rsqrt(jnp.maximum(_head_sum(kk * kk, e), 1e-24))
    k_mod = k * (1.0 + (a - 1.0) * ka_ref[...])
    bonus = _head_sum(r * k_mod * rk_ref[...], e) * v

    r_out[...] = r
    lw_out[...] = lw
    k_out[...] = k_mod
    v_out[...] = v.astype(BF16)
    kk_out[...] = kk
    b_out[...] = kk * a
    g_out[...] = g.astype(BF16)
    bonus_out[...] = bonus.astype(BF16)

    fq = p[:, N_SHIFT:N_SHIFT + D_MIX]
    fk = p[:, N_SHIFT + D_MIX:N_SHIFT + 2 * D_MIX]
    fv = p[:, N_SHIFT + 2 * D_MIX:N_SHIFT + 3 * D_MIX]
    f_lo = p[:, N_SHIFT + 3 * D_MIX:N_PROJ]
    inv_head = 1.0 / HEAD
    qn = fq * lax.rsqrt(_head_sum(fq * fq, e) * inv_head + RMS_EPS) * qn_ref[...] * (LOG2E * HEAD ** -0.5)
    kn = fk * lax.rsqrt(_head_sum(fk * fk, e) * inv_head + RMS_EPS) * kn_ref[...]
    z = f_lo + fb_ref[...]
    log_f = jnp.minimum(z, 0.0) - jnp.log(1.0 + jnp.exp(-jnp.abs(z)))
    tri = jnp.where(lax.broadcasted_iota(jnp.int32, (tm, tm), 1) <= lax.broadcasted_iota(jnp.int32, (tm, tm), 0),
                    1.0, 0.0).astype(BF16)
    c_loc = _cumsum_rows(tri, log_f)
    off_out[...] = carry_c[...] * LOG2E
    in_blk = jnp.where(i % tiles_per_kblock == 0, 0.0, carry_b[...])
    carry_c[...] = carry_c[...] + c_loc[tm - 1:tm, :]
    carry_b[...] = in_blk + c_loc[tm - 1:tm, :]
    c2 = (c_loc + in_blk) * LOG2E

    lane = lax.broadcasted_iota(jnp.int32, (tm, HEAD), 1)
    sub = lax.broadcasted_iota(jnp.int32, (HEAD, tm), 0)
    q_tail = jnp.where(sub < 3, -1.0, 0.0)
    v_tail = jnp.where(lax.broadcasted_iota(jnp.int32, (V_ROWS - HEAD, tm), 0) == 0, 1.0, 0.0)
    for pr in range(N_HEADS // 2):
        ps2 = slice(pr * 2 * HEAD, (pr + 1) * 2 * HEAD)
        q_t = qn[:, ps2].T
        v_t = fv[:, ps2].T
        for half in range(2):
            hh = 2 * pr + half
            rows = slice(half * HEAD, (half + 1) * HEAD)
            fq_out[hh] = jnp.concatenate([q_t[rows], q_tail], axis=0).astype(BF16)
            fv_out[hh] = jnp.concatenate([v_t[rows], v_tail], axis=0).astype(BF16)
            c_h = c2[:, hh:hh + 1]
            c_hi = c_h.astype(BF16).astype(F32)
            c_mid = (c_h - c_hi).astype(BF16).astype(F32)
            c_lo = c_h - c_hi - c_mid
            c_cols = jnp.where(lane == 0, c_hi, jnp.where(lane == 1, c_mid, jnp.where(lane == 2, c_lo, 0.0)))
            fk_out[hh] = jnp.concatenate([kn[:, hh * HEAD:(hh + 1) * HEAD], c_cols], axis=1).astype(BF16)


def _pad_cols(w, n):
    return jnp.pad(w, ((0, 0), (0, n - w.shape[1])))


def _pad_rows(w, n):
    return jnp.pad(w, ((0, n - w.shape[0]), (0, 0)))


def _in_proj(x, mix_norm, w_in, shift_mu, w0, w_up, a0, a_up, g_up, k_k, k_a, r_k, f_bias, q_norm, k_norm, *, tm=256, n_sub=2):
    bsz, s, d = x.shape
    lw_, la_, lg_ = w_up.shape[0], a_up.shape[0], g_up.shape[0]
    c0 = 3 * D_MIX
    c1 = c0 + lw_ + la_ + lg_
    c2 = c1 + 3 * D_MIX
    c3 = c2 + N_HEADS

    def seg(t, pad):
        return jnp.concatenate([
            t[..., :c0],
            pad(t[..., c0:c0 + lw_], LANE), pad(t[..., c0 + lw_:c0 + lw_ + la_], LANE), pad(t[..., c0 + lw_ + la_:c1], LANE),
        ], axis=-1)

    w_cat = jnp.concatenate([seg(w_in, _pad_cols), w_in[:, c1:c2], _pad_cols(w_in[:, c2:c3], LANE)], axis=-1).astype(BF16)
    mu_cat = seg(shift_mu.reshape(1, -1), _pad_cols)
    row = lambda t: t.reshape(1, -1).astype(F32)
    e = _head_mask()
    tile = lambda t: jnp.tile(t.reshape(1, HEAD), (1, N_HEADS)).astype(F32)
    consts = [
        row(mix_norm), w_cat, mu_cat, row(w0), _pad_rows(w_up, LANE).astype(BF16), row(a0),
        _pad_rows(a_up, LANE).astype(BF16), _pad_rows(g_up, LANE).astype(BF16), row(k_k), row(k_a), row(r_k),
        _pad_cols(row(f_bias), LANE), tile(q_norm), tile(k_norm), e,
    ]
    const_specs = [_resident(c.shape) for c in consts]
    tm, tile_rows = tm * n_sub, tm
    tok = lambda: pl.BlockSpec((1, tm, D_MIX), lambda b, i: (b, i, 0))
    f32_tok = jax.ShapeDtypeStruct((bsz, s, D_MIX), F32)
    bf_tok = jax.ShapeDtypeStruct((bsz, s, D_MIX), BF16)
    assert s % tm == 0 and FOX_KBLOCK % tile_rows == 0
    return pl.pallas_call(
        functools.partial(_in_proj_kernel, tm=tile_rows, n_sub=n_sub, tiles_per_kblock=FOX_KBLOCK // tile_rows),
        out_shape=[f32_tok, f32_tok, f32_tok, bf_tok, f32_tok, f32_tok, bf_tok, bf_tok] + [
            jax.ShapeDtypeStruct((bsz, N_HEADS, 2 * HEAD, s), BF16),
            jax.ShapeDtypeStruct((bsz, N_HEADS, s, 2 * HEAD), BF16),
            jax.ShapeDtypeStruct((bsz, N_HEADS, V_ROWS, s), BF16),
            jax.ShapeDtypeStruct((bsz, s // tile_rows, 1, LANE), F32),
        ],
        grid=(bsz, s // tm),
        in_specs=[pl.BlockSpec((1, tm, d), lambda b, i: (b, i, 0))] + const_specs,
        out_specs=[tok() for _ in range(8)] + [
            pl.BlockSpec((1, N_HEADS, 2 * HEAD, tm), lambda b, i: (b, 0, 0, i)),
            pl.BlockSpec((1, N_HEADS, tm, 2 * HEAD), lambda b, i: (b, 0, i, 0)),
            pl.BlockSpec((1, N_HEADS, V_ROWS, tm), lambda b, i: (b, 0, 0, i)),
            pl.BlockSpec((1, n_sub, 1, LANE), lambda b, i: (b, i, 0, 0)),
        ],
        scratch_shapes=[pltpu.VMEM((1, N_SHIFT), F32), pltpu.VMEM((1, LANE), F32), pltpu.VMEM((1, LANE), F32)],
        compiler_params=pltpu.CompilerParams(
            dimension_semantics=("parallel", "arbitrary"), vmem_limit_bytes=VMEM_LIMIT),
        name="in_proj",
    )(x, *consts)


def _rwkv_kernel(r_ref, lw_ref, k_ref, v_ref, kk_ref, b_ref, g_ref, bonus_ref, lng_ref, lnb_ref, e_ref, y_ref, s_scr, *, tb):
    i = pl.program_id(1)

    @pl.when(i == 0)
    def _():
        s_scr[...] = jnp.zeros_like(s_scr)

    c = CHUNK
    pw = 2 * HEAD
    n_chunk = tb // c
    n_pair = N_HEADS // 2
    rr = lax.broadcasted_iota(jnp.int32, (c, pw), 0)
    ll = lax.broadcasted_iota(jnp.int32, (c, pw), 1)
    cc = ll % c
    lo = ll < c
    strict = cc < rr
    incl = cc <= rr
    eye = (cc == rr).astype(F32)
    r2 = lax.broadcasted_iota(jnp.int32, (2 * c, pw), 0)
    l2 = lax.broadcasted_iota(jnp.int32, (2 * c, pw), 1)
    lo2 = l2 < c
    diag2 = (r2 < c) == lo2
    zeros = jnp.zeros((c, pw), F32)

    def level_mask(n):
        return ((rr // (2 * n)) == (cc // (2 * n))) & (((rr // n) % 2) == 1) & (((cc // n) % 2) == 0)

    def bd(x):
        return jnp.where(diag2, jnp.concatenate([x, x], axis=0), 0.0).astype(BF16)

    def bd_sw(x):
        return jnp.where(diag2, 0.0, jnp.concatenate([x, x], axis=0)).astype(BF16)

    slab = min(tb, MXU_DIM)
    rb = lax.broadcasted_iota(jnp.int32, (slab, slab), 0)
    cb = lax.broadcasted_iota(jnp.int32, (slab, slab), 1)
    tri = jnp.where((cb <= rb) & ((rb // c) == (cb // c)), 1.0, 0.0).astype(BF16)
    lw = lw_ref[0]
    cs = jnp.concatenate([_cumsum_rows(tri, lw[j:j + slab]) for j in range(0, tb, slab)], axis=0)
    w_inv = jnp.exp(-cs)
    rt = r_ref[0] * jnp.exp(cs)
    at = -kk_ref[0] * jnp.exp(cs - lw)
    bi = b_ref[0] * w_inv
    ki = k_ref[0] * w_inv
    v = v_ref[0].astype(F32)

    chains = [(j, p) for j in range(n_chunk) for p in range(n_pair)]
    cut = lambda x, j, p: x[j * c:(j + 1) * c, p * pw:(p + 1) * pw]
    w_tot = [jnp.exp(cs[(j + 1) * c - 1:(j + 1) * c, :]) for j in range(n_chunk)]
    a_t = [cut(at, j, p) for j, p in chains]
    r_t = [cut(rt, j, p) for j, p in chains]
    b_i = [cut(bi, j, p) for j, p in chains]
    k_i = [cut(ki, j, p) for j, p in chains]
    v_p = [cut(v, j, p) for j, p in chains]
    n = len(chains)

    a_ab, a_ak, a_rb, a_rk = [], [], [], []
    for q in range(n):
        ar = jnp.concatenate([a_t[q], r_t[q]], axis=0)
        gx = _dot(jnp.where(lo2, ar, 0.0).astype(BF16), jnp.concatenate([b_i[q], k_i[q]], axis=0).astype(BF16), _NT)
        gy = _dot(jnp.where(lo2, 0.0, ar).astype(BF16), jnp.concatenate([k_i[q], b_i[q]], axis=0).astype(BF16), _NT)
        a_ab.append(jnp.where(strict, jnp.where(lo, gx[:c], gy[:c]), 0.0))
        a_ak.append(jnp.where(strict, jnp.where(lo, gy[:c], gx[:c]), 0.0))
        a_rb.append(jnp.where(incl, jnp.where(lo, gx[c:], gy[c:]), 0.0))
        a_rk.append(jnp.where(incl, jnp.where(lo, gy[c:], gx[c:]), 0.0))

    t = [eye + jnp.where(level_mask(1), a_ab[q], 0.0) for q in range(n)]
    for m in (2, 4, 8, 16, 32):
        x = [_dot(jnp.where(level_mask(m), a_ab[q], 0.0).astype(BF16), bd(t[q])) for q in range(n)]
        t = [t[q] + _dot(t[q].astype(BF16), bd(x[q])) for q in range(n)]

    v_sw = [bd_sw(v_p[q]) for q in range(n)]
    akv = [_dot(a_ak[q].astype(BF16), v_sw[q]) for q in range(n)]
    x = [_dot(t[q].astype(BF16), jnp.concatenate([bd(a_t[q]), bd(akv[q])], axis=1)) for q in range(n)]
    a_p = [x[q][:, :pw] for q in range(n)]
    u_0 = [x[q][:, pw:] for q in range(n)]
    zero_bd = jnp.zeros((2 * c, pw), BF16)
    d1 = [_dot(jnp.concatenate([a_rb[q], a_rk[q]], axis=1).astype(BF16),
               jnp.concatenate([jnp.concatenate([bd(a_p[q]), bd(u_0[q])], axis=1),
                                jnp.concatenate([zero_bd, v_sw[q]], axis=1)], axis=0)) for q in range(n)]
    d2 = []
    for q, (j, p) in enumerate(chains):
        wt = w_tot[j][:, p * pw:(p + 1) * pw]
        lhs = jnp.concatenate([jnp.concatenate([a_p[q], u_0[q]], axis=1),
                               jnp.concatenate([zeros, v_p[q]], axis=1)], axis=0).astype(BF16)
        rhs = jnp.concatenate([b_i[q] * wt, k_i[q] * wt], axis=0).astype(BF16)
        d2.append(_dot(lhs, rhs, _TN))

    s = [s_scr[p] for p in range(n_pair)]
    ys = [None] * n
    for j in range(n_chunk):
        for p in range(n_pair):
            q = j * n_pair + p
            mt = jnp.where(diag2, d2[q][:pw], 0.0).astype(BF16)
            nt = jnp.where(lo, d2[q][pw:pw + c], d2[q][pw + c:])
            ys[q] = d1[q][:, pw:] + _dot((r_t[q] + d1[q][:, :pw]).astype(BF16), bd(s[p]), _NT)
            s[p] = s[p] * w_tot[j][:, p * pw:(p + 1) * pw] + _dot(s[p].astype(BF16), mt) + nt
    for p in range(n_pair):
        s_scr[p] = s[p]

    y = jnp.concatenate([jnp.concatenate([ys[j * n_pair + p] for p in range(n_pair)], axis=1) for j in range(n_chunk)], axis=0)
    e = e_ref[...]
    inv_head = 1.0 / HEAD
    yc = y - _head_sum(y, e) * inv_head
    yn = yc * lax.rsqrt(_head_sum(yc * yc, e) * inv_head + LNX_EPS)
    y_ref[0] = ((yn * lng_ref[...] + lnb_ref[...] + bonus_ref[0]) * g_ref[0]).astype(BF16)


def _head_mask():
    head_id = jnp.arange(MXU_DIM) // HEAD
    return (head_id[:, None] == head_id[None, :]).astype(BF16)


def _rwkv(r, lw, k, v, kk, b, g, bonus, lnx_g, lnx_b, *, tb=512):
    bsz, s, _ = r.shape
    assert s % tb == 0 and tb % CHUNK == 0
    tok = pl.BlockSpec((1, tb, D_MIX), lambda bb, i: (bb, i, 0))
    vec = pl.BlockSpec((1, D_MIX), lambda bb, i: (0, 0))
    return pl.pallas_call(
        functools.partial(_rwkv_kernel, tb=tb),
        out_shape=jax.ShapeDtypeStruct((bsz, s, D_MIX), BF16),
        grid=(bsz, s // tb),
        in_specs=[tok] * 8 + [vec, vec, pl.BlockSpec((MXU_DIM, MXU_DIM), lambda bb, i: (0, 0))],
        out_specs=tok,
        scratch_shapes=[pltpu.VMEM((N_HEADS // 2, HEAD, 2 * HEAD), F32)],
        compiler_params=pltpu.CompilerParams(
            dimension_semantics=("parallel", "arbitrary"), vmem_limit_bytes=VMEM_LIMIT),
        name="rwkv",
    )(r, lw, k, v, kk, b, g, bonus, lnx_g.reshape(1, -1).astype(F32), lnx_b.reshape(1, -1).astype(F32), _head_mask())


def _fox_kernel(off_ref, thr_ref, q_ref, k_ref, v_ref, o_ref, *scratch, tq, tk, qw, n_kblk):
    b, hp, qb = pl.program_id(0), pl.program_id(1), pl.program_id(2)
    kb_last = (qb * tq) // tk
    delta = qb * tq - kb_last * tk
    chains = [(hh, g) for hh in range(2) for g in range(tq // qw)]
    n_c = len(chains)
    s_scr = (scratch[:n_c], scratch[n_c:2 * n_c])
    m_scr, acc_scr = scratch[2 * n_c:]
    base = [(b * N_HEADS + 2 * hp + hh) * n_kblk for hh in range(2)]
    off_d = [off_ref[base[hh] + kb_last] for hh in range(2)]

    thr = thr_ref[0]

    def needed(kb):
        nxt = jnp.minimum(kb + 1, kb_last)
        return jnp.minimum(off_ref[base[0] + nxt] - off_d[0], off_ref[base[1] + nxt] - off_d[1]) <= thr

    kb_first = lax.while_loop(lambda kb: jnp.logical_and(kb > 0, needed(kb - 1)), lambda kb: kb - 1, kb_last)
    n_open = kb_last - kb_first

    def qk(kb, slot):
        ksl = pl.ds(pl.multiple_of(kb * tk, tk), tk)
        for c, (hh, g) in enumerate(chains):
            s_scr[slot][c][...] = _dot(k_ref[0, hh, ksl, :], q_ref[0, hh, :, g * qw:(g + 1) * qw])

    def soft_pv(kb, slot, masked):
        ksl = pl.ds(pl.multiple_of(kb * tk, tk), tk)
        for c, (hh, g) in enumerate(chains):
            s = s_scr[slot][c][...]
            if masked:
                key = lax.broadcasted_iota(jnp.int32, (tk, qw), 0)
                qry = lax.broadcasted_iota(jnp.int32, (tk, qw), 1)
                s = jnp.where(key - qry <= delta + g * qw, s, NEG_BIG)
            off = off_ref[base[hh] + kb] - off_d[hh]
            m = m_scr[c]
            m_new = jnp.maximum(m, jnp.max(s, axis=0, keepdims=True) - off)
            p = jnp.exp2(s - (m_new + off)).astype(BF16)
            acc_scr[c] = jnp.exp2(m - m_new) * acc_scr[c] + _dot(v_ref[0, hh, :, ksl], p)
            m_scr[c] = m_new

    m_scr[...] = jnp.full(m_scr.shape, NEG_BIG, F32)
    acc_scr[...] = jnp.zeros(acc_scr.shape, F32)
    qk(kb_first, 0)

    def pair(i, _):
        kb = kb_first + 2 * i
        qk(kb + 1, 1)
        soft_pv(kb, 0, False)
        qk(kb + 2, 0)
        soft_pv(kb + 1, 1, False)
        return 0

    lax.fori_loop(0, n_open // 2, pair, 0)

    @pl.when(n_open % 2 == 1)
    def _():
        qk(kb_last, 1)
        soft_pv(kb_last - 1, 0, False)
        soft_pv(kb_last, 1, True)

    @pl.when(n_open % 2 == 0)
    def _():
        soft_pv(kb_last, 0, True)

    acc = [acc_scr[c] for c in range(n_c)]
    o_t = jnp.concatenate([jnp.concatenate([acc[c][:HEAD] / acc[c][HEAD:HEAD + 1] for c, (h2, _) in enumerate(chains)
                                            if h2 == hh], axis=1) for hh in range(2)], axis=0)
    o_ref[0] = o_t.T.astype(BF16)


def _fox(q_t, k_aug, v_t, off, q_norm, k_norm, *, tq=512, tk=FOX_KBLOCK, qw=MXU_DIM):
    bsz, nh, s, _ = k_aug.shape
    n_kblk = s // tk
    n_chain = 2 * (tq // qw)
    assert tq % qw == 0
    assert s % tq == 0 and s % tk == 0 and tk % tq == 0 and nh % 2 == 0 and off.shape == (bsz * nh * n_kblk,)
    qk_bound = 1.02 * LOG2E * HEAD ** 0.5 * jnp.max(jnp.abs(q_norm)) * jnp.max(jnp.abs(k_norm))
    thr = (2.0 * qk_bound + EXP2_UNDERFLOW).reshape(1).astype(F32)
    return pl.pallas_call(
        functools.partial(_fox_kernel, tq=tq, tk=tk, qw=qw, n_kblk=n_kblk),
        out_shape=jax.ShapeDtypeStruct((bsz, s, nh * HEAD), BF16),
        grid=(bsz, nh // 2, s // tq),
        in_specs=[
            pl.BlockSpec(memory_space=pltpu.SMEM),
            pl.BlockSpec(memory_space=pltpu.SMEM),
            pl.BlockSpec((1, 2, 2 * HEAD, tq), lambda b, hp, i: (b, hp, 0, i)),
            pl.BlockSpec((1, 2, s, 2 * HEAD), lambda b, hp, i: (b, hp, 0, 0)),
            pl.BlockSpec((1, 2, V_ROWS, s), lambda b, hp, i: (b, hp, 0, 0)),
        ],
        out_specs=pl.BlockSpec((1, tq, 2 * HEAD), lambda b, hp, i: (b, i, hp)),
        scratch_shapes=[pltpu.VMEM((tk, qw), F32)] * (2 * n_chain)
        + [pltpu.VMEM((n_chain, 1, qw), F32), pltpu.VMEM((n_chain, V_ROWS, qw), F32)],
        compiler_params=pltpu.CompilerParams(
            dimension_semantics=("parallel", "parallel", "arbitrary"), vmem_limit_bytes=VMEM_LIMIT),
        name="fox",
    )(off, thr, q_t, k_aug, v_t)


def _merge_kernel(x_ref, g_ref, wg_ref, ya_ref, yb_ref, pa_ref, pb_ref, wo_ref, o_ref):
    x = x_ref[...]
    d = x.shape[-1]
    h = _rms(x, g_ref[...]).astype(BF16)
    gates = _sigmoid(_dot(h, wg_ref[...]))
    ya = _dot(ya_ref[...], pa_ref[...])
    yb = _dot(yb_ref[...], pb_ref[...])
    mix = (gates[:, :d] * ya + gates[:, d:] * yb).astype(BF16)
    o_ref[...] = x + _dot(mix, wo_ref[...])


def _merge(x2d, mix_norm, w_gates, ya, yb, p_a, p_b, w_out, *, tm=512):
    m, d = x2d.shape
    assert m % tm == 0
    full = lambda t: pl.BlockSpec(t.shape, lambda i: (0, 0))
    tok = lambda n: pl.BlockSpec((tm, n), lambda i: (i, 0))
    consts = [mix_norm.reshape(1, d).astype(F32), w_gates.astype(BF16)]
    mats = [p_a.astype(BF16), p_b.astype(BF16), w_out.astype(BF16)]
    return pl.pallas_call(
        _merge_kernel,
        out_shape=jax.ShapeDtypeStruct((m, d), F32),
        grid=(m // tm,),
        in_specs=[tok(d), full(consts[0]), full(consts[1]), tok(D_MIX), tok(D_MIX)] + [full(t) for t in mats],
        out_specs=tok(d),
        compiler_params=pltpu.CompilerParams(dimension_semantics=("parallel",), vmem_limit_bytes=VMEM_LIMIT),
        name="merge",
    )(x2d, consts[0], consts[1], ya, yb, *mats)


def kernel(x, ffn1_norm, ffn1_w1, ffn1_w3, ffn1_w2, mix_norm, w_in, shift_mu, rwkv_w0, rwkv_w_up, rwkv_a0, rwkv_a_up, rwkv_g_up, rwkv_k_k, rwkv_k_a, rwkv_r_k, rwkv_lnx_g, rwkv_lnx_b, rwkv_proj, fox_f_bias, fox_q_norm, fox_k_norm, fox_proj, w_out, ffn2_norm, ffn2_w1, ffn2_w3, ffn2_w2, final_norm):
    bsz, s, d = x.shape
    depth = ffn1_norm.shape[0]
    n_gate0 = w_in.shape[-1] - 2 * d
    x2d = x.reshape(bsz * s, d)
    for l in range(depth):
        x2d = _ffn(x2d, ffn1_norm[l], ffn1_w1[l], ffn1_w3[l], ffn1_w2[l], final_norm, final_norm=False)
        (r, lw, k, v, kk, b, g, bonus, fq_t, fk_aug, fv_t, off) = _in_proj(
            x2d.reshape(bsz, s, d), mix_norm[l], w_in[l], shift_mu[l], rwkv_w0[l], rwkv_w_up[l], rwkv_a0[l],
            rwkv_a_up[l], rwkv_g_up[l], rwkv_k_k[l], rwkv_k_a[l], rwkv_r_k[l], fox_f_bias[l], fox_q_norm[l],
            fox_k_norm[l])
        ya = _rwkv(r, lw, k, v, kk, b, g, bonus, rwkv_lnx_g[l], rwkv_lnx_b[l])
        off = off[:, ::FOX_KBLOCK // (s // off.shape[1]), 0, :N_HEADS]
        off = jnp.transpose(off, (0, 2, 1)).reshape(-1)
        yb = _fox(fq_t, fk_aug, fv_t, off, fox_q_norm[l], fox_k_norm[l])
        x2d = _merge(x2d, mix_norm[l], w_in[l][:, n_gate0:], ya.reshape(bsz * s, D_MIX), yb.reshape(bsz * s, D_MIX),
                     rwkv_proj[l], fox_proj[l], w_out[l])
        x2d = _ffn(x2d, ffn2_norm[l], ffn2_w1[l], ffn2_w3[l], ffn2_w2[l], final_norm, final_norm=(l == depth - 1))
    return x2d.reshape(bsz, s, d)
```

```python
import functools

import jax
import jax.numpy as jnp
from jax import lax
from jax.experimental import pallas as pl
from jax.experimental.pallas import tpu as pltpu

F32 = jnp.float32
BF16 = jnp.bfloat16

HEAD = 64
N_HEADS = 8
D_MIX = HEAD * N_HEADS
LANE = 128
MXU_DIM = 256
CHUNK = 64
RMS_EPS = 1e-6
LNX_EPS = 64e-5
NEG_BIG = -1e30
EXP2_UNDERFLOW = 160.0
VMEM_LIMIT = 56 * 1024 * 1024

_NT = (((1,), (1,)), ((), ()))
_TN = (((0,), (0,)), ((), ()))


def _dot(a, b, dims=None):
    if dims is None:
        return jnp.dot(a, b, preferred_element_type=F32)
    return lax.dot_general(a, b, dims, preferred_element_type=F32)


def _bdot(a, b, dims=None):
    return _dot(a.astype(BF16), b.astype(BF16), dims)


def _rms(x, g):
    return x * lax.rsqrt(jnp.mean(x * x, axis=-1, keepdims=True) + RMS_EPS) * g


def _softplus(z):
    return jnp.maximum(z, 0.0) + jnp.log(1.0 + jnp.exp(-jnp.abs(z)))


def _sigmoid(z):
    return 1.0 / (1.0 + jnp.exp(-z))


def _head_sum(x, e):
    xb = x.astype(BF16)
    return jnp.concatenate([_dot(xb[:, j:j + MXU_DIM], e) for j in range(0, x.shape[1], MXU_DIM)], axis=1)


def _cumsum_rows(tri, x):
    hi = x.astype(BF16)
    r1 = x - hi.astype(F32)
    mid = r1.astype(BF16)
    lo = (r1 - mid.astype(F32)).astype(BF16)
    return _dot(tri, hi) + _dot(tri, mid) + _dot(tri, lo)


WEIGHT_CHUNKS = 8


def _load_cast(w_hbm, w_ref, stage, sem):
    rows = w_hbm.shape[0] // WEIGHT_CHUNKS

    def copy(c):
        return pltpu.make_async_copy(w_hbm.at[pl.ds(c * rows, rows)], stage.at[c % 2], sem.at[c % 2])

    copy(0).start()
    for c in range(WEIGHT_CHUNKS):
        if c + 1 < WEIGHT_CHUNKS:
            copy(c + 1).start()
        copy(c).wait()
        w_ref[pl.ds(c * rows, rows), :] = stage[c % 2].astype(BF16)


def _ffn_kernel(x_ref, g_ref, w1_hbm, w3_hbm, w2_hbm, gf_ref, o_ref, w1_ref, w3_ref, w2_ref, stage_up, stage_down, sem,
                *, final_norm):
    @pl.when(pl.program_id(0) == 0)
    def _():
        _load_cast(w1_hbm, w1_ref, stage_up, sem)
        _load_cast(w3_hbm, w3_ref, stage_up, sem)
        _load_cast(w2_hbm, w2_ref, stage_down, sem)

    x = x_ref[...]
    h = _rms(x, g_ref[...]).astype(BF16)
    a = _dot(h, w1_ref[...])
    b = _dot(h, w3_ref[...])
    u = (a * _sigmoid(a) * b).astype(BF16)
    y = x + 0.5 * _dot(u, w2_ref[...])
    if final_norm:
        y = _rms(y, gf_ref[...])
    o_ref[...] = y


def _resident(shape):
    return pl.BlockSpec(shape, lambda *_: (0,) * len(shape), pipeline_mode=pl.Buffered(1))


def _ffn(x2d, g, w1, w3, w2, gf, *, final_norm, tm=512):
    m, d = x2d.shape
    f_dim = w1.shape[1]
    assert m % tm == 0 and d % WEIGHT_CHUNKS == 0 and f_dim % WEIGHT_CHUNKS == 0
    in_hbm = pl.BlockSpec(memory_space=pl.ANY)
    return pl.pallas_call(
        functools.partial(_ffn_kernel, final_norm=final_norm),
        out_shape=jax.ShapeDtypeStruct((m, d), F32),
        grid=(m // tm,),
        in_specs=[pl.BlockSpec((tm, d), lambda i: (i, 0)), _resident((1, d)), in_hbm, in_hbm, in_hbm, _resident((1, d))],
        out_specs=pl.BlockSpec((tm, d), lambda i: (i, 0)),
        scratch_shapes=[
            pltpu.VMEM((d, f_dim), BF16), pltpu.VMEM((d, f_dim), BF16), pltpu.VMEM((f_dim, d), BF16),
            pltpu.VMEM((2, d // WEIGHT_CHUNKS, f_dim), F32), pltpu.VMEM((2, f_dim // WEIGHT_CHUNKS, d), F32),
            pltpu.SemaphoreType.DMA((2,)),
        ],
        compiler_params=pltpu.CompilerParams(dimension_semantics=("arbitrary",), vmem_limit_bytes=VMEM_LIMIT),
        name="ffn",
    )(x2d, g.reshape(1, d), w1, w3, w2, gf.reshape(1, d))


N_SHIFT = 3 * D_MIX + 3 * LANE
N_PROJ = N_SHIFT + 3 * D_MIX + LANE
LOG2E = 1.4426950408889634
V_ROWS = HEAD + 16
FOX_KBLOCK = 512


def _in_proj_kernel(x_ref, g_ref, w_hbm, mu_ref, w0_ref, wup_ref, a0_ref, aup_ref, gup_ref, kk_ref, ka_ref,
                    rk_ref, fb_ref, qn_ref, kn_ref, e_ref,
                    r_out, lw_out, k_out, v_out, kk_out, b_out, g_out, bonus_out, fq_out, fk_out, fv_out, off_out,
                    wg_out, carry_p, carry_c, carry_b, w_ref, stage, sem, *, tm, n_sub, tiles_per_kblock, col_map, gate_cols):
    step = pl.program_id(1)

    @pl.when(jnp.logical_and(pl.program_id(0) == 0, step == 0))
    def _():
        w_ref[...] = jnp.zeros_like(w_ref)
        rows = w_hbm.shape[0] // WEIGHT_CHUNKS

        def copy(c):
            return pltpu.make_async_copy(w_hbm.at[pl.ds(c * rows, rows)], stage.at[c % 2], sem.at[c % 2])

        copy(0).start()
        for c in range(WEIGHT_CHUNKS):
            if c + 1 < WEIGHT_CHUNKS:
                copy(c + 1).start()
            copy(c).wait()
            blk = stage[c % 2]
            dst_rows = pl.ds(c * rows, rows)
            for src, width, dst in col_map:
                w_ref[dst_rows, dst:dst + width] = blk[:, src:src + width].astype(BF16)
            wg_out[dst_rows, :] = blk[:, gate_cols[0]:gate_cols[1]].astype(BF16)

    @pl.when(step == 0)
    def _():
        carry_p[...] = jnp.zeros_like(carry_p)
        carry_c[...] = jnp.zeros_like(carry_c)
        carry_b[...] = jnp.zeros_like(carry_b)

    outs = (r_out, lw_out, k_out, v_out, kk_out, b_out, g_out, bonus_out)
    for t in range(n_sub):
        rows = pl.ds(t * tm, tm)
        _in_proj_tile(step * n_sub + t, x_ref.at[0, rows], g_ref, w_ref, mu_ref, w0_ref, wup_ref, a0_ref, aup_ref, gup_ref,
                      kk_ref, ka_ref, rk_ref, fb_ref, qn_ref, kn_ref, e_ref, *[o.at[0, rows] for o in outs],
                      fq_out.at[0, :, :, rows], fk_out.at[0, :, rows], fv_out.at[0, :, :, rows], off_out.at[0, t],
                      carry_p, carry_c, carry_b, tm=tm, tiles_per_kblock=tiles_per_kblock)


def _in_proj_tile(i, x_ref, g_ref, w_ref, mu_ref, w0_ref, wup_ref, a0_ref, aup_ref, gup_ref, kk_ref, ka_ref,
                  rk_ref, fb_ref, qn_ref, kn_ref, e_ref,
                  r_out, lw_out, k_out, v_out, kk_out, b_out, g_out, bonus_out, fq_out, fk_out, fv_out, off_out,
                  carry_p, carry_c, carry_b, *, tm, tiles_per_kblock):
    h = _rms(x_ref[...], g_ref[...]).astype(BF16)
    p = _dot(h, w_ref[...])

    ps = p[:, :N_SHIFT]
    row = lax.broadcasted_iota(jnp.int32, (tm, 1), 0)
    prev = jnp.where(row == 0, carry_p[...], pltpu.roll(ps, 1, axis=0))
    carry_p[...] = ps[tm - 1:tm, :]
    sh = ps + (prev - ps) * mu_ref[...]

    e = e_ref[...]
    r = sh[:, 0:D_MIX]
    k = sh[:, D_MIX:2 * D_MIX]
    v = sh[:, 2 * D_MIX:3 * D_MIX]
    w_lo = sh[:, 3 * D_MIX:3 * D_MIX + LANE]
    a_lo = sh[:, 3 * D_MIX + LANE:3 * D_MIX + 2 * LANE]
    g_lo = sh[:, 3 * D_MIX + 2 * LANE:N_SHIFT]

    w_log = -_softplus(-(w0_ref[...] + _bdot(jnp.tanh(w_lo), wup_ref[...]))) - 0.5
    lw = -jnp.exp(w_log)
    a = _sigmoid(a0_ref[...] + _bdot(a_lo, aup_ref[...]))
    g = _bdot(_sigmoid(g_lo), gup_ref[...])
    kk = k * kk_ref[...]
    kk = kk * lax.rsqrt(jnp.maximum(_head_sum(kk * kk, e), 1e-24))
    k_mod = k * (1.0 + (a - 1.0) * ka_ref[...])
    bonus = _head_sum(r * k_mod * rk_ref[...], e) * v

    r_out[...] = r
    lw_out[...] = lw
    k_out[...] = k_mod
    v_out[...] = v.astype(BF16)
    kk_out[...] = kk
    b_out[...] = kk * a
    g_out[...] = g.astype(BF16)
    bonus_out[...] = bonus.astype(BF16)

    fq = p[:, N_SHIFT:N_SHIFT + D_MIX]
    fk = p[:, N_SHIFT + D_MIX:N_SHIFT + 2 * D_MIX]
    fv = p[:, N_SHIFT + 2 * D_MIX:N_SHIFT + 3 * D_MIX]
    f_lo = p[:, N_SHIFT + 3 * D_MIX:N_PROJ]
    inv_head = 1.0 / HEAD
    qn = fq * lax.rsqrt(_head_sum(fq * fq, e) * inv_head + RMS_EPS) * qn_ref[...] * (LOG2E * HEAD ** -0.5)
    kn = fk * lax.rsqrt(_head_sum(fk * fk, e) * inv_head + RMS_EPS) * kn_ref[...]
    z = f_lo + fb_ref[...]
    log_f = jnp.minimum(z, 0.0) - jnp.log(1.0 + jnp.exp(-jnp.abs(z)))
    tri = jnp.where(lax.broadcasted_iota(jnp.int32, (tm, tm), 1) <= lax.broadcasted_iota(jnp.int32, (tm, tm), 0),
                    1.0, 0.0).astype(BF16)
    c_loc = _cumsum_rows(tri, log_f)
    off_out[...] = carry_c[...] * LOG2E
    in_blk = jnp.where(i % tiles_per_kblock == 0, 0.0, carry_b[...])
    carry_c[...] = carry_c[...] + c_loc[tm - 1:tm, :]
    carry_b[...] = in_blk + c_loc[tm - 1:tm, :]
    c2 = (c_loc + in_blk) * LOG2E

    lane = lax.broadcasted_iota(jnp.int32, (tm, HEAD), 1)
    sub = lax.broadcasted_iota(jnp.int32, (HEAD, tm), 0)
    q_tail = jnp.where(sub < 3, -1.0, 0.0)
    v_tail = jnp.where(lax.broadcasted_iota(jnp.int32, (V_ROWS - HEAD, tm), 0) == 0, 1.0, 0.0)
    for pr in range(N_HEADS // 2):
        ps2 = slice(pr * 2 * HEAD, (pr + 1) * 2 * HEAD)
        q_t = qn[:, ps2].T
        v_t = fv[:, ps2].T
        for half in range(2):
            hh = 2 * pr + half
            rows = slice(half * HEAD, (half + 1) * HEAD)
            fq_out[hh] = jnp.concatenate([q_t[rows], q_tail], axis=0).astype(BF16)
            fv_out[hh] = jnp.concatenate([v_t[rows], v_tail], axis=0).astype(BF16)
            c_h = c2[:, hh:hh + 1]
            c_hi = c_h.astype(BF16).astype(F32)
            c_mid = (c_h - c_hi).astype(BF16).astype(F32)
            c_lo = c_h - c_hi - c_mid
            c_cols = jnp.where(lane == 0, c_hi, jnp.where(lane == 1, c_mid, jnp.where(lane == 2, c_lo, 0.0)))
            fk_out[hh] = jnp.concatenate([kn[:, hh * HEAD:(hh + 1) * HEAD], c_cols], axis=1).astype(BF16)


def _pad_cols(w, n):
    return jnp.pad(w, ((0, 0), (0, n - w.shape[1])))


def _pad_rows(w, n):
    return jnp.pad(w, ((0, n - w.shape[0]), (0, 0)))


def _in_proj(x, mix_norm, w_in, shift_mu, w0, w_up, a0, a_up, g_up, k_k, k_a, r_k, f_bias, q_norm, k_norm, *, tm=256, n_sub=2):
    bsz, s, d = x.shape
    lw_, la_, lg_ = w_up.shape[0], a_up.shape[0], g_up.shape[0]
    c0 = 3 * D_MIX
    c1 = c0 + lw_ + la_ + lg_
    c2 = c1 + 3 * D_MIX
    c3 = c2 + N_HEADS

    def seg(t, pad):
        return jnp.concatenate([
            t[..., :c0],
            pad(t[..., c0:c0 + lw_], LANE), pad(t[..., c0 + lw_:c0 + lw_ + la_], LANE), pad(t[..., c0 + lw_ + la_:c1], LANE),
        ], axis=-1)

    col_map = ((0, c0, 0), (c0, lw_, c0), (c0 + lw_, la_, c0 + LANE), (c0 + lw_ + la_, lg_, c0 + 2 * LANE),
               (c1, c2 - c1, N_SHIFT), (c2, c3 - c2, N_SHIFT + 3 * D_MIX))
    mu_cat = seg(shift_mu.reshape(1, -1), _pad_cols)
    row = lambda t: t.reshape(1, -1).astype(F32)
    e = _head_mask()
    tile = lambda t: jnp.tile(t.reshape(1, HEAD), (1, N_HEADS)).astype(F32)
    consts = [
        row(mix_norm), w_in, mu_cat, row(w0), _pad_rows(w_up, LANE).astype(BF16), row(a0),
        _pad_rows(a_up, LANE).astype(BF16), _pad_rows(g_up, LANE).astype(BF16), row(k_k), row(k_a), row(r_k),
        _pad_cols(row(f_bias), LANE), tile(q_norm), tile(k_norm), e,
    ]
    const_specs = [_resident(c.shape) for c in consts]
    const_specs[1] = pl.BlockSpec(memory_space=pl.ANY)
    n_cols = w_in.shape[1]
    assert d % WEIGHT_CHUNKS == 0 and n_cols - c3 == 2 * d
    tm, tile_rows = tm * n_sub, tm
    tok = lambda: pl.BlockSpec((1, tm, D_MIX), lambda b, i: (b, i, 0))
    f32_tok = jax.ShapeDtypeStruct((bsz, s, D_MIX), F32)
    bf_tok = jax.ShapeDtypeStruct((bsz, s, D_MIX), BF16)
    assert s % tm == 0 and FOX_KBLOCK % tile_rows == 0
    return pl.pallas_call(
        functools.partial(_in_proj_kernel, tm=tile_rows, n_sub=n_sub, tiles_per_kblock=FOX_KBLOCK // tile_rows,
                          col_map=col_map, gate_cols=(c3, n_cols)),
        out_shape=[f32_tok, f32_tok, f32_tok, bf_tok, f32_tok, f32_tok, bf_tok, bf_tok] + [
            jax.ShapeDtypeStruct((bsz, N_HEADS, 2 * HEAD, s), BF16),
            jax.ShapeDtypeStruct((bsz, N_HEADS, s, 2 * HEAD), BF16),
            jax.ShapeDtypeStruct((bsz, N_HEADS, V_ROWS, s), BF16),
            jax.ShapeDtypeStruct((bsz, s // tile_rows, 1, LANE), F32),
            jax.ShapeDtypeStruct((d, 2 * d), BF16),
        ],
        grid=(bsz, s // tm),
        in_specs=[pl.BlockSpec((1, tm, d), lambda b, i: (b, i, 0))] + const_specs,
        out_specs=[tok() for _ in range(8)] + [
            pl.BlockSpec((1, N_HEADS, 2 * HEAD, tm), lambda b, i: (b, 0, 0, i)),
            pl.BlockSpec((1, N_HEADS, tm, 2 * HEAD), lambda b, i: (b, 0, i, 0)),
            pl.BlockSpec((1, N_HEADS, V_ROWS, tm), lambda b, i: (b, 0, 0, i)),
            pl.BlockSpec((1, n_sub, 1, LANE), lambda b, i: (b, i, 0, 0)),
            pl.BlockSpec((d, 2 * d), lambda b, i: (0, 0)),
        ],
        scratch_shapes=[pltpu.VMEM((1, N_SHIFT), F32), pltpu.VMEM((1, LANE), F32), pltpu.VMEM((1, LANE), F32),
                        pltpu.VMEM((d, N_PROJ), BF16), pltpu.VMEM((2, d // WEIGHT_CHUNKS, n_cols), F32),
                        pltpu.SemaphoreType.DMA((2,))],
        compiler_params=pltpu.CompilerParams(
            dimension_semantics=("arbitrary", "arbitrary"), vmem_limit_bytes=VMEM_LIMIT),
        name="in_proj",
    )(x, *consts)


def _rwkv_kernel(r_ref, lw_ref, k_ref, v_ref, kk_ref, b_ref, g_ref, bonus_ref, lng_ref, lnb_ref, e_ref, y_ref, s_scr, *, tb):
    i = pl.program_id(1)

    @pl.when(i == 0)
    def _():
        s_scr[...] = jnp.zeros_like(s_scr)

    c = CHUNK
    pw = 2 * HEAD
    n_chunk = tb // c
    n_pair = N_HEADS // 2
    rr = lax.broadcasted_iota(jnp.int32, (c, pw), 0)
    ll = lax.broadcasted_iota(jnp.int32, (c, pw), 1)
    cc = ll % c
    lo = ll < c
    strict = cc < rr
    incl = cc <= rr
    eye = (cc == rr).astype(F32)
    r2 = lax.broadcasted_iota(jnp.int32, (2 * c, pw), 0)
    l2 = lax.broadcasted_iota(jnp.int32, (2 * c, pw), 1)
    lo2 = l2 < c
    diag2 = (r2 < c) == lo2
    zeros = jnp.zeros((c, pw), F32)

    def level_mask(n):
        return ((rr // (2 * n)) == (cc // (2 * n))) & (((rr // n) % 2) == 1) & (((cc // n) % 2) == 0)

    def bd(x):
        return jnp.where(diag2, jnp.concatenate([x, x], axis=0), 0.0).astype(BF16)

    def bd_sw(x):
        return jnp.where(diag2, 0.0, jnp.concatenate([x, x], axis=0)).astype(BF16)

    slab = min(tb, MXU_DIM)
    rb = lax.broadcasted_iota(jnp.int32, (slab, slab), 0)
    cb = lax.broadcasted_iota(jnp.int32, (slab, slab), 1)
    tri = jnp.where((cb <= rb) & ((rb // c) == (cb // c)), 1.0, 0.0).astype(BF16)
    lw = lw_ref[0]
    cs = jnp.concatenate([_cumsum_rows(tri, lw[j:j + slab]) for j in range(0, tb, slab)], axis=0)
    w_inv = jnp.exp(-cs)
    rt = r_ref[0] * jnp.exp(cs)
    at = -kk_ref[0] * jnp.exp(cs - lw)
    bi = b_ref[0] * w_inv
    ki = k_ref[0] * w_inv
    v = v_ref[0].astype(F32)

    chains = [(j, p) for j in range(n_chunk) for p in range(n_pair)]
    cut = lambda x, j, p: x[j * c:(j + 1) * c, p * pw:(p + 1) * pw]
    w_tot = [jnp.exp(cs[(j + 1) * c - 1:(j + 1) * c, :]) for j in range(n_chunk)]
    a_t = [cut(at, j, p) for j, p in chains]
    r_t = [cut(rt, j, p) for j, p in chains]
    b_i = [cut(bi, j, p) for j, p in chains]
    k_i = [cut(ki, j, p) for j, p in chains]
    v_p = [cut(v, j, p) for j, p in chains]
    n = len(chains)

    a_ab, a_ak, a_rb, a_rk = [], [], [], []
    for q in range(n):
        ar = jnp.concatenate([a_t[q], r_t[q]], axis=0)
        gx = _dot(jnp.where(lo2, ar, 0.0).astype(BF16), jnp.concatenate([b_i[q], k_i[q]], axis=0).astype(BF16), _NT)
        gy = _dot(jnp.where(lo2, 0.0, ar).astype(BF16), jnp.concatenate([k_i[q], b_i[q]], axis=0).astype(BF16), _NT)
        a_ab.append(jnp.where(strict, jnp.where(lo, gx[:c], gy[:c]), 0.0))
        a_ak.append(jnp.where(strict, jnp.where(lo, gy[:c], gx[:c]), 0.0))
        a_rb.append(jnp.where(incl, jnp.where(lo, gx[c:], gy[c:]), 0.0))
        a_rk.append(jnp.where(incl, jnp.where(lo, gy[c:], gx[c:]), 0.0))

    t = [eye + jnp.where(level_mask(1), a_ab[q], 0.0) for q in range(n)]
    for m in (2, 4, 8, 16, 32):
        x = [_dot(jnp.where(level_mask(m), a_ab[q], 0.0).astype(BF16), bd(t[q])) for q in range(n)]
        t = [t[q] + _dot(t[q].astype(BF16), bd(x[q])) for q in range(n)]

    v_sw = [bd_sw(v_p[q]) for q in range(n)]
    akv = [_dot(a_ak[q].astype(BF16), v_sw[q]) for q in range(n)]
    x = [_dot(t[q].astype(BF16), jnp.concatenate([bd(a_t[q]), bd(akv[q])], axis=1)) for q in range(n)]
    a_p = [x[q][:, :pw] for q in range(n)]
    u_0 = [x[q][:, pw:] for q in range(n)]
    zero_bd = jnp.zeros((2 * c, pw), BF16)
    d1 = [_dot(jnp.concatenate([a_rb[q], a_rk[q]], axis=1).astype(BF16),
               jnp.concatenate([jnp.concatenate([bd(a_p[q]), bd(u_0[q])], axis=1),
                                jnp.concatenate([zero_bd, v_sw[q]], axis=1)], axis=0)) for q in range(n)]
    d2 = []
    for q, (j, p) in enumerate(chains):
        wt = w_tot[j][:, p * pw:(p + 1) * pw]
        lhs = jnp.concatenate([jnp.concatenate([a_p[q], u_0[q]], axis=1),
                               jnp.concatenate([zeros, v_p[q]], axis=1)], axis=0).astype(BF16)
        rhs = jnp.concatenate([b_i[q] * wt, k_i[q] * wt], axis=0).astype(BF16)
        d2.append(_dot(lhs, rhs, _TN))

    s = [s_scr[p] for p in range(n_pair)]
    ys = [None] * n
    for j in range(n_chunk):
        for p in range(n_pair):
            q = j * n_pair + p
            mt = jnp.where(diag2, d2[q][:pw], 0.0).astype(BF16)
            nt = jnp.where(lo, d2[q][pw:pw + c], d2[q][pw + c:])
            ys[q] = d1[q][:, pw:] + _dot((r_t[q] + d1[q][:, :pw]).astype(BF16), bd(s[p]), _NT)
            s[p] = s[p] * w_tot[j][:, p * pw:(p + 1) * pw] + _dot(s[p].astype(BF16), mt) + nt
    for p in range(n_pair):
        s_scr[p] = s[p]

    y = jnp.concatenate([jnp.concatenate([ys[j * n_pair + p] for p in range(n_pair)], axis=1) for j in range(n_chunk)], axis=0)
    e = e_ref[...]
    inv_head = 1.0 / HEAD
    yc = y - _head_sum(y, e) * inv_head
    yn = yc * lax.rsqrt(_head_sum(yc * yc, e) * inv_head + LNX_EPS)
    y_ref[0] = ((yn * lng_ref[...] + lnb_ref[...] + bonus_ref[0]) * g_ref[0]).astype(BF16)


def _head_mask():
    head_id = jnp.arange(MXU_DIM) // HEAD
    return (head_id[:, None] == head_id[None, :]).astype(BF16)


def _rwkv(r, lw, k, v, kk, b, g, bonus, lnx_g, lnx_b, *, tb=512):
    bsz, s, _ = r.shape
    assert s % tb == 0 and tb % CHUNK == 0
    tok = pl.BlockSpec((1, tb, D_MIX), lambda bb, i: (bb, i, 0))
    vec = pl.BlockSpec((1, D_MIX), lambda bb, i: (0, 0))
    return pl.pallas_call(
        functools.partial(_rwkv_kernel, tb=tb),
        out_shape=jax.ShapeDtypeStruct((bsz, s, D_MIX), BF16),
        grid=(bsz, s // tb),
        in_specs=[tok] * 8 + [vec, vec, pl.BlockSpec((MXU_DIM, MXU_DIM), lambda bb, i: (0, 0))],
        out_specs=tok,
        scratch_shapes=[pltpu.VMEM((N_HEADS // 2, HEAD, 2 * HEAD), F32)],
        compiler_params=pltpu.CompilerParams(
            dimension_semantics=("parallel", "arbitrary"), vmem_limit_bytes=VMEM_LIMIT),
        name="rwkv",
    )(r, lw, k, v, kk, b, g, bonus, lnx_g.reshape(1, -1).astype(F32), lnx_b.reshape(1, -1).astype(F32), _head_mask())


def _fox_kernel(off_ref, thr_ref, q_ref, k_ref, v_ref, o_ref, *scratch, tq, tk, qw, n_kblk):
    b, hp, qb = pl.program_id(0), pl.program_id(1), pl.program_id(2)
    kb_last = (qb * tq) // tk
    delta = qb * tq - kb_last * tk
    chains = [(hh, g) for hh in range(2) for g in range(tq // qw)]
    n_c = len(chains)
    s_scr = (scratch[:n_c], scratch[n_c:2 * n_c])
    m_scr, acc_scr = scratch[2 * n_c:]
    base = [(b * N_HEADS + 2 * hp + hh) * n_kblk for hh in range(2)]
    off_d = [off_ref[base[hh] + kb_last] for hh in range(2)]

    thr = thr_ref[0]

    def needed(kb):
        nxt = jnp.minimum(kb + 1, kb_last)
        return jnp.minimum(off_ref[base[0] + nxt] - off_d[0], off_ref[base[1] + nxt] - off_d[1]) <= thr

    kb_first = lax.while_loop(lambda kb: jnp.logical_and(kb > 0, needed(kb - 1)), lambda kb: kb - 1, kb_last)
    n_open = kb_last - kb_first

    def qk(kb, slot):
        ksl = pl.ds(pl.multiple_of(kb * tk, tk), tk)
        for c, (hh, g) in enumerate(chains):
            s_scr[slot][c][...] = _dot(k_ref[0, hh, ksl, :], q_ref[0, hh, :, g * qw:(g + 1) * qw])

    def soft_pv(kb, slot, masked):
        ksl = pl.ds(pl.multiple_of(kb * tk, tk), tk)
        for c, (hh, g) in enumerate(chains):
            s = s_scr[slot][c][...]
            if masked:
                key = lax.broadcasted_iota(jnp.int32, (tk, qw), 0)
                qry = lax.broadcasted_iota(jnp.int32, (tk, qw), 1)
                s = jnp.where(key - qry <= delta + g * qw, s, NEG_BIG)
            off = off_ref[base[hh] + kb] - off_d[hh]
            m = m_scr[c]
            m_new = jnp.maximum(m, jnp.max(s, axis=0, keepdims=True) - off)
            p = jnp.exp2(s - (m_new + off)).astype(BF16)
            acc_scr[c] = jnp.exp2(m - m_new) * acc_scr[c] + _dot(v_ref[0, hh, :, ksl], p)
            m_scr[c] = m_new

    m_scr[...] = jnp.full(m_scr.shape, NEG_BIG, F32)
    acc_scr[...] = jnp.zeros(acc_scr.shape, F32)
    qk(kb_first, 0)

    def pair(i, _):
        kb = kb_first + 2 * i
        qk(kb + 1, 1)
        soft_pv(kb, 0, False)
        qk(kb + 2, 0)
        soft_pv(kb + 1, 1, False)
        return 0

    lax.fori_loop(0, n_open // 2, pair, 0)

    @pl.when(n_open % 2 == 1)
    def _():
        qk(kb_last, 1)
        soft_pv(kb_last - 1, 0, False)
        soft_pv(kb_last, 1, True)

    @pl.when(n_open % 2 == 0)
    def _():
        soft_pv(kb_last, 0, True)

    acc = [acc_scr[c] for c in range(n_c)]
    o_t = jnp.concatenate([jnp.concatenate([acc[c][:HEAD] / acc[c][HEAD:HEAD + 1] for c, (h2, _) in enumerate(chains)
                                            if h2 == hh], axis=1) for hh in range(2)], axis=0)
    o_ref[0] = o_t.T.astype(BF16)


def _fox(q_t, k_aug, v_t, off, q_norm, k_norm, *, tq=512, tk=FOX_KBLOCK, qw=MXU_DIM):
    bsz, nh, s, _ = k_aug.shape
    n_kblk = s // tk
    n_chain = 2 * (tq // qw)
    assert tq % qw == 0
    assert s % tq == 0 and s % tk == 0 and tk % tq == 0 and nh % 2 == 0 and off.shape == (bsz * nh * n_kblk,)
    qk_bound = 1.02 * LOG2E * HEAD ** 0.5 * jnp.max(jnp.abs(q_norm)) * jnp.max(jnp.abs(k_norm))
    thr = (2.0 * qk_bound + EXP2_UNDERFLOW).reshape(1).astype(F32)
    return pl.pallas_call(
        functools.partial(_fox_kernel, tq=tq, tk=tk, qw=qw, n_kblk=n_kblk),
        out_shape=jax.ShapeDtypeStruct((bsz, s, nh * HEAD), BF16),
        grid=(bsz, nh // 2, s // tq),
        in_specs=[
            pl.BlockSpec(memory_space=pltpu.SMEM),
            pl.BlockSpec(memory_space=pltpu.SMEM),
            pl.BlockSpec((1, 2, 2 * HEAD, tq), lambda b, hp, i: (b, hp, 0, i)),
            pl.BlockSpec((1, 2, s, 2 * HEAD), lambda b, hp, i: (b, hp, 0, 0)),
            pl.BlockSpec((1, 2, V_ROWS, s), lambda b, hp, i: (b, hp, 0, 0)),
        ],
        out_specs=pl.BlockSpec((1, tq, 2 * HEAD), lambda b, hp, i: (b, i, hp)),
        scratch_shapes=[pltpu.VMEM((tk, qw), F32)] * (2 * n_chain)
        + [pltpu.VMEM((n_chain, 1, qw), F32), pltpu.VMEM((n_chain, V_ROWS, qw), F32)],
        compiler_params=pltpu.CompilerParams(
            dimension_semantics=("parallel", "parallel", "arbitrary"), vmem_limit_bytes=VMEM_LIMIT),
        name="fox",
    )(off, thr, q_t, k_aug, v_t)


def _merge_kernel(x_ref, g_ref, wg_ref, ya_ref, yb_ref, pa_ref, pb_ref, wo_ref, o_ref):
    x = x_ref[...]
    d = x.shape[-1]
    h = _rms(x, g_ref[...]).astype(BF16)
    gates = _sigmoid(_dot(h, wg_ref[...]))
    ya = _dot(ya_ref[...], pa_ref[...])
    yb = _dot(yb_ref[...], pb_ref[...])
    mix = (gates[:, :d] * ya + gates[:, d:] * yb).astype(BF16)
    o_ref[...] = x + _dot(mix, wo_ref[...])


def _merge(x2d, mix_norm, w_gates, ya, yb, p_a, p_b, w_out, *, tm=512):
    m, d = x2d.shape
    assert m % tm == 0
    full = lambda t: pl.BlockSpec(t.shape, lambda i: (0, 0))
    tok = lambda n: pl.BlockSpec((tm, n), lambda i: (i, 0))
    consts = [mix_norm.reshape(1, d).astype(F32), w_gates.astype(BF16)]
    mats = [p_a.astype(BF16), p_b.astype(BF16), w_out.astype(BF16)]
    return pl.pallas_call(
        _merge_kernel,
        out_shape=jax.ShapeDtypeStruct((m, d), F32),
        grid=(m // tm,),
        in_specs=[tok(d), full(consts[0]), full(consts[1]), tok(D_MIX), tok(D_MIX)] + [full(t) for t in mats],
        out_specs=tok(d),
        compiler_params=pltpu.CompilerParams(dimension_semantics=("parallel",), vmem_limit_bytes=VMEM_LIMIT),
        name="merge",
    )(x2d, consts[0], consts[1], ya, yb, *mats)


def kernel(x, ffn1_norm, ffn1_w1, ffn1_w3, ffn1_w2, mix_norm, w_in, shift_mu, rwkv_w0, rwkv_w_up, rwkv_a0, rwkv_a_up, rwkv_g_up, rwkv_k_k, rwkv_k_a, rwkv_r_k, rwkv_lnx_g, rwkv_lnx_b, rwkv_proj, fox_f_bias, fox_q_norm, fox_k_norm, fox_proj, w_out, ffn2_norm, ffn2_w1, ffn2_w3, ffn2_w2, final_norm):
    bsz, s, d = x.shape
    depth = ffn1_norm.shape[0]
    x2d = x.reshape(bsz * s, d)
    for l in range(depth):
        x2d = _ffn(x2d, ffn1_norm[l], ffn1_w1[l], ffn1_w3[l], ffn1_w2[l], final_norm, final_norm=False)
        (r, lw, k, v, kk, b, g, bonus, fq_t, fk_aug, fv_t, off, w_gates) = _in_proj(
            x2d.reshape(bsz, s, d), mix_norm[l], w_in[l], shift_mu[l], rwkv_w0[l], rwkv_w_up[l], rwkv_a0[l],
            rwkv_a_up[l], rwkv_g_up[l], rwkv_k_k[l], rwkv_k_a[l], rwkv_r_k[l], fox_f_bias[l], fox_q_norm[l],
            fox_k_norm[l])
        ya = _rwkv(r, lw, k, v, kk, b, g, bonus, rwkv_lnx_g[l], rwkv_lnx_b[l])
        off = off[:, ::FOX_KBLOCK // (s // off.shape[1]), 0, :N_HEADS]
        off = jnp.transpose(off, (0, 2, 1)).reshape(-1)
        yb = _fox(fq_t, fk_aug, fv_t, off, fox_q_norm[l], fox_k_norm[l])
        x2d = _merge(x2d, mix_norm[l], w_gates, ya.reshape(bsz * s, D_MIX), yb.reshape(bsz * s, D_MIX),
                     rwkv_proj[l], fox_proj[l], w_out[l])
        x2d = _ffn(x2d, ffn2_norm[l], ffn2_w1[l], ffn2_w3[l], ffn2_w2[l], final_norm, final_norm=(l == depth - 1))
    return x2d.reshape(bsz, s, d)
```

```python
import functools

import jax
import jax.numpy as jnp
from jax import lax
from jax.experimental import pallas as pl
from jax.experimental.pallas import tpu as pltpu

F32 = jnp.float32
BF16 = jnp.bfloat16

HEAD = 64
N_HEADS = 8
D_MIX = HEAD * N_HEADS
LANE = 128
MXU_DIM = 256
CHUNK = 64
RMS_EPS = 1e-6
LNX_EPS = 64e-5
NEG_BIG = -1e30
EXP2_UNDERFLOW = 160.0
VMEM_LIMIT = 56 * 1024 * 1024

_NT = (((1,), (1,)), ((), ()))
_TN = (((0,), (0,)), ((), ()))


def _dot(a, b, dims=None):
    if dims is None:
        return jnp.dot(a, b, preferred_element_type=F32)
    return lax.dot_general(a, b, dims, preferred_element_type=F32)


def _bdot(a, b, dims=None):
    return _dot(a.astype(BF16), b.astype(BF16), dims)


def _rms(x, g):
    return x * lax.rsqrt(jnp.mean(x * x, axis=-1, keepdims=True) + RMS_EPS) * g


def _softplus(z):
    return jnp.maximum(z, 0.0) + jnp.log(1.0 + jnp.exp(-jnp.abs(z)))


def _sigmoid(z):
    return 1.0 / (1.0 + jnp.exp(-z))


def _head_sum(x, e):
    xb = x.astype(BF16)
    return jnp.concatenate([_dot(xb[:, j:j + MXU_DIM], e) for j in range(0, x.shape[1], MXU_DIM)], axis=1)


def _cumsum_rows(tri, x):
    hi = x.astype(BF16)
    r1 = x - hi.astype(F32)
    mid = r1.astype(BF16)
    lo = (r1 - mid.astype(F32)).astype(BF16)
    return _dot(tri, hi) + _dot(tri, mid) + _dot(tri, lo)


WEIGHT_CHUNKS = 8
STAGE_ROWS = 512


def _load_cast(w_hbm, w_ref, stage, sem):
    rows = w_hbm.shape[0] // WEIGHT_CHUNKS

    def copy(c):
        return pltpu.make_async_copy(w_hbm.at[pl.ds(c * rows, rows)], stage.at[c % 2], sem.at[c % 2])

    copy(0).start()
    for c in range(WEIGHT_CHUNKS):
        if c + 1 < WEIGHT_CHUNKS:
            copy(c + 1).start()
        copy(c).wait()
        w_ref[pl.ds(c * rows, rows), :] = stage[c % 2].astype(BF16)


def _ffn_kernel(x_ref, g_ref, w1_hbm, w3_hbm, w2_hbm, gf_ref, o_ref, w1_ref, w3_ref, w2_ref, stage_up, stage_down, sem,
                *, final_norm):
    @pl.when(pl.program_id(0) == 0)
    def _():
        _load_cast(w1_hbm, w1_ref, stage_up, sem)
        _load_cast(w3_hbm, w3_ref, stage_up, sem)
        _load_cast(w2_hbm, w2_ref, stage_down, sem)

    x = x_ref[...]
    h = _rms(x, g_ref[...]).astype(BF16)
    a = _dot(h, w1_ref[...])
    b = _dot(h, w3_ref[...])
    u = (a * _sigmoid(a) * b).astype(BF16)
    y = x + 0.5 * _dot(u, w2_ref[...])
    if final_norm:
        y = _rms(y, gf_ref[...])
    o_ref[...] = y


def _resident(shape):
    return pl.BlockSpec(shape, lambda *_: (0,) * len(shape), pipeline_mode=pl.Buffered(1))


def _ffn(x2d, g, w1, w3, w2, gf, *, final_norm, tm=512):
    m, d = x2d.shape
    f_dim = w1.shape[1]
    assert m % tm == 0 and d % WEIGHT_CHUNKS == 0 and f_dim % WEIGHT_CHUNKS == 0
    in_hbm = pl.BlockSpec(memory_space=pl.ANY)
    return pl.pallas_call(
        functools.partial(_ffn_kernel, final_norm=final_norm),
        out_shape=jax.ShapeDtypeStruct((m, d), F32),
        grid=(m // tm,),
        in_specs=[pl.BlockSpec((tm, d), lambda i: (i, 0)), _resident((1, d)), in_hbm, in_hbm, in_hbm, _resident((1, d))],
        out_specs=pl.BlockSpec((tm, d), lambda i: (i, 0)),
        scratch_shapes=[
            pltpu.VMEM((d, f_dim), BF16), pltpu.VMEM((d, f_dim), BF16), pltpu.VMEM((f_dim, d), BF16),
            pltpu.VMEM((2, d // WEIGHT_CHUNKS, f_dim), F32), pltpu.VMEM((2, f_dim // WEIGHT_CHUNKS, d), F32),
            pltpu.SemaphoreType.DMA((2,)),
        ],
        compiler_params=pltpu.CompilerParams(dimension_semantics=("arbitrary",), vmem_limit_bytes=VMEM_LIMIT),
        name="ffn",
    )(x2d, g.reshape(1, d), w1, w3, w2, gf.reshape(1, d))


N_SHIFT = 3 * D_MIX + 3 * LANE
N_PROJ = N_SHIFT + 3 * D_MIX + LANE
LOG2E = 1.4426950408889634
V_ROWS = HEAD + 16
FOX_KBLOCK = 512


def _in_proj_kernel(x_ref, g_ref, w_hbm, mu_ref, w0_ref, wup_ref, a0_ref, aup_ref, gup_ref, kk_ref, ka_ref,
                    rk_ref, fb_ref, qn_ref, kn_ref, e_ref,
                    r_out, lw_out, k_out, v_out, kk_out, b_out, g_out, bonus_out, fq_out, fk_out, fv_out, off_out,
                    wg_out, carry_p, carry_c, carry_b, w_ref, stage, sem, *, tm, n_sub, tiles_per_kblock, pieces):
    step = pl.program_id(1)

    @pl.when(jnp.logical_and(pl.program_id(0) == 0, step == 0))
    def _():
        w_ref[...] = jnp.zeros_like(w_ref)

        def copy(i):
            src, n = pieces[i][:2]
            return pltpu.make_async_copy(w_hbm.at[pl.ds(src, n)], stage.at[i % 2, pl.ds(0, n)], sem.at[i % 2])

        copy(0).start()
        for i, (_, n, to_gates, dst) in enumerate(pieces):
            if i + 1 < len(pieces):
                copy(i + 1).start()
            copy(i).wait()
            (wg_out if to_gates else w_ref)[pl.ds(dst, n), :] = stage[i % 2, 0:n].astype(BF16)

    @pl.when(step == 0)
    def _():
        carry_p[...] = jnp.zeros_like(carry_p)
        carry_c[...] = jnp.zeros_like(carry_c)
        carry_b[...] = jnp.zeros_like(carry_b)

    outs = (r_out, lw_out, k_out, v_out, kk_out, b_out, g_out, bonus_out)
    for t in range(n_sub):
        rows = pl.ds(t * tm, tm)
        _in_proj_tile(step * n_sub + t, x_ref.at[0, rows], g_ref, w_ref, mu_ref, w0_ref, wup_ref, a0_ref, aup_ref, gup_ref,
                      kk_ref, ka_ref, rk_ref, fb_ref, qn_ref, kn_ref, e_ref, *[o.at[0, rows] for o in outs],
                      fq_out.at[0, :, :, rows], fk_out.at[0, :, rows], fv_out.at[0, :, :, rows], off_out.at[0, t],
                      carry_p, carry_c, carry_b, tm=tm, tiles_per_kblock=tiles_per_kblock)


def _in_proj_tile(i, x_ref, g_ref, w_ref, mu_ref, w0_ref, wup_ref, a0_ref, aup_ref, gup_ref, kk_ref, ka_ref,
                  rk_ref, fb_ref, qn_ref, kn_ref, e_ref,
                  r_out, lw_out, k_out, v_out, kk_out, b_out, g_out, bonus_out, fq_out, fk_out, fv_out, off_out,
                  carry_p, carry_c, carry_b, *, tm, tiles_per_kblock):
    h = _rms(x_ref[...], g_ref[...]).astype(BF16)
    p = _dot(h, w_ref[...], _NT)

    ps = p[:, :N_SHIFT]
    row = lax.broadcasted_iota(jnp.int32, (tm, 1), 0)
    prev = jnp.where(row == 0, carry_p[...], pltpu.roll(ps, 1, axis=0))
    carry_p[...] = ps[tm - 1:tm, :]
    sh = ps + (prev - ps) * mu_ref[...]

    e = e_ref[...]
    r = sh[:, 0:D_MIX]
    k = sh[:, D_MIX:2 * D_MIX]
    v = sh[:, 2 * D_MIX:3 * D_MIX]
    w_lo = sh[:, 3 * D_MIX:3 * D_MIX + LANE]
    a_lo = sh[:, 3 * D_MIX + LANE:3 * D_MIX + 2 * LANE]
    g_lo = sh[:, 3 * D_MIX + 2 * LANE:N_SHIFT]

    w_log = -_softplus(-(w0_ref[...] + _bdot(jnp.tanh(w_lo), wup_ref[...]))) - 0.5
    lw = -jnp.exp(w_log)
    a = _sigmoid(a0_ref[...] + _bdot(a_lo, aup_ref[...]))
    g = _bdot(_sigmoid(g_lo), gup_ref[...])
    kk = k * kk_ref[...]
    kk = kk * lax.rsqrt(jnp.maximum(_head_sum(kk * kk, e), 1e-24))
    k_mod = k * (1.0 + (a - 1.0) * ka_ref[...])
    bonus = _head_sum(r * k_mod * rk_ref[...], e) * v

    r_out[...] = r
    lw_out[...] = lw
    k_out[...] = k_mod
    v_out[...] = v.astype(BF16)
    kk_out[...] = kk
    b_out[...] = kk * a
    g_out[...] = g.astype(BF16)
    bonus_out[...] = bonus.astype(BF16)

    fq = p[:, N_SHIFT:N_SHIFT + D_MIX]
    fk = p[:, N_SHIFT + D_MIX:N_SHIFT + 2 * D_MIX]
    fv = p[:, N_SHIFT + 2 * D_MIX:N_SHIFT + 3 * D_MIX]
    f_lo = p[:, N_SHIFT + 3 * D_MIX:N_PROJ]
    inv_head = 1.0 / HEAD
    qn = fq * lax.rsqrt(_head_sum(fq * fq, e) * inv_head + RMS_EPS) * qn_ref[...] * (LOG2E * HEAD ** -0.5)
    kn = fk * lax.rsqrt(_head_sum(fk * fk, e) * inv_head + RMS_EPS) * kn_ref[...]
    z = f_lo + fb_ref[...]
    log_f = jnp.minimum(z, 0.0) - jnp.log(1.0 + jnp.exp(-jnp.abs(z)))
    tri = jnp.where(lax.broadcasted_iota(jnp.int32, (tm, tm), 1) <= lax.broadcasted_iota(jnp.int32, (tm, tm), 0),
                    1.0, 0.0).astype(BF16)
    c_loc = _cumsum_rows(tri, log_f)
    off_out[...] = carry_c[...] * LOG2E
    in_blk = jnp.where(i % tiles_per_kblock == 0, 0.0, carry_b[...])
    carry_c[...] = carry_c[...] + c_loc[tm - 1:tm, :]
    carry_b[...] = in_blk + c_loc[tm - 1:tm, :]
    c2 = (c_loc + in_blk) * LOG2E

    lane = lax.broadcasted_iota(jnp.int32, (tm, HEAD), 1)
    sub = lax.broadcasted_iota(jnp.int32, (HEAD, tm), 0)
    q_tail = jnp.where(sub < 3, -1.0, 0.0)
    v_tail = jnp.where(lax.broadcasted_iota(jnp.int32, (V_ROWS - HEAD, tm), 0) == 0, 1.0, 0.0)
    for pr in range(N_HEADS // 2):
        ps2 = slice(pr * 2 * HEAD, (pr + 1) * 2 * HEAD)
        q_t = qn[:, ps2].T
        v_t = fv[:, ps2].T
        for half in range(2):
            hh = 2 * pr + half
            rows = slice(half * HEAD, (half + 1) * HEAD)
            fq_out[hh] = jnp.concatenate([q_t[rows], q_tail], axis=0).astype(BF16)
            fv_out[hh] = jnp.concatenate([v_t[rows], v_tail], axis=0).astype(BF16)
            c_h = c2[:, hh:hh + 1]
            c_hi = c_h.astype(BF16).astype(F32)
            c_mid = (c_h - c_hi).astype(BF16).astype(F32)
            c_lo = c_h - c_hi - c_mid
            c_cols = jnp.where(lane == 0, c_hi, jnp.where(lane == 1, c_mid, jnp.where(lane == 2, c_lo, 0.0)))
            fk_out[hh] = jnp.concatenate([kn[:, hh * HEAD:(hh + 1) * HEAD], c_cols], axis=1).astype(BF16)


def _pad_cols(w, n):
    return jnp.pad(w, ((0, 0), (0, n - w.shape[1])))


def _pad_rows(w, n):
    return jnp.pad(w, ((0, n - w.shape[0]), (0, 0)))


def _in_proj(x, mix_norm, w_in, shift_mu, w0, w_up, a0, a_up, g_up, k_k, k_a, r_k, f_bias, q_norm, k_norm, *, tm=256, n_sub=2):
    bsz, s, d = x.shape
    lw_, la_, lg_ = w_up.shape[0], a_up.shape[0], g_up.shape[0]
    c0 = 3 * D_MIX
    c1 = c0 + lw_ + la_ + lg_
    c2 = c1 + 3 * D_MIX
    c3 = c2 + N_HEADS

    def seg(t, pad):
        return jnp.concatenate([
            t[..., :c0],
            pad(t[..., c0:c0 + lw_], LANE), pad(t[..., c0 + lw_:c0 + lw_ + la_], LANE), pad(t[..., c0 + lw_ + la_:c1], LANE),
        ], axis=-1)

    n_cols = w_in.shape[1]
    groups = ((0, c0, False, 0), (c0, lw_, False, c0), (c0 + lw_, la_, False, c0 + LANE),
              (c0 + lw_ + la_, lg_, False, c0 + 2 * LANE), (c1, c2 - c1, False, N_SHIFT),
              (c2, c3 - c2, False, N_SHIFT + 3 * D_MIX), (c3, n_cols - c3, True, 0))
    pieces = tuple((src + o, min(STAGE_ROWS, width - o), to_gates, dst + o)
                   for src, width, to_gates, dst in groups for o in range(0, width, STAGE_ROWS))
    mu_cat = seg(shift_mu.reshape(1, -1), _pad_cols)
    row = lambda t: t.reshape(1, -1).astype(F32)
    e = _head_mask()
    tile = lambda t: jnp.tile(t.reshape(1, HEAD), (1, N_HEADS)).astype(F32)
    consts = [
        row(mix_norm), jnp.swapaxes(w_in, 0, 1), mu_cat, row(w0), _pad_rows(w_up, LANE).astype(BF16), row(a0),
        _pad_rows(a_up, LANE).astype(BF16), _pad_rows(g_up, LANE).astype(BF16), row(k_k), row(k_a), row(r_k),
        _pad_cols(row(f_bias), LANE), tile(q_norm), tile(k_norm), e,
    ]
    const_specs = [_resident(c.shape) for c in consts]
    const_specs[1] = pl.BlockSpec(memory_space=pl.ANY)
    assert n_cols - c3 == 2 * d
    tm, tile_rows = tm * n_sub, tm
    tok = lambda: pl.BlockSpec((1, tm, D_MIX), lambda b, i: (b, i, 0))
    f32_tok = jax.ShapeDtypeStruct((bsz, s, D_MIX), F32)
    bf_tok = jax.ShapeDtypeStruct((bsz, s, D_MIX), BF16)
    assert s % tm == 0 and FOX_KBLOCK % tile_rows == 0
    return pl.pallas_call(
        functools.partial(_in_proj_kernel, tm=tile_rows, n_sub=n_sub, tiles_per_kblock=FOX_KBLOCK // tile_rows,
                          pieces=pieces),
        out_shape=[f32_tok, f32_tok, f32_tok, bf_tok, f32_tok, f32_tok, bf_tok, bf_tok] + [
            jax.ShapeDtypeStruct((bsz, N_HEADS, 2 * HEAD, s), BF16),
            jax.ShapeDtypeStruct((bsz, N_HEADS, s, 2 * HEAD), BF16),
            jax.ShapeDtypeStruct((bsz, N_HEADS, V_ROWS, s), BF16),
            jax.ShapeDtypeStruct((bsz, s // tile_rows, 1, LANE), F32),
            jax.ShapeDtypeStruct((2 * d, d), BF16),
        ],
        grid=(bsz, s // tm),
        in_specs=[pl.BlockSpec((1, tm, d), lambda b, i: (b, i, 0))] + const_specs,
        out_specs=[tok() for _ in range(8)] + [
            pl.BlockSpec((1, N_HEADS, 2 * HEAD, tm), lambda b, i: (b, 0, 0, i)),
            pl.BlockSpec((1, N_HEADS, tm, 2 * HEAD), lambda b, i: (b, 0, i, 0)),
            pl.BlockSpec((1, N_HEADS, V_ROWS, tm), lambda b, i: (b, 0, 0, i)),
            pl.BlockSpec((1, n_sub, 1, LANE), lambda b, i: (b, i, 0, 0)),
            pl.BlockSpec((2 * d, d), lambda b, i: (0, 0)),
        ],
        scratch_shapes=[pltpu.VMEM((1, N_SHIFT), F32), pltpu.VMEM((1, LANE), F32), pltpu.VMEM((1, LANE), F32),
                        pltpu.VMEM((N_PROJ, d), BF16), pltpu.VMEM((2, STAGE_ROWS, d), F32),
                        pltpu.SemaphoreType.DMA((2,))],
        compiler_params=pltpu.CompilerParams(
            dimension_semantics=("arbitrary", "arbitrary"), vmem_limit_bytes=VMEM_LIMIT),
        name="in_proj",
    )(x, *consts)


def _rwkv_kernel(r_ref, lw_ref, k_ref, v_ref, kk_ref, b_ref, g_ref, bonus_ref, lng_ref, lnb_ref, e_ref, y_ref, s_scr, *, tb):
    i = pl.program_id(1)

    @pl.when(i == 0)
    def _():
        s_scr[...] = jnp.zeros_like(s_scr)

    c = CHUNK
    pw = 2 * HEAD
    n_chunk = tb // c
    n_pair = N_HEADS // 2
    rr = lax.broadcasted_iota(jnp.int32, (c, pw), 0)
    ll = lax.broadcasted_iota(jnp.int32, (c, pw), 1)
    cc = ll % c
    lo = ll < c
    strict = cc < rr
    incl = cc <= rr
    eye = (cc == rr).astype(F32)
    r2 = lax.broadcasted_iota(jnp.int32, (2 * c, pw), 0)
    l2 = lax.broadcasted_iota(jnp.int32, (2 * c, pw), 1)
    lo2 = l2 < c
    diag2 = (r2 < c) == lo2
    zeros = jnp.zeros((c, pw), F32)

    def level_mask(n):
        return ((rr // (2 * n)) == (cc // (2 * n))) & (((rr // n) % 2) == 1) & (((cc // n) % 2) == 0)

    def bd(x):
        return jnp.where(diag2, jnp.concatenate([x, x], axis=0), 0.0).astype(BF16)

    def bd_sw(x):
        return jnp.where(diag2, 0.0, jnp.concatenate([x, x], axis=0)).astype(BF16)

    slab = min(tb, MXU_DIM)
    rb = lax.broadcasted_iota(jnp.int32, (slab, slab), 0)
    cb = lax.broadcasted_iota(jnp.int32, (slab, slab), 1)
    tri = jnp.where((cb <= rb) & ((rb // c) == (cb // c)), 1.0, 0.0).astype(BF16)
    lw = lw_ref[0]
    cs = jnp.concatenate([_cumsum_rows(tri, lw[j:j + slab]) for j in range(0, tb, slab)], axis=0)
    w_inv = jnp.exp(-cs)
    rt = r_ref[0] * jnp.exp(cs)
    at = -kk_ref[0] * jnp.exp(cs - lw)
    bi = b_ref[0] * w_inv
    ki = k_ref[0] * w_inv
    v = v_ref[0].astype(F32)

    chains = [(j, p) for j in range(n_chunk) for p in range(n_pair)]
    cut = lambda x, j, p: x[j * c:(j + 1) * c, p * pw:(p + 1) * pw]
    w_tot = [jnp.exp(cs[(j + 1) * c - 1:(j + 1) * c, :]) for j in range(n_chunk)]
    a_t = [cut(at, j, p) for j, p in chains]
    r_t = [cut(rt, j, p) for j, p in chains]
    b_i = [cut(bi, j, p) for j, p in chains]
    k_i = [cut(ki, j, p) for j, p in chains]
    v_p = [cut(v, j, p) for j, p in chains]
    n = len(chains)

    a_ab, a_ak, a_rb, a_rk = [], [], [], []
    for q in range(n):
        ar = jnp.concatenate([a_t[q], r_t[q]], axis=0)
        gx = _dot(jnp.where(lo2, ar, 0.0).astype(BF16), jnp.concatenate([b_i[q], k_i[q]], axis=0).astype(BF16), _NT)
        gy = _dot(jnp.where(lo2, 0.0, ar).astype(BF16), jnp.concatenate([k_i[q], b_i[q]], axis=0).astype(BF16), _NT)
        a_ab.append(jnp.where(strict, jnp.where(lo, gx[:c], gy[:c]), 0.0))
        a_ak.append(jnp.where(strict, jnp.where(lo, gy[:c], gx[:c]), 0.0))
        a_rb.append(jnp.where(incl, jnp.where(lo, gx[c:], gy[c:]), 0.0))
        a_rk.append(jnp.where(incl, jnp.where(lo, gy[c:], gx[c:]), 0.0))

    t = [eye + jnp.where(level_mask(1), a_ab[q], 0.0) for q in range(n)]
    for m in (2, 4, 8, 16, 32):
        x = [_dot(jnp.where(level_mask(m), a_ab[q], 0.0).astype(BF16), bd(t[q])) for q in range(n)]
        t = [t[q] + _dot(t[q].astype(BF16), bd(x[q])) for q in range(n)]

    v_sw = [bd_sw(v_p[q]) for q in range(n)]
    akv = [_dot(a_ak[q].astype(BF16), v_sw[q]) for q in range(n)]
    x = [_dot(t[q].astype(BF16), jnp.concatenate([bd(a_t[q]), bd(akv[q])], axis=1)) for q in range(n)]
    a_p = [x[q][:, :pw] for q in range(n)]
    u_0 = [x[q][:, pw:] for q in range(n)]
    zero_bd = jnp.zeros((2 * c, pw), BF16)
    d1 = [_dot(jnp.concatenate([a_rb[q], a_rk[q]], axis=1).astype(BF16),
               jnp.concatenate([jnp.concatenate([bd(a_p[q]), bd(u_0[q])], axis=1),
                                jnp.concatenate([zero_bd, v_sw[q]], axis=1)], axis=0)) for q in range(n)]
    d2 = []
    for q, (j, p) in enumerate(chains):
        wt = w_tot[j][:, p * pw:(p + 1) * pw]
        lhs = jnp.concatenate([jnp.concatenate([a_p[q], u_0[q]], axis=1),
                               jnp.concatenate([zeros, v_p[q]], axis=1)], axis=0).astype(BF16)
        rhs = jnp.concatenate([b_i[q] * wt, k_i[q] * wt], axis=0).astype(BF16)
        d2.append(_dot(lhs, rhs, _TN))

    s = [s_scr[p] for p in range(n_pair)]
    ys = [None] * n
    for j in range(n_chunk):
        for p in range(n_pair):
            q = j * n_pair + p
            mt = jnp.where(diag2, d2[q][:pw], 0.0).astype(BF16)
            nt = jnp.where(lo, d2[q][pw:pw + c], d2[q][pw + c:])
            ys[q] = d1[q][:, pw:] + _dot((r_t[q] + d1[q][:, :pw]).astype(BF16), bd(s[p]), _NT)
            s[p] = s[p] * w_tot[j][:, p * pw:(p + 1) * pw] + _dot(s[p].astype(BF16), mt) + nt
    for p in range(n_pair):
        s_scr[p] = s[p]

    y = jnp.concatenate([jnp.concatenate([ys[j * n_pair + p] for p in range(n_pair)], axis=1) for j in range(n_chunk)], axis=0)
    e = e_ref[...]
    inv_head = 1.0 / HEAD
    yc = y - _head_sum(y, e) * inv_head
    yn = yc * lax.rsqrt(_head_sum(yc * yc, e) * inv_head + LNX_EPS)
    y_ref[0] = ((yn * lng_ref[...] + lnb_ref[...] + bonus_ref[0]) * g_ref[0]).astype(BF16)


def _head_mask():
    head_id = jnp.arange(MXU_DIM) // HEAD
    return (head_id[:, None] == head_id[None, :]).astype(BF16)


def _rwkv(r, lw, k, v, kk, b, g, bonus, lnx_g, lnx_b, *, tb=512):
    bsz, s, _ = r.shape
    assert s % tb == 0 and tb % CHUNK == 0
    tok = pl.BlockSpec((1, tb, D_MIX), lambda bb, i: (bb, i, 0))
    vec = pl.BlockSpec((1, D_MIX), lambda bb, i: (0, 0))
    return pl.pallas_call(
        functools.partial(_rwkv_kernel, tb=tb),
        out_shape=jax.ShapeDtypeStruct((bsz, s, D_MIX), BF16),
        grid=(bsz, s // tb),
        in_specs=[tok] * 8 + [vec, vec, pl.BlockSpec((MXU_DIM, MXU_DIM), lambda bb, i: (0, 0))],
        out_specs=tok,
        scratch_shapes=[pltpu.VMEM((N_HEADS // 2, HEAD, 2 * HEAD), F32)],
        compiler_params=pltpu.CompilerParams(
            dimension_semantics=("parallel", "arbitrary"), vmem_limit_bytes=VMEM_LIMIT),
        name="rwkv",
    )(r, lw, k, v, kk, b, g, bonus, lnx_g.reshape(1, -1).astype(F32), lnx_b.reshape(1, -1).astype(F32), _head_mask())


def _fox_kernel(off_ref, thr_ref, q_ref, k_ref, v_ref, o_ref, *scratch, tq, tk, qw, n_kblk):
    b, hp, qb = pl.program_id(0), pl.program_id(1), pl.program_id(2)
    kb_last = (qb * tq) // tk
    delta = qb * tq - kb_last * tk
    chains = [(hh, g) for hh in range(2) for g in range(tq // qw)]
    n_c = len(chains)
    s_scr = (scratch[:n_c], scratch[n_c:2 * n_c])
    m_scr, acc_scr = scratch[2 * n_c:]
    base = [(b * N_HEADS + 2 * hp + hh) * n_kblk for hh in range(2)]
    off_d = [off_ref[base[hh] + kb_last] for hh in range(2)]

    thr = thr_ref[0]

    def needed(kb):
        nxt = jnp.minimum(kb + 1, kb_last)
        return jnp.minimum(off_ref[base[0] + nxt] - off_d[0], off_ref[base[1] + nxt] - off_d[1]) <= thr

    kb_first = lax.while_loop(lambda kb: jnp.logical_and(kb > 0, needed(kb - 1)), lambda kb: kb - 1, kb_last)
    n_open = kb_last - kb_first

    def qk(kb, slot):
        ksl = pl.ds(pl.multiple_of(kb * tk, tk), tk)
        for c, (hh, g) in enumerate(chains):
            s_scr[slot][c][...] = _dot(k_ref[0, hh, ksl, :], q_ref[0, hh, :, g * qw:(g + 1) * qw])

    def soft_pv(kb, slot, masked):
        ksl = pl.ds(pl.multiple_of(kb * tk, tk), tk)
        for c, (hh, g) in enumerate(chains):
            s = s_scr[slot][c][...]
            if masked:
                key = lax.broadcasted_iota(jnp.int32, (tk, qw), 0)
                qry = lax.broadcasted_iota(jnp.int32, (tk, qw), 1)
                s = jnp.where(key - qry <= delta + g * qw, s, NEG_BIG)
            off = off_ref[base[hh] + kb] - off_d[hh]
            m = m_scr[c]
            m_new = jnp.maximum(m, jnp.max(s, axis=0, keepdims=True) - off)
            p = jnp.exp2(s - (m_new + off)).astype(BF16)
            acc_scr[c] = jnp.exp2(m - m_new) * acc_scr[c] + _dot(v_ref[0, hh, :, ksl], p)
            m_scr[c] = m_new

    m_scr[...] = jnp.full(m_scr.shape, NEG_BIG, F32)
    acc_scr[...] = jnp.zeros(acc_scr.shape, F32)
    qk(kb_first, 0)

    def pair(i, _):
        kb = kb_first + 2 * i
        qk(kb + 1, 1)
        soft_pv(kb, 0, False)
        qk(kb + 2, 0)
        soft_pv(kb + 1, 1, False)
        return 0

    lax.fori_loop(0, n_open // 2, pair, 0)

    @pl.when(n_open % 2 == 1)
    def _():
        qk(kb_last, 1)
        soft_pv(kb_last - 1, 0, False)
        soft_pv(kb_last, 1, True)

    @pl.when(n_open % 2 == 0)
    def _():
        soft_pv(kb_last, 0, True)

    acc = [acc_scr[c] for c in range(n_c)]
    o_t = jnp.concatenate([jnp.concatenate([acc[c][:HEAD] / acc[c][HEAD:HEAD + 1] for c, (h2, _) in enumerate(chains)
                                            if h2 == hh], axis=1) for hh in range(2)], axis=0)
    o_ref[0] = o_t.T.astype(BF16)


def _fox(q_t, k_aug, v_t, off, q_norm, k_norm, *, tq=512, tk=FOX_KBLOCK, qw=MXU_DIM):
    bsz, nh, s, _ = k_aug.shape
    n_kblk = s // tk
    n_chain = 2 * (tq // qw)
    assert tq % qw == 0
    assert s % tq == 0 and s % tk == 0 and tk % tq == 0 and nh % 2 == 0 and off.shape == (bsz * nh * n_kblk,)
    qk_bound = 1.02 * LOG2E * HEAD ** 0.5 * jnp.max(jnp.abs(q_norm)) * jnp.max(jnp.abs(k_norm))
    thr = (2.0 * qk_bound + EXP2_UNDERFLOW).reshape(1).astype(F32)
    return pl.pallas_call(
        functools.partial(_fox_kernel, tq=tq, tk=tk, qw=qw, n_kblk=n_kblk),
        out_shape=jax.ShapeDtypeStruct((bsz, s, nh * HEAD), BF16),
        grid=(bsz, nh // 2, s // tq),
        in_specs=[
            pl.BlockSpec(memory_space=pltpu.SMEM),
            pl.BlockSpec(memory_space=pltpu.SMEM),
            pl.BlockSpec((1, 2, 2 * HEAD, tq), lambda b, hp, i: (b, hp, 0, i)),
            pl.BlockSpec((1, 2, s, 2 * HEAD), lambda b, hp, i: (b, hp, 0, 0)),
            pl.BlockSpec((1, 2, V_ROWS, s), lambda b, hp, i: (b, hp, 0, 0)),
        ],
        out_specs=pl.BlockSpec((1, tq, 2 * HEAD), lambda b, hp, i: (b, i, hp)),
        scratch_shapes=[pltpu.VMEM((tk, qw), F32)] * (2 * n_chain)
        + [pltpu.VMEM((n_chain, 1, qw), F32), pltpu.VMEM((n_chain, V_ROWS, qw), F32)],
        compiler_params=pltpu.CompilerParams(
            dimension_semantics=("parallel", "parallel", "arbitrary"), vmem_limit_bytes=VMEM_LIMIT),
        name="fox",
    )(off, thr, q_t, k_aug, v_t)


def _merge_kernel(x_ref, g_ref, wg_ref, ya_ref, yb_ref, pa_ref, pb_ref, wo_ref, o_ref):
    x = x_ref[...]
    d = x.shape[-1]
    h = _rms(x, g_ref[...]).astype(BF16)
    gates = _sigmoid(_dot(h, wg_ref[...], _NT))
    ya = _dot(ya_ref[...], pa_ref[...])
    yb = _dot(yb_ref[...], pb_ref[...])
    mix = (gates[:, :d] * ya + gates[:, d:] * yb).astype(BF16)
    o_ref[...] = x + _dot(mix, wo_ref[...])


def _merge(x2d, mix_norm, w_gates, ya, yb, p_a, p_b, w_out, *, tm=512):
    m, d = x2d.shape
    assert m % tm == 0
    full = lambda t: pl.BlockSpec(t.shape, lambda i: (0, 0))
    tok = lambda n: pl.BlockSpec((tm, n), lambda i: (i, 0))
    consts = [mix_norm.reshape(1, d).astype(F32), w_gates.astype(BF16)]
    mats = [p_a.astype(BF16), p_b.astype(BF16), w_out.astype(BF16)]
    return pl.pallas_call(
        _merge_kernel,
        out_shape=jax.ShapeDtypeStruct((m, d), F32),
        grid=(m // tm,),
        in_specs=[tok(d), full(consts[0]), full(consts[1]), tok(D_MIX), tok(D_MIX)] + [full(t) for t in mats],
        out_specs=tok(d),
        compiler_params=pltpu.CompilerParams(dimension_semantics=("parallel",), vmem_limit_bytes=VMEM_LIMIT),
        name="merge",
    )(x2d, consts[0], consts[1], ya, yb, *mats)


def kernel(x, ffn1_norm, ffn1_w1, ffn1_w3, ffn1_w2, mix_norm, w_in, shift_mu, rwkv_w0, rwkv_w_up, rwkv_a0, rwkv_a_up, rwkv_g_up, rwkv_k_k, rwkv_k_a, rwkv_r_k, rwkv_lnx_g, rwkv_lnx_b, rwkv_proj, fox_f_bias, fox_q_norm, fox_k_norm, fox_proj, w_out, ffn2_norm, ffn2_w1, ffn2_w3, ffn2_w2, final_norm):
    bsz, s, d = x.shape
    depth = ffn1_norm.shape[0]
    x2d = x.reshape(bsz * s, d)
    for l in range(depth):
        x2d = _ffn(x2d, ffn1_norm[l], ffn1_w1[l], ffn1_w3[l], ffn1_w2[l], final_norm, final_norm=False)
        (r, lw, k, v, kk, b, g, bonus, fq_t, fk_aug, fv_t, off, w_gates) = _in_proj(
            x2d.reshape(bsz, s, d), mix_norm[l], w_in[l], shift_mu[l], rwkv_w0[l], rwkv_w_up[l], rwkv_a0[l],
            rwkv_a_up[l], rwkv_g_up[l], rwkv_k_k[l], rwkv_k_a[l], rwkv_r_k[l], fox_f_bias[l], fox_q_norm[l],
            fox_k_norm[l])
        ya = _rwkv(r, lw, k, v, kk, b, g, bonus, rwkv_lnx_g[l], rwkv_lnx_b[l])
        off = off[:, ::FOX_KBLOCK // (s // off.shape[1]), 0, :N_HEADS]
        off = jnp.transpose(off, (0, 2, 1)).reshape(-1)
        yb = _fox(fq_t, fk_aug, fv_t, off, fox_q_norm[l], fox_k_norm[l])
        x2d = _merge(x2d, mix_norm[l], w_gates, ya.reshape(bsz * s, D_MIX), yb.reshape(bsz * s, D_MIX),
                     rwkv_proj[l], fox_proj[l], w_out[l])
        x2d = _ffn(x2d, ffn2_norm[l], ffn2_w1[l], ffn2_w3[l], ffn2_w2[l], final_norm, final_norm=(l == depth - 1))
    return x2d.reshape(bsz, s, d)
```

```python
import functools

import jax
import jax.numpy as jnp
from jax import lax
from jax.experimental import pallas as pl
from jax.experimental.pallas import tpu as pltpu

F32 = jnp.float32
BF16 = jnp.bfloat16

HEAD = 64
N_HEADS = 8
D_MIX = HEAD * N_HEADS
LANE = 128
MXU_DIM = 256
CHUNK = 64
RMS_EPS = 1e-6
LNX_EPS = 64e-5
NEG_BIG = -1e30
EXP2_UNDERFLOW = 160.0
VMEM_LIMIT = 56 * 1024 * 1024

_NT = (((1,), (1,)), ((), ()))
_TN = (((0,), (0,)), ((), ()))


def _dot(a, b, dims=None):
    if dims is None:
        return jnp.dot(a, b, preferred_element_type=F32)
    return lax.dot_general(a, b, dims, preferred_element_type=F32)


def _bdot(a, b, dims=None):
    return _dot(a.astype(BF16), b.astype(BF16), dims)


def _rms(x, g):
    return x * lax.rsqrt(jnp.mean(x * x, axis=-1, keepdims=True) + RMS_EPS) * g


def _softplus(z):
    return jnp.maximum(z, 0.0) + jnp.log(1.0 + jnp.exp(-jnp.abs(z)))


def _sigmoid(z):
    return 1.0 / (1.0 + jnp.exp(-z))


def _head_sum(x, e):
    xb = x.astype(BF16)
    return jnp.concatenate([_dot(xb[:, j:j + MXU_DIM], e) for j in range(0, x.shape[1], MXU_DIM)], axis=1)


def _cumsum_rows(tri, x):
    hi = x.astype(BF16)
    r1 = x - hi.astype(F32)
    mid = r1.astype(BF16)
    lo = (r1 - mid.astype(F32)).astype(BF16)
    return _dot(tri, hi) + _dot(tri, mid) + _dot(tri, lo)


WEIGHT_CHUNKS = 8
PIECE_ROWS = 512
STAGE_ROWS = 640


def _load_cast(w_hbm, w_ref, stage, sem):
    rows = w_hbm.shape[0] // WEIGHT_CHUNKS

    def copy(c):
        return pltpu.make_async_copy(w_hbm.at[pl.ds(c * rows, rows)], stage.at[c % 2], sem.at[c % 2])

    copy(0).start()
    for c in range(WEIGHT_CHUNKS):
        if c + 1 < WEIGHT_CHUNKS:
            copy(c + 1).start()
        copy(c).wait()
        w_ref[pl.ds(c * rows, rows), :] = stage[c % 2].astype(BF16)


def _ffn_kernel(x_ref, g_ref, w1_hbm, w3_hbm, w2_hbm, gf_ref, o_ref, w1_ref, w3_ref, w2_ref, stage_up, stage_down, sem,
                *, final_norm):
    @pl.when(pl.program_id(0) == 0)
    def _():
        _load_cast(w1_hbm, w1_ref, stage_up, sem)
        _load_cast(w3_hbm, w3_ref, stage_up, sem)
        _load_cast(w2_hbm, w2_ref, stage_down, sem)

    x = x_ref[...]
    h = _rms(x, g_ref[...]).astype(BF16)
    a = _dot(h, w1_ref[...])
    b = _dot(h, w3_ref[...])
    u = (a * _sigmoid(a) * b).astype(BF16)
    y = x + 0.5 * _dot(u, w2_ref[...])
    if final_norm:
        y = _rms(y, gf_ref[...])
    o_ref[...] = y


def _resident(shape):
    return pl.BlockSpec(shape, lambda *_: (0,) * len(shape), pipeline_mode=pl.Buffered(1))


def _ffn(x2d, g, w1, w3, w2, gf, *, final_norm, tm=512):
    m, d = x2d.shape
    f_dim = w1.shape[1]
    assert m % tm == 0 and d % WEIGHT_CHUNKS == 0 and f_dim % WEIGHT_CHUNKS == 0
    in_hbm = pl.BlockSpec(memory_space=pl.ANY)
    return pl.pallas_call(
        functools.partial(_ffn_kernel, final_norm=final_norm),
        out_shape=jax.ShapeDtypeStruct((m, d), F32),
        grid=(m // tm,),
        in_specs=[pl.BlockSpec((tm, d), lambda i: (i, 0)), _resident((1, d)), in_hbm, in_hbm, in_hbm, _resident((1, d))],
        out_specs=pl.BlockSpec((tm, d), lambda i: (i, 0)),
        scratch_shapes=[
            pltpu.VMEM((d, f_dim), BF16), pltpu.VMEM((d, f_dim), BF16), pltpu.VMEM((f_dim, d), BF16),
            pltpu.VMEM((2, d // WEIGHT_CHUNKS, f_dim), F32), pltpu.VMEM((2, f_dim // WEIGHT_CHUNKS, d), F32),
            pltpu.SemaphoreType.DMA((2,)),
        ],
        compiler_params=pltpu.CompilerParams(dimension_semantics=("arbitrary",), vmem_limit_bytes=VMEM_LIMIT),
        name="ffn",
    )(x2d, g.reshape(1, d), w1, w3, w2, gf.reshape(1, d))


N_SHIFT = 3 * D_MIX + 3 * LANE
N_PROJ = N_SHIFT + 3 * D_MIX + LANE
LOG2E = 1.4426950408889634
V_ROWS = HEAD + 16
FOX_KBLOCK = 512


def _in_proj_kernel(x_ref, g_ref, w_hbm, mu_ref, w0_ref, wup_ref, a0_ref, aup_ref, gup_ref, kk_ref, ka_ref,
                    rk_ref, fb_ref, qn_ref, kn_ref, e_ref,
                    r_out, lw_out, k_out, v_out, kk_out, b_out, g_out, bonus_out, fq_out, fk_out, fv_out, off_out,
                    wg_out, carry_p, carry_c, carry_b, w_ref, stage, sem, *, tm, n_sub, tiles_per_kblock, chunks):
    step = pl.program_id(1)

    @pl.when(jnp.logical_and(pl.program_id(0) == 0, step == 0))
    def _():
        w_ref[...] = jnp.zeros_like(w_ref)

        def copy(i):
            src, n = chunks[i][:2]
            return pltpu.make_async_copy(w_hbm.at[pl.ds(src, n)], stage.at[i % 2, pl.ds(0, n)], sem.at[i % 2])

        copy(0).start()
        for i, (_, _, stores) in enumerate(chunks):
            if i + 1 < len(chunks):
                copy(i + 1).start()
            copy(i).wait()
            for at, n, to_gates, dst in stores:
                (wg_out if to_gates else w_ref)[pl.ds(dst, n), :] = stage[i % 2, at:at + n].astype(BF16)

    @pl.when(step == 0)
    def _():
        carry_p[...] = jnp.zeros_like(carry_p)
        carry_c[...] = jnp.zeros_like(carry_c)
        carry_b[...] = jnp.zeros_like(carry_b)

    outs = (r_out, lw_out, k_out, v_out, kk_out, b_out, g_out, bonus_out)
    for t in range(n_sub):
        rows = pl.ds(t * tm, tm)
        _in_proj_tile(step * n_sub + t, x_ref.at[0, rows], g_ref, w_ref, mu_ref, w0_ref, wup_ref, a0_ref, aup_ref, gup_ref,
                      kk_ref, ka_ref, rk_ref, fb_ref, qn_ref, kn_ref, e_ref, *[o.at[0, rows] for o in outs],
                      fq_out.at[0, :, :, rows], fk_out.at[0, :, rows], fv_out.at[0, :, :, rows], off_out.at[0, t],
                      carry_p, carry_c, carry_b, tm=tm, tiles_per_kblock=tiles_per_kblock)


def _in_proj_tile(i, x_ref, g_ref, w_ref, mu_ref, w0_ref, wup_ref, a0_ref, aup_ref, gup_ref, kk_ref, ka_ref,
                  rk_ref, fb_ref, qn_ref, kn_ref, e_ref,
                  r_out, lw_out, k_out, v_out, kk_out, b_out, g_out, bonus_out, fq_out, fk_out, fv_out, off_out,
                  carry_p, carry_c, carry_b, *, tm, tiles_per_kblock):
    h = _rms(x_ref[...], g_ref[...]).astype(BF16)
    p = _dot(h, w_ref[...], _NT)

    ps = p[:, :N_SHIFT]
    row = lax.broadcasted_iota(jnp.int32, (tm, 1), 0)
    prev = jnp.where(row == 0, carry_p[...], pltpu.roll(ps, 1, axis=0))
    carry_p[...] = ps[tm - 1:tm, :]
    sh = ps + (prev - ps) * mu_ref[...]

    e = e_ref[...]
    r = sh[:, 0:D_MIX]
    k = sh[:, D_MIX:2 * D_MIX]
    v = sh[:, 2 * D_MIX:3 * D_MIX]
    w_lo = sh[:, 3 * D_MIX:3 * D_MIX + LANE]
    a_lo = sh[:, 3 * D_MIX + LANE:3 * D_MIX + 2 * LANE]
    g_lo = sh[:, 3 * D_MIX + 2 * LANE:N_SHIFT]

    w_log = -_softplus(-(w0_ref[...] + _bdot(jnp.tanh(w_lo), wup_ref[...]))) - 0.5
    lw = -jnp.exp(w_log)
    a = _sigmoid(a0_ref[...] + _bdot(a_lo, aup_ref[...]))
    g = _bdot(_sigmoid(g_lo), gup_ref[...])
    kk = k * kk_ref[...]
    kk = kk * lax.rsqrt(jnp.maximum(_head_sum(kk * kk, e), 1e-24))
    k_mod = k * (1.0 + (a - 1.0) * ka_ref[...])
    bonus = _head_sum(r * k_mod * rk_ref[...], e) * v

    r_out[...] = r
    lw_out[...] = lw
    k_out[...] = k_mod
    v_out[...] = v.astype(BF16)
    kk_out[...] = kk
    b_out[...] = kk * a
    g_out[...] = g.astype(BF16)
    bonus_out[...] = bonus.astype(BF16)

    fq = p[:, N_SHIFT:N_SHIFT + D_MIX]
    fk = p[:, N_SHIFT + D_MIX:N_SHIFT + 2 * D_MIX]
    fv = p[:, N_SHIFT + 2 * D_MIX:N_SHIFT + 3 * D_MIX]
    f_lo = p[:, N_SHIFT + 3 * D_MIX:N_PROJ]
    inv_head = 1.0 / HEAD
    qn = fq * lax.rsqrt(_head_sum(fq * fq, e) * inv_head + RMS_EPS) * qn_ref[...] * (LOG2E * HEAD ** -0.5)
    kn = fk * lax.rsqrt(_head_sum(fk * fk, e) * inv_head + RMS_EPS) * kn_ref[...]
    z = f_lo + fb_ref[...]
    log_f = jnp.minimum(z, 0.0) - jnp.log(1.0 + jnp.exp(-jnp.abs(z)))
    tri = jnp.where(lax.broadcasted_iota(jnp.int32, (tm, tm), 1) <= lax.broadcasted_iota(jnp.int32, (tm, tm), 0),
                    1.0, 0.0).astype(BF16)
    c_loc = _cumsum_rows(tri, log_f)
    off_out[...] = carry_c[...] * LOG2E
    in_blk = jnp.where(i % tiles_per_kblock == 0, 0.0, carry_b[...])
    carry_c[...] = carry_c[...] + c_loc[tm - 1:tm, :]
    carry_b[...] = in_blk + c_loc[tm - 1:tm, :]
    c2 = (c_loc + in_blk) * LOG2E

    lane = lax.broadcasted_iota(jnp.int32, (tm, HEAD), 1)
    sub = lax.broadcasted_iota(jnp.int32, (HEAD, tm), 0)
    q_tail = jnp.where(sub < 3, -1.0, 0.0)
    v_tail = jnp.where(lax.broadcasted_iota(jnp.int32, (V_ROWS - HEAD, tm), 0) == 0, 1.0, 0.0)
    for pr in range(N_HEADS // 2):
        ps2 = slice(pr * 2 * HEAD, (pr + 1) * 2 * HEAD)
        q_t = qn[:, ps2].T
        v_t = fv[:, ps2].T
        for half in range(2):
            hh = 2 * pr + half
            rows = slice(half * HEAD, (half + 1) * HEAD)
            fq_out[hh] = jnp.concatenate([q_t[rows], q_tail], axis=0).astype(BF16)
            fv_out[hh] = jnp.concatenate([v_t[rows], v_tail], axis=0).astype(BF16)
            c_h = c2[:, hh:hh + 1]
            c_hi = c_h.astype(BF16).astype(F32)
            c_mid = (c_h - c_hi).astype(BF16).astype(F32)
            c_lo = c_h - c_hi - c_mid
            c_cols = jnp.where(lane == 0, c_hi, jnp.where(lane == 1, c_mid, jnp.where(lane == 2, c_lo, 0.0)))
            fk_out[hh] = jnp.concatenate([kn[:, hh * HEAD:(hh + 1) * HEAD], c_cols], axis=1).astype(BF16)


def _pad_cols(w, n):
    return jnp.pad(w, ((0, 0), (0, n - w.shape[1])))


def _pad_rows(w, n):
    return jnp.pad(w, ((0, n - w.shape[0]), (0, 0)))


def _in_proj(x, mix_norm, w_in, shift_mu, w0, w_up, a0, a_up, g_up, k_k, k_a, r_k, f_bias, q_norm, k_norm, *, tm=256, n_sub=2):
    bsz, s, d = x.shape
    lw_, la_, lg_ = w_up.shape[0], a_up.shape[0], g_up.shape[0]
    c0 = 3 * D_MIX
    c1 = c0 + lw_ + la_ + lg_
    c2 = c1 + 3 * D_MIX
    c3 = c2 + N_HEADS

    def seg(t, pad):
        return jnp.concatenate([
            t[..., :c0],
            pad(t[..., c0:c0 + lw_], LANE), pad(t[..., c0 + lw_:c0 + lw_ + la_], LANE), pad(t[..., c0 + lw_ + la_:c1], LANE),
        ], axis=-1)

    n_cols = w_in.shape[1]
    groups = ((0, c0, False, 0), (c0, lw_, False, c0), (c0 + lw_, la_, False, c0 + LANE),
              (c0 + lw_ + la_, lg_, False, c0 + 2 * LANE), (c1, c2 - c1, False, N_SHIFT),
              (c2, c3 - c2, False, N_SHIFT + 3 * D_MIX), (c3, n_cols - c3, True, 0))
    pieces = [(src + o, min(PIECE_ROWS, width - o), to_gates, dst + o)
              for src, width, to_gates, dst in groups for o in range(0, width, PIECE_ROWS)]
    chunks = []
    for src, n, to_gates, dst in pieces:
        if chunks and chunks[-1][0] + chunks[-1][1] == src and chunks[-1][1] + n <= STAGE_ROWS:
            c_src, c_n, stores = chunks[-1]
            chunks[-1] = (c_src, c_n + n, stores + ((c_n, n, to_gates, dst),))
        else:
            chunks.append((src, n, ((0, n, to_gates, dst),)))
    chunks = tuple(chunks)
    mu_cat = seg(shift_mu.reshape(1, -1), _pad_cols)
    row = lambda t: t.reshape(1, -1).astype(F32)
    e = _head_mask()
    tile = lambda t: jnp.tile(t.reshape(1, HEAD), (1, N_HEADS)).astype(F32)
    consts = [
        row(mix_norm), jnp.swapaxes(w_in, 0, 1), mu_cat, row(w0), _pad_rows(w_up, LANE).astype(BF16), row(a0),
        _pad_rows(a_up, LANE).astype(BF16), _pad_rows(g_up, LANE).astype(BF16), row(k_k), row(k_a), row(r_k),
        _pad_cols(row(f_bias), LANE), tile(q_norm), tile(k_norm), e,
    ]
    const_specs = [_resident(c.shape) for c in consts]
    const_specs[1] = pl.BlockSpec(memory_space=pl.ANY)
    assert n_cols - c3 == 2 * d
    tm, tile_rows = tm * n_sub, tm
    tok = lambda: pl.BlockSpec((1, tm, D_MIX), lambda b, i: (b, i, 0))
    f32_tok = jax.ShapeDtypeStruct((bsz, s, D_MIX), F32)
    bf_tok = jax.ShapeDtypeStruct((bsz, s, D_MIX), BF16)
    assert s % tm == 0 and FOX_KBLOCK % tile_rows == 0
    return pl.pallas_call(
        functools.partial(_in_proj_kernel, tm=tile_rows, n_sub=n_sub, tiles_per_kblock=FOX_KBLOCK // tile_rows,
                          chunks=chunks),
        out_shape=[f32_tok, f32_tok, f32_tok, bf_tok, f32_tok, f32_tok, bf_tok, bf_tok] + [
            jax.ShapeDtypeStruct((bsz, N_HEADS, 2 * HEAD, s), BF16),
            jax.ShapeDtypeStruct((bsz, N_HEADS, s, 2 * HEAD), BF16),
            jax.ShapeDtypeStruct((bsz, N_HEADS, V_ROWS, s), BF16),
            jax.ShapeDtypeStruct((bsz, s // tile_rows, 1, LANE), F32),
            jax.ShapeDtypeStruct((2 * d, d), BF16),
        ],
        grid=(bsz, s // tm),
        in_specs=[pl.BlockSpec((1, tm, d), lambda b, i: (b, i, 0))] + const_specs,
        out_specs=[tok() for _ in range(8)] + [
            pl.BlockSpec((1, N_HEADS, 2 * HEAD, tm), lambda b, i: (b, 0, 0, i)),
            pl.BlockSpec((1, N_HEADS, tm, 2 * HEAD), lambda b, i: (b, 0, i, 0)),
            pl.BlockSpec((1, N_HEADS, V_ROWS, tm), lambda b, i: (b, 0, 0, i)),
            pl.BlockSpec((1, n_sub, 1, LANE), lambda b, i: (b, i, 0, 0)),
            pl.BlockSpec((2 * d, d), lambda b, i: (0, 0)),
        ],
        scratch_shapes=[pltpu.VMEM((1, N_SHIFT), F32), pltpu.VMEM((1, LANE), F32), pltpu.VMEM((1, LANE), F32),
                        pltpu.VMEM((N_PROJ, d), BF16), pltpu.VMEM((2, STAGE_ROWS, d), F32),
                        pltpu.SemaphoreType.DMA((2,))],
        compiler_params=pltpu.CompilerParams(
            dimension_semantics=("arbitrary", "arbitrary"), vmem_limit_bytes=VMEM_LIMIT),
        name="in_proj",
    )(x, *consts)


def _rwkv_kernel(r_ref, lw_ref, k_ref, v_ref, kk_ref, b_ref, g_ref, bonus_ref, lng_ref, lnb_ref, e_ref, y_ref, s_scr, *, tb):
    i = pl.program_id(1)

    @pl.when(i == 0)
    def _():
        s_scr[...] = jnp.zeros_like(s_scr)

    c = CHUNK
    pw = 2 * HEAD
    n_chunk = tb // c
    n_pair = N_HEADS // 2
    rr = lax.broadcasted_iota(jnp.int32, (c, pw), 0)
    ll = lax.broadcasted_iota(jnp.int32, (c, pw), 1)
    cc = ll % c
    lo = ll < c
    strict = cc < rr
    incl = cc <= rr
    eye = (cc == rr).astype(F32)
    r2 = lax.broadcasted_iota(jnp.int32, (2 * c, pw), 0)
    l2 = lax.broadcasted_iota(jnp.int32, (2 * c, pw), 1)
    lo2 = l2 < c
    diag2 = (r2 < c) == lo2
    zeros = jnp.zeros((c, pw), F32)

    def level_mask(n):
        return ((rr // (2 * n)) == (cc // (2 * n))) & (((rr // n) % 2) == 1) & (((cc // n) % 2) == 0)

    def bd(x):
        return jnp.where(diag2, jnp.concatenate([x, x], axis=0), 0.0).astype(BF16)

    def bd_sw(x):
        return jnp.where(diag2, 0.0, jnp.concatenate([x, x], axis=0)).astype(BF16)

    slab = min(tb, MXU_DIM)
    rb = lax.broadcasted_iota(jnp.int32, (slab, slab), 0)
    cb = lax.broadcasted_iota(jnp.int32, (slab, slab), 1)
    tri = jnp.where((cb <= rb) & ((rb // c) == (cb // c)), 1.0, 0.0).astype(BF16)
    lw = lw_ref[0]
    cs = jnp.concatenate([_cumsum_rows(tri, lw[j:j + slab]) for j in range(0, tb, slab)], axis=0)
    w_inv = jnp.exp(-cs)
    rt = r_ref[0] * jnp.exp(cs)
    at = -kk_ref[0] * jnp.exp(cs - lw)
    bi = b_ref[0] * w_inv
    ki = k_ref[0] * w_inv
    v = v_ref[0].astype(F32)

    chains = [(j, p) for j in range(n_chunk) for p in range(n_pair)]
    cut = lambda x, j, p: x[j * c:(j + 1) * c, p * pw:(p + 1) * pw]
    w_tot = [jnp.exp(cs[(j + 1) * c - 1:(j + 1) * c, :]) for j in range(n_chunk)]
    a_t = [cut(at, j, p) for j, p in chains]
    r_t = [cut(rt, j, p) for j, p in chains]
    b_i = [cut(bi, j, p) for j, p in chains]
    k_i = [cut(ki, j, p) for j, p in chains]
    v_p = [cut(v, j, p) for j, p in chains]
    n = len(chains)

    a_ab, a_ak, a_rb, a_rk = [], [], [], []
    for q in range(n):
        ar = jnp.concatenate([a_t[q], r_t[q]], axis=0)
        gx = _dot(jnp.where(lo2, ar, 0.0).astype(BF16), jnp.concatenate([b_i[q], k_i[q]], axis=0).astype(BF16), _NT)
        gy = _dot(jnp.where(lo2, 0.0, ar).astype(BF16), jnp.concatenate([k_i[q], b_i[q]], axis=0).astype(BF16), _NT)
        a_ab.append(jnp.where(strict, jnp.where(lo, gx[:c], gy[:c]), 0.0))
        a_ak.append(jnp.where(strict, jnp.where(lo, gy[:c], gx[:c]), 0.0))
        a_rb.append(jnp.where(incl, jnp.where(lo, gx[c:], gy[c:]), 0.0))
        a_rk.append(jnp.where(incl, jnp.where(lo, gy[c:], gx[c:]), 0.0))

    t = [eye + jnp.where(level_mask(1), a_ab[q], 0.0) for q in range(n)]
    for m in (2, 4, 8, 16, 32):
        x = [_dot(jnp.where(level_mask(m), a_ab[q], 0.0).astype(BF16), bd(t[q])) for q in range(n)]
        t = [t[q] + _dot(t[q].astype(BF16), bd(x[q])) for q in range(n)]

    v_sw = [bd_sw(v_p[q]) for q in range(n)]
    akv = [_dot(a_ak[q].astype(BF16), v_sw[q]) for q in range(n)]
    x = [_dot(t[q].astype(BF16), jnp.concatenate([bd(a_t[q]), bd(akv[q])], axis=1)) for q in range(n)]
    a_p = [x[q][:, :pw] for q in range(n)]
    u_0 = [x[q][:, pw:] for q in range(n)]
    zero_bd = jnp.zeros((2 * c, pw), BF16)
    d1 = [_dot(jnp.concatenate([a_rb[q], a_rk[q]], axis=1).astype(BF16),
               jnp.concatenate([jnp.concatenate([bd(a_p[q]), bd(u_0[q])], axis=1),
                                jnp.concatenate([zero_bd, v_sw[q]], axis=1)], axis=0)) for q in range(n)]
    d2 = []
    for q, (j, p) in enumerate(chains):
        wt = w_tot[j][:, p * pw:(p + 1) * pw]
        lhs = jnp.concatenate([jnp.concatenate([a_p[q], u_0[q]], axis=1),
                               jnp.concatenate([zeros, v_p[q]], axis=1)], axis=0).astype(BF16)
        rhs = jnp.concatenate([b_i[q] * wt, k_i[q] * wt], axis=0).astype(BF16)
        d2.append(_dot(lhs, rhs, _TN))

    s = [s_scr[p] for p in range(n_pair)]
    ys = [None] * n
    for j in range(n_chunk):
        for p in range(n_pair):
            q = j * n_pair + p
            mt = jnp.where(diag2, d2[q][:pw], 0.0).astype(BF16)
            nt = jnp.where(lo, d2[q][pw:pw + c], d2[q][pw + c:])
            ys[q] = d1[q][:, pw:] + _dot((r_t[q] + d1[q][:, :pw]).astype(BF16), bd(s[p]), _NT)
            s[p] = s[p] * w_tot[j][:, p * pw:(p + 1) * pw] + _dot(s[p].astype(BF16), mt) + nt
    for p in range(n_pair):
        s_scr[p] = s[p]

    y = jnp.concatenate([jnp.concatenate([ys[j * n_pair + p] for p in range(n_pair)], axis=1) for j in range(n_chunk)], axis=0)
    e = e_ref[...]
    inv_head = 1.0 / HEAD
    yc = y - _head_sum(y, e) * inv_head
    yn = yc * lax.rsqrt(_head_sum(yc * yc, e) * inv_head + LNX_EPS)
    y_ref[0] = ((yn * lng_ref[...] + lnb_ref[...] + bonus_ref[0]) * g_ref[0]).astype(BF16)


def _head_mask():
    head_id = jnp.arange(MXU_DIM) // HEAD
    return (head_id[:, None] == head_id[None, :]).astype(BF16)


def _rwkv(r, lw, k, v, kk, b, g, bonus, lnx_g, lnx_b, *, tb=512):
    bsz, s, _ = r.shape
    assert s % tb == 0 and tb % CHUNK == 0
    tok = pl.BlockSpec((1, tb, D_MIX), lambda bb, i: (bb, i, 0))
    vec = pl.BlockSpec((1, D_MIX), lambda bb, i: (0, 0))
    return pl.pallas_call(
        functools.partial(_rwkv_kernel, tb=tb),
        out_shape=jax.ShapeDtypeStruct((bsz, s, D_MIX), BF16),
        grid=(bsz, s // tb),
        in_specs=[tok] * 8 + [vec, vec, pl.BlockSpec((MXU_DIM, MXU_DIM), lambda bb, i: (0, 0))],
        out_specs=tok,
        scratch_shapes=[pltpu.VMEM((N_HEADS // 2, HEAD, 2 * HEAD), F32)],
        compiler_params=pltpu.CompilerParams(
            dimension_semantics=("parallel", "arbitrary"), vmem_limit_bytes=VMEM_LIMIT),
        name="rwkv",
    )(r, lw, k, v, kk, b, g, bonus, lnx_g.reshape(1, -1).astype(F32), lnx_b.reshape(1, -1).astype(F32), _head_mask())


def _fox_kernel(off_ref, thr_ref, q_ref, k_ref, v_ref, o_ref, *scratch, tq, tk, qw, n_kblk):
    b, hp, qb = pl.program_id(0), pl.program_id(1), pl.program_id(2)
    kb_last = (qb * tq) // tk
    delta = qb * tq - kb_last * tk
    chains = [(hh, g) for hh in range(2) for g in range(tq // qw)]
    n_c = len(chains)
    s_scr = (scratch[:n_c], scratch[n_c:2 * n_c])
    m_scr, acc_scr = scratch[2 * n_c:]
    base = [(b * N_HEADS + 2 * hp + hh) * n_kblk for hh in range(2)]
    off_d = [off_ref[base[hh] + kb_last] for hh in range(2)]

    thr = thr_ref[0]

    def needed(kb):
        nxt = jnp.minimum(kb + 1, kb_last)
        return jnp.minimum(off_ref[base[0] + nxt] - off_d[0], off_ref[base[1] + nxt] - off_d[1]) <= thr

    kb_first = lax.while_loop(lambda kb: jnp.logical_and(kb > 0, needed(kb - 1)), lambda kb: kb - 1, kb_last)
    n_open = kb_last - kb_first

    def qk(kb, slot):
        ksl = pl.ds(pl.multiple_of(kb * tk, tk), tk)
        for c, (hh, g) in enumerate(chains):
            s_scr[slot][c][...] = _dot(k_ref[0, hh, ksl, :], q_ref[0, hh, :, g * qw:(g + 1) * qw])

    def soft_pv(kb, slot, masked):
        ksl = pl.ds(pl.multiple_of(kb * tk, tk), tk)
        for c, (hh, g) in enumerate(chains):
            s = s_scr[slot][c][...]
            if masked:
                key = lax.broadcasted_iota(jnp.int32, (tk, qw), 0)
                qry = lax.broadcasted_iota(jnp.int32, (tk, qw), 1)
                s = jnp.where(key - qry <= delta + g * qw, s, NEG_BIG)
            off = off_ref[base[hh] + kb] - off_d[hh]
            m = m_scr[c]
            m_new = jnp.maximum(m, jnp.max(s, axis=0, keepdims=True) - off)
            p = jnp.exp2(s - (m_new + off)).astype(BF16)
            acc_scr[c] = jnp.exp2(m - m_new) * acc_scr[c] + _dot(v_ref[0, hh, :, ksl], p)
            m_scr[c] = m_new

    m_scr[...] = jnp.full(m_scr.shape, NEG_BIG, F32)
    acc_scr[...] = jnp.zeros(acc_scr.shape, F32)
    qk(kb_first, 0)

    def pair(i, _):
        kb = kb_first + 2 * i
        qk(kb + 1, 1)
        soft_pv(kb, 0, False)
        qk(kb + 2, 0)
        soft_pv(kb + 1, 1, False)
        return 0

    lax.fori_loop(0, n_open // 2, pair, 0)

    @pl.when(n_open % 2 == 1)
    def _():
        qk(kb_last, 1)
        soft_pv(kb_last - 1, 0, False)
        soft_pv(kb_last, 1, True)

    @pl.when(n_open % 2 == 0)
    def _():
        soft_pv(kb_last, 0, True)

    acc = [acc_scr[c] for c in range(n_c)]
    o_t = jnp.concatenate([jnp.concatenate([acc[c][:HEAD] / acc[c][HEAD:HEAD + 1] for c, (h2, _) in enumerate(chains)
                                            if h2 == hh], axis=1) for hh in range(2)], axis=0)
    o_ref[0] = o_t.T.astype(BF16)


def _fox(q_t, k_aug, v_t, off, q_norm, k_norm, *, tq=512, tk=FOX_KBLOCK, qw=MXU_DIM):
    bsz, nh, s, _ = k_aug.shape
    n_kblk = s // tk
    n_chain = 2 * (tq // qw)
    assert tq % qw == 0
    assert s % tq == 0 and s % tk == 0 and tk % tq == 0 and nh % 2 == 0 and off.shape == (bsz * nh * n_kblk,)
    qk_bound = 1.02 * LOG2E * HEAD ** 0.5 * jnp.max(jnp.abs(q_norm)) * jnp.max(jnp.abs(k_norm))
    thr = (2.0 * qk_bound + EXP2_UNDERFLOW).reshape(1).astype(F32)
    return pl.pallas_call(
        functools.partial(_fox_kernel, tq=tq, tk=tk, qw=qw, n_kblk=n_kblk),
        out_shape=jax.ShapeDtypeStruct((bsz, s, nh * HEAD), BF16),
        grid=(bsz, nh // 2, s // tq),
        in_specs=[
            pl.BlockSpec(memory_space=pltpu.SMEM),
            pl.BlockSpec(memory_space=pltpu.SMEM),
            pl.BlockSpec((1, 2, 2 * HEAD, tq), lambda b, hp, i: (b, hp, 0, i)),
            pl.BlockSpec((1, 2, s, 2 * HEAD), lambda b, hp, i: (b, hp, 0, 0)),
            pl.BlockSpec((1, 2, V_ROWS, s), lambda b, hp, i: (b, hp, 0, 0)),
        ],
        out_specs=pl.BlockSpec((1, tq, 2 * HEAD), lambda b, hp, i: (b, i, hp)),
        scratch_shapes=[pltpu.VMEM((tk, qw), F32)] * (2 * n_chain)
        + [pltpu.VMEM((n_chain, 1, qw), F32), pltpu.VMEM((n_chain, V_ROWS, qw), F32)],
        compiler_params=pltpu.CompilerParams(
            dimension_semantics=("parallel", "parallel", "arbitrary"), vmem_limit_bytes=VMEM_LIMIT),
        name="fox",
    )(off, thr, q_t, k_aug, v_t)


def _merge_kernel(x_ref, g_ref, wg_ref, ya_ref, yb_ref, pa_ref, pb_ref, wo_ref, o_ref):
    x = x_ref[...]
    d = x.shape[-1]
    h = _rms(x, g_ref[...]).astype(BF16)
    gates = _sigmoid(_dot(h, wg_ref[...], _NT))
    ya = _dot(ya_ref[...], pa_ref[...])
    yb = _dot(yb_ref[...], pb_ref[...])
    mix = (gates[:, :d] * ya + gates[:, d:] * yb).astype(BF16)
    o_ref[...] = x + _dot(mix, wo_ref[...])


def _merge(x2d, mix_norm, w_gates, ya, yb, p_a, p_b, w_out, *, tm=512):
    m, d = x2d.shape
    assert m % tm == 0
    full = lambda t: pl.BlockSpec(t.shape, lambda i: (0, 0))
    tok = lambda n: pl.BlockSpec((tm, n), lambda i: (i, 0))
    consts = [mix_norm.reshape(1, d).astype(F32), w_gates.astype(BF16)]
    mats = [p_a.astype(BF16), p_b.astype(BF16), w_out.astype(BF16)]
    return pl.pallas_call(
        _merge_kernel,
        out_shape=jax.ShapeDtypeStruct((m, d), F32),
        grid=(m // tm,),
        in_specs=[tok(d), full(consts[0]), full(consts[1]), tok(D_MIX), tok(D_MIX)] + [full(t) for t in mats],
        out_specs=tok(d),
        compiler_params=pltpu.CompilerParams(dimension_semantics=("parallel",), vmem_limit_bytes=VMEM_LIMIT),
        name="merge",
    )(x2d, consts[0], consts[1], ya, yb, *mats)


def kernel(x, ffn1_norm, ffn1_w1, ffn1_w3, ffn1_w2, mix_norm, w_in, shift_mu, rwkv_w0, rwkv_w_up, rwkv_a0, rwkv_a_up, rwkv_g_up, rwkv_k_k, rwkv_k_a, rwkv_r_k, rwkv_lnx_g, rwkv_lnx_b, rwkv_proj, fox_f_bias, fox_q_norm, fox_k_norm, fox_proj, w_out, ffn2_norm, ffn2_w1, ffn2_w3, ffn2_w2, final_norm):
    bsz, s, d = x.shape
    depth = ffn1_norm.shape[0]
    x2d = x.reshape(bsz * s, d)
    for l in range(depth):
        x2d = _ffn(x2d, ffn1_norm[l], ffn1_w1[l], ffn1_w3[l], ffn1_w2[l], final_norm, final_norm=False)
        (r, lw, k, v, kk, b, g, bonus, fq_t, fk_aug, fv_t, off, w_gates) = _in_proj(
            x2d.reshape(bsz, s, d), mix_norm[l], w_in[l], shift_mu[l], rwkv_w0[l], rwkv_w_up[l], rwkv_a0[l],
            rwkv_a_up[l], rwkv_g_up[l], rwkv_k_k[l], rwkv_k_a[l], rwkv_r_k[l], fox_f_bias[l], fox_q_norm[l],
            fox_k_norm[l])
        ya = _rwkv(r, lw, k, v, kk, b, g, bonus, rwkv_lnx_g[l], rwkv_lnx_b[l])
        off = off[:, ::FOX_KBLOCK // (s // off.shape[1]), 0, :N_HEADS]
        off = jnp.transpose(off, (0, 2, 1)).reshape(-1)
        yb = _fox(fq_t, fk_aug, fv_t, off, fox_q_norm[l], fox_k_norm[l])
        x2d = _merge(x2d, mix_norm[l], w_gates, ya.reshape(bsz * s, D_MIX), yb.reshape(bsz * s, D_MIX),
                     rwkv_proj[l], fox_proj[l], w_out[l])
        x2d = _ffn(x2d, ffn2_norm[l], ffn2_w1[l], ffn2_w3[l], ffn2_w2[l], final_norm, final_norm=(l == depth - 1))
    return x2d.reshape(bsz, s, d)
```

```python
import functools

import jax
import jax.numpy as jnp
from jax import lax
from jax.experimental import pallas as pl
from jax.experimental.pallas import tpu as pltpu

F32 = jnp.float32
BF16 = jnp.bfloat16

HEAD = 64
N_HEADS = 8
D_MIX = HEAD * N_HEADS
LANE = 128
MXU_DIM = 256
CHUNK = 64
RMS_EPS = 1e-6
LNX_EPS = 64e-5
NEG_BIG = -1e30
EXP2_UNDERFLOW = 160.0
VMEM_LIMIT = 56 * 1024 * 1024

_NT = (((1,), (1,)), ((), ()))
_TN = (((0,), (0,)), ((), ()))


def _dot(a, b, dims=None):
    if dims is None:
        return jnp.dot(a, b, preferred_element_type=F32)
    return lax.dot_general(a, b, dims, preferred_element_type=F32)


def _bdot(a, b, dims=None):
    return _dot(a.astype(BF16), b.astype(BF16), dims)


def _rms(x, g):
    return x * lax.rsqrt(jnp.mean(x * x, axis=-1, keepdims=True) + RMS_EPS) * g


def _softplus(z):
    return jnp.maximum(z, 0.0) + jnp.log(1.0 + jnp.exp(-jnp.abs(z)))


def _sigmoid(z):
    return 1.0 / (1.0 + jnp.exp(-z))


def _head_sum(x, e):
    xb = x.astype(BF16)
    return jnp.concatenate([_dot(xb[:, j:j + MXU_DIM], e) for j in range(0, x.shape[1], MXU_DIM)], axis=1)


def _cumsum_rows(tri, x):
    hi = x.astype(BF16)
    r1 = x - hi.astype(F32)
    mid = r1.astype(BF16)
    lo = (r1 - mid.astype(F32)).astype(BF16)
    return _dot(tri, hi) + _dot(tri, mid) + _dot(tri, lo)


WEIGHT_CHUNKS = 8
PIECE_ROWS = 512
STAGE_ROWS = 640


def _load_cast(w_hbm, w_ref, stage, sem):
    rows = w_hbm.shape[0] // WEIGHT_CHUNKS

    def copy(c):
        return pltpu.make_async_copy(w_hbm.at[pl.ds(c * rows, rows)], stage.at[c % 2], sem.at[c % 2])

    copy(0).start()
    for c in range(WEIGHT_CHUNKS):
        if c + 1 < WEIGHT_CHUNKS:
            copy(c + 1).start()
        copy(c).wait()
        w_ref[pl.ds(c * rows, rows), :] = stage[c % 2].astype(BF16)


def _ffn_kernel(x_ref, g_ref, w1_hbm, w3_hbm, w2_hbm, gf_ref, o_ref, w1_ref, w3_ref, w2_ref, stage_up, stage_down, sem,
                *, final_norm):
    @pl.when(pl.program_id(0) == 0)
    def _():
        _load_cast(w1_hbm, w1_ref, stage_up, sem)
        _load_cast(w3_hbm, w3_ref, stage_up, sem)
        _load_cast(w2_hbm, w2_ref, stage_down, sem)

    x = x_ref[...]
    h = _rms(x, g_ref[...]).astype(BF16)
    a = _dot(h, w1_ref[...])
    b = _dot(h, w3_ref[...])
    u = (a * _sigmoid(a) * b).astype(BF16)
    y = x + 0.5 * _dot(u, w2_ref[...])
    if final_norm:
        y = _rms(y, gf_ref[...])
    o_ref[...] = y


def _resident(shape):
    return pl.BlockSpec(shape, lambda *_: (0,) * len(shape), pipeline_mode=pl.Buffered(1))


def _ffn(x2d, g, w1, w3, w2, gf, *, final_norm, tm=512):
    m, d = x2d.shape
    f_dim = w1.shape[1]
    assert m % tm == 0 and d % WEIGHT_CHUNKS == 0 and f_dim % WEIGHT_CHUNKS == 0
    in_hbm = pl.BlockSpec(memory_space=pl.ANY)
    return pl.pallas_call(
        functools.partial(_ffn_kernel, final_norm=final_norm),
        out_shape=jax.ShapeDtypeStruct((m, d), F32),
        grid=(m // tm,),
        in_specs=[pl.BlockSpec((tm, d), lambda i: (i, 0)), _resident((1, d)), in_hbm, in_hbm, in_hbm, _resident((1, d))],
        out_specs=pl.BlockSpec((tm, d), lambda i: (i, 0)),
        scratch_shapes=[
            pltpu.VMEM((d, f_dim), BF16), pltpu.VMEM((d, f_dim), BF16), pltpu.VMEM((f_dim, d), BF16),
            pltpu.VMEM((2, d // WEIGHT_CHUNKS, f_dim), F32), pltpu.VMEM((2, f_dim // WEIGHT_CHUNKS, d), F32),
            pltpu.SemaphoreType.DMA((2,)),
        ],
        compiler_params=pltpu.CompilerParams(dimension_semantics=("arbitrary",), vmem_limit_bytes=VMEM_LIMIT),
        name="ffn",
    )(x2d, g.reshape(1, d), w1, w3, w2, gf.reshape(1, d))


N_SHIFT = 3 * D_MIX + 3 * LANE
N_PROJ = N_SHIFT + 3 * D_MIX + LANE
LOG2E = 1.4426950408889634
V_ROWS = HEAD + 16
FOX_KBLOCK = 512


def _in_proj_kernel(x_ref, g_ref, w_hbm, mu_ref, w0_ref, wup_ref, a0_ref, aup_ref, gup_ref, kk_ref, ka_ref,
                    rk_ref, fb_ref, qn_ref, kn_ref, e_ref,
                    r_out, lw_out, k_out, v_out, kk_out, b_out, g_out, bonus_out, fq_out, fk_out, fv_out, off_out,
                    wg_out, carry_p, carry_c, carry_b, w_ref, stage, sem, *, tm, n_sub, tiles_per_kblock, chunks):
    step = pl.program_id(1)

    @pl.when(jnp.logical_and(pl.program_id(0) == 0, step == 0))
    def _():
        w_ref[...] = jnp.zeros_like(w_ref)

        def copy(i):
            src, n = chunks[i][:2]
            return pltpu.make_async_copy(w_hbm.at[pl.ds(src, n)], stage.at[i % 2, pl.ds(0, n)], sem.at[i % 2])

        copy(0).start()
        for i, (_, _, stores) in enumerate(chunks):
            if i + 1 < len(chunks):
                copy(i + 1).start()
            copy(i).wait()
            for at, n, to_gates, dst in stores:
                (wg_out if to_gates else w_ref)[pl.ds(dst, n), :] = stage[i % 2, at:at + n].astype(BF16)

    @pl.when(step == 0)
    def _():
        carry_p[...] = jnp.zeros_like(carry_p)
        carry_c[...] = jnp.zeros_like(carry_c)
        carry_b[...] = jnp.zeros_like(carry_b)

    outs = (r_out, lw_out, k_out, v_out, kk_out, b_out, g_out, bonus_out)
    for t in range(n_sub):
        rows = pl.ds(t * tm, tm)
        _in_proj_tile(step * n_sub + t, x_ref.at[0, rows], g_ref, w_ref, mu_ref, w0_ref, wup_ref, a0_ref, aup_ref, gup_ref,
                      kk_ref, ka_ref, rk_ref, fb_ref, qn_ref, kn_ref, e_ref, *[o.at[0, rows] for o in outs],
                      fq_out.at[0, :, :, rows], fk_out.at[0, :, rows], fv_out.at[0, :, :, rows], off_out.at[0, t],
                      carry_p, carry_c, carry_b, tm=tm, tiles_per_kblock=tiles_per_kblock)


def _in_proj_tile(i, x_ref, g_ref, w_ref, mu_ref, w0_ref, wup_ref, a0_ref, aup_ref, gup_ref, kk_ref, ka_ref,
                  rk_ref, fb_ref, qn_ref, kn_ref, e_ref,
                  r_out, lw_out, k_out, v_out, kk_out, b_out, g_out, bonus_out, fq_out, fk_out, fv_out, off_out,
                  carry_p, carry_c, carry_b, *, tm, tiles_per_kblock):
    h = _rms(x_ref[...], g_ref[...]).astype(BF16)
    p = _dot(h, w_ref[...], _NT)

    ps = p[:, :N_SHIFT]
    row = lax.broadcasted_iota(jnp.int32, (tm, 1), 0)
    prev = jnp.where(row == 0, carry_p[...], pltpu.roll(ps, 1, axis=0))
    carry_p[...] = ps[tm - 1:tm, :]
    sh = ps + (prev - ps) * mu_ref[...]

    e = e_ref[...]
    r = sh[:, 0:D_MIX]
    k = sh[:, D_MIX:2 * D_MIX]
    v = sh[:, 2 * D_MIX:3 * D_MIX]
    w_lo = sh[:, 3 * D_MIX:3 * D_MIX + LANE]
    a_lo = sh[:, 3 * D_MIX + LANE:3 * D_MIX + 2 * LANE]
    g_lo = sh[:, 3 * D_MIX + 2 * LANE:N_SHIFT]

    w_log = -_softplus(-(w0_ref[...] + _bdot(jnp.tanh(w_lo), wup_ref[...]))) - 0.5
    lw = -jnp.exp(w_log)
    a = _sigmoid(a0_ref[...] + _bdot(a_lo, aup_ref[...]))
    g = _bdot(_sigmoid(g_lo), gup_ref[...])
    kk = k * kk_ref[...]
    kk = kk * lax.rsqrt(jnp.maximum(_head_sum(kk * kk, e), 1e-24))
    k_mod = k * (1.0 + (a - 1.0) * ka_ref[...])
    bonus = _head_sum(r * k_mod * rk_ref[...], e) * v

    r_out[...] = r
    lw_out[...] = lw
    k_out[...] = k_mod
    v_out[...] = v.astype(BF16)
    kk_out[...] = kk
    b_out[...] = kk * a
    g_out[...] = g.astype(BF16)
    bonus_out[...] = bonus.astype(BF16)

    fq = p[:, N_SHIFT:N_SHIFT + D_MIX]
    fk = p[:, N_SHIFT + D_MIX:N_SHIFT + 2 * D_MIX]
    fv = p[:, N_SHIFT + 2 * D_MIX:N_SHIFT + 3 * D_MIX]
    f_lo = p[:, N_SHIFT + 3 * D_MIX:N_PROJ]
    inv_head = 1.0 / HEAD
    qn = fq * lax.rsqrt(_head_sum(fq * fq, e) * inv_head + RMS_EPS) * qn_ref[...] * (LOG2E * HEAD ** -0.5)
    kn = fk * lax.rsqrt(_head_sum(fk * fk, e) * inv_head + RMS_EPS) * kn_ref[...]
    z = f_lo + fb_ref[...]
    log_f = jnp.minimum(z, 0.0) - jnp.log(1.0 + jnp.exp(-jnp.abs(z)))
    tri = jnp.where(lax.broadcasted_iota(jnp.int32, (tm, tm), 1) <= lax.broadcasted_iota(jnp.int32, (tm, tm), 0),
                    1.0, 0.0).astype(BF16)
    c_loc = _cumsum_rows(tri, log_f)
    off_out[...] = carry_c[...] * LOG2E
    in_blk = jnp.where(i % tiles_per_kblock == 0, 0.0, carry_b[...])
    carry_c[...] = carry_c[...] + c_loc[tm - 1:tm, :]
    carry_b[...] = in_blk + c_loc[tm - 1:tm, :]
    c2 = (c_loc + in_blk) * LOG2E

    lane = lax.broadcasted_iota(jnp.int32, (tm, HEAD), 1)
    sub = lax.broadcasted_iota(jnp.int32, (HEAD, tm), 0)
    q_tail = jnp.where(sub < 3, -1.0, 0.0)
    v_tail = jnp.where(lax.broadcasted_iota(jnp.int32, (V_ROWS - HEAD, tm), 0) == 0, 1.0, 0.0)
    for pr in range(N_HEADS // 2):
        ps2 = slice(pr * 2 * HEAD, (pr + 1) * 2 * HEAD)
        q_t = qn[:, ps2].T
        v_t = fv[:, ps2].T
        for half in range(2):
            hh = 2 * pr + half
            rows = slice(half * HEAD, (half + 1) * HEAD)
            fq_out[hh] = jnp.concatenate([q_t[rows], q_tail], axis=0).astype(BF16)
            fv_out[hh] = jnp.concatenate([v_t[rows], v_tail], axis=0).astype(BF16)
            c_h = c2[:, hh:hh + 1]
            c_hi = c_h.astype(BF16).astype(F32)
            c_mid = (c_h - c_hi).astype(BF16).astype(F32)
            c_lo = c_h - c_hi - c_mid
            c_cols = jnp.where(lane == 0, c_hi, jnp.where(lane == 1, c_mid, jnp.where(lane == 2, c_lo, 0.0)))
            fk_out[hh] = jnp.concatenate([kn[:, hh * HEAD:(hh + 1) * HEAD], c_cols], axis=1).astype(BF16)


def _pad_cols(w, n):
    return jnp.pad(w, ((0, 0), (0, n - w.shape[1])))


def _pad_rows(w, n):
    return jnp.pad(w, ((0, n - w.shape[0]), (0, 0)))


def _in_proj(x, mix_norm, w_in, shift_mu, w0, w_up, a0, a_up, g_up, k_k, k_a, r_k, f_bias, q_norm, k_norm, *, tm=256, n_sub=2):
    bsz, s, d = x.shape
    lw_, la_, lg_ = w_up.shape[0], a_up.shape[0], g_up.shape[0]
    c0 = 3 * D_MIX
    c1 = c0 + lw_ + la_ + lg_
    c2 = c1 + 3 * D_MIX
    c3 = c2 + N_HEADS

    def seg(t, pad):
        return jnp.concatenate([
            t[..., :c0],
            pad(t[..., c0:c0 + lw_], LANE), pad(t[..., c0 + lw_:c0 + lw_ + la_], LANE), pad(t[..., c0 + lw_ + la_:c1], LANE),
        ], axis=-1)

    n_cols = w_in.shape[1]
    groups = ((0, c0, False, 0), (c0, lw_, False, c0), (c0 + lw_, la_, False, c0 + LANE),
              (c0 + lw_ + la_, lg_, False, c0 + 2 * LANE), (c1, c2 - c1, False, N_SHIFT),
              (c2, c3 - c2, False, N_SHIFT + 3 * D_MIX), (c3, n_cols - c3, True, 0))
    pieces = [(src + o, min(PIECE_ROWS, width - o), to_gates, dst + o)
              for src, width, to_gates, dst in groups for o in range(0, width, PIECE_ROWS)]
    chunks = []
    for src, n, to_gates, dst in pieces:
        if chunks and chunks[-1][0] + chunks[-1][1] == src and chunks[-1][1] + n <= STAGE_ROWS:
            c_src, c_n, stores = chunks[-1]
            chunks[-1] = (c_src, c_n + n, stores + ((c_n, n, to_gates, dst),))
        else:
            chunks.append((src, n, ((0, n, to_gates, dst),)))
    chunks = tuple(chunks)
    mu_cat = seg(shift_mu.reshape(1, -1), _pad_cols)
    row = lambda t: t.reshape(1, -1).astype(F32)
    e = _head_mask()
    tile = lambda t: jnp.tile(t.reshape(1, HEAD), (1, N_HEADS)).astype(F32)
    consts = [
        row(mix_norm), jnp.swapaxes(w_in, 0, 1), mu_cat, row(w0), _pad_rows(w_up, LANE).astype(BF16), row(a0),
        _pad_rows(a_up, LANE).astype(BF16), _pad_rows(g_up, LANE).astype(BF16), row(k_k), row(k_a), row(r_k),
        _pad_cols(row(f_bias), LANE), tile(q_norm), tile(k_norm), e,
    ]
    const_specs = [_resident(c.shape) for c in consts]
    const_specs[1] = pl.BlockSpec(memory_space=pl.ANY)
    assert n_cols - c3 == 2 * d
    tm, tile_rows = tm * n_sub, tm
    tok = lambda: pl.BlockSpec((1, tm, D_MIX), lambda b, i: (b, i, 0))
    f32_tok = jax.ShapeDtypeStruct((bsz, s, D_MIX), F32)
    bf_tok = jax.ShapeDtypeStruct((bsz, s, D_MIX), BF16)
    assert s % tm == 0 and FOX_KBLOCK % tile_rows == 0
    return pl.pallas_call(
        functools.partial(_in_proj_kernel, tm=tile_rows, n_sub=n_sub, tiles_per_kblock=FOX_KBLOCK // tile_rows,
                          chunks=chunks),
        out_shape=[f32_tok, f32_tok, f32_tok, bf_tok, f32_tok, f32_tok, bf_tok, bf_tok] + [
            jax.ShapeDtypeStruct((bsz, N_HEADS, 2 * HEAD, s), BF16),
            jax.ShapeDtypeStruct((bsz, N_HEADS, s, 2 * HEAD), BF16),
            jax.ShapeDtypeStruct((bsz, N_HEADS, V_ROWS, s), BF16),
            jax.ShapeDtypeStruct((bsz, s // tile_rows, 1, LANE), F32),
            jax.ShapeDtypeStruct((2 * d, d), BF16),
        ],
        grid=(bsz, s // tm),
        in_specs=[pl.BlockSpec((1, tm, d), lambda b, i: (b, i, 0))] + const_specs,
        out_specs=[tok() for _ in range(8)] + [
            pl.BlockSpec((1, N_HEADS, 2 * HEAD, tm), lambda b, i: (b, 0, 0, i)),
            pl.BlockSpec((1, N_HEADS, tm, 2 * HEAD), lambda b, i: (b, 0, i, 0)),
            pl.BlockSpec((1, N_HEADS, V_ROWS, tm), lambda b, i: (b, 0, 0, i)),
            pl.BlockSpec((1, n_sub, 1, LANE), lambda b, i: (b, i, 0, 0)),
            pl.BlockSpec((2 * d, d), lambda b, i: (0, 0)),
        ],
        scratch_shapes=[pltpu.VMEM((1, N_SHIFT), F32), pltpu.VMEM((1, LANE), F32), pltpu.VMEM((1, LANE), F32),
                        pltpu.VMEM((N_PROJ, d), BF16), pltpu.VMEM((2, STAGE_ROWS, d), F32),
                        pltpu.SemaphoreType.DMA((2,))],
        compiler_params=pltpu.CompilerParams(
            dimension_semantics=("arbitrary", "arbitrary"), vmem_limit_bytes=VMEM_LIMIT),
        name="in_proj",
    )(x, *consts)


def _rwkv_kernel(r_ref, lw_ref, k_ref, v_ref, kk_ref, b_ref, g_ref, bonus_ref, lng_ref, lnb_ref, e_ref, y_ref, s_scr, *, tb):
    i = pl.program_id(1)

    @pl.when(i == 0)
    def _():
        s_scr[...] = jnp.zeros_like(s_scr)

    c = CHUNK
    pw = 2 * HEAD
    n_chunk = tb // c
    n_pair = N_HEADS // 2
    rr = lax.broadcasted_iota(jnp.int32, (c, pw), 0)
    ll = lax.broadcasted_iota(jnp.int32, (c, pw), 1)
    cc = ll % c
    lo = ll < c
    strict = cc < rr
    incl = cc <= rr
    eye = (cc == rr).astype(F32)
    r2 = lax.broadcasted_iota(jnp.int32, (2 * c, pw), 0)
    l2 = lax.broadcasted_iota(jnp.int32, (2 * c, pw), 1)
    lo2 = l2 < c
    diag2 = (r2 < c) == lo2
    zeros = jnp.zeros((c, pw), F32)

    def level_mask(n):
        return ((rr // (2 * n)) == (cc // (2 * n))) & (((rr // n) % 2) == 1) & (((cc // n) % 2) == 0)

    def bd(x):
        return jnp.where(diag2, jnp.concatenate([x, x], axis=0), 0.0).astype(BF16)

    def bd_sw(x):
        return jnp.where(diag2, 0.0, jnp.concatenate([x, x], axis=0)).astype(BF16)

    slab = min(tb, MXU_DIM)
    rb = lax.broadcasted_iota(jnp.int32, (slab, slab), 0)
    cb = lax.broadcasted_iota(jnp.int32, (slab, slab), 1)
    tri = jnp.where((cb <= rb) & ((rb // c) == (cb // c)), 1.0, 0.0).astype(BF16)
    lw = lw_ref[0]
    cs = jnp.concatenate([_cumsum_rows(tri, lw[j:j + slab]) for j in range(0, tb, slab)], axis=0)
    w_inv = jnp.exp(-cs)
    rt = r_ref[0] * jnp.exp(cs)
    at = -kk_ref[0] * jnp.exp(cs - lw)
    bi = b_ref[0] * w_inv
    ki = k_ref[0] * w_inv
    v = v_ref[0].astype(F32)

    chains = [(j, p) for j in range(n_chunk) for p in range(n_pair)]
    cut = lambda x, j, p: x[j * c:(j + 1) * c, p * pw:(p + 1) * pw]
    w_tot = [jnp.exp(cs[(j + 1) * c - 1:(j + 1) * c, :]) for j in range(n_chunk)]
    a_t = [cut(at, j, p) for j, p in chains]
    r_t = [cut(rt, j, p) for j, p in chains]
    b_i = [cut(bi, j, p) for j, p in chains]
    k_i = [cut(ki, j, p) for j, p in chains]
    v_p = [cut(v, j, p) for j, p in chains]
    n = len(chains)

    a_ab, a_ak, a_rb, a_rk = [], [], [], []
    for q in range(n):
        ar = jnp.concatenate([a_t[q], r_t[q]], axis=0)
        gx = _dot(jnp.where(lo2, ar, 0.0).astype(BF16), jnp.concatenate([b_i[q], k_i[q]], axis=0).astype(BF16), _NT)
        gy = _dot(jnp.where(lo2, 0.0, ar).astype(BF16), jnp.concatenate([k_i[q], b_i[q]], axis=0).astype(BF16), _NT)
        a_ab.append(jnp.where(strict, jnp.where(lo, gx[:c], gy[:c]), 0.0))
        a_ak.append(jnp.where(strict, jnp.where(lo, gy[:c], gx[:c]), 0.0))
        a_rb.append(jnp.where(incl, jnp.where(lo, gx[c:], gy[c:]), 0.0))
        a_rk.append(jnp.where(incl, jnp.where(lo, gy[c:], gx[c:]), 0.0))

    t = [eye + jnp.where(level_mask(1), a_ab[q], 0.0) for q in range(n)]
    for m in (2, 4, 8, 16, 32):
        x = [_dot(jnp.where(level_mask(m), a_ab[q], 0.0).astype(BF16), bd(t[q])) for q in range(n)]
        t = [t[q] + _dot(t[q].astype(BF16), bd(x[q])) for q in range(n)]

    v_sw = [bd_sw(v_p[q]) for q in range(n)]
    av = [_dot(jnp.concatenate([a_ak[q], a_rk[q]], axis=0).astype(BF16), v_sw[q]) for q in range(n)]
    x = [_dot(t[q].astype(BF16), jnp.concatenate([bd(a_t[q]), bd(av[q][:c])], axis=1)) for q in range(n)]
    a_p = [x[q][:, :pw] for q in range(n)]
    u_0 = [x[q][:, pw:] for q in range(n)]
    d1 = [_dot(a_rb[q].astype(BF16), jnp.concatenate([bd(a_p[q]), bd(u_0[q])], axis=1)) for q in range(n)]
    rp = [r_t[q] + d1[q][:, :pw] for q in range(n)]
    y_0 = [d1[q][:, pw:] + av[q][c:] for q in range(n)]
    d2 = []
    for q, (j, p) in enumerate(chains):
        wt = w_tot[j][:, p * pw:(p + 1) * pw]
        lhs = jnp.concatenate([jnp.concatenate([a_p[q], u_0[q]], axis=1),
                               jnp.concatenate([zeros, v_p[q]], axis=1)], axis=0).astype(BF16)
        rhs = jnp.concatenate([b_i[q] * wt, k_i[q] * wt], axis=0).astype(BF16)
        d2.append(_dot(lhs, rhs, _TN))

    s = [s_scr[p] for p in range(n_pair)]
    ys = [None] * n
    for j in range(n_chunk):
        for p in range(n_pair):
            q = j * n_pair + p
            mt = jnp.where(diag2, d2[q][:pw], 0.0).astype(BF16)
            nt = jnp.where(lo, d2[q][pw:pw + c], d2[q][pw + c:])
            ys[q] = y_0[q] + _dot(rp[q].astype(BF16), bd(s[p]), _NT)
            s[p] = s[p] * w_tot[j][:, p * pw:(p + 1) * pw] + _dot(s[p].astype(BF16), mt) + nt
    for p in range(n_pair):
        s_scr[p] = s[p]

    y = jnp.concatenate([jnp.concatenate([ys[j * n_pair + p] for p in range(n_pair)], axis=1) for j in range(n_chunk)], axis=0)
    e = e_ref[...]
    inv_head = 1.0 / HEAD
    yc = y - _head_sum(y, e) * inv_head
    yn = yc * lax.rsqrt(_head_sum(yc * yc, e) * inv_head + LNX_EPS)
    y_ref[0] = ((yn * lng_ref[...] + lnb_ref[...] + bonus_ref[0]) * g_ref[0]).astype(BF16)


def _head_mask():
    head_id = jnp.arange(MXU_DIM) // HEAD
    return (head_id[:, None] == head_id[None, :]).astype(BF16)


def _rwkv(r, lw, k, v, kk, b, g, bonus, lnx_g, lnx_b, *, tb=512):
    bsz, s, _ = r.shape
    assert s % tb == 0 and tb % CHUNK == 0
    tok = pl.BlockSpec((1, tb, D_MIX), lambda bb, i: (bb, i, 0))
    vec = pl.BlockSpec((1, D_MIX), lambda bb, i: (0, 0))
    return pl.pallas_call(
        functools.partial(_rwkv_kernel, tb=tb),
        out_shape=jax.ShapeDtypeStruct((bsz, s, D_MIX), BF16),
        grid=(bsz, s // tb),
        in_specs=[tok] * 8 + [vec, vec, pl.BlockSpec((MXU_DIM, MXU_DIM), lambda bb, i: (0, 0))],
        out_specs=tok,
        scratch_shapes=[pltpu.VMEM((N_HEADS // 2, HEAD, 2 * HEAD), F32)],
        compiler_params=pltpu.CompilerParams(
            dimension_semantics=("parallel", "arbitrary"), vmem_limit_bytes=VMEM_LIMIT),
        name="rwkv",
    )(r, lw, k, v, kk, b, g, bonus, lnx_g.reshape(1, -1).astype(F32), lnx_b.reshape(1, -1).astype(F32), _head_mask())


def _fox_kernel(off_ref, thr_ref, q_ref, k_ref, v_ref, o_ref, *scratch, tq, tk, qw, n_kblk):
    b, hp, qb = pl.program_id(0), pl.program_id(1), pl.program_id(2)
    kb_last = (qb * tq) // tk
    delta = qb * tq - kb_last * tk
    chains = [(hh, g) for hh in range(2) for g in range(tq // qw)]
    n_c = len(chains)
    s_scr = (scratch[:n_c], scratch[n_c:2 * n_c])
    m_scr, acc_scr = scratch[2 * n_c:]
    base = [(b * N_HEADS + 2 * hp + hh) * n_kblk for hh in range(2)]
    off_d = [off_ref[base[hh] + kb_last] for hh in range(2)]

    thr = thr_ref[0]

    def needed(kb):
        nxt = jnp.minimum(kb + 1, kb_last)
        return jnp.minimum(off_ref[base[0] + nxt] - off_d[0], off_ref[base[1] + nxt] - off_d[1]) <= thr

    kb_first = lax.while_loop(lambda kb: jnp.logical_and(kb > 0, needed(kb - 1)), lambda kb: kb - 1, kb_last)
    n_open = kb_last - kb_first

    def qk(kb, slot):
        ksl = pl.ds(pl.multiple_of(kb * tk, tk), tk)
        for c, (hh, g) in enumerate(chains):
            s_scr[slot][c][...] = _dot(k_ref[0, hh, ksl, :], q_ref[0, hh, :, g * qw:(g + 1) * qw])

    def soft_pv(kb, slot, masked):
        ksl = pl.ds(pl.multiple_of(kb * tk, tk), tk)
        for c, (hh, g) in enumerate(chains):
            s = s_scr[slot][c][...]
            if masked:
                key = lax.broadcasted_iota(jnp.int32, (tk, qw), 0)
                qry = lax.broadcasted_iota(jnp.int32, (tk, qw), 1)
                s = jnp.where(key - qry <= delta + g * qw, s, NEG_BIG)
            off = off_ref[base[hh] + kb] - off_d[hh]
            m = m_scr[c]
            m_new = jnp.maximum(m, jnp.max(s, axis=0, keepdims=True) - off)
            p = jnp.exp2(s - (m_new + off)).astype(BF16)
            acc_scr[c] = jnp.exp2(m - m_new) * acc_scr[c] + _dot(v_ref[0, hh, :, ksl], p)
            m_scr[c] = m_new

    m_scr[...] = jnp.full(m_scr.shape, NEG_BIG, F32)
    acc_scr[...] = jnp.zeros(acc_scr.shape, F32)
    qk(kb_first, 0)

    def pair(i, _):
        kb = kb_first + 2 * i
        qk(kb + 1, 1)
        soft_pv(kb, 0, False)
        qk(kb + 2, 0)
        soft_pv(kb + 1, 1, False)
        return 0

    lax.fori_loop(0, n_open // 2, pair, 0)

    @pl.when(n_open % 2 == 1)
    def _():
        qk(kb_last, 1)
        soft_pv(kb_last - 1, 0, False)
        soft_pv(kb_last, 1, True)

    @pl.when(n_open % 2 == 0)
    def _():
        soft_pv(kb_last, 0, True)

    acc = [acc_scr[c] for c in range(n_c)]
    o_t = jnp.concatenate([jnp.concatenate([acc[c][:HEAD] / acc[c][HEAD:HEAD + 1] for c, (h2, _) in enumerate(chains)
                                            if h2 == hh], axis=1) for hh in range(2)], axis=0)
    o_ref[0] = o_t.T.astype(BF16)


def _fox(q_t, k_aug, v_t, off, q_norm, k_norm, *, tq=512, tk=FOX_KBLOCK, qw=MXU_DIM):
    bsz, nh, s, _ = k_aug.shape
    n_kblk = s // tk
    n_chain = 2 * (tq // qw)
    assert tq % qw == 0
    assert s % tq == 0 and s % tk == 0 and tk % tq == 0 and nh % 2 == 0 and off.shape == (bsz * nh * n_kblk,)
    qk_bound = 1.02 * LOG2E * HEAD ** 0.5 * jnp.max(jnp.abs(q_norm)) * jnp.max(jnp.abs(k_norm))
    thr = (2.0 * qk_bound + EXP2_UNDERFLOW).reshape(1).astype(F32)
    return pl.pallas_call(
        functools.partial(_fox_kernel, tq=tq, tk=tk, qw=qw, n_kblk=n_kblk),
        out_shape=jax.ShapeDtypeStruct((bsz, s, nh * HEAD), BF16),
        grid=(bsz, nh // 2, s // tq),
        in_specs=[
            pl.BlockSpec(memory_space=pltpu.SMEM),
            pl.BlockSpec(memory_space=pltpu.SMEM),
            pl.BlockSpec((1, 2, 2 * HEAD, tq), lambda b, hp, i: (b, hp, 0, i)),
            pl.BlockSpec((1, 2, s, 2 * HEAD), lambda b, hp, i: (b, hp, 0, 0)),
            pl.BlockSpec((1, 2, V_ROWS, s), lambda b, hp, i: (b, hp, 0, 0)),
        ],
        out_specs=pl.BlockSpec((1, tq, 2 * HEAD), lambda b, hp, i: (b, i, hp)),
        scratch_shapes=[pltpu.VMEM((tk, qw), F32)] * (2 * n_chain)
        + [pltpu.VMEM((n_chain, 1, qw), F32), pltpu.VMEM((n_chain, V_ROWS, qw), F32)],
        compiler_params=pltpu.CompilerParams(
            dimension_semantics=("parallel", "parallel", "arbitrary"), vmem_limit_bytes=VMEM_LIMIT),
        name="fox",
    )(off, thr, q_t, k_aug, v_t)


def _merge_kernel(x_ref, g_ref, wg_ref, ya_ref, yb_ref, pa_ref, pb_ref, wo_ref, o_ref):
    x = x_ref[...]
    d = x.shape[-1]
    h = _rms(x, g_ref[...]).astype(BF16)
    gates = _sigmoid(_dot(h, wg_ref[...], _NT))
    ya = _dot(ya_ref[...], pa_ref[...])
    yb = _dot(yb_ref[...], pb_ref[...])
    mix = (gates[:, :d] * ya + gates[:, d:] * yb).astype(BF16)
    o_ref[...] = x + _dot(mix, wo_ref[...])


def _merge(x2d, mix_norm, w_gates, ya, yb, p_a, p_b, w_out, *, tm=512):
    m, d = x2d.shape
    assert m % tm == 0
    full = lambda t: pl.BlockSpec(t.shape, lambda i: (0, 0))
    tok = lambda n: pl.BlockSpec((tm, n), lambda i: (i, 0))
    consts = [mix_norm.reshape(1, d).astype(F32), w_gates.astype(BF16)]
    mats = [p_a.astype(BF16), p_b.astype(BF16), w_out.astype(BF16)]
    return pl.pallas_call(
        _merge_kernel,
        out_shape=jax.ShapeDtypeStruct((m, d), F32),
        grid=(m // tm,),
        in_specs=[tok(d), full(consts[0]), full(consts[1]), tok(D_MIX), tok(D_MIX)] + [full(t) for t in mats],
        out_specs=tok(d),
        compiler_params=pltpu.CompilerParams(dimension_semantics=("parallel",), vmem_limit_bytes=VMEM_LIMIT),
        name="merge",
    )(x2d, consts[0], consts[1], ya, yb, *mats)


def kernel(x, ffn1_norm, ffn1_w1, ffn1_w3, ffn1_w2, mix_norm, w_in, shift_mu, rwkv_w0, rwkv_w_up, rwkv_a0, rwkv_a_up, rwkv_g_up, rwkv_k_k, rwkv_k_a, rwkv_r_k, rwkv_lnx_g, rwkv_lnx_b, rwkv_proj, fox_f_bias, fox_q_norm, fox_k_norm, fox_proj, w_out, ffn2_norm, ffn2_w1, ffn2_w3, ffn2_w2, final_norm):
    bsz, s, d = x.shape
    depth = ffn1_norm.shape[0]
    x2d = x.reshape(bsz * s, d)
    for l in range(depth):
        x2d = _ffn(x2d, ffn1_norm[l], ffn1_w1[l], ffn1_w3[l], ffn1_w2[l], final_norm, final_norm=False)
        (r, lw, k, v, kk, b, g, bonus, fq_t, fk_aug, fv_t, off, w_gates) = _in_proj(
            x2d.reshape(bsz, s, d), mix_norm[l], w_in[l], shift_mu[l], rwkv_w0[l], rwkv_w_up[l], rwkv_a0[l],
            rwkv_a_up[l], rwkv_g_up[l], rwkv_k_k[l], rwkv_k_a[l], rwkv_r_k[l], fox_f_bias[l], fox_q_norm[l],
            fox_k_norm[l])
        ya = _rwkv(r, lw, k, v, kk, b, g, bonus, rwkv_lnx_g[l], rwkv_lnx_b[l])
        off = off[:, ::FOX_KBLOCK // (s // off.shape[1]), 0, :N_HEADS]
        off = jnp.transpose(off, (0, 2, 1)).reshape(-1)
        yb = _fox(fq_t, fk_aug, fv_t, off, fox_q_norm[l], fox_k_norm[l])
        x2d = _merge(x2d, mix_norm[l], w_gates, ya.reshape(bsz * s, D_MIX), yb.reshape(bsz * s, D_MIX),
                     rwkv_proj[l], fox_proj[l], w_out[l])
        x2d = _ffn(x2d, ffn2_norm[l], ffn2_w1[l], ffn2_w3[l], ffn2_w2[l], final_norm, final_norm=(l == depth - 1))
    return x2d.reshape(bsz, s, d)
```

```python
import functools

import jax
import jax.numpy as jnp
from jax import lax
from jax.experimental import pallas as pl
from jax.experimental.pallas import tpu as pltpu

F32 = jnp.float32
BF16 = jnp.bfloat16

HEAD = 64
N_HEADS = 8
D_MIX = HEAD * N_HEADS
LANE = 128
MXU_DIM = 256
CHUNK = 64
RMS_EPS = 1e-6
LNX_EPS = 64e-5
NEG_BIG = -1e30
EXP2_UNDERFLOW = 160.0
VMEM_LIMIT = 56 * 1024 * 1024

_NT = (((1,), (1,)), ((), ()))
_TN = (((0,), (0,)), ((), ()))


def _dot(a, b, dims=None):
    if dims is None:
        return jnp.dot(a, b, preferred_element_type=F32)
    return lax.dot_general(a, b, dims, preferred_element_type=F32)


def _bdot(a, b, dims=None):
    return _dot(a.astype(BF16), b.astype(BF16), dims)


def _rms(x, g):
    return x * lax.rsqrt(jnp.mean(x * x, axis=-1, keepdims=True) + RMS_EPS) * g


def _softplus(z):
    return jnp.maximum(z, 0.0) + jnp.log(1.0 + jnp.exp(-jnp.abs(z)))


def _sigmoid(z):
    return 1.0 / (1.0 + jnp.exp(-z))


def _head_sum(x, e):
    xb = x.astype(BF16)
    return jnp.concatenate([_dot(xb[:, j:j + MXU_DIM], e) for j in range(0, x.shape[1], MXU_DIM)], axis=1)


def _cumsum_rows(tri, x):
    hi = x.astype(BF16)
    r1 = x - hi.astype(F32)
    mid = r1.astype(BF16)
    lo = (r1 - mid.astype(F32)).astype(BF16)
    return _dot(tri, hi) + _dot(tri, mid) + _dot(tri, lo)


WEIGHT_CHUNKS = 8
PIECE_ROWS = 512
STAGE_ROWS = 640
STAGE_SLOTS = 3


def _load_cast(w_hbm, w_ref, stage, sem):
    rows = w_hbm.shape[0] // WEIGHT_CHUNKS

    def copy(c):
        slot = c % STAGE_SLOTS
        return pltpu.make_async_copy(w_hbm.at[pl.ds(c * rows, rows)], stage.at[slot], sem.at[slot])

    for c in range(STAGE_SLOTS - 1):
        copy(c).start()
    for c in range(WEIGHT_CHUNKS):
        if c + STAGE_SLOTS - 1 < WEIGHT_CHUNKS:
            copy(c + STAGE_SLOTS - 1).start()
        copy(c).wait()
        w_ref[pl.ds(c * rows, rows), :] = stage[c % STAGE_SLOTS].astype(BF16)


def _ffn_kernel(x_ref, g_ref, w1_hbm, w3_hbm, w2_hbm, gf_ref, o_ref, w1_ref, w3_ref, w2_ref, stage_up, stage_down, sem,
                *, final_norm):
    @pl.when(pl.program_id(0) == 0)
    def _():
        _load_cast(w1_hbm, w1_ref, stage_up, sem)
        _load_cast(w3_hbm, w3_ref, stage_up, sem)
        _load_cast(w2_hbm, w2_ref, stage_down, sem)

    x = x_ref[...]
    h = _rms(x, g_ref[...]).astype(BF16)
    a = _dot(h, w1_ref[...])
    b = _dot(h, w3_ref[...])
    u = (a * _sigmoid(a) * b).astype(BF16)
    y = x + 0.5 * _dot(u, w2_ref[...])
    if final_norm:
        y = _rms(y, gf_ref[...])
    o_ref[...] = y


def _resident(shape):
    return pl.BlockSpec(shape, lambda *_: (0,) * len(shape), pipeline_mode=pl.Buffered(1))


def _ffn(x2d, g, w1, w3, w2, gf, *, final_norm, tm=512):
    m, d = x2d.shape
    f_dim = w1.shape[1]
    assert m % tm == 0 and d % WEIGHT_CHUNKS == 0 and f_dim % WEIGHT_CHUNKS == 0
    in_hbm = pl.BlockSpec(memory_space=pl.ANY)
    return pl.pallas_call(
        functools.partial(_ffn_kernel, final_norm=final_norm),
        out_shape=jax.ShapeDtypeStruct((m, d), F32),
        grid=(m // tm,),
        in_specs=[pl.BlockSpec((tm, d), lambda i: (i, 0)), _resident((1, d)), in_hbm, in_hbm, in_hbm, _resident((1, d))],
        out_specs=pl.BlockSpec((tm, d), lambda i: (i, 0)),
        scratch_shapes=[
            pltpu.VMEM((d, f_dim), BF16), pltpu.VMEM((d, f_dim), BF16), pltpu.VMEM((f_dim, d), BF16),
            pltpu.VMEM((STAGE_SLOTS, d // WEIGHT_CHUNKS, f_dim), F32),
            pltpu.VMEM((STAGE_SLOTS, f_dim // WEIGHT_CHUNKS, d), F32),
            pltpu.SemaphoreType.DMA((STAGE_SLOTS,)),
        ],
        compiler_params=pltpu.CompilerParams(dimension_semantics=("arbitrary",), vmem_limit_bytes=VMEM_LIMIT),
        name="ffn",
    )(x2d, g.reshape(1, d), w1, w3, w2, gf.reshape(1, d))


N_SHIFT = 3 * D_MIX + 3 * LANE
N_PROJ = N_SHIFT + 3 * D_MIX + LANE
LOG2E = 1.4426950408889634
V_ROWS = HEAD + 16
FOX_KBLOCK = 512


def _in_proj_kernel(x_ref, g_ref, w_hbm, mu_ref, w0_ref, wup_ref, a0_ref, aup_ref, gup_ref, kk_ref, ka_ref,
                    rk_ref, fb_ref, qn_ref, kn_ref, e_ref,
                    r_out, lw_out, k_out, v_out, kk_out, b_out, g_out, bonus_out, fq_out, fk_out, fv_out, off_out,
                    wg_out, carry_p, carry_c, carry_b, w_ref, stage, sem, *, tm, n_sub, tiles_per_kblock, chunks):
    step = pl.program_id(1)

    @pl.when(jnp.logical_and(pl.program_id(0) == 0, step == 0))
    def _():
        w_ref[...] = jnp.zeros_like(w_ref)

        def copy(i):
            src, n = chunks[i][:2]
            slot = i % STAGE_SLOTS
            return pltpu.make_async_copy(w_hbm.at[pl.ds(src, n)], stage.at[slot, pl.ds(0, n)], sem.at[slot])

        for i in range(min(STAGE_SLOTS - 1, len(chunks))):
            copy(i).start()
        for i, (_, _, stores) in enumerate(chunks):
            if i + STAGE_SLOTS - 1 < len(chunks):
                copy(i + STAGE_SLOTS - 1).start()
            copy(i).wait()
            for at, n, to_gates, dst in stores:
                (wg_out if to_gates else w_ref)[pl.ds(dst, n), :] = stage[i % STAGE_SLOTS, at:at + n].astype(BF16)

    @pl.when(step == 0)
    def _():
        carry_p[...] = jnp.zeros_like(carry_p)
        carry_c[...] = jnp.zeros_like(carry_c)
        carry_b[...] = jnp.zeros_like(carry_b)

    outs = (r_out, lw_out, k_out, v_out, kk_out, b_out, g_out, bonus_out)
    for t in range(n_sub):
        rows = pl.ds(t * tm, tm)
        _in_proj_tile(step * n_sub + t, x_ref.at[0, rows], g_ref, w_ref, mu_ref, w0_ref, wup_ref, a0_ref, aup_ref, gup_ref,
                      kk_ref, ka_ref, rk_ref, fb_ref, qn_ref, kn_ref, e_ref, *[o.at[0, rows] for o in outs],
                      fq_out.at[0, :, :, rows], fk_out.at[0, :, rows], fv_out.at[0, :, :, rows], off_out.at[0, t],
                      carry_p, carry_c, carry_b, tm=tm, tiles_per_kblock=tiles_per_kblock)


def _in_proj_tile(i, x_ref, g_ref, w_ref, mu_ref, w0_ref, wup_ref, a0_ref, aup_ref, gup_ref, kk_ref, ka_ref,
                  rk_ref, fb_ref, qn_ref, kn_ref, e_ref,
                  r_out, lw_out, k_out, v_out, kk_out, b_out, g_out, bonus_out, fq_out, fk_out, fv_out, off_out,
                  carry_p, carry_c, carry_b, *, tm, tiles_per_kblock):
    h = _rms(x_ref[...], g_ref[...]).astype(BF16)
    p = _dot(h, w_ref[...], _NT)

    ps = p[:, :N_SHIFT]
    row = lax.broadcasted_iota(jnp.int32, (tm, 1), 0)
    prev = jnp.where(row == 0, carry_p[...], pltpu.roll(ps, 1, axis=0))
    carry_p[...] = ps[tm - 1:tm, :]
    sh = ps + (prev - ps) * mu_ref[...]

    e = e_ref[...]
    r = sh[:, 0:D_MIX]
    k = sh[:, D_MIX:2 * D_MIX]
    v = sh[:, 2 * D_MIX:3 * D_MIX]
    w_lo = sh[:, 3 * D_MIX:3 * D_MIX + LANE]
    a_lo = sh[:, 3 * D_MIX + LANE:3 * D_MIX + 2 * LANE]
    g_lo = sh[:, 3 * D_MIX + 2 * LANE:N_SHIFT]

    w_log = -_softplus(-(w0_ref[...] + _bdot(jnp.tanh(w_lo), wup_ref[...]))) - 0.5
    lw = -jnp.exp(w_log)
    a = _sigmoid(a0_ref[...] + _bdot(a_lo, aup_ref[...]))
    g = _bdot(_sigmoid(g_lo), gup_ref[...])
    kk = k * kk_ref[...]
    kk = kk * lax.rsqrt(jnp.maximum(_head_sum(kk * kk, e), 1e-24))
    k_mod = k * (1.0 + (a - 1.0) * ka_ref[...])
    bonus = _head_sum(r * k_mod * rk_ref[...], e) * v

    r_out[...] = r
    lw_out[...] = lw
    k_out[...] = k_mod
    v_out[...] = v.astype(BF16)
    kk_out[...] = kk
    b_out[...] = kk * a
    g_out[...] = g.astype(BF16)
    bonus_out[...] = bonus.astype(BF16)

    fq = p[:, N_SHIFT:N_SHIFT + D_MIX]
    fk = p[:, N_SHIFT + D_MIX:N_SHIFT + 2 * D_MIX]
    fv = p[:, N_SHIFT + 2 * D_MIX:N_SHIFT + 3 * D_MIX]
    f_lo = p[:, N_SHIFT + 3 * D_MIX:N_PROJ]
    inv_head = 1.0 / HEAD
    qn = fq * lax.rsqrt(_head_sum(fq * fq, e) * inv_head + RMS_EPS) * qn_ref[...] * (LOG2E * HEAD ** -0.5)
    kn = fk * lax.rsqrt(_head_sum(fk * fk, e) * inv_head + RMS_EPS) * kn_ref[...]
    z = f_lo + fb_ref[...]
    log_f = jnp.minimum(z, 0.0) - jnp.log(1.0 + jnp.exp(-jnp.abs(z)))
    tri = jnp.where(lax.broadcasted_iota(jnp.int32, (tm, tm), 1) <= lax.broadcasted_iota(jnp.int32, (tm, tm), 0),
                    1.0, 0.0).astype(BF16)
    c_loc = _cumsum_rows(tri, log_f)
    off_out[...] = carry_c[...] * LOG2E
    in_blk = jnp.where(i % tiles_per_kblock == 0, 0.0, carry_b[...])
    carry_c[...] = carry_c[...] + c_loc[tm - 1:tm, :]
    carry_b[...] = in_blk + c_loc[tm - 1:tm, :]
    c2 = (c_loc + in_blk) * LOG2E

    lane = lax.broadcasted_iota(jnp.int32, (tm, HEAD), 1)
    sub = lax.broadcasted_iota(jnp.int32, (HEAD, tm), 0)
    q_tail = jnp.where(sub < 3, -1.0, 0.0)
    v_tail = jnp.where(lax.broadcasted_iota(jnp.int32, (V_ROWS - HEAD, tm), 0) == 0, 1.0, 0.0)
    for pr in range(N_HEADS // 2):
        ps2 = slice(pr * 2 * HEAD, (pr + 1) * 2 * HEAD)
        q_t = qn[:, ps2].T
        v_t = fv[:, ps2].T
        for half in range(2):
            hh = 2 * pr + half
            rows = slice(half * HEAD, (half + 1) * HEAD)
            fq_out[hh] = jnp.concatenate([q_t[rows], q_tail], axis=0).astype(BF16)
            fv_out[hh] = jnp.concatenate([v_t[rows], v_tail], axis=0).astype(BF16)
            c_h = c2[:, hh:hh + 1]
            c_hi = c_h.astype(BF16).astype(F32)
            c_mid = (c_h - c_hi).astype(BF16).astype(F32)
            c_lo = c_h - c_hi - c_mid
            c_cols = jnp.where(lane == 0, c_hi, jnp.where(lane == 1, c_mid, jnp.where(lane == 2, c_lo, 0.0)))
            fk_out[hh] = jnp.concatenate([kn[:, hh * HEAD:(hh + 1) * HEAD], c_cols], axis=1).astype(BF16)


def _pad_cols(w, n):
    return jnp.pad(w, ((0, 0), (0, n - w.shape[1])))


def _pad_rows(w, n):
    return jnp.pad(w, ((0, n - w.shape[0]), (0, 0)))


def _in_proj(x, mix_norm, w_in, shift_mu, w0, w_up, a0, a_up, g_up, k_k, k_a, r_k, f_bias, q_norm, k_norm, *, tm=256, n_sub=2):
    bsz, s, d = x.shape
    lw_, la_, lg_ = w_up.shape[0], a_up.shape[0], g_up.shape[0]
    c0 = 3 * D_MIX
    c1 = c0 + lw_ + la_ + lg_
    c2 = c1 + 3 * D_MIX
    c3 = c2 + N_HEADS

    def seg(t, pad):
        return jnp.concatenate([
            t[..., :c0],
            pad(t[..., c0:c0 + lw_], LANE), pad(t[..., c0 + lw_:c0 + lw_ + la_], LANE), pad(t[..., c0 + lw_ + la_:c1], LANE),
        ], axis=-1)

    n_cols = w_in.shape[1]
    groups = ((0, c0, False, 0), (c0, lw_, False, c0), (c0 + lw_, la_, False, c0 + LANE),
              (c0 + lw_ + la_, lg_, False, c0 + 2 * LANE), (c1, c2 - c1, False, N_SHIFT),
              (c2, c3 - c2, False, N_SHIFT + 3 * D_MIX), (c3, n_cols - c3, True, 0))
    pieces = [(src + o, min(PIECE_ROWS, width - o), to_gates, dst + o)
              for src, width, to_gates, dst in groups for o in range(0, width, PIECE_ROWS)]
    chunks = []
    for src, n, to_gates, dst in pieces:
        if chunks and chunks[-1][0] + chunks[-1][1] == src and chunks[-1][1] + n <= STAGE_ROWS:
            c_src, c_n, stores = chunks[-1]
            chunks[-1] = (c_src, c_n + n, stores + ((c_n, n, to_gates, dst),))
        else:
            chunks.append((src, n, ((0, n, to_gates, dst),)))
    chunks = tuple(chunks)
    mu_cat = seg(shift_mu.reshape(1, -1), _pad_cols)
    row = lambda t: t.reshape(1, -1).astype(F32)
    e = _head_mask()
    tile = lambda t: jnp.tile(t.reshape(1, HEAD), (1, N_HEADS)).astype(F32)
    consts = [
        row(mix_norm), jnp.swapaxes(w_in, 0, 1), mu_cat, row(w0), _pad_rows(w_up, LANE).astype(BF16), row(a0),
        _pad_rows(a_up, LANE).astype(BF16), _pad_rows(g_up, LANE).astype(BF16), row(k_k), row(k_a), row(r_k),
        _pad_cols(row(f_bias), LANE), tile(q_norm), tile(k_norm), e,
    ]
    const_specs = [_resident(c.shape) for c in consts]
    const_specs[1] = pl.BlockSpec(memory_space=pl.ANY)
    assert n_cols - c3 == 2 * d
    tm, tile_rows = tm * n_sub, tm
    tok = lambda: pl.BlockSpec((1, tm, D_MIX), lambda b, i: (b, i, 0))
    f32_tok = jax.ShapeDtypeStruct((bsz, s, D_MIX), F32)
    bf_tok = jax.ShapeDtypeStruct((bsz, s, D_MIX), BF16)
    assert s % tm == 0 and FOX_KBLOCK % tile_rows == 0
    return pl.pallas_call(
        functools.partial(_in_proj_kernel, tm=tile_rows, n_sub=n_sub, tiles_per_kblock=FOX_KBLOCK // tile_rows,
                          chunks=chunks),
        out_shape=[f32_tok, f32_tok, f32_tok, bf_tok, f32_tok, f32_tok, bf_tok, bf_tok] + [
            jax.ShapeDtypeStruct((bsz, N_HEADS, 2 * HEAD, s), BF16),
            jax.ShapeDtypeStruct((bsz, N_HEADS, s, 2 * HEAD), BF16),
            jax.ShapeDtypeStruct((bsz, N_HEADS, V_ROWS, s), BF16),
            jax.ShapeDtypeStruct((bsz, s // tile_rows, 1, LANE), F32),
            jax.ShapeDtypeStruct((2 * d, d), BF16),
        ],
        grid=(bsz, s // tm),
        in_specs=[pl.BlockSpec((1, tm, d), lambda b, i: (b, i, 0))] + const_specs,
        out_specs=[tok() for _ in range(8)] + [
            pl.BlockSpec((1, N_HEADS, 2 * HEAD, tm), lambda b, i: (b, 0, 0, i)),
            pl.BlockSpec((1, N_HEADS, tm, 2 * HEAD), lambda b, i: (b, 0, i, 0)),
            pl.BlockSpec((1, N_HEADS, V_ROWS, tm), lambda b, i: (b, 0, 0, i)),
            pl.BlockSpec((1, n_sub, 1, LANE), lambda b, i: (b, i, 0, 0)),
            pl.BlockSpec((2 * d, d), lambda b, i: (0, 0)),
        ],
        scratch_shapes=[pltpu.VMEM((1, N_SHIFT), F32), pltpu.VMEM((1, LANE), F32), pltpu.VMEM((1, LANE), F32),
                        pltpu.VMEM((N_PROJ, d), BF16), pltpu.VMEM((STAGE_SLOTS, STAGE_ROWS, d), F32),
                        pltpu.SemaphoreType.DMA((STAGE_SLOTS,))],
        compiler_params=pltpu.CompilerParams(
            dimension_semantics=("arbitrary", "arbitrary"), vmem_limit_bytes=VMEM_LIMIT),
        name="in_proj",
    )(x, *consts)


def _rwkv_kernel(r_ref, lw_ref, k_ref, v_ref, kk_ref, b_ref, g_ref, bonus_ref, lng_ref, lnb_ref, e_ref, y_ref, s_scr, *, tb):
    i = pl.program_id(1)

    @pl.when(i == 0)
    def _():
        s_scr[...] = jnp.zeros_like(s_scr)

    c = CHUNK
    pw = 2 * HEAD
    n_chunk = tb // c
    n_pair = N_HEADS // 2
    rr = lax.broadcasted_iota(jnp.int32, (c, pw), 0)
    ll = lax.broadcasted_iota(jnp.int32, (c, pw), 1)
    cc = ll % c
    lo = ll < c
    strict = cc < rr
    incl = cc <= rr
    eye = (cc == rr).astype(F32)
    r2 = lax.broadcasted_iota(jnp.int32, (2 * c, pw), 0)
    l2 = lax.broadcasted_iota(jnp.int32, (2 * c, pw), 1)
    lo2 = l2 < c
    diag2 = (r2 < c) == lo2
    zeros = jnp.zeros((c, pw), F32)

    def level_mask(n):
        return ((rr // (2 * n)) == (cc // (2 * n))) & (((rr // n) % 2) == 1) & (((cc // n) % 2) == 0)

    def bd(x):
        return jnp.where(diag2, jnp.concatenate([x, x], axis=0), 0.0).astype(BF16)

    def bd_sw(x):
        return jnp.where(diag2, 0.0, jnp.concatenate([x, x], axis=0)).astype(BF16)

    slab = min(tb, MXU_DIM)
    rb = lax.broadcasted_iota(jnp.int32, (slab, slab), 0)
    cb = lax.broadcasted_iota(jnp.int32, (slab, slab), 1)
    tri = jnp.where((cb <= rb) & ((rb // c) == (cb // c)), 1.0, 0.0).astype(BF16)
    lw = lw_ref[0]
    cs = jnp.concatenate([_cumsum_rows(tri, lw[j:j + slab]) for j in range(0, tb, slab)], axis=0)
    w_inv = jnp.exp(-cs)
    rt = r_ref[0] * jnp.exp(cs)
    at = -kk_ref[0] * jnp.exp(cs - lw)
    bi = b_ref[0] * w_inv
    ki = k_ref[0] * w_inv
    v = v_ref[0].astype(F32)

    chains = [(j, p) for j in range(n_chunk) for p in range(n_pair)]
    cut = lambda x, j, p: x[j * c:(j + 1) * c, p * pw:(p + 1) * pw]
    w_tot = [jnp.exp(cs[(j + 1) * c - 1:(j + 1) * c, :]) for j in range(n_chunk)]
    a_t = [cut(at, j, p) for j, p in chains]
    r_t = [cut(rt, j, p) for j, p in chains]
    b_i = [cut(bi, j, p) for j, p in chains]
    k_i = [cut(ki, j, p) for j, p in chains]
    v_p = [cut(v, j, p) for j, p in chains]
    n = len(chains)

    a_ab, a_ak, a_rb, a_rk = [], [], [], []
    for q in range(n):
        ar = jnp.concatenate([a_t[q], r_t[q]], axis=0)
        gx = _dot(jnp.where(lo2, ar, 0.0).astype(BF16), jnp.concatenate([b_i[q], k_i[q]], axis=0).astype(BF16), _NT)
        gy = _dot(jnp.where(lo2, 0.0, ar).astype(BF16), jnp.concatenate([k_i[q], b_i[q]], axis=0).astype(BF16), _NT)
        a_ab.append(jnp.where(strict, jnp.where(lo, gx[:c], gy[:c]), 0.0))
        a_ak.append(jnp.where(strict, jnp.where(lo, gy[:c], gx[:c]), 0.0))
        a_rb.append(jnp.where(incl, jnp.where(lo, gx[c:], gy[c:]), 0.0))
        a_rk.append(jnp.where(incl, jnp.where(lo, gy[c:], gx[c:]), 0.0))

    t = [eye + jnp.where(level_mask(1), a_ab[q], 0.0) for q in range(n)]
    for m in (2, 4, 8, 16, 32):
        x = [_dot(jnp.where(level_mask(m), a_ab[q], 0.0).astype(BF16), bd(t[q])) for q in range(n)]
        t = [t[q] + _dot(t[q].astype(BF16), bd(x[q])) for q in range(n)]

    v_sw = [bd_sw(v_p[q]) for q in range(n)]
    av = [_dot(jnp.concatenate([a_ak[q], a_rk[q]], axis=0).astype(BF16), v_sw[q]) for q in range(n)]
    x = [_dot(t[q].astype(BF16), jnp.concatenate([bd(a_t[q]), bd(av[q][:c])], axis=1)) for q in range(n)]
    a_p = [x[q][:, :pw] for q in range(n)]
    u_0 = [x[q][:, pw:] for q in range(n)]
    d1 = [_dot(a_rb[q].astype(BF16), jnp.concatenate([bd(a_p[q]), bd(u_0[q])], axis=1)) for q in range(n)]
    rp = [r_t[q] + d1[q][:, :pw] for q in range(n)]
    y_0 = [d1[q][:, pw:] + av[q][c:] for q in range(n)]
    d2 = []
    for q, (j, p) in enumerate(chains):
        wt = w_tot[j][:, p * pw:(p + 1) * pw]
        lhs = jnp.concatenate([jnp.concatenate([a_p[q], u_0[q]], axis=1),
                               jnp.concatenate([zeros, v_p[q]], axis=1)], axis=0).astype(BF16)
        rhs = jnp.concatenate([b_i[q] * wt, k_i[q] * wt], axis=0).astype(BF16)
        d2.append(_dot(lhs, rhs, _TN))

    s = [s_scr[p] for p in range(n_pair)]
    ys = [None] * n
    for j in range(n_chunk):
        for p in range(n_pair):
            q = j * n_pair + p
            mt = jnp.where(diag2, d2[q][:pw], 0.0).astype(BF16)
            nt = jnp.where(lo, d2[q][pw:pw + c], d2[q][pw + c:])
            ys[q] = y_0[q] + _dot(rp[q].astype(BF16), bd(s[p]), _NT)
            s[p] = s[p] * w_tot[j][:, p * pw:(p + 1) * pw] + _dot(s[p].astype(BF16), mt) + nt
    for p in range(n_pair):
        s_scr[p] = s[p]

    y = jnp.concatenate([jnp.concatenate([ys[j * n_pair + p] for p in range(n_pair)], axis=1) for j in range(n_chunk)], axis=0)
    e = e_ref[...]
    inv_head = 1.0 / HEAD
    yc = y - _head_sum(y, e) * inv_head
    yn = yc * lax.rsqrt(_head_sum(yc * yc, e) * inv_head + LNX_EPS)
    y_ref[0] = ((yn * lng_ref[...] + lnb_ref[...] + bonus_ref[0]) * g_ref[0]).astype(BF16)


def _head_mask():
    head_id = jnp.arange(MXU_DIM) // HEAD
    return (head_id[:, None] == head_id[None, :]).astype(BF16)


def _rwkv(r, lw, k, v, kk, b, g, bonus, lnx_g, lnx_b, *, tb=512):
    bsz, s, _ = r.shape
    assert s % tb == 0 and tb % CHUNK == 0
    tok = pl.BlockSpec((1, tb, D_MIX), lambda bb, i: (bb, i, 0))
    vec = pl.BlockSpec((1, D_MIX), lambda bb, i: (0, 0))
    return pl.pallas_call(
        functools.partial(_rwkv_kernel, tb=tb),
        out_shape=jax.ShapeDtypeStruct((bsz, s, D_MIX), BF16),
        grid=(bsz, s // tb),
        in_specs=[tok] * 8 + [vec, vec, pl.BlockSpec((MXU_DIM, MXU_DIM), lambda bb, i: (0, 0))],
        out_specs=tok,
        scratch_shapes=[pltpu.VMEM((N_HEADS // 2, HEAD, 2 * HEAD), F32)],
        compiler_params=pltpu.CompilerParams(
            dimension_semantics=("parallel", "arbitrary"), vmem_limit_bytes=VMEM_LIMIT),
        name="rwkv",
    )(r, lw, k, v, kk, b, g, bonus, lnx_g.reshape(1, -1).astype(F32), lnx_b.reshape(1, -1).astype(F32), _head_mask())


def _fox_kernel(off_ref, thr_ref, q_ref, k_ref, v_ref, o_ref, *scratch, tq, tk, qw, n_kblk):
    b, hp, qb = pl.program_id(0), pl.program_id(1), pl.program_id(2)
    kb_last = (qb * tq) // tk
    delta = qb * tq - kb_last * tk
    chains = [(hh, g) for hh in range(2) for g in range(tq // qw)]
    n_c = len(chains)
    s_scr = (scratch[:n_c], scratch[n_c:2 * n_c])
    m_scr, acc_scr = scratch[2 * n_c:]
    base = [(b * N_HEADS + 2 * hp + hh) * n_kblk for hh in range(2)]
    off_d = [off_ref[base[hh] + kb_last] for hh in range(2)]

    thr = thr_ref[0]

    def needed(kb):
        nxt = jnp.minimum(kb + 1, kb_last)
        return jnp.minimum(off_ref[base[0] + nxt] - off_d[0], off_ref[base[1] + nxt] - off_d[1]) <= thr

    kb_first = lax.while_loop(lambda kb: jnp.logical_and(kb > 0, needed(kb - 1)), lambda kb: kb - 1, kb_last)
    n_open = kb_last - kb_first

    def qk(kb, slot):
        ksl = pl.ds(pl.multiple_of(kb * tk, tk), tk)
        for c, (hh, g) in enumerate(chains):
            s_scr[slot][c][...] = _dot(k_ref[0, hh, ksl, :], q_ref[0, hh, :, g * qw:(g + 1) * qw])

    def soft_pv(kb, slot, masked):
        ksl = pl.ds(pl.multiple_of(kb * tk, tk), tk)
        for c, (hh, g) in enumerate(chains):
            s = s_scr[slot][c][...]
            if masked:
                key = lax.broadcasted_iota(jnp.int32, (tk, qw), 0)
                qry = lax.broadcasted_iota(jnp.int32, (tk, qw), 1)
                s = jnp.where(key - qry <= delta + g * qw, s, NEG_BIG)
            off = off_ref[base[hh] + kb] - off_d[hh]
            m = m_scr[c]
            m_new = jnp.maximum(m, jnp.max(s, axis=0, keepdims=True) - off)
            p = jnp.exp2(s - (m_new + off)).astype(BF16)
            acc_scr[c] = jnp.exp2(m - m_new) * acc_scr[c] + _dot(v_ref[0, hh, :, ksl], p)
            m_scr[c] = m_new

    m_scr[...] = jnp.full(m_scr.shape, NEG_BIG, F32)
    acc_scr[...] = jnp.zeros(acc_scr.shape, F32)
    qk(kb_first, 0)

    def pair(i, _):
        kb = kb_first + 2 * i
        qk(kb + 1, 1)
        soft_pv(kb, 0, False)
        qk(kb + 2, 0)
        soft_pv(kb + 1, 1, False)
        return 0

    lax.fori_loop(0, n_open // 2, pair, 0)

    @pl.when(n_open % 2 == 1)
    def _():
        qk(kb_last, 1)
        soft_pv(kb_last - 1, 0, False)
        soft_pv(kb_last, 1, True)

    @pl.when(n_open % 2 == 0)
    def _():
        soft_pv(kb_last, 0, True)

    acc = [acc_scr[c] for c in range(n_c)]
    o_t = jnp.concatenate([jnp.concatenate([acc[c][:HEAD] / acc[c][HEAD:HEAD + 1] for c, (h2, _) in enumerate(chains)
                                            if h2 == hh], axis=1) for hh in range(2)], axis=0)
    o_ref[0] = o_t.T.astype(BF16)


def _fox(q_t, k_aug, v_t, off, q_norm, k_norm, *, tq=512, tk=FOX_KBLOCK, qw=MXU_DIM):
    bsz, nh, s, _ = k_aug.shape
    n_kblk = s // tk
    n_chain = 2 * (tq // qw)
    assert tq % qw == 0
    assert s % tq == 0 and s % tk == 0 and tk % tq == 0 and nh % 2 == 0 and off.shape == (bsz * nh * n_kblk,)
    qk_bound = 1.02 * LOG2E * HEAD ** 0.5 * jnp.max(jnp.abs(q_norm)) * jnp.max(jnp.abs(k_norm))
    thr = (2.0 * qk_bound + EXP2_UNDERFLOW).reshape(1).astype(F32)
    return pl.pallas_call(
        functools.partial(_fox_kernel, tq=tq, tk=tk, qw=qw, n_kblk=n_kblk),
        out_shape=jax.ShapeDtypeStruct((bsz, s, nh * HEAD), BF16),
        grid=(bsz, nh // 2, s // tq),
        in_specs=[
            pl.BlockSpec(memory_space=pltpu.SMEM),
            pl.BlockSpec(memory_space=pltpu.SMEM),
            pl.BlockSpec((1, 2, 2 * HEAD, tq), lambda b, hp, i: (b, hp, 0, i)),
            pl.BlockSpec((1, 2, s, 2 * HEAD), lambda b, hp, i: (b, hp, 0, 0)),
            pl.BlockSpec((1, 2, V_ROWS, s), lambda b, hp, i: (b, hp, 0, 0)),
        ],
        out_specs=pl.BlockSpec((1, tq, 2 * HEAD), lambda b, hp, i: (b, i, hp)),
        scratch_shapes=[pltpu.VMEM((tk, qw), F32)] * (2 * n_chain)
        + [pltpu.VMEM((n_chain, 1, qw), F32), pltpu.VMEM((n_chain, V_ROWS, qw), F32)],
        compiler_params=pltpu.CompilerParams(
            dimension_semantics=("parallel", "parallel", "arbitrary"), vmem_limit_bytes=VMEM_LIMIT),
        name="fox",
    )(off, thr, q_t, k_aug, v_t)


def _merge_kernel(x_ref, g_ref, wg_ref, ya_ref, yb_ref, pa_ref, pb_ref, wo_ref, o_ref):
    x = x_ref[...]
    d = x.shape[-1]
    h = _rms(x, g_ref[...]).astype(BF16)
    gates = _sigmoid(_dot(h, wg_ref[...], _NT))
    ya = _dot(ya_ref[...], pa_ref[...])
    yb = _dot(yb_ref[...], pb_ref[...])
    mix = (gates[:, :d] * ya + gates[:, d:] * yb).astype(BF16)
    o_ref[...] = x + _dot(mix, wo_ref[...])


def _merge(x2d, mix_norm, w_gates, ya, yb, p_a, p_b, w_out, *, tm=512):
    m, d = x2d.shape
    assert m % tm == 0
    full = lambda t: pl.BlockSpec(t.shape, lambda i: (0, 0))
    tok = lambda n: pl.BlockSpec((tm, n), lambda i: (i, 0))
    consts = [mix_norm.reshape(1, d).astype(F32), w_gates.astype(BF16)]
    mats = [p_a.astype(BF16), p_b.astype(BF16), w_out.astype(BF16)]
    return pl.pallas_call(
        _merge_kernel,
        out_shape=jax.ShapeDtypeStruct((m, d), F32),
        grid=(m // tm,),
        in_specs=[tok(d), full(consts[0]), full(consts[1]), tok(D_MIX), tok(D_MIX)] + [full(t) for t in mats],
        out_specs=tok(d),
        compiler_params=pltpu.CompilerParams(dimension_semantics=("parallel",), vmem_limit_bytes=VMEM_LIMIT),
        name="merge",
    )(x2d, consts[0], consts[1], ya, yb, *mats)


def kernel(x, ffn1_norm, ffn1_w1, ffn1_w3, ffn1_w2, mix_norm, w_in, shift_mu, rwkv_w0, rwkv_w_up, rwkv_a0, rwkv_a_up, rwkv_g_up, rwkv_k_k, rwkv_k_a, rwkv_r_k, rwkv_lnx_g, rwkv_lnx_b, rwkv_proj, fox_f_bias, fox_q_norm, fox_k_norm, fox_proj, w_out, ffn2_norm, ffn2_w1, ffn2_w3, ffn2_w2, final_norm):
    bsz, s, d = x.shape
    depth = ffn1_norm.shape[0]
    x2d = x.reshape(bsz * s, d)
    for l in range(depth):
        x2d = _ffn(x2d, ffn1_norm[l], ffn1_w1[l], ffn1_w3[l], ffn1_w2[l], final_norm, final_norm=False)
        (r, lw, k, v, kk, b, g, bonus, fq_t, fk_aug, fv_t, off, w_gates) = _in_proj(
            x2d.reshape(bsz, s, d), mix_norm[l], w_in[l], shift_mu[l], rwkv_w0[l], rwkv_w_up[l], rwkv_a0[l],
            rwkv_a_up[l], rwkv_g_up[l], rwkv_k_k[l], rwkv_k_a[l], rwkv_r_k[l], fox_f_bias[l], fox_q_norm[l],
            fox_k_norm[l])
        ya = _rwkv(r, lw, k, v, kk, b, g, bonus, rwkv_lnx_g[l], rwkv_lnx_b[l])
        off = off[:, ::FOX_KBLOCK // (s // off.shape[1]), 0, :N_HEADS]
        off = jnp.transpose(off, (0, 2, 1)).reshape(-1)
        yb = _fox(fq_t, fk_aug, fv_t, off, fox_q_norm[l], fox_k_norm[l])
        x2d = _merge(x2d, mix_norm[l], w_gates, ya.reshape(bsz * s, D_MIX), yb.reshape(bsz * s, D_MIX),
                     rwkv_proj[l], fox_proj[l], w_out[l])
        x2d = _ffn(x2d, ffn2_norm[l], ffn2_w1[l], ffn2_w3[l], ffn2_w2[l], final_norm, final_norm=(l == depth - 1))
    return x2d.reshape(bsz, s, d)
```

```python
import functools

import jax
import jax.numpy as jnp
from jax import lax
from jax.experimental import pallas as pl
from jax.experimental.pallas import tpu as pltpu

F32 = jnp.float32
BF16 = jnp.bfloat16

HEAD = 64
N_HEADS = 8
D_MIX = HEAD * N_HEADS
LANE = 128
MXU_DIM = 256
CHUNK = 64
RMS_EPS = 1e-6
LNX_EPS = 64e-5
NEG_BIG = -1e30
EXP2_UNDERFLOW = 160.0
VMEM_LIMIT = 56 * 1024 * 1024

_NT = (((1,), (1,)), ((), ()))
_TN = (((0,), (0,)), ((), ()))


def _dot(a, b, dims=None):
    if dims is None:
        return jnp.dot(a, b, preferred_element_type=F32)
    return lax.dot_general(a, b, dims, preferred_element_type=F32)


def _bdot(a, b, dims=None):
    return _dot(a.astype(BF16), b.astype(BF16), dims)


def _rms(x, g):
    return x * lax.rsqrt(jnp.mean(x * x, axis=-1, keepdims=True) + RMS_EPS) * g


def _softplus(z):
    return jnp.maximum(z, 0.0) + jnp.log(1.0 + jnp.exp(-jnp.abs(z)))


def _sigmoid(z):
    return 1.0 / (1.0 + jnp.exp(-z))


def _head_sum(x, e):
    xb = x.astype(BF16)
    return jnp.concatenate([_dot(xb[:, j:j + MXU_DIM], e) for j in range(0, x.shape[1], MXU_DIM)], axis=1)


def _cumsum_rows(tri, x):
    hi = x.astype(BF16)
    r1 = x - hi.astype(F32)
    mid = r1.astype(BF16)
    lo = (r1 - mid.astype(F32)).astype(BF16)
    return _dot(tri, hi) + _dot(tri, mid) + _dot(tri, lo)


WEIGHT_CHUNKS = 8
PIECE_ROWS = 512
STAGE_ROWS = 640
STAGE_SLOTS = 3


def _load_cast(w_hbm, w_ref, stage, sem):
    rows = w_hbm.shape[0] // WEIGHT_CHUNKS

    def copy(c):
        slot = c % STAGE_SLOTS
        return pltpu.make_async_copy(w_hbm.at[pl.ds(c * rows, rows)], stage.at[slot], sem.at[slot])

    for c in range(STAGE_SLOTS - 1):
        copy(c).start()
    for c in range(WEIGHT_CHUNKS):
        if c + STAGE_SLOTS - 1 < WEIGHT_CHUNKS:
            copy(c + STAGE_SLOTS - 1).start()
        copy(c).wait()
        w_ref[pl.ds(c * rows, rows), :] = stage[c % STAGE_SLOTS].astype(BF16)


def _ffn_kernel(x_ref, g_ref, w1_hbm, w3_hbm, w2_hbm, gf_ref, o_ref, w1_ref, w3_ref, w2_ref, stage_up, stage_down, sem,
                *, final_norm):
    @pl.when(pl.program_id(0) == 0)
    def _():
        _load_cast(w1_hbm, w1_ref, stage_up, sem)
        _load_cast(w3_hbm, w3_ref, stage_up, sem)
        _load_cast(w2_hbm, w2_ref, stage_down, sem)

    x = x_ref[...]
    h = _rms(x, g_ref[...]).astype(BF16)
    a = _dot(h, w1_ref[...])
    b = _dot(h, w3_ref[...])
    u = (a * _sigmoid(a) * b).astype(BF16)
    y = x + 0.5 * _dot(u, w2_ref[...])
    if final_norm:
        y = _rms(y, gf_ref[...])
    o_ref[...] = y


def _resident(shape):
    return pl.BlockSpec(shape, lambda *_: (0,) * len(shape), pipeline_mode=pl.Buffered(1))


def _ffn(x2d, g, w1, w3, w2, gf, *, final_norm, tm=512):
    m, d = x2d.shape
    f_dim = w1.shape[1]
    assert m % tm == 0 and d % WEIGHT_CHUNKS == 0 and f_dim % WEIGHT_CHUNKS == 0
    in_hbm = pl.BlockSpec(memory_space=pl.ANY)
    return pl.pallas_call(
        functools.partial(_ffn_kernel, final_norm=final_norm),
        out_shape=jax.ShapeDtypeStruct((m, d), F32),
        grid=(m // tm,),
        in_specs=[pl.BlockSpec((tm, d), lambda i: (i, 0)), _resident((1, d)), in_hbm, in_hbm, in_hbm, _resident((1, d))],
        out_specs=pl.BlockSpec((tm, d), lambda i: (i, 0)),
        scratch_shapes=[
            pltpu.VMEM((d, f_dim), BF16), pltpu.VMEM((d, f_dim), BF16), pltpu.VMEM((f_dim, d), BF16),
            pltpu.VMEM((STAGE_SLOTS, d // WEIGHT_CHUNKS, f_dim), F32),
            pltpu.VMEM((STAGE_SLOTS, f_dim // WEIGHT_CHUNKS, d), F32),
            pltpu.SemaphoreType.DMA((STAGE_SLOTS,)),
        ],
        compiler_params=pltpu.CompilerParams(dimension_semantics=("arbitrary",), vmem_limit_bytes=VMEM_LIMIT),
        name="ffn",
    )(x2d, g.reshape(1, d), w1, w3, w2, gf.reshape(1, d))


N_SHIFT = 3 * D_MIX + 3 * LANE
N_PROJ = N_SHIFT + 3 * D_MIX + LANE
LOG2E = 1.4426950408889634
V_ROWS = HEAD + 16
FOX_KBLOCK = 256


def _in_proj_kernel(x_ref, g_ref, w_hbm, mu_ref, w0_ref, wup_ref, a0_ref, aup_ref, gup_ref, kk_ref, ka_ref,
                    rk_ref, fb_ref, qn_ref, kn_ref, e_ref,
                    r_out, lw_out, k_out, v_out, kk_out, b_out, g_out, bonus_out, fq_out, fk_out, fv_out, off_out,
                    wg_out, carry_p, carry_c, carry_b, w_ref, stage, sem, *, tm, n_sub, tiles_per_kblock, chunks):
    step = pl.program_id(1)

    @pl.when(jnp.logical_and(pl.program_id(0) == 0, step == 0))
    def _():
        w_ref[...] = jnp.zeros_like(w_ref)

        def copy(i):
            src, n = chunks[i][:2]
            slot = i % STAGE_SLOTS
            return pltpu.make_async_copy(w_hbm.at[pl.ds(src, n)], stage.at[slot, pl.ds(0, n)], sem.at[slot])

        for i in range(min(STAGE_SLOTS - 1, len(chunks))):
            copy(i).start()
        for i, (_, _, stores) in enumerate(chunks):
            if i + STAGE_SLOTS - 1 < len(chunks):
                copy(i + STAGE_SLOTS - 1).start()
            copy(i).wait()
            for at, n, to_gates, dst in stores:
                (wg_out if to_gates else w_ref)[pl.ds(dst, n), :] = stage[i % STAGE_SLOTS, at:at + n].astype(BF16)

    @pl.when(step == 0)
    def _():
        carry_p[...] = jnp.zeros_like(carry_p)
        carry_c[...] = jnp.zeros_like(carry_c)
        carry_b[...] = jnp.zeros_like(carry_b)

    outs = (r_out, lw_out, k_out, v_out, kk_out, b_out, g_out, bonus_out)
    for t in range(n_sub):
        rows = pl.ds(t * tm, tm)
        _in_proj_tile(step * n_sub + t, x_ref.at[0, rows], g_ref, w_ref, mu_ref, w0_ref, wup_ref, a0_ref, aup_ref, gup_ref,
                      kk_ref, ka_ref, rk_ref, fb_ref, qn_ref, kn_ref, e_ref, *[o.at[0, rows] for o in outs],
                      fq_out.at[0, :, :, rows], fk_out.at[0, :, rows], fv_out.at[0, :, :, rows], off_out.at[0, t],
                      carry_p, carry_c, carry_b, tm=tm, tiles_per_kblock=tiles_per_kblock)


def _in_proj_tile(i, x_ref, g_ref, w_ref, mu_ref, w0_ref, wup_ref, a0_ref, aup_ref, gup_ref, kk_ref, ka_ref,
                  rk_ref, fb_ref, qn_ref, kn_ref, e_ref,
                  r_out, lw_out, k_out, v_out, kk_out, b_out, g_out, bonus_out, fq_out, fk_out, fv_out, off_out,
                  carry_p, carry_c, carry_b, *, tm, tiles_per_kblock):
    h = _rms(x_ref[...], g_ref[...]).astype(BF16)
    p = _dot(h, w_ref[...], _NT)

    ps = p[:, :N_SHIFT]
    row = lax.broadcasted_iota(jnp.int32, (tm, 1), 0)
    prev = jnp.where(row == 0, carry_p[...], pltpu.roll(ps, 1, axis=0))
    carry_p[...] = ps[tm - 1:tm, :]
    sh = ps + (prev - ps) * mu_ref[...]

    e = e_ref[...]
    r = sh[:, 0:D_MIX]
    k = sh[:, D_MIX:2 * D_MIX]
    v = sh[:, 2 * D_MIX:3 * D_MIX]
    w_lo = sh[:, 3 * D_MIX:3 * D_MIX + LANE]
    a_lo = sh[:, 3 * D_MIX + LANE:3 * D_MIX + 2 * LANE]
    g_lo = sh[:, 3 * D_MIX + 2 * LANE:N_SHIFT]

    w_log = -_softplus(-(w0_ref[...] + _bdot(jnp.tanh(w_lo), wup_ref[...]))) - 0.5
    lw = -jnp.exp(w_log)
    a = _sigmoid(a0_ref[...] + _bdot(a_lo, aup_ref[...]))
    g = _bdot(_sigmoid(g_lo), gup_ref[...])
    kk = k * kk_ref[...]
    kk = kk * lax.rsqrt(jnp.maximum(_head_sum(kk * kk, e), 1e-24))
    k_mod = k * (1.0 + (a - 1.0) * ka_ref[...])
    bonus = _head_sum(r * k_mod * rk_ref[...], e) * v

    r_out[...] = r
    lw_out[...] = lw
    k_out[...] = k_mod
    v_out[...] = v.astype(BF16)
    kk_out[...] = kk
    b_out[...] = kk * a
    g_out[...] = g.astype(BF16)
    bonus_out[...] = bonus.astype(BF16)

    fq = p[:, N_SHIFT:N_SHIFT + D_MIX]
    fk = p[:, N_SHIFT + D_MIX:N_SHIFT + 2 * D_MIX]
    fv = p[:, N_SHIFT + 2 * D_MIX:N_SHIFT + 3 * D_MIX]
    f_lo = p[:, N_SHIFT + 3 * D_MIX:N_PROJ]
    inv_head = 1.0 / HEAD
    qn = fq * lax.rsqrt(_head_sum(fq * fq, e) * inv_head + RMS_EPS) * qn_ref[...] * (LOG2E * HEAD ** -0.5)
    kn = fk * lax.rsqrt(_head_sum(fk * fk, e) * inv_head + RMS_EPS) * kn_ref[...]
    z = f_lo + fb_ref[...]
    log_f = jnp.minimum(z, 0.0) - jnp.log(1.0 + jnp.exp(-jnp.abs(z)))
    tri = jnp.where(lax.broadcasted_iota(jnp.int32, (tm, tm), 1) <= lax.broadcasted_iota(jnp.int32, (tm, tm), 0),
                    1.0, 0.0).astype(BF16)
    c_loc = _cumsum_rows(tri, log_f)
    off_out[...] = carry_c[...] * LOG2E
    in_blk = jnp.where(i % tiles_per_kblock == 0, 0.0, carry_b[...])
    carry_c[...] = carry_c[...] + c_loc[tm - 1:tm, :]
    carry_b[...] = in_blk + c_loc[tm - 1:tm, :]
    c2 = (c_loc + in_blk) * LOG2E

    lane = lax.broadcasted_iota(jnp.int32, (tm, HEAD), 1)
    sub = lax.broadcasted_iota(jnp.int32, (HEAD, tm), 0)
    q_tail = jnp.where(sub < 3, -1.0, 0.0)
    v_tail = jnp.where(lax.broadcasted_iota(jnp.int32, (V_ROWS - HEAD, tm), 0) == 0, 1.0, 0.0)
    for pr in range(N_HEADS // 2):
        ps2 = slice(pr * 2 * HEAD, (pr + 1) * 2 * HEAD)
        q_t = qn[:, ps2].T
        v_t = fv[:, ps2].T
        for half in range(2):
            hh = 2 * pr + half
            rows = slice(half * HEAD, (half + 1) * HEAD)
            fq_out[hh] = jnp.concatenate([q_t[rows], q_tail], axis=0).astype(BF16)
            fv_out[hh] = jnp.concatenate([v_t[rows], v_tail], axis=0).astype(BF16)
            c_h = c2[:, hh:hh + 1]
            c_hi = c_h.astype(BF16).astype(F32)
            c_mid = (c_h - c_hi).astype(BF16).astype(F32)
            c_lo = c_h - c_hi - c_mid
            c_cols = jnp.where(lane == 0, c_hi, jnp.where(lane == 1, c_mid, jnp.where(lane == 2, c_lo, 0.0)))
            fk_out[hh] = jnp.concatenate([kn[:, hh * HEAD:(hh + 1) * HEAD], c_cols], axis=1).astype(BF16)


def _pad_cols(w, n):
    return jnp.pad(w, ((0, 0), (0, n - w.shape[1])))


def _pad_rows(w, n):
    return jnp.pad(w, ((0, n - w.shape[0]), (0, 0)))


def _in_proj(x, mix_norm, w_in, shift_mu, w0, w_up, a0, a_up, g_up, k_k, k_a, r_k, f_bias, q_norm, k_norm, *, tm=256, n_sub=2):
    bsz, s, d = x.shape
    lw_, la_, lg_ = w_up.shape[0], a_up.shape[0], g_up.shape[0]
    c0 = 3 * D_MIX
    c1 = c0 + lw_ + la_ + lg_
    c2 = c1 + 3 * D_MIX
    c3 = c2 + N_HEADS

    def seg(t, pad):
        return jnp.concatenate([
            t[..., :c0],
            pad(t[..., c0:c0 + lw_], LANE), pad(t[..., c0 + lw_:c0 + lw_ + la_], LANE), pad(t[..., c0 + lw_ + la_:c1], LANE),
        ], axis=-1)

    n_cols = w_in.shape[1]
    groups = ((0, c0, False, 0), (c0, lw_, False, c0), (c0 + lw_, la_, False, c0 + LANE),
              (c0 + lw_ + la_, lg_, False, c0 + 2 * LANE), (c1, c2 - c1, False, N_SHIFT),
              (c2, c3 - c2, False, N_SHIFT + 3 * D_MIX), (c3, n_cols - c3, True, 0))
    pieces = [(src + o, min(PIECE_ROWS, width - o), to_gates, dst + o)
              for src, width, to_gates, dst in groups for o in range(0, width, PIECE_ROWS)]
    chunks = []
    for src, n, to_gates, dst in pieces:
        if chunks and chunks[-1][0] + chunks[-1][1] == src and chunks[-1][1] + n <= STAGE_ROWS:
            c_src, c_n, stores = chunks[-1]
            chunks[-1] = (c_src, c_n + n, stores + ((c_n, n, to_gates, dst),))
        else:
            chunks.append((src, n, ((0, n, to_gates, dst),)))
    chunks = tuple(chunks)
    mu_cat = seg(shift_mu.reshape(1, -1), _pad_cols)
    row = lambda t: t.reshape(1, -1).astype(F32)
    e = _head_mask()
    tile = lambda t: jnp.tile(t.reshape(1, HEAD), (1, N_HEADS)).astype(F32)
    consts = [
        row(mix_norm), jnp.swapaxes(w_in, 0, 1), mu_cat, row(w0), _pad_rows(w_up, LANE).astype(BF16), row(a0),
        _pad_rows(a_up, LANE).astype(BF16), _pad_rows(g_up, LANE).astype(BF16), row(k_k), row(k_a), row(r_k),
        _pad_cols(row(f_bias), LANE), tile(q_norm), tile(k_norm), e,
    ]
    const_specs = [_resident(c.shape) for c in consts]
    const_specs[1] = pl.BlockSpec(memory_space=pl.ANY)
    assert n_cols - c3 == 2 * d
    tm, tile_rows = tm * n_sub, tm
    tok = lambda: pl.BlockSpec((1, tm, D_MIX), lambda b, i: (b, i, 0))
    f32_tok = jax.ShapeDtypeStruct((bsz, s, D_MIX), F32)
    bf_tok = jax.ShapeDtypeStruct((bsz, s, D_MIX), BF16)
    assert s % tm == 0 and FOX_KBLOCK % tile_rows == 0
    return pl.pallas_call(
        functools.partial(_in_proj_kernel, tm=tile_rows, n_sub=n_sub, tiles_per_kblock=FOX_KBLOCK // tile_rows,
                          chunks=chunks),
        out_shape=[f32_tok, f32_tok, f32_tok, bf_tok, f32_tok, f32_tok, bf_tok, bf_tok] + [
            jax.ShapeDtypeStruct((bsz, N_HEADS, 2 * HEAD, s), BF16),
            jax.ShapeDtypeStruct((bsz, N_HEADS, s, 2 * HEAD), BF16),
            jax.ShapeDtypeStruct((bsz, N_HEADS, V_ROWS, s), BF16),
            jax.ShapeDtypeStruct((bsz, s // tile_rows, 1, LANE), F32),
            jax.ShapeDtypeStruct((2 * d, d), BF16),
        ],
        grid=(bsz, s // tm),
        in_specs=[pl.BlockSpec((1, tm, d), lambda b, i: (b, i, 0))] + const_specs,
        out_specs=[tok() for _ in range(8)] + [
            pl.BlockSpec((1, N_HEADS, 2 * HEAD, tm), lambda b, i: (b, 0, 0, i)),
            pl.BlockSpec((1, N_HEADS, tm, 2 * HEAD), lambda b, i: (b, 0, i, 0)),
            pl.BlockSpec((1, N_HEADS, V_ROWS, tm), lambda b, i: (b, 0, 0, i)),
            pl.BlockSpec((1, n_sub, 1, LANE), lambda b, i: (b, i, 0, 0)),
            pl.BlockSpec((2 * d, d), lambda b, i: (0, 0)),
        ],
        scratch_shapes=[pltpu.VMEM((1, N_SHIFT), F32), pltpu.VMEM((1, LANE), F32), pltpu.VMEM((1, LANE), F32),
                        pltpu.VMEM((N_PROJ, d), BF16), pltpu.VMEM((STAGE_SLOTS, STAGE_ROWS, d), F32),
                        pltpu.SemaphoreType.DMA((STAGE_SLOTS,))],
        compiler_params=pltpu.CompilerParams(
            dimension_semantics=("arbitrary", "arbitrary"), vmem_limit_bytes=VMEM_LIMIT),
        name="in_proj",
    )(x, *consts)


def _rwkv_kernel(r_ref, lw_ref, k_ref, v_ref, kk_ref, b_ref, g_ref, bonus_ref, lng_ref, lnb_ref, e_ref, y_ref, s_scr, *, tb):
    i = pl.program_id(1)

    @pl.when(i == 0)
    def _():
        s_scr[...] = jnp.zeros_like(s_scr)

    c = CHUNK
    pw = 2 * HEAD
    n_chunk = tb // c
    n_pair = N_HEADS // 2
    rr = lax.broadcasted_iota(jnp.int32, (c, pw), 0)
    ll = lax.broadcasted_iota(jnp.int32, (c, pw), 1)
    cc = ll % c
    lo = ll < c
    strict = cc < rr
    incl = cc <= rr
    eye = (cc == rr).astype(F32)
    r2 = lax.broadcasted_iota(jnp.int32, (2 * c, pw), 0)
    l2 = lax.broadcasted_iota(jnp.int32, (2 * c, pw), 1)
    lo2 = l2 < c
    diag2 = (r2 < c) == lo2
    zeros = jnp.zeros((c, pw), F32)

    def level_mask(n):
        return ((rr // (2 * n)) == (cc // (2 * n))) & (((rr // n) % 2) == 1) & (((cc // n) % 2) == 0)

    def bd(x):
        return jnp.where(diag2, jnp.concatenate([x, x], axis=0), 0.0).astype(BF16)

    def bd_sw(x):
        return jnp.where(diag2, 0.0, jnp.concatenate([x, x], axis=0)).astype(BF16)

    slab = min(tb, MXU_DIM)
    rb = lax.broadcasted_iota(jnp.int32, (slab, slab), 0)
    cb = lax.broadcasted_iota(jnp.int32, (slab, slab), 1)
    tri = jnp.where((cb <= rb) & ((rb // c) == (cb // c)), 1.0, 0.0).astype(BF16)
    lw = lw_ref[0]
    cs = jnp.concatenate([_cumsum_rows(tri, lw[j:j + slab]) for j in range(0, tb, slab)], axis=0)
    w_inv = jnp.exp(-cs)
    rt = r_ref[0] * jnp.exp(cs)
    at = -kk_ref[0] * jnp.exp(cs - lw)
    bi = b_ref[0] * w_inv
    ki = k_ref[0] * w_inv
    v = v_ref[0].astype(F32)

    chains = [(j, p) for j in range(n_chunk) for p in range(n_pair)]
    cut = lambda x, j, p: x[j * c:(j + 1) * c, p * pw:(p + 1) * pw]
    w_tot = [jnp.exp(cs[(j + 1) * c - 1:(j + 1) * c, :]) for j in range(n_chunk)]
    a_t = [cut(at, j, p) for j, p in chains]
    r_t = [cut(rt, j, p) for j, p in chains]
    b_i = [cut(bi, j, p) for j, p in chains]
    k_i = [cut(ki, j, p) for j, p in chains]
    v_p = [cut(v, j, p) for j, p in chains]
    n = len(chains)

    a_ab, a_ak, a_rb, a_rk = [], [], [], []
    for q in range(n):
        ar = jnp.concatenate([a_t[q], r_t[q]], axis=0)
        gx = _dot(jnp.where(lo2, ar, 0.0).astype(BF16), jnp.concatenate([b_i[q], k_i[q]], axis=0).astype(BF16), _NT)
        gy = _dot(jnp.where(lo2, 0.0, ar).astype(BF16), jnp.concatenate([k_i[q], b_i[q]], axis=0).astype(BF16), _NT)
        a_ab.append(jnp.where(strict, jnp.where(lo, gx[:c], gy[:c]), 0.0))
        a_ak.append(jnp.where(strict, jnp.where(lo, gy[:c], gx[:c]), 0.0))
        a_rb.append(jnp.where(incl, jnp.where(lo, gx[c:], gy[c:]), 0.0))
        a_rk.append(jnp.where(incl, jnp.where(lo, gy[c:], gx[c:]), 0.0))

    t = [eye + jnp.where(level_mask(1), a_ab[q], 0.0) for q in range(n)]
    for m in (2, 4, 8, 16, 32):
        x = [_dot(jnp.where(level_mask(m), a_ab[q], 0.0).astype(BF16), bd(t[q])) for q in range(n)]
        t = [t[q] + _dot(t[q].astype(BF16), bd(x[q])) for q in range(n)]

    v_sw = [bd_sw(v_p[q]) for q in range(n)]
    av = [_dot(jnp.concatenate([a_ak[q], a_rk[q]], axis=0).astype(BF16), v_sw[q]) for q in range(n)]
    x = [_dot(t[q].astype(BF16), jnp.concatenate([bd(a_t[q]), bd(av[q][:c])], axis=1)) for q in range(n)]
    a_p = [x[q][:, :pw] for q in range(n)]
    u_0 = [x[q][:, pw:] for q in range(n)]
    d1 = [_dot(a_rb[q].astype(BF16), jnp.concatenate([bd(a_p[q]), bd(u_0[q])], axis=1)) for q in range(n)]
    rp = [r_t[q] + d1[q][:, :pw] for q in range(n)]
    y_0 = [d1[q][:, pw:] + av[q][c:] for q in range(n)]
    d2 = []
    for q, (j, p) in enumerate(chains):
        wt = w_tot[j][:, p * pw:(p + 1) * pw]
        lhs = jnp.concatenate([jnp.concatenate([a_p[q], u_0[q]], axis=1),
                               jnp.concatenate([zeros, v_p[q]], axis=1)], axis=0).astype(BF16)
        rhs = jnp.concatenate([b_i[q] * wt, k_i[q] * wt], axis=0).astype(BF16)
        d2.append(_dot(lhs, rhs, _TN))

    s = [s_scr[p] for p in range(n_pair)]
    ys = [None] * n
    for j in range(n_chunk):
        for p in range(n_pair):
            q = j * n_pair + p
            mt = jnp.where(diag2, d2[q][:pw], 0.0).astype(BF16)
            nt = jnp.where(lo, d2[q][pw:pw + c], d2[q][pw + c:])
            ys[q] = y_0[q] + _dot(rp[q].astype(BF16), bd(s[p]), _NT)
            s[p] = s[p] * w_tot[j][:, p * pw:(p + 1) * pw] + _dot(s[p].astype(BF16), mt) + nt
    for p in range(n_pair):
        s_scr[p] = s[p]

    y = jnp.concatenate([jnp.concatenate([ys[j * n_pair + p] for p in range(n_pair)], axis=1) for j in range(n_chunk)], axis=0)
    e = e_ref[...]
    inv_head = 1.0 / HEAD
    yc = y - _head_sum(y, e) * inv_head
    yn = yc * lax.rsqrt(_head_sum(yc * yc, e) * inv_head + LNX_EPS)
    y_ref[0] = ((yn * lng_ref[...] + lnb_ref[...] + bonus_ref[0]) * g_ref[0]).astype(BF16)


def _head_mask():
    head_id = jnp.arange(MXU_DIM) // HEAD
    return (head_id[:, None] == head_id[None, :]).astype(BF16)


def _rwkv(r, lw, k, v, kk, b, g, bonus, lnx_g, lnx_b, *, tb=512):
    bsz, s, _ = r.shape
    assert s % tb == 0 and tb % CHUNK == 0
    tok = pl.BlockSpec((1, tb, D_MIX), lambda bb, i: (bb, i, 0))
    vec = pl.BlockSpec((1, D_MIX), lambda bb, i: (0, 0))
    return pl.pallas_call(
        functools.partial(_rwkv_kernel, tb=tb),
        out_shape=jax.ShapeDtypeStruct((bsz, s, D_MIX), BF16),
        grid=(bsz, s // tb),
        in_specs=[tok] * 8 + [vec, vec, pl.BlockSpec((MXU_DIM, MXU_DIM), lambda bb, i: (0, 0))],
        out_specs=tok,
        scratch_shapes=[pltpu.VMEM((N_HEADS // 2, HEAD, 2 * HEAD), F32)],
        compiler_params=pltpu.CompilerParams(
            dimension_semantics=("parallel", "arbitrary"), vmem_limit_bytes=VMEM_LIMIT),
        name="rwkv",
    )(r, lw, k, v, kk, b, g, bonus, lnx_g.reshape(1, -1).astype(F32), lnx_b.reshape(1, -1).astype(F32), _head_mask())


def _fox_kernel(off_ref, thr_ref, q_ref, k_ref, v_ref, o_ref, *scratch, tq, tk, qw, n_kblk):
    b, hp, qb = pl.program_id(0), pl.program_id(1), pl.program_id(2)
    n_g = tq // qw
    kb_last = (qb * tq) // tk
    chains = [(hh, g) for hh in range(2) for g in range(n_g)]
    n_c = len(chains)
    s_scr = (scratch[:n_c], scratch[n_c:2 * n_c])
    m_scr, acc_scr = scratch[2 * n_c:]
    base = [(b * N_HEADS + 2 * hp + hh) * n_kblk for hh in range(2)]
    off_d = [off_ref[base[hh] + kb_last] for hh in range(2)]

    thr = thr_ref[0]

    def needed(kb):
        nxt = jnp.minimum(kb + 1, kb_last)
        return jnp.minimum(off_ref[base[0] + nxt] - off_d[0], off_ref[base[1] + nxt] - off_d[1]) <= thr

    kb_first = lax.while_loop(lambda kb: jnp.logical_and(kb > 0, needed(kb - 1)), lambda kb: kb - 1, kb_last)
    n_open = kb_last - kb_first

    def qk(kb, slot, first_group=0):
        ksl = pl.ds(pl.multiple_of(kb * tk, tk), tk)
        for c, (hh, g) in enumerate(chains):
            if g >= first_group:
                s_scr[slot][c][...] = _dot(k_ref[0, hh, ksl, :], q_ref[0, hh, :, g * qw:(g + 1) * qw])

    def soft_pv(kb, slot, diag_group=None):
        ksl = pl.ds(pl.multiple_of(kb * tk, tk), tk)
        for c, (hh, g) in enumerate(chains):
            if diag_group is not None and g < diag_group:
                continue
            s = s_scr[slot][c][...]
            if g == diag_group:
                key = lax.broadcasted_iota(jnp.int32, (tk, qw), 0)
                qry = lax.broadcasted_iota(jnp.int32, (tk, qw), 1)
                s = jnp.where(key <= qry, s, NEG_BIG)
            off = off_ref[base[hh] + kb] - off_d[hh]
            m = m_scr[c]
            m_new = jnp.maximum(m, jnp.max(s, axis=0, keepdims=True) - off)
            p = jnp.exp2(s - (m_new + off)).astype(BF16)
            acc_scr[c] = jnp.exp2(m - m_new) * acc_scr[c] + _dot(v_ref[0, hh, :, ksl], p)
            m_scr[c] = m_new

    m_scr[...] = jnp.full(m_scr.shape, NEG_BIG, F32)
    acc_scr[...] = jnp.zeros(acc_scr.shape, F32)
    qk(kb_first, 0)

    def pair(i, _):
        kb = kb_first + 2 * i
        qk(kb + 1, 1)
        soft_pv(kb, 0)
        qk(kb + 2, 0)
        soft_pv(kb + 1, 1)
        return 0

    lax.fori_loop(0, n_open // 2, pair, 0)

    def finish(blocks):
        for j, (kb, diag_group) in enumerate(blocks):
            if j + 1 < len(blocks):
                qk(blocks[j + 1][0], (j + 1) % 2, first_group=blocks[j + 1][1] or 0)
            soft_pv(kb, j % 2, diag_group)

    diagonal = [(kb_last + g, g) for g in range(n_g)]

    @pl.when(n_open % 2 == 1)
    def _():
        finish([(kb_last - 1, None)] + diagonal)

    @pl.when(n_open % 2 == 0)
    def _():
        finish(diagonal)

    acc = [acc_scr[c] for c in range(n_c)]
    o_t = jnp.concatenate([jnp.concatenate([acc[c][:HEAD] / acc[c][HEAD:HEAD + 1] for c, (h2, _) in enumerate(chains)
                                            if h2 == hh], axis=1) for hh in range(2)], axis=0)
    o_ref[0] = o_t.T.astype(BF16)


def _fox(q_t, k_aug, v_t, off, q_norm, k_norm, *, tq=512, tk=FOX_KBLOCK, qw=MXU_DIM):
    bsz, nh, s, _ = k_aug.shape
    n_kblk = s // tk
    n_chain = 2 * (tq // qw)
    assert tq % qw == 0 and tk == qw
    assert s % tq == 0 and s % tk == 0 and nh % 2 == 0 and off.shape == (bsz * nh * n_kblk,)
    qk_bound = 1.02 * LOG2E * HEAD ** 0.5 * jnp.max(jnp.abs(q_norm)) * jnp.max(jnp.abs(k_norm))
    thr = (2.0 * qk_bound + EXP2_UNDERFLOW).reshape(1).astype(F32)
    return pl.pallas_call(
        functools.partial(_fox_kernel, tq=tq, tk=tk, qw=qw, n_kblk=n_kblk),
        out_shape=jax.ShapeDtypeStruct((bsz, s, nh * HEAD), BF16),
        grid=(bsz, nh // 2, s // tq),
        in_specs=[
            pl.BlockSpec(memory_space=pltpu.SMEM),
            pl.BlockSpec(memory_space=pltpu.SMEM),
            pl.BlockSpec((1, 2, 2 * HEAD, tq), lambda b, hp, i: (b, hp, 0, i)),
            pl.BlockSpec((1, 2, s, 2 * HEAD), lambda b, hp, i: (b, hp, 0, 0)),
            pl.BlockSpec((1, 2, V_ROWS, s), lambda b, hp, i: (b, hp, 0, 0)),
        ],
        out_specs=pl.BlockSpec((1, tq, 2 * HEAD), lambda b, hp, i: (b, i, hp)),
        scratch_shapes=[pltpu.VMEM((tk, qw), F32)] * (2 * n_chain)
        + [pltpu.VMEM((n_chain, 1, qw), F32), pltpu.VMEM((n_chain, V_ROWS, qw), F32)],
        compiler_params=pltpu.CompilerParams(
            dimension_semantics=("parallel", "parallel", "arbitrary"), vmem_limit_bytes=VMEM_LIMIT),
        name="fox",
    )(off, thr, q_t, k_aug, v_t)


def _merge_kernel(x_ref, g_ref, wg_ref, ya_ref, yb_ref, pa_ref, pb_ref, wo_ref, o_ref):
    x = x_ref[...]
    d = x.shape[-1]
    h = _rms(x, g_ref[...]).astype(BF16)
    gates = _sigmoid(_dot(h, wg_ref[...], _NT))
    ya = _dot(ya_ref[...], pa_ref[...])
    yb = _dot(yb_ref[...], pb_ref[...])
    mix = (gates[:, :d] * ya + gates[:, d:] * yb).astype(BF16)
    o_ref[...] = x + _dot(mix, wo_ref[...])


def _merge(x2d, mix_norm, w_gates, ya, yb, p_a, p_b, w_out, *, tm=512):
    m, d = x2d.shape
    assert m % tm == 0
    full = lambda t: pl.BlockSpec(t.shape, lambda i: (0, 0))
    tok = lambda n: pl.BlockSpec((tm, n), lambda i: (i, 0))
    consts = [mix_norm.reshape(1, d).astype(F32), w_gates.astype(BF16)]
    mats = [p_a.astype(BF16), p_b.astype(BF16), w_out.astype(BF16)]
    return pl.pallas_call(
        _merge_kernel,
        out_shape=jax.ShapeDtypeStruct((m, d), F32),
        grid=(m // tm,),
        in_specs=[tok(d), full(consts[0]), full(consts[1]), tok(D_MIX), tok(D_MIX)] + [full(t) for t in mats],
        out_specs=tok(d),
        compiler_params=pltpu.CompilerParams(dimension_semantics=("parallel",), vmem_limit_bytes=VMEM_LIMIT),
        name="merge",
    )(x2d, consts[0], consts[1], ya, yb, *mats)


def kernel(x, ffn1_norm, ffn1_w1, ffn1_w3, ffn1_w2, mix_norm, w_in, shift_mu, rwkv_w0, rwkv_w_up, rwkv_a0, rwkv_a_up, rwkv_g_up, rwkv_k_k, rwkv_k_a, rwkv_r_k, rwkv_lnx_g, rwkv_lnx_b, rwkv_proj, fox_f_bias, fox_q_norm, fox_k_norm, fox_proj, w_out, ffn2_norm, ffn2_w1, ffn2_w3, ffn2_w2, final_norm):
    bsz, s, d = x.shape
    depth = ffn1_norm.shape[0]
    x2d = x.reshape(bsz * s, d)
    for l in range(depth):
        x2d = _ffn(x2d, ffn1_norm[l], ffn1_w1[l], ffn1_w3[l], ffn1_w2[l], final_norm, final_norm=False)
        (r, lw, k, v, kk, b, g, bonus, fq_t, fk_aug, fv_t, off, w_gates) = _in_proj(
            x2d.reshape(bsz, s, d), mix_norm[l], w_in[l], shift_mu[l], rwkv_w0[l], rwkv_w_up[l], rwkv_a0[l],
            rwkv_a_up[l], rwkv_g_up[l], rwkv_k_k[l], rwkv_k_a[l], rwkv_r_k[l], fox_f_bias[l], fox_q_norm[l],
            fox_k_norm[l])
        ya = _rwkv(r, lw, k, v, kk, b, g, bonus, rwkv_lnx_g[l], rwkv_lnx_b[l])
        off = off[:, ::FOX_KBLOCK // (s // off.shape[1]), 0, :N_HEADS]
        off = jnp.transpose(off, (0, 2, 1)).reshape(-1)
        yb = _fox(fq_t, fk_aug, fv_t, off, fox_q_norm[l], fox_k_norm[l])
        x2d = _merge(x2d, mix_norm[l], w_gates, ya.reshape(bsz * s, D_MIX), yb.reshape(bsz * s, D_MIX),
                     rwkv_proj[l], fox_proj[l], w_out[l])
        x2d = _ffn(x2d, ffn2_norm[l], ffn2_w1[l], ffn2_w3[l], ffn2_w2[l], final_norm, final_norm=(l == depth - 1))
    return x2d.reshape(bsz, s, d)
```

```python
import functools

import jax
import jax.numpy as jnp
from jax import lax
from jax.experimental import pallas as pl
from jax.experimental.pallas import tpu as pltpu

F32 = jnp.float32
BF16 = jnp.bfloat16

HEAD = 64
N_HEADS = 8
D_MIX = HEAD * N_HEADS
LANE = 128
MXU_DIM = 256
CHUNK = 64
RMS_EPS = 1e-6
LNX_EPS = 64e-5
NEG_BIG = -1e30
EXP2_UNDERFLOW = 160.0
VMEM_LIMIT = 56 * 1024 * 1024

_NT = (((1,), (1,)), ((), ()))
_TN = (((0,), (0,)), ((), ()))


def _dot(a, b, dims=None):
    if dims is None:
        return jnp.dot(a, b, preferred_element_type=F32)
    return lax.dot_general(a, b, dims, preferred_element_type=F32)


def _bdot(a, b, dims=None):
    return _dot(a.astype(BF16), b.astype(BF16), dims)


def _rms(x, g):
    return x * lax.rsqrt(jnp.mean(x * x, axis=-1, keepdims=True) + RMS_EPS) * g


def _softplus(z):
    return jnp.maximum(z, 0.0) + jnp.log(1.0 + jnp.exp(-jnp.abs(z)))


def _sigmoid(z):
    return 1.0 / (1.0 + jnp.exp(-z))


def _head_sum(x, e):
    xb = x.astype(BF16)
    return jnp.concatenate([_dot(xb[:, j:j + MXU_DIM], e) for j in range(0, x.shape[1], MXU_DIM)], axis=1)


def _cumsum_rows(tri, x):
    hi = x.astype(BF16)
    r1 = x - hi.astype(F32)
    mid = r1.astype(BF16)
    lo = (r1 - mid.astype(F32)).astype(BF16)
    return _dot(tri, hi) + _dot(tri, mid) + _dot(tri, lo)


WEIGHT_CHUNKS = 8
PIECE_ROWS = 512
STAGE_ROWS = 640
STAGE_SLOTS = 3


def _load_cast(w_hbm, w_ref, stage, sem):
    rows = w_hbm.shape[0] // WEIGHT_CHUNKS

    def copy(c):
        slot = c % STAGE_SLOTS
        return pltpu.make_async_copy(w_hbm.at[pl.ds(c * rows, rows)], stage.at[slot], sem.at[slot])

    for c in range(STAGE_SLOTS - 1):
        copy(c).start()
    for c in range(WEIGHT_CHUNKS):
        if c + STAGE_SLOTS - 1 < WEIGHT_CHUNKS:
            copy(c + STAGE_SLOTS - 1).start()
        copy(c).wait()
        w_ref[pl.ds(c * rows, rows), :] = stage[c % STAGE_SLOTS].astype(BF16)


def _ffn_kernel(x_ref, g_ref, w1_hbm, w3_hbm, w2_hbm, gf_ref, o_ref, w1_ref, w3_ref, w2_ref, stage_up, stage_down, sem,
                *, final_norm):
    @pl.when(pl.program_id(0) == 0)
    def _():
        _load_cast(w1_hbm, w1_ref, stage_up, sem)
        _load_cast(w3_hbm, w3_ref, stage_up, sem)
        _load_cast(w2_hbm, w2_ref, stage_down, sem)

    x = x_ref[...]
    h = _rms(x, g_ref[...]).astype(BF16)
    a = _dot(h, w1_ref[...])
    b = _dot(h, w3_ref[...])
    u = (a * _sigmoid(a) * b).astype(BF16)
    y = x + 0.5 * _dot(u, w2_ref[...])
    if final_norm:
        y = _rms(y, gf_ref[...])
    o_ref[...] = y


def _resident(shape):
    return pl.BlockSpec(shape, lambda *_: (0,) * len(shape), pipeline_mode=pl.Buffered(1))


def _ffn(x2d, g, w1, w3, w2, gf, *, final_norm, tm=512):
    m, d = x2d.shape
    f_dim = w1.shape[1]
    assert m % tm == 0 and d % WEIGHT_CHUNKS == 0 and f_dim % WEIGHT_CHUNKS == 0
    in_hbm = pl.BlockSpec(memory_space=pl.ANY)
    return pl.pallas_call(
        functools.partial(_ffn_kernel, final_norm=final_norm),
        out_shape=jax.ShapeDtypeStruct((m, d), F32),
        grid=(m // tm,),
        in_specs=[pl.BlockSpec((tm, d), lambda i: (i, 0)), _resident((1, d)), in_hbm, in_hbm, in_hbm, _resident((1, d))],
        out_specs=pl.BlockSpec((tm, d), lambda i: (i, 0)),
        scratch_shapes=[
            pltpu.VMEM((d, f_dim), BF16), pltpu.VMEM((d, f_dim), BF16), pltpu.VMEM((f_dim, d), BF16),
            pltpu.VMEM((STAGE_SLOTS, d // WEIGHT_CHUNKS, f_dim), F32),
            pltpu.VMEM((STAGE_SLOTS, f_dim // WEIGHT_CHUNKS, d), F32),
            pltpu.SemaphoreType.DMA((STAGE_SLOTS,)),
        ],
        compiler_params=pltpu.CompilerParams(dimension_semantics=("arbitrary",), vmem_limit_bytes=VMEM_LIMIT),
        name="ffn",
    )(x2d, g.reshape(1, d), w1, w3, w2, gf.reshape(1, d))


N_SHIFT = 3 * D_MIX + 3 * LANE
N_PROJ = N_SHIFT + 3 * D_MIX + LANE
LOG2E = 1.4426950408889634
V_ROWS = HEAD + 16
FOX_KBLOCK = 256


def _in_proj_kernel(x_ref, g_ref, w_hbm, mu_ref, w0_ref, wup_ref, a0_ref, aup_ref, gup_ref, kk_ref, ka_ref,
                    rk_ref, fb_ref, qn_ref, kn_ref, e_ref,
                    r_out, lw_out, k_out, v_out, kk_out, b_out, g_out, bonus_out, fq_out, fk_out, fv_out, off_out,
                    wg_out, carry_p, carry_c, carry_b, w_ref, stage, sem, *, tm, n_sub, tiles_per_kblock, chunks):
    step = pl.program_id(1)

    @pl.when(jnp.logical_and(pl.program_id(0) == 0, step == 0))
    def _():
        w_ref[...] = jnp.zeros_like(w_ref)

        def copy(i):
            src, n = chunks[i][:2]
            slot = i % STAGE_SLOTS
            return pltpu.make_async_copy(w_hbm.at[pl.ds(src, n)], stage.at[slot, pl.ds(0, n)], sem.at[slot])

        for i in range(min(STAGE_SLOTS - 1, len(chunks))):
            copy(i).start()
        for i, (_, _, stores) in enumerate(chunks):
            if i + STAGE_SLOTS - 1 < len(chunks):
                copy(i + STAGE_SLOTS - 1).start()
            copy(i).wait()
            for at, n, to_gates, dst in stores:
                (wg_out if to_gates else w_ref)[pl.ds(dst, n), :] = stage[i % STAGE_SLOTS, at:at + n].astype(BF16)

    @pl.when(step == 0)
    def _():
        carry_p[...] = jnp.zeros_like(carry_p)
        carry_c[...] = jnp.zeros_like(carry_c)
        carry_b[...] = jnp.zeros_like(carry_b)

    outs = (r_out, lw_out, k_out, v_out, kk_out, b_out, g_out, bonus_out)
    for t in range(n_sub):
        rows = pl.ds(t * tm, tm)
        _in_proj_tile(step * n_sub + t, x_ref.at[0, rows], g_ref, w_ref, mu_ref, w0_ref, wup_ref, a0_ref, aup_ref, gup_ref,
                      kk_ref, ka_ref, rk_ref, fb_ref, qn_ref, kn_ref, e_ref, *[o.at[0, rows] for o in outs],
                      fq_out.at[0, :, :, rows], fk_out.at[0, :, rows], fv_out.at[0, :, :, rows], off_out.at[0, t],
                      carry_p, carry_c, carry_b, tm=tm, tiles_per_kblock=tiles_per_kblock)


def _in_proj_tile(i, x_ref, g_ref, w_ref, mu_ref, w0_ref, wup_ref, a0_ref, aup_ref, gup_ref, kk_ref, ka_ref,
                  rk_ref, fb_ref, qn_ref, kn_ref, e_ref,
                  r_out, lw_out, k_out, v_out, kk_out, b_out, g_out, bonus_out, fq_out, fk_out, fv_out, off_out,
                  carry_p, carry_c, carry_b, *, tm, tiles_per_kblock):
    h = _rms(x_ref[...], g_ref[...]).astype(BF16)
    p = _dot(h, w_ref[...], _NT)

    ps = p[:, :N_SHIFT]
    row = lax.broadcasted_iota(jnp.int32, (tm, 1), 0)
    prev = jnp.where(row == 0, carry_p[...], pltpu.roll(ps, 1, axis=0))
    carry_p[...] = ps[tm - 1:tm, :]
    sh = ps + (prev - ps) * mu_ref[...]

    e = e_ref[...]
    r = sh[:, 0:D_MIX]
    k = sh[:, D_MIX:2 * D_MIX]
    v = sh[:, 2 * D_MIX:3 * D_MIX]
    w_lo = sh[:, 3 * D_MIX:3 * D_MIX + LANE]
    a_lo = sh[:, 3 * D_MIX + LANE:3 * D_MIX + 2 * LANE]
    g_lo = sh[:, 3 * D_MIX + 2 * LANE:N_SHIFT]

    w_log = -_softplus(-(w0_ref[...] + _bdot(jnp.tanh(w_lo), wup_ref[...]))) - 0.5
    lw = -jnp.exp(w_log)
    a = _sigmoid(a0_ref[...] + _bdot(a_lo, aup_ref[...]))
    g = _bdot(_sigmoid(g_lo), gup_ref[...])
    kk = k * kk_ref[...]
    kk = kk * lax.rsqrt(jnp.maximum(_head_sum(kk * kk, e), 1e-24))
    k_mod = k * (1.0 + (a - 1.0) * ka_ref[...])
    bonus = _head_sum(r * k_mod * rk_ref[...], e) * v

    r_out[...] = r
    lw_out[...] = lw
    k_out[...] = k_mod
    v_out[...] = v.astype(BF16)
    kk_out[...] = kk
    b_out[...] = kk * a
    g_out[...] = g.astype(BF16)
    bonus_out[...] = bonus.astype(BF16)

    fq = p[:, N_SHIFT:N_SHIFT + D_MIX]
    fk = p[:, N_SHIFT + D_MIX:N_SHIFT + 2 * D_MIX]
    fv = p[:, N_SHIFT + 2 * D_MIX:N_SHIFT + 3 * D_MIX]
    f_lo = p[:, N_SHIFT + 3 * D_MIX:N_PROJ]
    inv_head = 1.0 / HEAD
    qn = fq * lax.rsqrt(_head_sum(fq * fq, e) * inv_head + RMS_EPS) * qn_ref[...] * (LOG2E * HEAD ** -0.5)
    kn = fk * lax.rsqrt(_head_sum(fk * fk, e) * inv_head + RMS_EPS) * kn_ref[...]
    z = f_lo + fb_ref[...]
    log_f = jnp.minimum(z, 0.0) - jnp.log(1.0 + jnp.exp(-jnp.abs(z)))
    tri = jnp.where(lax.broadcasted_iota(jnp.int32, (tm, tm), 1) <= lax.broadcasted_iota(jnp.int32, (tm, tm), 0),
                    1.0, 0.0).astype(BF16)
    c_loc = _cumsum_rows(tri, log_f)
    off_out[...] = carry_c[...] * LOG2E
    in_blk = jnp.where(i % tiles_per_kblock == 0, 0.0, carry_b[...])
    carry_c[...] = carry_c[...] + c_loc[tm - 1:tm, :]
    carry_b[...] = in_blk + c_loc[tm - 1:tm, :]
    c2 = (c_loc + in_blk) * LOG2E

    lane = lax.broadcasted_iota(jnp.int32, (tm, HEAD), 1)
    sub = lax.broadcasted_iota(jnp.int32, (HEAD, tm), 0)
    q_tail = jnp.where(sub < 3, -1.0, 0.0)
    v_tail = jnp.where(lax.broadcasted_iota(jnp.int32, (V_ROWS - HEAD, tm), 0) == 0, 1.0, 0.0)
    for pr in range(N_HEADS // 2):
        ps2 = slice(pr * 2 * HEAD, (pr + 1) * 2 * HEAD)
        q_t = qn[:, ps2].T
        v_t = fv[:, ps2].T
        for half in range(2):
            hh = 2 * pr + half
            rows = slice(half * HEAD, (half + 1) * HEAD)
            fq_out[hh] = jnp.concatenate([q_t[rows], q_tail], axis=0).astype(BF16)
            fv_out[hh] = jnp.concatenate([v_t[rows], v_tail], axis=0).astype(BF16)
            c_h = c2[:, hh:hh + 1]
            c_hi = c_h.astype(BF16).astype(F32)
            c_mid = (c_h - c_hi).astype(BF16).astype(F32)
            c_lo = c_h - c_hi - c_mid
            c_cols = jnp.where(lane == 0, c_hi, jnp.where(lane == 1, c_mid, jnp.where(lane == 2, c_lo, 0.0)))
            fk_out[hh] = jnp.concatenate([kn[:, hh * HEAD:(hh + 1) * HEAD], c_cols], axis=1).astype(BF16)


def _pad_cols(w, n):
    return jnp.pad(w, ((0, 0), (0, n - w.shape[1])))


def _pad_rows(w, n):
    return jnp.pad(w, ((0, n - w.shape[0]), (0, 0)))


def _in_proj(x, mix_norm, w_in, shift_mu, w0, w_up, a0, a_up, g_up, k_k, k_a, r_k, f_bias, q_norm, k_norm, *, tm=256, n_sub=2):
    bsz, s, d = x.shape
    lw_, la_, lg_ = w_up.shape[0], a_up.shape[0], g_up.shape[0]
    c0 = 3 * D_MIX
    c1 = c0 + lw_ + la_ + lg_
    c2 = c1 + 3 * D_MIX
    c3 = c2 + N_HEADS

    def seg(t, pad):
        return jnp.concatenate([
            t[..., :c0],
            pad(t[..., c0:c0 + lw_], LANE), pad(t[..., c0 + lw_:c0 + lw_ + la_], LANE), pad(t[..., c0 + lw_ + la_:c1], LANE),
        ], axis=-1)

    n_cols = w_in.shape[1]
    groups = ((0, c0, False, 0), (c0, lw_, False, c0), (c0 + lw_, la_, False, c0 + LANE),
              (c0 + lw_ + la_, lg_, False, c0 + 2 * LANE), (c1, c2 - c1, False, N_SHIFT),
              (c2, c3 - c2, False, N_SHIFT + 3 * D_MIX), (c3, n_cols - c3, True, 0))
    pieces = [(src + o, min(PIECE_ROWS, width - o), to_gates, dst + o)
              for src, width, to_gates, dst in groups for o in range(0, width, PIECE_ROWS)]
    chunks = []
    for src, n, to_gates, dst in pieces:
        if chunks and chunks[-1][0] + chunks[-1][1] == src and chunks[-1][1] + n <= STAGE_ROWS:
            c_src, c_n, stores = chunks[-1]
            chunks[-1] = (c_src, c_n + n, stores + ((c_n, n, to_gates, dst),))
        else:
            chunks.append((src, n, ((0, n, to_gates, dst),)))
    chunks = tuple(chunks)
    mu_cat = seg(shift_mu.reshape(1, -1), _pad_cols)
    row = lambda t: t.reshape(1, -1).astype(F32)
    e = _head_mask()
    tile = lambda t: jnp.tile(t.reshape(1, HEAD), (1, N_HEADS)).astype(F32)
    consts = [
        row(mix_norm), jnp.swapaxes(w_in, 0, 1), mu_cat, row(w0), _pad_rows(w_up, LANE).astype(BF16), row(a0),
        _pad_rows(a_up, LANE).astype(BF16), _pad_rows(g_up, LANE).astype(BF16), row(k_k), row(k_a), row(r_k),
        _pad_cols(row(f_bias), LANE), tile(q_norm), tile(k_norm), e,
    ]
    const_specs = [_resident(c.shape) for c in consts]
    const_specs[1] = pl.BlockSpec(memory_space=pl.ANY)
    assert n_cols - c3 == 2 * d
    tm, tile_rows = tm * n_sub, tm
    tok = lambda: pl.BlockSpec((1, tm, D_MIX), lambda b, i: (b, i, 0))
    f32_tok = jax.ShapeDtypeStruct((bsz, s, D_MIX), F32)
    bf_tok = jax.ShapeDtypeStruct((bsz, s, D_MIX), BF16)
    assert s % tm == 0 and FOX_KBLOCK % tile_rows == 0
    return pl.pallas_call(
        functools.partial(_in_proj_kernel, tm=tile_rows, n_sub=n_sub, tiles_per_kblock=FOX_KBLOCK // tile_rows,
                          chunks=chunks),
        out_shape=[f32_tok, f32_tok, f32_tok, bf_tok, f32_tok, f32_tok, bf_tok, bf_tok] + [
            jax.ShapeDtypeStruct((bsz, N_HEADS, 2 * HEAD, s), BF16),
            jax.ShapeDtypeStruct((bsz, N_HEADS, s, 2 * HEAD), BF16),
            jax.ShapeDtypeStruct((bsz, N_HEADS, V_ROWS, s), BF16),
            jax.ShapeDtypeStruct((bsz, s // tile_rows, 1, LANE), F32),
            jax.ShapeDtypeStruct((2 * d, d), BF16),
        ],
        grid=(bsz, s // tm),
        in_specs=[pl.BlockSpec((1, tm, d), lambda b, i: (b, i, 0))] + const_specs,
        out_specs=[tok() for _ in range(8)] + [
            pl.BlockSpec((1, N_HEADS, 2 * HEAD, tm), lambda b, i: (b, 0, 0, i)),
            pl.BlockSpec((1, N_HEADS, tm, 2 * HEAD), lambda b, i: (b, 0, i, 0)),
            pl.BlockSpec((1, N_HEADS, V_ROWS, tm), lambda b, i: (b, 0, 0, i)),
            pl.BlockSpec((1, n_sub, 1, LANE), lambda b, i: (b, i, 0, 0)),
            pl.BlockSpec((2 * d, d), lambda b, i: (0, 0)),
        ],
        scratch_shapes=[pltpu.VMEM((1, N_SHIFT), F32), pltpu.VMEM((1, LANE), F32), pltpu.VMEM((1, LANE), F32),
                        pltpu.VMEM((N_PROJ, d), BF16), pltpu.VMEM((STAGE_SLOTS, STAGE_ROWS, d), F32),
                        pltpu.SemaphoreType.DMA((STAGE_SLOTS,))],
        compiler_params=pltpu.CompilerParams(
            dimension_semantics=("arbitrary", "arbitrary"), vmem_limit_bytes=VMEM_LIMIT),
        name="in_proj",
    )(x, *consts)


def _rwkv_kernel(r_ref, lw_ref, k_ref, v_ref, kk_ref, b_ref, g_ref, bonus_ref, lng_ref, lnb_ref, e_ref, y_ref, s_scr, *, tb):
    i = pl.program_id(1)

    @pl.when(i == 0)
    def _():
        s_scr[...] = jnp.zeros_like(s_scr)

    c = CHUNK
    pw = 2 * HEAD
    n_chunk = tb // c
    n_pair = N_HEADS // 2
    rr = lax.broadcasted_iota(jnp.int32, (c, pw), 0)
    ll = lax.broadcasted_iota(jnp.int32, (c, pw), 1)
    cc = ll % c
    lo = ll < c
    strict = cc < rr
    incl = cc <= rr
    eye = (cc == rr).astype(F32)
    r2 = lax.broadcasted_iota(jnp.int32, (2 * c, pw), 0)
    l2 = lax.broadcasted_iota(jnp.int32, (2 * c, pw), 1)
    lo2 = l2 < c
    diag2 = (r2 < c) == lo2
    zeros = jnp.zeros((c, pw), F32)

    def level_mask(n):
        return ((rr // (2 * n)) == (cc // (2 * n))) & (((rr // n) % 2) == 1) & (((cc // n) % 2) == 0)

    def bd(x):
        return jnp.where(diag2, jnp.concatenate([x, x], axis=0), 0.0).astype(BF16)

    def bd_sw(x):
        return jnp.where(diag2, 0.0, jnp.concatenate([x, x], axis=0)).astype(BF16)

    slab = min(tb, MXU_DIM)
    rb = lax.broadcasted_iota(jnp.int32, (slab, slab), 0)
    cb = lax.broadcasted_iota(jnp.int32, (slab, slab), 1)
    tri = jnp.where((cb <= rb) & ((rb // c) == (cb // c)), 1.0, 0.0).astype(BF16)
    lw = lw_ref[0]
    cs = jnp.concatenate([_cumsum_rows(tri, lw[j:j + slab]) for j in range(0, tb, slab)], axis=0)
    w_inv = jnp.exp(-cs)
    rt = r_ref[0] * jnp.exp(cs)
    at = -kk_ref[0] * jnp.exp(cs - lw)
    bi = b_ref[0] * w_inv
    ki = k_ref[0] * w_inv
    v = v_ref[0].astype(F32)

    chains = [(j, p) for j in range(n_chunk) for p in range(n_pair)]
    cut = lambda x, j, p: x[j * c:(j + 1) * c, p * pw:(p + 1) * pw]
    w_tot = [jnp.exp(cs[(j + 1) * c - 1:(j + 1) * c, :]) for j in range(n_chunk)]
    a_t = [cut(at, j, p) for j, p in chains]
    r_t = [cut(rt, j, p) for j, p in chains]
    b_i = [cut(bi, j, p) for j, p in chains]
    k_i = [cut(ki, j, p) for j, p in chains]
    v_p = [cut(v, j, p) for j, p in chains]
    n = len(chains)

    a_ab, a_ak, a_rb, a_rk = [], [], [], []
    for q in range(n):
        ar = jnp.concatenate([a_t[q], r_t[q]], axis=0)
        gx = _dot(jnp.where(lo2, ar, 0.0).astype(BF16), jnp.concatenate([b_i[q], k_i[q]], axis=0).astype(BF16), _NT)
        gy = _dot(jnp.where(lo2, 0.0, ar).astype(BF16), jnp.concatenate([k_i[q], b_i[q]], axis=0).astype(BF16), _NT)
        a_ab.append(jnp.where(strict, jnp.where(lo, gx[:c], gy[:c]), 0.0))
        a_ak.append(jnp.where(strict, jnp.where(lo, gy[:c], gx[:c]), 0.0))
        a_rb.append(jnp.where(incl, jnp.where(lo, gx[c:], gy[c:]), 0.0))
        a_rk.append(jnp.where(incl, jnp.where(lo, gy[c:], gx[c:]), 0.0))

    t = [eye + jnp.where(level_mask(1), a_ab[q], 0.0) for q in range(n)]
    for m in (2, 4, 8, 16, 32):
        x = [_dot(jnp.where(level_mask(m), a_ab[q], 0.0).astype(BF16), bd(t[q])) for q in range(n)]
        t = [t[q] + _dot(t[q].astype(BF16), bd(x[q])) for q in range(n)]

    v_sw = [bd_sw(v_p[q]) for q in range(n)]
    av = [_dot(jnp.concatenate([a_ak[q], a_rk[q]], axis=0).astype(BF16), v_sw[q]) for q in range(n)]
    x = [_dot(t[q].astype(BF16), jnp.concatenate([bd(a_t[q]), bd(av[q][:c])], axis=1)) for q in range(n)]
    a_p = [x[q][:, :pw] for q in range(n)]
    u_0 = [x[q][:, pw:] for q in range(n)]
    d1 = [_dot(a_rb[q].astype(BF16), jnp.concatenate([bd(a_p[q]), bd(u_0[q])], axis=1)) for q in range(n)]
    rp = [r_t[q] + d1[q][:, :pw] for q in range(n)]
    y_0 = [d1[q][:, pw:] + av[q][c:] for q in range(n)]
    d2 = []
    for q, (j, p) in enumerate(chains):
        wt = w_tot[j][:, p * pw:(p + 1) * pw]
        lhs = jnp.concatenate([jnp.concatenate([a_p[q], u_0[q]], axis=1),
                               jnp.concatenate([zeros, v_p[q]], axis=1)], axis=0).astype(BF16)
        rhs = jnp.concatenate([b_i[q] * wt, k_i[q] * wt], axis=0).astype(BF16)
        d2.append(_dot(lhs, rhs, _TN))

    s = [s_scr[p] for p in range(n_pair)]
    ys = [None] * n
    for j in range(n_chunk):
        for p in range(n_pair):
            q = j * n_pair + p
            mt = jnp.where(diag2, d2[q][:pw], 0.0).astype(BF16)
            nt = jnp.where(lo, d2[q][pw:pw + c], d2[q][pw + c:])
            ys[q] = y_0[q] + _dot(rp[q].astype(BF16), bd(s[p]), _NT)
            s[p] = s[p] * w_tot[j][:, p * pw:(p + 1) * pw] + _dot(s[p].astype(BF16), mt) + nt
    for p in range(n_pair):
        s_scr[p] = s[p]

    y = jnp.concatenate([jnp.concatenate([ys[j * n_pair + p] for p in range(n_pair)], axis=1) for j in range(n_chunk)], axis=0)
    e = e_ref[...]
    inv_head = 1.0 / HEAD
    yc = y - _head_sum(y, e) * inv_head
    yn = yc * lax.rsqrt(_head_sum(yc * yc, e) * inv_head + LNX_EPS)
    y_ref[0] = ((yn * lng_ref[...] + lnb_ref[...] + bonus_ref[0]) * g_ref[0]).astype(BF16)


def _head_mask():
    head_id = jnp.arange(MXU_DIM) // HEAD
    return (head_id[:, None] == head_id[None, :]).astype(BF16)


def _rwkv(r, lw, k, v, kk, b, g, bonus, lnx_g, lnx_b, *, tb=512):
    bsz, s, _ = r.shape
    assert s % tb == 0 and tb % CHUNK == 0
    tok = pl.BlockSpec((1, tb, D_MIX), lambda bb, i: (bb, i, 0))
    vec = pl.BlockSpec((1, D_MIX), lambda bb, i: (0, 0))
    return pl.pallas_call(
        functools.partial(_rwkv_kernel, tb=tb),
        out_shape=jax.ShapeDtypeStruct((bsz, s, D_MIX), BF16),
        grid=(bsz, s // tb),
        in_specs=[tok] * 8 + [vec, vec, pl.BlockSpec((MXU_DIM, MXU_DIM), lambda bb, i: (0, 0))],
        out_specs=tok,
        scratch_shapes=[pltpu.VMEM((N_HEADS // 2, HEAD, 2 * HEAD), F32)],
        compiler_params=pltpu.CompilerParams(
            dimension_semantics=("parallel", "arbitrary"), vmem_limit_bytes=VMEM_LIMIT),
        name="rwkv",
    )(r, lw, k, v, kk, b, g, bonus, lnx_g.reshape(1, -1).astype(F32), lnx_b.reshape(1, -1).astype(F32), _head_mask())


def _fox_kernel(off_ref, thr_ref, q_ref, k_ref, v_ref, o_ref, *scratch, tq, tk, qw, n_kblk):
    b, hp, qb = pl.program_id(0), pl.program_id(1), pl.program_id(2)
    n_g = tq // qw
    kb_last = (qb * tq) // tk
    chains = [(hh, g) for hh in range(2) for g in range(n_g)]
    n_c = len(chains)
    s_scr = (scratch[:n_c], scratch[n_c:2 * n_c])
    m_scr, acc_scr = scratch[2 * n_c:]
    base = [(b * N_HEADS + 2 * hp + hh) * n_kblk for hh in range(2)]
    off_d = [off_ref[base[hh] + kb_last] for hh in range(2)]

    thr = thr_ref[0]

    def first_needed(g):
        last = kb_last + g
        ref = [off_ref[base[hh] + last] for hh in range(2)]

        def needed(j):
            nxt = jnp.minimum(j + 1, last)
            return jnp.minimum(off_ref[base[0] + nxt] - ref[0], off_ref[base[1] + nxt] - ref[1]) <= thr

        return lax.while_loop(lambda j: jnp.logical_and(j > 0, needed(j - 1)), lambda j: j - 1, last)

    assert n_g <= 2
    kb_start = functools.reduce(jnp.minimum, [first_needed(g) - g for g in range(n_g)])
    lead = kb_start < 0
    kb_first = jnp.maximum(kb_start, 0)
    n_open = kb_last - kb_first

    def qk(kb, slot, groups=None):
        for c, (hh, g) in enumerate(chains):
            if groups is None or g in groups:
                ksl = pl.ds(pl.multiple_of((kb + g) * tk, tk), tk)
                s_scr[slot][c][...] = _dot(k_ref[0, hh, ksl, :], q_ref[0, hh, :, g * qw:(g + 1) * qw])

    def soft_pv(kb, slot, masked, groups=None):
        for c, (hh, g) in enumerate(chains):
            if groups is not None and g not in groups:
                continue
            ksl = pl.ds(pl.multiple_of((kb + g) * tk, tk), tk)
            s = s_scr[slot][c][...]
            if masked:
                key = lax.broadcasted_iota(jnp.int32, (tk, qw), 0)
                qry = lax.broadcasted_iota(jnp.int32, (tk, qw), 1)
                s = jnp.where(key <= qry, s, NEG_BIG)
            off = off_ref[base[hh] + kb + g] - off_d[hh]
            m = m_scr[c]
            m_new = jnp.maximum(m, jnp.max(s, axis=0, keepdims=True) - off)
            p = jnp.exp2(s - (m_new + off)).astype(BF16)
            acc_scr[c] = jnp.exp2(m - m_new) * acc_scr[c] + _dot(v_ref[0, hh, :, ksl], p)
            m_scr[c] = m_new

    m_scr[...] = jnp.full(m_scr.shape, NEG_BIG, F32)
    acc_scr[...] = jnp.zeros(acc_scr.shape, F32)

    @pl.when(lead)
    def _():
        qk(-1, 1, groups=(1,))
        soft_pv(-1, 1, False, groups=(1,))

    qk(kb_first, 0)

    def pair(i, _):
        kb = kb_first + 2 * i
        qk(kb + 1, 1)
        soft_pv(kb, 0, False)
        qk(kb + 2, 0)
        soft_pv(kb + 1, 1, False)
        return 0

    lax.fori_loop(0, n_open // 2, pair, 0)

    @pl.when(n_open % 2 == 1)
    def _():
        qk(kb_last, 1)
        soft_pv(kb_last - 1, 0, False)
        soft_pv(kb_last, 1, True)

    @pl.when(n_open % 2 == 0)
    def _():
        soft_pv(kb_last, 0, True)

    acc = [acc_scr[c] for c in range(n_c)]
    o_t = jnp.concatenate([jnp.concatenate([acc[c][:HEAD] / acc[c][HEAD:HEAD + 1] for c, (h2, _) in enumerate(chains)
                                            if h2 == hh], axis=1) for hh in range(2)], axis=0)
    o_ref[0] = o_t.T.astype(BF16)


def _fox(q_t, k_aug, v_t, off, q_norm, k_norm, *, tq=512, tk=FOX_KBLOCK, qw=MXU_DIM):
    bsz, nh, s, _ = k_aug.shape
    n_kblk = s // tk
    n_chain = 2 * (tq // qw)
    assert tq % qw == 0 and tk == qw
    assert s % tq == 0 and s % tk == 0 and nh % 2 == 0 and off.shape == (bsz * nh * n_kblk,)
    qk_bound = 1.02 * LOG2E * HEAD ** 0.5 * jnp.max(jnp.abs(q_norm)) * jnp.max(jnp.abs(k_norm))
    thr = (2.0 * qk_bound + EXP2_UNDERFLOW).reshape(1).astype(F32)
    return pl.pallas_call(
        functools.partial(_fox_kernel, tq=tq, tk=tk, qw=qw, n_kblk=n_kblk),
        out_shape=jax.ShapeDtypeStruct((bsz, s, nh * HEAD), BF16),
        grid=(bsz, nh // 2, s // tq),
        in_specs=[
            pl.BlockSpec(memory_space=pltpu.SMEM),
            pl.BlockSpec(memory_space=pltpu.SMEM),
            pl.BlockSpec((1, 2, 2 * HEAD, tq), lambda b, hp, i: (b, hp, 0, i)),
            pl.BlockSpec((1, 2, s, 2 * HEAD), lambda b, hp, i: (b, hp, 0, 0)),
            pl.BlockSpec((1, 2, V_ROWS, s), lambda b, hp, i: (b, hp, 0, 0)),
        ],
        out_specs=pl.BlockSpec((1, tq, 2 * HEAD), lambda b, hp, i: (b, i, hp)),
        scratch_shapes=[pltpu.VMEM((tk, qw), F32)] * (2 * n_chain)
        + [pltpu.VMEM((n_chain, 1, qw), F32), pltpu.VMEM((n_chain, V_ROWS, qw), F32)],
        compiler_params=pltpu.CompilerParams(
            dimension_semantics=("parallel", "parallel", "arbitrary"), vmem_limit_bytes=VMEM_LIMIT),
        name="fox",
    )(off, thr, q_t, k_aug, v_t)


def _merge_kernel(x_ref, g_ref, wg_ref, ya_ref, yb_ref, pa_ref, pb_ref, wo_ref, o_ref):
    x = x_ref[...]
    d = x.shape[-1]
    h = _rms(x, g_ref[...]).astype(BF16)
    gates = _sigmoid(_dot(h, wg_ref[...], _NT))
    ya = _dot(ya_ref[...], pa_ref[...])
    yb = _dot(yb_ref[...], pb_ref[...])
    mix = (gates[:, :d] * ya + gates[:, d:] * yb).astype(BF16)
    o_ref[...] = x + _dot(mix, wo_ref[...])


def _merge(x2d, mix_norm, w_gates, ya, yb, p_a, p_b, w_out, *, tm=512):
    m, d = x2d.shape
    assert m % tm == 0
    full = lambda t: pl.BlockSpec(t.shape, lambda i: (0, 0))
    tok = lambda n: pl.BlockSpec((tm, n), lambda i: (i, 0))
    consts = [mix_norm.reshape(1, d).astype(F32), w_gates.astype(BF16)]
    mats = [p_a.astype(BF16), p_b.astype(BF16), w_out.astype(BF16)]
    return pl.pallas_call(
        _merge_kernel,
        out_shape=jax.ShapeDtypeStruct((m, d), F32),
        grid=(m // tm,),
        in_specs=[tok(d), full(consts[0]), full(consts[1]), tok(D_MIX), tok(D_MIX)] + [full(t) for t in mats],
        out_specs=tok(d),
        compiler_params=pltpu.CompilerParams(dimension_semantics=("parallel",), vmem_limit_bytes=VMEM_LIMIT),
        name="merge",
    )(x2d, consts[0], consts[1], ya, yb, *mats)


def kernel(x, ffn1_norm, ffn1_w1, ffn1_w3, ffn1_w2, mix_norm, w_in, shift_mu, rwkv_w0, rwkv_w_up, rwkv_a0, rwkv_a_up, rwkv_g_up, rwkv_k_k, rwkv_k_a, rwkv_r_k, rwkv_lnx_g, rwkv_lnx_b, rwkv_proj, fox_f_bias, fox_q_norm, fox_k_norm, fox_proj, w_out, ffn2_norm, ffn2_w1, ffn2_w3, ffn2_w2, final_norm):
    bsz, s, d = x.shape
    depth = ffn1_norm.shape[0]
    x2d = x.reshape(bsz * s, d)
    for l in range(depth):
        x2d = _ffn(x2d, ffn1_norm[l], ffn1_w1[l], ffn1_w3[l], ffn1_w2[l], final_norm, final_norm=False)
        (r, lw, k, v, kk, b, g, bonus, fq_t, fk_aug, fv_t, off, w_gates) = _in_proj(
            x2d.reshape(bsz, s, d), mix_norm[l], w_in[l], shift_mu[l], rwkv_w0[l], rwkv_w_up[l], rwkv_a0[l],
            rwkv_a_up[l], rwkv_g_up[l], rwkv_k_k[l], rwkv_k_a[l], rwkv_r_k[l], fox_f_bias[l], fox_q_norm[l],
            fox_k_norm[l])
        ya = _rwkv(r, lw, k, v, kk, b, g, bonus, rwkv_lnx_g[l], rwkv_lnx_b[l])
        off = off[:, ::FOX_KBLOCK // (s // off.shape[1]), 0, :N_HEADS]
        off = jnp.transpose(off, (0, 2, 1)).reshape(-1)
        yb = _fox(fq_t, fk_aug, fv_t, off, fox_q_norm[l], fox_k_norm[l])
        x2d = _merge(x2d, mix_norm[l], w_gates, ya.reshape(bsz * s, D_MIX), yb.reshape(bsz * s, D_MIX),
                     rwkv_proj[l], fox_proj[l], w_out[l])
        x2d = _ffn(x2d, ffn2_norm[l], ffn2_w1[l], ffn2_w3[l], ffn2_w2[l], final_norm, final_norm=(l == depth - 1))
    return x2d.reshape(bsz, s, d)
```

```python
import functools

import jax
import jax.numpy as jnp
from jax import lax
from jax.experimental import pallas as pl
from jax.experimental.pallas import tpu as pltpu

F32 = jnp.float32
BF16 = jnp.bfloat16

HEAD = 64
N_HEADS = 8
D_MIX = HEAD * N_HEADS
LANE = 128
MXU_DIM = 256
CHUNK = 64
RMS_EPS = 1e-6
LNX_EPS = 64e-5
NEG_BIG = -1e30
EXP2_UNDERFLOW = 160.0
VMEM_LIMIT = 56 * 1024 * 1024

_NT = (((1,), (1,)), ((), ()))
_TN = (((0,), (0,)), ((), ()))


def _dot(a, b, dims=None):
    if dims is None:
        return jnp.dot(a, b, preferred_element_type=F32)
    return lax.dot_general(a, b, dims, preferred_element_type=F32)


def _bdot(a, b, dims=None):
    return _dot(a.astype(BF16), b.astype(BF16), dims)


def _rms(x, g):
    return x * lax.rsqrt(jnp.mean(x * x, axis=-1, keepdims=True) + RMS_EPS) * g


def _softplus(z):
    return jnp.maximum(z, 0.0) + jnp.log(1.0 + jnp.exp(-jnp.abs(z)))


def _sigmoid(z):
    return 1.0 / (1.0 + jnp.exp(-z))


def _head_sum(x, e):
    xb = x.astype(BF16)
    return jnp.concatenate([_dot(xb[:, j:j + MXU_DIM], e) for j in range(0, x.shape[1], MXU_DIM)], axis=1)


def _cumsum_rows(tri, x):
    hi = x.astype(BF16)
    r1 = x - hi.astype(F32)
    mid = r1.astype(BF16)
    lo = (r1 - mid.astype(F32)).astype(BF16)
    return _dot(tri, hi) + _dot(tri, mid) + _dot(tri, lo)


WEIGHT_CHUNKS = 8
PIECE_ROWS = 512
STAGE_ROWS = 640
STAGE_SLOTS = 3


def _load_cast(w_hbm, w_ref, stage, sem):
    rows = w_hbm.shape[0] // WEIGHT_CHUNKS

    def copy(c):
        slot = c % STAGE_SLOTS
        return pltpu.make_async_copy(w_hbm.at[pl.ds(c * rows, rows)], stage.at[slot], sem.at[slot])

    for c in range(STAGE_SLOTS - 1):
        copy(c).start()
    for c in range(WEIGHT_CHUNKS):
        if c + STAGE_SLOTS - 1 < WEIGHT_CHUNKS:
            copy(c + STAGE_SLOTS - 1).start()
        copy(c).wait()
        w_ref[pl.ds(c * rows, rows), :] = stage[c % STAGE_SLOTS].astype(BF16)


def _ffn_kernel(x_ref, g_ref, w1_hbm, w3_hbm, w2_hbm, gf_ref, o_ref, w1_ref, w3_ref, w2_ref, stage_up, stage_down, sem,
                *, final_norm):
    @pl.when(pl.program_id(0) == 0)
    def _():
        _load_cast(w1_hbm, w1_ref, stage_up, sem)
        _load_cast(w3_hbm, w3_ref, stage_up, sem)
        _load_cast(w2_hbm, w2_ref, stage_down, sem)

    x = x_ref[...]
    h = _rms(x, g_ref[...]).astype(BF16)
    a = _dot(h, w1_ref[...])
    b = _dot(h, w3_ref[...])
    u = (a * _sigmoid(a) * b).astype(BF16)
    y = x + 0.5 * _dot(u, w2_ref[...])
    if final_norm:
        y = _rms(y, gf_ref[...])
    o_ref[...] = y


def _resident(shape):
    return pl.BlockSpec(shape, lambda *_: (0,) * len(shape), pipeline_mode=pl.Buffered(1))


def _ffn(x2d, g, w1, w3, w2, gf, *, final_norm, tm=512):
    m, d = x2d.shape
    f_dim = w1.shape[1]
    assert m % tm == 0 and d % WEIGHT_CHUNKS == 0 and f_dim % WEIGHT_CHUNKS == 0
    in_hbm = pl.BlockSpec(memory_space=pl.ANY)
    return pl.pallas_call(
        functools.partial(_ffn_kernel, final_norm=final_norm),
        out_shape=jax.ShapeDtypeStruct((m, d), F32),
        grid=(m // tm,),
        in_specs=[pl.BlockSpec((tm, d), lambda i: (i, 0)), _resident((1, d)), in_hbm, in_hbm, in_hbm, _resident((1, d))],
        out_specs=pl.BlockSpec((tm, d), lambda i: (i, 0)),
        scratch_shapes=[
            pltpu.VMEM((d, f_dim), BF16), pltpu.VMEM((d, f_dim), BF16), pltpu.VMEM((f_dim, d), BF16),
            pltpu.VMEM((STAGE_SLOTS, d // WEIGHT_CHUNKS, f_dim), F32),
            pltpu.VMEM((STAGE_SLOTS, f_dim // WEIGHT_CHUNKS, d), F32),
            pltpu.SemaphoreType.DMA((STAGE_SLOTS,)),
        ],
        compiler_params=pltpu.CompilerParams(dimension_semantics=("arbitrary",), vmem_limit_bytes=VMEM_LIMIT),
        name="ffn",
    )(x2d, g.reshape(1, d), w1, w3, w2, gf.reshape(1, d))


N_SHIFT = 3 * D_MIX + 3 * LANE
N_PROJ = N_SHIFT + 3 * D_MIX + LANE
LOG2E = 1.4426950408889634
V_ROWS = HEAD + 16
FOX_KBLOCK = 256


def _in_proj_kernel(x_ref, g_ref, w_hbm, mu_ref, w0_ref, wup_ref, a0_ref, aup_ref, gup_ref, kk_ref, ka_ref,
                    rk_ref, fb_ref, qn_ref, kn_ref, e_ref,
                    r_out, lw_out, k_out, v_out, kk_out, b_out, g_out, bonus_out, fq_out, fk_out, fv_out, off_out,
                    wg_out, carry_p, carry_c, carry_b, w_ref, stage, sem, *, tm, n_sub, tiles_per_kblock, chunks):
    step = pl.program_id(1)

    @pl.when(jnp.logical_and(pl.program_id(0) == 0, step == 0))
    def _():
        w_ref[...] = jnp.zeros_like(w_ref)

        def copy(i):
            src, n = chunks[i][:2]
            slot = i % STAGE_SLOTS
            return pltpu.make_async_copy(w_hbm.at[pl.ds(src, n)], stage.at[slot, pl.ds(0, n)], sem.at[slot])

        for i in range(min(STAGE_SLOTS - 1, len(chunks))):
            copy(i).start()
        for i, (_, _, stores) in enumerate(chunks):
            if i + STAGE_SLOTS - 1 < len(chunks):
                copy(i + STAGE_SLOTS - 1).start()
            copy(i).wait()
            for at, n, to_gates, dst in stores:
                (wg_out if to_gates else w_ref)[pl.ds(dst, n), :] = stage[i % STAGE_SLOTS, at:at + n].astype(BF16)

    @pl.when(step == 0)
    def _():
        carry_p[...] = jnp.zeros_like(carry_p)
        carry_c[...] = jnp.zeros_like(carry_c)
        carry_b[...] = jnp.zeros_like(carry_b)

    outs = (r_out, lw_out, k_out, v_out, kk_out, b_out, g_out, bonus_out)
    for t in range(n_sub):
        rows = pl.ds(t * tm, tm)
        _in_proj_tile(step * n_sub + t, x_ref.at[0, rows], g_ref, w_ref, mu_ref, w0_ref, wup_ref, a0_ref, aup_ref, gup_ref,
                      kk_ref, ka_ref, rk_ref, fb_ref, qn_ref, kn_ref, e_ref, *[o.at[0, rows] for o in outs],
                      fq_out.at[0, :, :, rows], fk_out.at[0, :, rows], fv_out.at[0, :, :, rows], off_out.at[0, t],
                      carry_p, carry_c, carry_b, tm=tm, tiles_per_kblock=tiles_per_kblock)


def _in_proj_tile(i, x_ref, g_ref, w_ref, mu_ref, w0_ref, wup_ref, a0_ref, aup_ref, gup_ref, kk_ref, ka_ref,
                  rk_ref, fb_ref, qn_ref, kn_ref, e_ref,
                  r_out, lw_out, k_out, v_out, kk_out, b_out, g_out, bonus_out, fq_out, fk_out, fv_out, off_out,
                  carry_p, carry_c, carry_b, *, tm, tiles_per_kblock):
    h = _rms(x_ref[...], g_ref[...]).astype(BF16)
    p = _dot(h, w_ref[...], _NT)

    ps = p[:, :N_SHIFT]
    row = lax.broadcasted_iota(jnp.int32, (tm, 1), 0)
    prev = jnp.where(row == 0, carry_p[...], pltpu.roll(ps, 1, axis=0))
    carry_p[...] = ps[tm - 1:tm, :]
    sh = ps + (prev - ps) * mu_ref[...]

    e = e_ref[...]
    r = sh[:, 0:D_MIX]
    k = sh[:, D_MIX:2 * D_MIX]
    v = sh[:, 2 * D_MIX:3 * D_MIX]
    w_lo = sh[:, 3 * D_MIX:3 * D_MIX + LANE]
    a_lo = sh[:, 3 * D_MIX + LANE:3 * D_MIX + 2 * LANE]
    g_lo = sh[:, 3 * D_MIX + 2 * LANE:N_SHIFT]

    w_log = -_softplus(-(w0_ref[...] + _bdot(jnp.tanh(w_lo), wup_ref[...]))) - 0.5
    lw = -jnp.exp(w_log)
    a = _sigmoid(a0_ref[...] + _bdot(a_lo, aup_ref[...]))
    g = _bdot(_sigmoid(g_lo), gup_ref[...])
    kk = k * kk_ref[...]
    kk = kk * lax.rsqrt(jnp.maximum(_head_sum(kk * kk, e), 1e-24))
    k_mod = k * (1.0 + (a - 1.0) * ka_ref[...])
    bonus = _head_sum(r * k_mod * rk_ref[...], e) * v

    r_out[...] = r
    lw_out[...] = lw
    k_out[...] = k_mod
    v_out[...] = v.astype(BF16)
    kk_out[...] = kk
    b_out[...] = kk * a
    g_out[...] = g.astype(BF16)
    bonus_out[...] = bonus.astype(BF16)

    fq = p[:, N_SHIFT:N_SHIFT + D_MIX]
    fk = p[:, N_SHIFT + D_MIX:N_SHIFT + 2 * D_MIX]
    fv = p[:, N_SHIFT + 2 * D_MIX:N_SHIFT + 3 * D_MIX]
    f_lo = p[:, N_SHIFT + 3 * D_MIX:N_PROJ]
    inv_head = 1.0 / HEAD
    qn = fq * lax.rsqrt(_head_sum(fq * fq, e) * inv_head + RMS_EPS) * qn_ref[...] * (LOG2E * HEAD ** -0.5)
    kn = fk * lax.rsqrt(_head_sum(fk * fk, e) * inv_head + RMS_EPS) * kn_ref[...]
    z = f_lo + fb_ref[...]
    log_f = jnp.minimum(z, 0.0) - jnp.log(1.0 + jnp.exp(-jnp.abs(z)))
    tri = jnp.where(lax.broadcasted_iota(jnp.int32, (tm, tm), 1) <= lax.broadcasted_iota(jnp.int32, (tm, tm), 0),
                    1.0, 0.0).astype(BF16)
    c_loc = _cumsum_rows(tri, log_f)
    off_out[...] = carry_c[...] * LOG2E
    in_blk = jnp.where(i % tiles_per_kblock == 0, 0.0, carry_b[...])
    carry_c[...] = carry_c[...] + c_loc[tm - 1:tm, :]
    carry_b[...] = in_blk + c_loc[tm - 1:tm, :]
    c2 = (c_loc + in_blk) * LOG2E

    lane = lax.broadcasted_iota(jnp.int32, (tm, HEAD), 1)
    sub = lax.broadcasted_iota(jnp.int32, (HEAD, tm), 0)
    q_tail = jnp.where(sub < 3, -1.0, 0.0)
    v_tail = jnp.where(lax.broadcasted_iota(jnp.int32, (V_ROWS - HEAD, tm), 0) == 0, 1.0, 0.0)
    for pr in range(N_HEADS // 2):
        ps2 = slice(pr * 2 * HEAD, (pr + 1) * 2 * HEAD)
        q_t = qn[:, ps2].T
        v_t = fv[:, ps2].T
        for half in range(2):
            hh = 2 * pr + half
            rows = slice(half * HEAD, (half + 1) * HEAD)
            fq_out[hh] = jnp.concatenate([q_t[rows], q_tail], axis=0).astype(BF16)
            fv_out[hh] = jnp.concatenate([v_t[rows], v_tail], axis=0).astype(BF16)
            c_h = c2[:, hh:hh + 1]
            c_hi = c_h.astype(BF16).astype(F32)
            c_mid = (c_h - c_hi).astype(BF16).astype(F32)
            c_lo = c_h - c_hi - c_mid
            c_cols = jnp.where(lane == 0, c_hi, jnp.where(lane == 1, c_mid, jnp.where(lane == 2, c_lo, 0.0)))
            fk_out[hh] = jnp.concatenate([kn[:, hh * HEAD:(hh + 1) * HEAD], c_cols], axis=1).astype(BF16)


def _pad_cols(w, n):
    return jnp.pad(w, ((0, 0), (0, n - w.shape[1])))


def _pad_rows(w, n):
    return jnp.pad(w, ((0, n - w.shape[0]), (0, 0)))


def _in_proj(x, mix_norm, w_in, shift_mu, w0, w_up, a0, a_up, g_up, k_k, k_a, r_k, f_bias, q_norm, k_norm, *, tm=256, n_sub=2):
    bsz, s, d = x.shape
    lw_, la_, lg_ = w_up.shape[0], a_up.shape[0], g_up.shape[0]
    c0 = 3 * D_MIX
    c1 = c0 + lw_ + la_ + lg_
    c2 = c1 + 3 * D_MIX
    c3 = c2 + N_HEADS

    def seg(t, pad):
        return jnp.concatenate([
            t[..., :c0],
            pad(t[..., c0:c0 + lw_], LANE), pad(t[..., c0 + lw_:c0 + lw_ + la_], LANE), pad(t[..., c0 + lw_ + la_:c1], LANE),
        ], axis=-1)

    n_cols = w_in.shape[1]
    groups = ((0, c0, False, 0), (c0, lw_, False, c0), (c0 + lw_, la_, False, c0 + LANE),
              (c0 + lw_ + la_, lg_, False, c0 + 2 * LANE), (c1, c2 - c1, False, N_SHIFT),
              (c2, c3 - c2, False, N_SHIFT + 3 * D_MIX), (c3, n_cols - c3, True, 0))
    pieces = [(src + o, min(PIECE_ROWS, width - o), to_gates, dst + o)
              for src, width, to_gates, dst in groups for o in range(0, width, PIECE_ROWS)]
    chunks = []
    for src, n, to_gates, dst in pieces:
        if chunks and chunks[-1][0] + chunks[-1][1] == src and chunks[-1][1] + n <= STAGE_ROWS:
            c_src, c_n, stores = chunks[-1]
            chunks[-1] = (c_src, c_n + n, stores + ((c_n, n, to_gates, dst),))
        else:
            chunks.append((src, n, ((0, n, to_gates, dst),)))
    chunks = tuple(chunks)
    mu_cat = seg(shift_mu.reshape(1, -1), _pad_cols)
    row = lambda t: t.reshape(1, -1).astype(F32)
    e = _head_mask()
    tile = lambda t: jnp.tile(t.reshape(1, HEAD), (1, N_HEADS)).astype(F32)
    consts = [
        row(mix_norm), jnp.swapaxes(w_in, 0, 1), mu_cat, row(w0), _pad_rows(w_up, LANE).astype(BF16), row(a0),
        _pad_rows(a_up, LANE).astype(BF16), _pad_rows(g_up, LANE).astype(BF16), row(k_k), row(k_a), row(r_k),
        _pad_cols(row(f_bias), LANE), tile(q_norm), tile(k_norm), e,
    ]
    const_specs = [_resident(c.shape) for c in consts]
    const_specs[1] = pl.BlockSpec(memory_space=pl.ANY)
    assert n_cols - c3 == 2 * d
    tm, tile_rows = tm * n_sub, tm
    tok = lambda: pl.BlockSpec((1, tm, D_MIX), lambda b, i: (b, i, 0))
    f32_tok = jax.ShapeDtypeStruct((bsz, s, D_MIX), F32)
    bf_tok = jax.ShapeDtypeStruct((bsz, s, D_MIX), BF16)
    assert s % tm == 0 and FOX_KBLOCK % tile_rows == 0
    return pl.pallas_call(
        functools.partial(_in_proj_kernel, tm=tile_rows, n_sub=n_sub, tiles_per_kblock=FOX_KBLOCK // tile_rows,
                          chunks=chunks),
        out_shape=[f32_tok, f32_tok, f32_tok, bf_tok, f32_tok, f32_tok, bf_tok, bf_tok] + [
            jax.ShapeDtypeStruct((bsz, N_HEADS, 2 * HEAD, s), BF16),
            jax.ShapeDtypeStruct((bsz, N_HEADS, s, 2 * HEAD), BF16),
            jax.ShapeDtypeStruct((bsz, N_HEADS, V_ROWS, s), BF16),
            jax.ShapeDtypeStruct((bsz, s // tile_rows, 1, LANE), F32),
            jax.ShapeDtypeStruct((2 * d, d), BF16),
        ],
        grid=(bsz, s // tm),
        in_specs=[pl.BlockSpec((1, tm, d), lambda b, i: (b, i, 0))] + const_specs,
        out_specs=[tok() for _ in range(8)] + [
            pl.BlockSpec((1, N_HEADS, 2 * HEAD, tm), lambda b, i: (b, 0, 0, i)),
            pl.BlockSpec((1, N_HEADS, tm, 2 * HEAD), lambda b, i: (b, 0, i, 0)),
            pl.BlockSpec((1, N_HEADS, V_ROWS, tm), lambda b, i: (b, 0, 0, i)),
            pl.BlockSpec((1, n_sub, 1, LANE), lambda b, i: (b, i, 0, 0)),
            pl.BlockSpec((2 * d, d), lambda b, i: (0, 0)),
        ],
        scratch_shapes=[pltpu.VMEM((1, N_SHIFT), F32), pltpu.VMEM((1, LANE), F32), pltpu.VMEM((1, LANE), F32),
                        pltpu.VMEM((N_PROJ, d), BF16), pltpu.VMEM((STAGE_SLOTS, STAGE_ROWS, d), F32),
                        pltpu.SemaphoreType.DMA((STAGE_SLOTS,))],
        compiler_params=pltpu.CompilerParams(
            dimension_semantics=("arbitrary", "arbitrary"), vmem_limit_bytes=VMEM_LIMIT),
        name="in_proj",
    )(x, *consts)


def _rwkv_kernel(r_ref, lw_ref, k_ref, v_ref, kk_ref, b_ref, g_ref, bonus_ref, lng_ref, lnb_ref, e_ref, y_ref, s_scr, *, tb):
    i = pl.program_id(1)

    @pl.when(i == 0)
    def _():
        s_scr[...] = jnp.zeros_like(s_scr)

    c = CHUNK
    pw = 2 * HEAD
    n_chunk = tb // c
    n_pair = N_HEADS // 2
    rr = lax.broadcasted_iota(jnp.int32, (c, pw), 0)
    ll = lax.broadcasted_iota(jnp.int32, (c, pw), 1)
    cc = ll % c
    lo = ll < c
    strict = cc < rr
    incl = cc <= rr
    eye = (cc == rr).astype(F32)
    r2 = lax.broadcasted_iota(jnp.int32, (2 * c, pw), 0)
    l2 = lax.broadcasted_iota(jnp.int32, (2 * c, pw), 1)
    lo2 = l2 < c
    diag2 = (r2 < c) == lo2
    zeros = jnp.zeros((c, pw), F32)

    def level_mask(n):
        return ((rr // (2 * n)) == (cc // (2 * n))) & (((rr // n) % 2) == 1) & (((cc // n) % 2) == 0)

    def bd(x):
        return jnp.where(diag2, jnp.concatenate([x, x], axis=0), 0.0).astype(BF16)

    def bd_sw(x):
        return jnp.where(diag2, 0.0, jnp.concatenate([x, x], axis=0)).astype(BF16)

    slab = min(tb, MXU_DIM)
    rb = lax.broadcasted_iota(jnp.int32, (slab, slab), 0)
    cb = lax.broadcasted_iota(jnp.int32, (slab, slab), 1)
    tri = jnp.where((cb <= rb) & ((rb // c) == (cb // c)), 1.0, 0.0).astype(BF16)
    lw = lw_ref[0]
    cs = jnp.concatenate([_cumsum_rows(tri, lw[j:j + slab]) for j in range(0, tb, slab)], axis=0)
    w_inv = jnp.exp(-cs)
    rt = r_ref[0] * jnp.exp(cs)
    at = -kk_ref[0] * jnp.exp(cs - lw)
    bi = b_ref[0] * w_inv
    ki = k_ref[0] * w_inv
    v = v_ref[0].astype(F32)

    chains = [(j, p) for j in range(n_chunk) for p in range(n_pair)]
    cut = lambda x, j, p: x[j * c:(j + 1) * c, p * pw:(p + 1) * pw]
    w_tot = [jnp.exp(cs[(j + 1) * c - 1:(j + 1) * c, :]) for j in range(n_chunk)]
    a_t = [cut(at, j, p) for j, p in chains]
    r_t = [cut(rt, j, p) for j, p in chains]
    b_i = [cut(bi, j, p) for j, p in chains]
    k_i = [cut(ki, j, p) for j, p in chains]
    v_p = [cut(v, j, p) for j, p in chains]
    n = len(chains)

    a_ab, a_ak, a_rb, a_rk = [], [], [], []
    for q in range(n):
        ar = jnp.concatenate([a_t[q], r_t[q]], axis=0)
        gx = _dot(jnp.where(lo2, ar, 0.0).astype(BF16), jnp.concatenate([b_i[q], k_i[q]], axis=0).astype(BF16), _NT)
        gy = _dot(jnp.where(lo2, 0.0, ar).astype(BF16), jnp.concatenate([k_i[q], b_i[q]], axis=0).astype(BF16), _NT)
        a_ab.append(jnp.where(strict, jnp.where(lo, gx[:c], gy[:c]), 0.0))
        a_ak.append(jnp.where(strict, jnp.where(lo, gy[:c], gx[:c]), 0.0))
        a_rb.append(jnp.where(incl, jnp.where(lo, gx[c:], gy[c:]), 0.0))
        a_rk.append(jnp.where(incl, jnp.where(lo, gy[c:], gx[c:]), 0.0))

    t = [eye + jnp.where(level_mask(1), a_ab[q], 0.0) for q in range(n)]
    for m in (2, 4, 8, 16, 32):
        x = [_dot(jnp.where(level_mask(m), a_ab[q], 0.0).astype(BF16), bd(t[q])) for q in range(n)]
        t = [t[q] + _dot(t[q].astype(BF16), bd(x[q])) for q in range(n)]

    v_sw = [bd_sw(v_p[q]) for q in range(n)]
    av = [_dot(jnp.concatenate([a_ak[q], a_rk[q]], axis=0).astype(BF16), v_sw[q]) for q in range(n)]
    x = [_dot(t[q].astype(BF16), jnp.concatenate([bd(a_t[q]), bd(av[q][:c])], axis=1)) for q in range(n)]
    a_p = [x[q][:, :pw] for q in range(n)]
    u_0 = [x[q][:, pw:] for q in range(n)]
    d1 = [_dot(a_rb[q].astype(BF16), jnp.concatenate([bd(a_p[q]), bd(u_0[q])], axis=1)) for q in range(n)]
    rp = [r_t[q] + d1[q][:, :pw] for q in range(n)]
    y_0 = [d1[q][:, pw:] + av[q][c:] for q in range(n)]
    d2 = []
    for q, (j, p) in enumerate(chains):
        wt = w_tot[j][:, p * pw:(p + 1) * pw]
        lhs = jnp.concatenate([jnp.concatenate([a_p[q], u_0[q]], axis=1),
                               jnp.concatenate([zeros, v_p[q]], axis=1)], axis=0).astype(BF16)
        rhs = jnp.concatenate([b_i[q] * wt, k_i[q] * wt], axis=0).astype(BF16)
        d2.append(_dot(lhs, rhs, _TN))

    s = [s_scr[p] for p in range(n_pair)]
    ys = [None] * n
    for j in range(n_chunk):
        for p in range(n_pair):
            q = j * n_pair + p
            mt = jnp.where(diag2, d2[q][:pw], 0.0).astype(BF16)
            nt = jnp.where(lo, d2[q][pw:pw + c], d2[q][pw + c:])
            ys[q] = y_0[q] + _dot(rp[q].astype(BF16), bd(s[p]), _NT)
            s[p] = s[p] * w_tot[j][:, p * pw:(p + 1) * pw] + _dot(s[p].astype(BF16), mt) + nt
    for p in range(n_pair):
        s_scr[p] = s[p]

    y = jnp.concatenate([jnp.concatenate([ys[j * n_pair + p] for p in range(n_pair)], axis=1) for j in range(n_chunk)], axis=0)
    e = e_ref[...]
    inv_head = 1.0 / HEAD
    yc = y - _head_sum(y, e) * inv_head
    yn = yc * lax.rsqrt(_head_sum(yc * yc, e) * inv_head + LNX_EPS)
    y_ref[0] = ((yn * lng_ref[...] + lnb_ref[...] + bonus_ref[0]) * g_ref[0]).astype(BF16)


def _head_mask():
    head_id = jnp.arange(MXU_DIM) // HEAD
    return (head_id[:, None] == head_id[None, :]).astype(BF16)


def _rwkv(r, lw, k, v, kk, b, g, bonus, lnx_g, lnx_b, *, tb=1024):
    bsz, s, _ = r.shape
    assert s % tb == 0 and tb % CHUNK == 0
    tok = pl.BlockSpec((1, tb, D_MIX), lambda bb, i: (bb, i, 0))
    vec = pl.BlockSpec((1, D_MIX), lambda bb, i: (0, 0))
    return pl.pallas_call(
        functools.partial(_rwkv_kernel, tb=tb),
        out_shape=jax.ShapeDtypeStruct((bsz, s, D_MIX), BF16),
        grid=(bsz, s // tb),
        in_specs=[tok] * 8 + [vec, vec, pl.BlockSpec((MXU_DIM, MXU_DIM), lambda bb, i: (0, 0))],
        out_specs=tok,
        scratch_shapes=[pltpu.VMEM((N_HEADS // 2, HEAD, 2 * HEAD), F32)],
        compiler_params=pltpu.CompilerParams(
            dimension_semantics=("parallel", "arbitrary"), vmem_limit_bytes=VMEM_LIMIT),
        name="rwkv",
    )(r, lw, k, v, kk, b, g, bonus, lnx_g.reshape(1, -1).astype(F32), lnx_b.reshape(1, -1).astype(F32), _head_mask())


def _fox_kernel(off_ref, thr_ref, q_ref, k_ref, v_ref, o_ref, *scratch, tq, tk, qw, n_kblk):
    b, hp, qb = pl.program_id(0), pl.program_id(1), pl.program_id(2)
    n_g = tq // qw
    kb_last = (qb * tq) // tk
    chains = [(hh, g) for hh in range(2) for g in range(n_g)]
    n_c = len(chains)
    s_scr = (scratch[:n_c], scratch[n_c:2 * n_c])
    m_scr, acc_scr = scratch[2 * n_c:]
    base = [(b * N_HEADS + 2 * hp + hh) * n_kblk for hh in range(2)]
    off_d = [off_ref[base[hh] + kb_last] for hh in range(2)]

    thr = thr_ref[0]

    def needed(kb):
        nxt = jnp.minimum(kb + 1, kb_last)
        return jnp.minimum(off_ref[base[0] + nxt] - off_d[0], off_ref[base[1] + nxt] - off_d[1]) <= thr

    kb_first = lax.while_loop(lambda kb: jnp.logical_and(kb > 0, needed(kb - 1)), lambda kb: kb - 1, kb_last)
    n_open = kb_last - kb_first

    def qk(kb, slot, first_group=0):
        ksl = pl.ds(pl.multiple_of(kb * tk, tk), tk)
        for c, (hh, g) in enumerate(chains):
            if g >= first_group:
                s_scr[slot][c][...] = _dot(k_ref[0, hh, ksl, :], q_ref[0, hh, :, g * qw:(g + 1) * qw])

    def soft_pv(kb, slot, diag_group=None):
        ksl = pl.ds(pl.multiple_of(kb * tk, tk), tk)
        for c, (hh, g) in enumerate(chains):
            if diag_group is not None and g < diag_group:
                continue
            s = s_scr[slot][c][...]
            if g == diag_group:
                key = lax.broadcasted_iota(jnp.int32, (tk, qw), 0)
                qry = lax.broadcasted_iota(jnp.int32, (tk, qw), 1)
                s = jnp.where(key <= qry, s, NEG_BIG)
            off = off_ref[base[hh] + kb] - off_d[hh]
            m = m_scr[c]
            m_new = jnp.maximum(m, jnp.max(s, axis=0, keepdims=True) - off)
            p = jnp.exp2(s - (m_new + off)).astype(BF16)
            acc_scr[c] = jnp.exp2(m - m_new) * acc_scr[c] + _dot(v_ref[0, hh, :, ksl], p)
            m_scr[c] = m_new

    m_scr[...] = jnp.full(m_scr.shape, NEG_BIG, F32)
    acc_scr[...] = jnp.zeros(acc_scr.shape, F32)
    qk(kb_first, 0)

    def pair(i, _):
        kb = kb_first + 2 * i
        qk(kb + 1, 1)
        soft_pv(kb, 0)
        qk(kb + 2, 0)
        soft_pv(kb + 1, 1)
        return 0

    lax.fori_loop(0, n_open // 2, pair, 0)

    def finish(blocks):
        for j, (kb, diag_group) in enumerate(blocks):
            if j + 1 < len(blocks):
                qk(blocks[j + 1][0], (j + 1) % 2, first_group=blocks[j + 1][1] or 0)
            soft_pv(kb, j % 2, diag_group)

    diagonal = [(kb_last + g, g) for g in range(n_g)]

    @pl.when(n_open % 2 == 1)
    def _():
        finish([(kb_last - 1, None)] + diagonal)

    @pl.when(n_open % 2 == 0)
    def _():
        finish(diagonal)

    acc = [acc_scr[c] for c in range(n_c)]
    o_t = jnp.concatenate([jnp.concatenate([acc[c][:HEAD] / acc[c][HEAD:HEAD + 1] for c, (h2, _) in enumerate(chains)
                                            if h2 == hh], axis=1) for hh in range(2)], axis=0)
    o_ref[0] = o_t.T.astype(BF16)


def _fox(q_t, k_aug, v_t, off, q_norm, k_norm, *, tq=512, tk=FOX_KBLOCK, qw=MXU_DIM):
    bsz, nh, s, _ = k_aug.shape
    n_kblk = s // tk
    n_chain = 2 * (tq // qw)
    assert tq % qw == 0 and tk == qw
    assert s % tq == 0 and s % tk == 0 and nh % 2 == 0 and off.shape == (bsz * nh * n_kblk,)
    qk_bound = 1.02 * LOG2E * HEAD ** 0.5 * jnp.max(jnp.abs(q_norm)) * jnp.max(jnp.abs(k_norm))
    thr = (2.0 * qk_bound + EXP2_UNDERFLOW).reshape(1).astype(F32)
    return pl.pallas_call(
        functools.partial(_fox_kernel, tq=tq, tk=tk, qw=qw, n_kblk=n_kblk),
        out_shape=jax.ShapeDtypeStruct((bsz, s, nh * HEAD), BF16),
        grid=(bsz, nh // 2, s // tq),
        in_specs=[
            pl.BlockSpec(memory_space=pltpu.SMEM),
            pl.BlockSpec(memory_space=pltpu.SMEM),
            pl.BlockSpec((1, 2, 2 * HEAD, tq), lambda b, hp, i: (b, hp, 0, i)),
            pl.BlockSpec((1, 2, s, 2 * HEAD), lambda b, hp, i: (b, hp, 0, 0)),
            pl.BlockSpec((1, 2, V_ROWS, s), lambda b, hp, i: (b, hp, 0, 0)),
        ],
        out_specs=pl.BlockSpec((1, tq, 2 * HEAD), lambda b, hp, i: (b, i, hp)),
        scratch_shapes=[pltpu.VMEM((tk, qw), F32)] * (2 * n_chain)
        + [pltpu.VMEM((n_chain, 1, qw), F32), pltpu.VMEM((n_chain, V_ROWS, qw), F32)],
        compiler_params=pltpu.CompilerParams(
            dimension_semantics=("parallel", "parallel", "arbitrary"), vmem_limit_bytes=VMEM_LIMIT),
        name="fox",
    )(off, thr, q_t, k_aug, v_t)


def _merge_kernel(x_ref, g_ref, wg_ref, ya_ref, yb_ref, pa_ref, pb_ref, wo_ref, o_ref):
    x = x_ref[...]
    d = x.shape[-1]
    h = _rms(x, g_ref[...]).astype(BF16)
    gates = _sigmoid(_dot(h, wg_ref[...], _NT))
    ya = _dot(ya_ref[...], pa_ref[...])
    yb = _dot(yb_ref[...], pb_ref[...])
    mix = (gates[:, :d] * ya + gates[:, d:] * yb).astype(BF16)
    o_ref[...] = x + _dot(mix, wo_ref[...])


def _merge(x2d, mix_norm, w_gates, ya, yb, p_a, p_b, w_out, *, tm=512):
    m, d = x2d.shape
    assert m % tm == 0
    full = lambda t: pl.BlockSpec(t.shape, lambda i: (0, 0))
    tok = lambda n: pl.BlockSpec((tm, n), lambda i: (i, 0))
    consts = [mix_norm.reshape(1, d).astype(F32), w_gates.astype(BF16)]
    mats = [p_a.astype(BF16), p_b.astype(BF16), w_out.astype(BF16)]
    return pl.pallas_call(
        _merge_kernel,
        out_shape=jax.ShapeDtypeStruct((m, d), F32),
        grid=(m // tm,),
        in_specs=[tok(d), full(consts[0]), full(consts[1]), tok(D_MIX), tok(D_MIX)] + [full(t) for t in mats],
        out_specs=tok(d),
        compiler_params=pltpu.CompilerParams(dimension_semantics=("parallel",), vmem_limit_bytes=VMEM_LIMIT),
        name="merge",
    )(x2d, consts[0], consts[1], ya, yb, *mats)


def kernel(x, ffn1_norm, ffn1_w1, ffn1_w3, ffn1_w2, mix_norm, w_in, shift_mu, rwkv_w0, rwkv_w_up, rwkv_a0, rwkv_a_up, rwkv_g_up, rwkv_k_k, rwkv_k_a, rwkv_r_k, rwkv_lnx_g, rwkv_lnx_b, rwkv_proj, fox_f_bias, fox_q_norm, fox_k_norm, fox_proj, w_out, ffn2_norm, ffn2_w1, ffn2_w3, ffn2_w2, final_norm):
    bsz, s, d = x.shape
    depth = ffn1_norm.shape[0]
    x2d = x.reshape(bsz * s, d)
    for l in range(depth):
        x2d = _ffn(x2d, ffn1_norm[l], ffn1_w1[l], ffn1_w3[l], ffn1_w2[l], final_norm, final_norm=False)
        (r, lw, k, v, kk, b, g, bonus, fq_t, fk_aug, fv_t, off, w_gates) = _in_proj(
            x2d.reshape(bsz, s, d), mix_norm[l], w_in[l], shift_mu[l], rwkv_w0[l], rwkv_w_up[l], rwkv_a0[l],
            rwkv_a_up[l], rwkv_g_up[l], rwkv_k_k[l], rwkv_k_a[l], rwkv_r_k[l], fox_f_bias[l], fox_q_norm[l],
            fox_k_norm[l])
        ya = _rwkv(r, lw, k, v, kk, b, g, bonus, rwkv_lnx_g[l], rwkv_lnx_b[l])
        off = off[:, ::FOX_KBLOCK // (s // off.shape[1]), 0, :N_HEADS]
        off = jnp.transpose(off, (0, 2, 1)).reshape(-1)
        yb = _fox(fq_t, fk_aug, fv_t, off, fox_q_norm[l], fox_k_norm[l])
        x2d = _merge(x2d, mix_norm[l], w_gates, ya.reshape(bsz * s, D_MIX), yb.reshape(bsz * s, D_MIX),
                     rwkv_proj[l], fox_proj[l], w_out[l])
        x2d = _ffn(x2d, ffn2_norm[l], ffn2_w1[l], ffn2_w3[l], ffn2_w2[l], final_norm, final_norm=(l == depth - 1))
    return x2d.reshape(bsz, s, d)
```

```python
import functools

import jax
import jax.numpy as jnp
from jax import lax
from jax.experimental import pallas as pl
from jax.experimental.pallas import tpu as pltpu

F32 = jnp.float32
BF16 = jnp.bfloat16

HEAD = 64
N_HEADS = 8
D_MIX = HEAD * N_HEADS
LANE = 128
MXU_DIM = 256
CHUNK = 64
RMS_EPS = 1e-6
LNX_EPS = 64e-5
NEG_BIG = -1e30
EXP2_UNDERFLOW = 160.0
VMEM_LIMIT = 56 * 1024 * 1024

_NT = (((1,), (1,)), ((), ()))
_TN = (((0,), (0,)), ((), ()))


def _dot(a, b, dims=None):
    if dims is None:
        return jnp.dot(a, b, preferred_element_type=F32)
    return lax.dot_general(a, b, dims, preferred_element_type=F32)


def _bdot(a, b, dims=None):
    return _dot(a.astype(BF16), b.astype(BF16), dims)


def _rms(x, g):
    return x * lax.rsqrt(jnp.mean(x * x, axis=-1, keepdims=True) + RMS_EPS) * g


def _softplus(z):
    return jnp.maximum(z, 0.0) + jnp.log(1.0 + jnp.exp(-jnp.abs(z)))


def _sigmoid(z):
    return 1.0 / (1.0 + jnp.exp(-z))


def _head_sum(x, e):
    xb = x.astype(BF16)
    return jnp.concatenate([_dot(xb[:, j:j + MXU_DIM], e) for j in range(0, x.shape[1], MXU_DIM)], axis=1)


def _cumsum_rows(tri, x):
    hi = x.astype(BF16)
    r1 = x - hi.astype(F32)
    mid = r1.astype(BF16)
    lo = (r1 - mid.astype(F32)).astype(BF16)
    return _dot(tri, hi) + _dot(tri, mid) + _dot(tri, lo)


WEIGHT_CHUNKS = 8
PIECE_ROWS = 512
STAGE_ROWS = 640
STAGE_SLOTS = 3


def _load_cast(w_hbm, w_ref, stage, sem):
    rows = w_hbm.shape[0] // WEIGHT_CHUNKS

    def copy(c):
        slot = c % STAGE_SLOTS
        return pltpu.make_async_copy(w_hbm.at[pl.ds(c * rows, rows)], stage.at[slot], sem.at[slot])

    for c in range(STAGE_SLOTS - 1):
        copy(c).start()
    for c in range(WEIGHT_CHUNKS):
        if c + STAGE_SLOTS - 1 < WEIGHT_CHUNKS:
            copy(c + STAGE_SLOTS - 1).start()
        copy(c).wait()
        w_ref[pl.ds(c * rows, rows), :] = stage[c % STAGE_SLOTS].astype(BF16)


def _ffn_kernel(x_ref, g_ref, w1_hbm, w3_hbm, w2_hbm, gf_ref, o_ref, w1_ref, w3_ref, w2_ref, stage_up, stage_down, sem,
                *, final_norm):
    @pl.when(pl.program_id(0) == 0)
    def _():
        _load_cast(w1_hbm, w1_ref, stage_up, sem)
        _load_cast(w3_hbm, w3_ref, stage_up, sem)
        _load_cast(w2_hbm, w2_ref, stage_down, sem)

    x = x_ref[...]
    h = _rms(x, g_ref[...]).astype(BF16)
    a = _dot(h, w1_ref[...])
    b = _dot(h, w3_ref[...])
    u = (a * _sigmoid(a) * b).astype(BF16)
    y = x + 0.5 * _dot(u, w2_ref[...])
    if final_norm:
        y = _rms(y, gf_ref[...])
    o_ref[...] = y


def _resident(shape):
    return pl.BlockSpec(shape, lambda *_: (0,) * len(shape), pipeline_mode=pl.Buffered(1))


def _ffn(x2d, g, w1, w3, w2, gf, *, final_norm, tm=512):
    m, d = x2d.shape
    f_dim = w1.shape[1]
    assert m % tm == 0 and d % WEIGHT_CHUNKS == 0 and f_dim % WEIGHT_CHUNKS == 0
    in_hbm = pl.BlockSpec(memory_space=pl.ANY)
    return pl.pallas_call(
        functools.partial(_ffn_kernel, final_norm=final_norm),
        out_shape=jax.ShapeDtypeStruct((m, d), F32),
        grid=(m // tm,),
        in_specs=[pl.BlockSpec((tm, d), lambda i: (i, 0)), _resident((1, d)), in_hbm, in_hbm, in_hbm, _resident((1, d))],
        out_specs=pl.BlockSpec((tm, d), lambda i: (i, 0)),
        scratch_shapes=[
            pltpu.VMEM((d, f_dim), BF16), pltpu.VMEM((d, f_dim), BF16), pltpu.VMEM((f_dim, d), BF16),
            pltpu.VMEM((STAGE_SLOTS, d // WEIGHT_CHUNKS, f_dim), F32),
            pltpu.VMEM((STAGE_SLOTS, f_dim // WEIGHT_CHUNKS, d), F32),
            pltpu.SemaphoreType.DMA((STAGE_SLOTS,)),
        ],
        compiler_params=pltpu.CompilerParams(dimension_semantics=("arbitrary",), vmem_limit_bytes=VMEM_LIMIT),
        name="ffn",
    )(x2d, g.reshape(1, d), w1, w3, w2, gf.reshape(1, d))


N_SHIFT = 3 * D_MIX + 3 * LANE
N_PROJ = N_SHIFT + 3 * D_MIX + LANE
LOG2E = 1.4426950408889634
V_ROWS = HEAD + 16
FOX_KBLOCK = 256


def _in_proj_kernel(x_ref, g_ref, w_hbm, mu_ref, w0_ref, wup_ref, a0_ref, aup_ref, gup_ref, kk_ref, ka_ref,
                    rk_ref, fb_ref, qn_ref, kn_ref, e_ref,
                    r_out, lw_out, k_out, v_out, kk_out, b_out, g_out, bonus_out, fq_out, fk_out, fv_out, off_out,
                    wg_out, carry_p, carry_c, carry_b, w_ref, stage, sem, *, tm, n_sub, tiles_per_kblock, chunks):
    step = pl.program_id(1)

    @pl.when(jnp.logical_and(pl.program_id(0) == 0, step == 0))
    def _():
        w_ref[...] = jnp.zeros_like(w_ref)

        def copy(i):
            src, n = chunks[i][:2]
            slot = i % STAGE_SLOTS
            return pltpu.make_async_copy(w_hbm.at[pl.ds(src, n)], stage.at[slot, pl.ds(0, n)], sem.at[slot])

        for i in range(min(STAGE_SLOTS - 1, len(chunks))):
            copy(i).start()
        for i, (_, _, stores) in enumerate(chunks):
            if i + STAGE_SLOTS - 1 < len(chunks):
                copy(i + STAGE_SLOTS - 1).start()
            copy(i).wait()
            for at, n, to_gates, dst in stores:
                (wg_out if to_gates else w_ref)[pl.ds(dst, n), :] = stage[i % STAGE_SLOTS, at:at + n].astype(BF16)

    @pl.when(step == 0)
    def _():
        carry_p[...] = jnp.zeros_like(carry_p)
        carry_c[...] = jnp.zeros_like(carry_c)
        carry_b[...] = jnp.zeros_like(carry_b)

    outs = (r_out, lw_out, k_out, v_out, kk_out, b_out, g_out, bonus_out)
    for t in range(n_sub):
        rows = pl.ds(t * tm, tm)
        _in_proj_tile(step * n_sub + t, x_ref.at[0, rows], g_ref, w_ref, mu_ref, w0_ref, wup_ref, a0_ref, aup_ref, gup_ref,
                      kk_ref, ka_ref, rk_ref, fb_ref, qn_ref, kn_ref, e_ref, *[o.at[0, rows] for o in outs],
                      fq_out.at[0, :, :, rows], fk_out.at[0, :, rows], fv_out.at[0, :, :, rows], off_out.at[0, t],
                      carry_p, carry_c, carry_b, tm=tm, tiles_per_kblock=tiles_per_kblock)


def _in_proj_tile(i, x_ref, g_ref, w_ref, mu_ref, w0_ref, wup_ref, a0_ref, aup_ref, gup_ref, kk_ref, ka_ref,
                  rk_ref, fb_ref, qn_ref, kn_ref, e_ref,
                  r_out, lw_out, k_out, v_out, kk_out, b_out, g_out, bonus_out, fq_out, fk_out, fv_out, off_out,
                  carry_p, carry_c, carry_b, *, tm, tiles_per_kblock):
    h = _rms(x_ref[...], g_ref[...]).astype(BF16)
    p = _dot(h, w_ref[...], _NT)

    ps = p[:, :N_SHIFT]
    row = lax.broadcasted_iota(jnp.int32, (tm, 1), 0)
    prev = jnp.where(row == 0, carry_p[...], pltpu.roll(ps, 1, axis=0))
    carry_p[...] = ps[tm - 1:tm, :]
    sh = ps + (prev - ps) * mu_ref[...]

    e = e_ref[...]
    r = sh[:, 0:D_MIX]
    k = sh[:, D_MIX:2 * D_MIX]
    v = sh[:, 2 * D_MIX:3 * D_MIX]
    w_lo = sh[:, 3 * D_MIX:3 * D_MIX + LANE]
    a_lo = sh[:, 3 * D_MIX + LANE:3 * D_MIX + 2 * LANE]
    g_lo = sh[:, 3 * D_MIX + 2 * LANE:N_SHIFT]

    w_log = -_softplus(-(w0_ref[...] + _bdot(jnp.tanh(w_lo), wup_ref[...]))) - 0.5
    lw = -jnp.exp(w_log)
    a = _sigmoid(a0_ref[...] + _bdot(a_lo, aup_ref[...]))
    g = _bdot(_sigmoid(g_lo), gup_ref[...])
    kk = k * kk_ref[...]
    kk = kk * lax.rsqrt(jnp.maximum(_head_sum(kk * kk, e), 1e-24))
    k_mod = k * (1.0 + (a - 1.0) * ka_ref[...])
    bonus = _head_sum(r * k_mod * rk_ref[...], e) * v

    r_out[...] = r
    lw_out[...] = lw
    k_out[...] = k_mod
    v_out[...] = v.astype(BF16)
    kk_out[...] = kk
    b_out[...] = kk * a
    g_out[...] = g.astype(BF16)
    bonus_out[...] = bonus.astype(BF16)

    fq = p[:, N_SHIFT:N_SHIFT + D_MIX]
    fk = p[:, N_SHIFT + D_MIX:N_SHIFT + 2 * D_MIX]
    fv = p[:, N_SHIFT + 2 * D_MIX:N_SHIFT + 3 * D_MIX]
    f_lo = p[:, N_SHIFT + 3 * D_MIX:N_PROJ]
    inv_head = 1.0 / HEAD
    qn = fq * lax.rsqrt(_head_sum(fq * fq, e) * inv_head + RMS_EPS) * qn_ref[...] * (LOG2E * HEAD ** -0.5)
    kn = fk * lax.rsqrt(_head_sum(fk * fk, e) * inv_head + RMS_EPS) * kn_ref[...]
    z = f_lo + fb_ref[...]
    log_f = jnp.minimum(z, 0.0) - jnp.log(1.0 + jnp.exp(-jnp.abs(z)))
    tri = jnp.where(lax.broadcasted_iota(jnp.int32, (tm, tm), 1) <= lax.broadcasted_iota(jnp.int32, (tm, tm), 0),
                    1.0, 0.0).astype(BF16)
    c_loc = _cumsum_rows(tri, log_f)
    off_out[...] = carry_c[...] * LOG2E
    in_blk = jnp.where(i % tiles_per_kblock == 0, 0.0, carry_b[...])
    carry_c[...] = carry_c[...] + c_loc[tm - 1:tm, :]
    carry_b[...] = in_blk + c_loc[tm - 1:tm, :]
    c2 = (c_loc + in_blk) * LOG2E

    lane = lax.broadcasted_iota(jnp.int32, (tm, HEAD), 1)
    sub = lax.broadcasted_iota(jnp.int32, (HEAD, tm), 0)
    q_tail = jnp.where(sub < 3, -1.0, 0.0)
    v_tail = jnp.where(lax.broadcasted_iota(jnp.int32, (V_ROWS - HEAD, tm), 0) == 0, 1.0, 0.0)
    for pr in range(N_HEADS // 2):
        ps2 = slice(pr * 2 * HEAD, (pr + 1) * 2 * HEAD)
        q_t = qn[:, ps2].T
        v_t = fv[:, ps2].T
        for half in range(2):
            hh = 2 * pr + half
            rows = slice(half * HEAD, (half + 1) * HEAD)
            fq_out[hh] = jnp.concatenate([q_t[rows], q_tail], axis=0).astype(BF16)
            fv_out[hh] = jnp.concatenate([v_t[rows], v_tail], axis=0).astype(BF16)
            c_h = c2[:, hh:hh + 1]
            c_hi = c_h.astype(BF16).astype(F32)
            c_mid = (c_h - c_hi).astype(BF16).astype(F32)
            c_lo = c_h - c_hi - c_mid
            c_cols = jnp.where(lane == 0, c_hi, jnp.where(lane == 1, c_mid, jnp.where(lane == 2, c_lo, 0.0)))
            fk_out[hh] = jnp.concatenate([kn[:, hh * HEAD:(hh + 1) * HEAD], c_cols], axis=1).astype(BF16)


def _pad_cols(w, n):
    return jnp.pad(w, ((0, 0), (0, n - w.shape[1])))


def _pad_rows(w, n):
    return jnp.pad(w, ((0, n - w.shape[0]), (0, 0)))


def _in_proj(x, mix_norm, w_in, shift_mu, w0, w_up, a0, a_up, g_up, k_k, k_a, r_k, f_bias, q_norm, k_norm, *, tm=256, n_sub=2):
    bsz, s, d = x.shape
    lw_, la_, lg_ = w_up.shape[0], a_up.shape[0], g_up.shape[0]
    c0 = 3 * D_MIX
    c1 = c0 + lw_ + la_ + lg_
    c2 = c1 + 3 * D_MIX
    c3 = c2 + N_HEADS

    def seg(t, pad):
        return jnp.concatenate([
            t[..., :c0],
            pad(t[..., c0:c0 + lw_], LANE), pad(t[..., c0 + lw_:c0 + lw_ + la_], LANE), pad(t[..., c0 + lw_ + la_:c1], LANE),
        ], axis=-1)

    n_cols = w_in.shape[1]
    groups = ((0, c0, False, 0), (c0, lw_, False, c0), (c0 + lw_, la_, False, c0 + LANE),
              (c0 + lw_ + la_, lg_, False, c0 + 2 * LANE), (c1, c2 - c1, False, N_SHIFT),
              (c2, c3 - c2, False, N_SHIFT + 3 * D_MIX), (c3, n_cols - c3, True, 0))
    pieces = [(src + o, min(PIECE_ROWS, width - o), to_gates, dst + o)
              for src, width, to_gates, dst in groups for o in range(0, width, PIECE_ROWS)]
    chunks = []
    for src, n, to_gates, dst in pieces:
        if chunks and chunks[-1][0] + chunks[-1][1] == src and chunks[-1][1] + n <= STAGE_ROWS:
            c_src, c_n, stores = chunks[-1]
            chunks[-1] = (c_src, c_n + n, stores + ((c_n, n, to_gates, dst),))
        else:
            chunks.append((src, n, ((0, n, to_gates, dst),)))
    chunks = tuple(chunks)
    mu_cat = seg(shift_mu.reshape(1, -1), _pad_cols)
    row = lambda t: t.reshape(1, -1).astype(F32)
    e = _head_mask()
    tile = lambda t: jnp.tile(t.reshape(1, HEAD), (1, N_HEADS)).astype(F32)
    consts = [
        row(mix_norm), jnp.swapaxes(w_in, 0, 1), mu_cat, row(w0), _pad_rows(w_up, LANE).astype(BF16), row(a0),
        _pad_rows(a_up, LANE).astype(BF16), _pad_rows(g_up, LANE).astype(BF16), row(k_k), row(k_a), row(r_k),
        _pad_cols(row(f_bias), LANE), tile(q_norm), tile(k_norm), e,
    ]
    const_specs = [_resident(c.shape) for c in consts]
    const_specs[1] = pl.BlockSpec(memory_space=pl.ANY)
    assert n_cols - c3 == 2 * d
    tm, tile_rows = tm * n_sub, tm
    tok = lambda: pl.BlockSpec((1, tm, D_MIX), lambda b, i: (b, i, 0))
    f32_tok = jax.ShapeDtypeStruct((bsz, s, D_MIX), F32)
    bf_tok = jax.ShapeDtypeStruct((bsz, s, D_MIX), BF16)
    assert s % tm == 0 and FOX_KBLOCK % tile_rows == 0
    return pl.pallas_call(
        functools.partial(_in_proj_kernel, tm=tile_rows, n_sub=n_sub, tiles_per_kblock=FOX_KBLOCK // tile_rows,
                          chunks=chunks),
        out_shape=[f32_tok, f32_tok, f32_tok, bf_tok, f32_tok, f32_tok, bf_tok, bf_tok] + [
            jax.ShapeDtypeStruct((bsz, N_HEADS, 2 * HEAD, s), BF16),
            jax.ShapeDtypeStruct((bsz, N_HEADS, s, 2 * HEAD), BF16),
            jax.ShapeDtypeStruct((bsz, N_HEADS, V_ROWS, s), BF16),
            jax.ShapeDtypeStruct((bsz, s // tile_rows, 1, LANE), F32),
            jax.ShapeDtypeStruct((2 * d, d), BF16),
        ],
        grid=(bsz, s // tm),
        in_specs=[pl.BlockSpec((1, tm, d), lambda b, i: (b, i, 0))] + const_specs,
        out_specs=[tok() for _ in range(8)] + [
            pl.BlockSpec((1, N_HEADS, 2 * HEAD, tm), lambda b, i: (b, 0, 0, i)),
            pl.BlockSpec((1, N_HEADS, tm, 2 * HEAD), lambda b, i: (b, 0, i, 0)),
            pl.BlockSpec((1, N_HEADS, V_ROWS, tm), lambda b, i: (b, 0, 0, i)),
            pl.BlockSpec((1, n_sub, 1, LANE), lambda b, i: (b, i, 0, 0)),
            pl.BlockSpec((2 * d, d), lambda b, i: (0, 0)),
        ],
        scratch_shapes=[pltpu.VMEM((1, N_SHIFT), F32), pltpu.VMEM((1, LANE), F32), pltpu.VMEM((1, LANE), F32),
                        pltpu.VMEM((N_PROJ, d), BF16), pltpu.VMEM((STAGE_SLOTS, STAGE_ROWS, d), F32),
                        pltpu.SemaphoreType.DMA((STAGE_SLOTS,))],
        compiler_params=pltpu.CompilerParams(
            dimension_semantics=("arbitrary", "arbitrary"), vmem_limit_bytes=VMEM_LIMIT),
        name="in_proj",
    )(x, *consts)


def _rwkv_kernel(r_ref, lw_ref, k_ref, v_ref, kk_ref, b_ref, g_ref, bonus_ref, lng_ref, lnb_ref, e_ref, y_ref, s_scr, *, tb):
    i = pl.program_id(1)

    @pl.when(i == 0)
    def _():
        s_scr[...] = jnp.zeros_like(s_scr)

    c = CHUNK
    pw = 2 * HEAD
    n_chunk = tb // c
    n_pair = N_HEADS // 2
    rr = lax.broadcasted_iota(jnp.int32, (c, pw), 0)
    ll = lax.broadcasted_iota(jnp.int32, (c, pw), 1)
    cc = ll % c
    lo = ll < c
    strict = cc < rr
    incl = cc <= rr
    eye = (cc == rr).astype(F32)
    r2 = lax.broadcasted_iota(jnp.int32, (2 * c, pw), 0)
    l2 = lax.broadcasted_iota(jnp.int32, (2 * c, pw), 1)
    lo2 = l2 < c
    diag2 = (r2 < c) == lo2
    zeros = jnp.zeros((c, pw), F32)

    def level_mask(n):
        return ((rr // (2 * n)) == (cc // (2 * n))) & (((rr // n) % 2) == 1) & (((cc // n) % 2) == 0)

    def bd(x):
        return jnp.where(diag2, jnp.concatenate([x, x], axis=0), 0.0).astype(BF16)

    def bd_sw(x):
        return jnp.where(diag2, 0.0, jnp.concatenate([x, x], axis=0)).astype(BF16)

    slab = min(tb, MXU_DIM)
    rb = lax.broadcasted_iota(jnp.int32, (slab, slab), 0)
    cb = lax.broadcasted_iota(jnp.int32, (slab, slab), 1)
    tri = jnp.where((cb <= rb) & ((rb // c) == (cb // c)), 1.0, 0.0).astype(BF16)
    lw = lw_ref[0]
    cs = jnp.concatenate([_cumsum_rows(tri, lw[j:j + slab]) for j in range(0, tb, slab)], axis=0)
    w_inv = jnp.exp(-cs)
    rt = r_ref[0] * jnp.exp(cs)
    at = -kk_ref[0] * jnp.exp(cs - lw)
    bi = b_ref[0] * w_inv
    ki = k_ref[0] * w_inv
    v = v_ref[0].astype(F32)

    chains = [(j, p) for j in range(n_chunk) for p in range(n_pair)]
    cut = lambda x, j, p: x[j * c:(j + 1) * c, p * pw:(p + 1) * pw]
    w_tot = [jnp.exp(cs[(j + 1) * c - 1:(j + 1) * c, :]) for j in range(n_chunk)]
    a_t = [cut(at, j, p) for j, p in chains]
    r_t = [cut(rt, j, p) for j, p in chains]
    b_i = [cut(bi, j, p) for j, p in chains]
    k_i = [cut(ki, j, p) for j, p in chains]
    v_p = [cut(v, j, p) for j, p in chains]
    n = len(chains)

    a_ab, a_ak, a_rb, a_rk = [], [], [], []
    for q in range(n):
        ar = jnp.concatenate([a_t[q], r_t[q]], axis=0)
        gx = _dot(jnp.where(lo2, ar, 0.0).astype(BF16), jnp.concatenate([b_i[q], k_i[q]], axis=0).astype(BF16), _NT)
        gy = _dot(jnp.where(lo2, 0.0, ar).astype(BF16), jnp.concatenate([k_i[q], b_i[q]], axis=0).astype(BF16), _NT)
        a_ab.append(jnp.where(strict, jnp.where(lo, gx[:c], gy[:c]), 0.0))
        a_ak.append(jnp.where(strict, jnp.where(lo, gy[:c], gx[:c]), 0.0))
        a_rb.append(jnp.where(incl, jnp.where(lo, gx[c:], gy[c:]), 0.0))
        a_rk.append(jnp.where(incl, jnp.where(lo, gy[c:], gx[c:]), 0.0))

    t = [eye + jnp.where(level_mask(1), a_ab[q], 0.0) for q in range(n)]
    for m in (2, 4, 8, 16, 32):
        x = [_dot(jnp.where(level_mask(m), a_ab[q], 0.0).astype(BF16), bd(t[q])) for q in range(n)]
        t = [t[q] + _dot(t[q].astype(BF16), bd(x[q])) for q in range(n)]

    v_sw = [bd_sw(v_p[q]) for q in range(n)]
    av = [_dot(jnp.concatenate([a_ak[q], a_rk[q]], axis=0).astype(BF16), v_sw[q]) for q in range(n)]
    x = [_dot(t[q].astype(BF16), jnp.concatenate([bd(a_t[q]), bd(av[q][:c])], axis=1)) for q in range(n)]
    a_p = [x[q][:, :pw] for q in range(n)]
    u_0 = [x[q][:, pw:] for q in range(n)]
    d1 = [_dot(a_rb[q].astype(BF16), jnp.concatenate([bd(a_p[q]), bd(u_0[q])], axis=1)) for q in range(n)]
    rp = [r_t[q] + d1[q][:, :pw] for q in range(n)]
    y_0 = [d1[q][:, pw:] + av[q][c:] for q in range(n)]
    d2 = []
    for q, (j, p) in enumerate(chains):
        wt = w_tot[j][:, p * pw:(p + 1) * pw]
        lhs = jnp.concatenate([jnp.concatenate([a_p[q], u_0[q]], axis=1),
                               jnp.concatenate([zeros, v_p[q]], axis=1)], axis=0).astype(BF16)
        rhs = jnp.concatenate([b_i[q] * wt, k_i[q] * wt], axis=0).astype(BF16)
        d2.append(_dot(lhs, rhs, _TN))

    s = [s_scr[p] for p in range(n_pair)]
    ys = [None] * n
    for j in range(n_chunk):
        for p in range(n_pair):
            q = j * n_pair + p
            mt = jnp.where(diag2, d2[q][:pw], 0.0).astype(BF16)
            nt = jnp.where(lo, d2[q][pw:pw + c], d2[q][pw + c:])
            ys[q] = y_0[q] + _dot(rp[q].astype(BF16), bd(s[p]), _NT)
            s[p] = s[p] * w_tot[j][:, p * pw:(p + 1) * pw] + _dot(s[p].astype(BF16), mt) + nt
    for p in range(n_pair):
        s_scr[p] = s[p]

    y = jnp.concatenate([jnp.concatenate([ys[j * n_pair + p] for p in range(n_pair)], axis=1) for j in range(n_chunk)], axis=0)
    e = e_ref[...]
    inv_head = 1.0 / HEAD
    yc = y - _head_sum(y, e) * inv_head
    yn = yc * lax.rsqrt(_head_sum(yc * yc, e) * inv_head + LNX_EPS)
    y_ref[0] = ((yn * lng_ref[...] + lnb_ref[...] + bonus_ref[0]) * g_ref[0]).astype(BF16)


def _head_mask():
    head_id = jnp.arange(MXU_DIM) // HEAD
    return (head_id[:, None] == head_id[None, :]).astype(BF16)


def _rwkv(r, lw, k, v, kk, b, g, bonus, lnx_g, lnx_b, *, tb=512):
    bsz, s, _ = r.shape
    assert s % tb == 0 and tb % CHUNK == 0
    tok = pl.BlockSpec((1, tb, D_MIX), lambda bb, i: (bb, i, 0))
    vec = pl.BlockSpec((1, D_MIX), lambda bb, i: (0, 0))
    return pl.pallas_call(
        functools.partial(_rwkv_kernel, tb=tb),
        out_shape=jax.ShapeDtypeStruct((bsz, s, D_MIX), BF16),
        grid=(bsz, s // tb),
        in_specs=[tok] * 8 + [vec, vec, pl.BlockSpec((MXU_DIM, MXU_DIM), lambda bb, i: (0, 0))],
        out_specs=tok,
        scratch_shapes=[pltpu.VMEM((N_HEADS // 2, HEAD, 2 * HEAD), F32)],
        compiler_params=pltpu.CompilerParams(
            dimension_semantics=("parallel", "arbitrary"), vmem_limit_bytes=VMEM_LIMIT),
        name="rwkv",
    )(r, lw, k, v, kk, b, g, bonus, lnx_g.reshape(1, -1).astype(F32), lnx_b.reshape(1, -1).astype(F32), _head_mask())


def _fox_kernel(off_ref, thr_ref, q_ref, k_ref, v_ref, o_ref, *scratch, tq, tk, qw, n_kblk):
    b, hp, qb = pl.program_id(0), pl.program_id(1), pl.program_id(2)
    n_g = tq // qw
    kb_last = (qb * tq) // tk
    chains = [(hh, g) for hh in range(2) for g in range(n_g)]
    n_c = len(chains)
    s_scr = (scratch[:n_c], scratch[n_c:2 * n_c])
    m_scr, acc_scr = scratch[2 * n_c:]
    base = [(b * N_HEADS + 2 * hp + hh) * n_kblk for hh in range(2)]
    off_d = [off_ref[base[hh] + kb_last] for hh in range(2)]

    thr = thr_ref[0]

    def needed(kb):
        nxt = jnp.minimum(kb + 1, kb_last)
        return jnp.minimum(off_ref[base[0] + nxt] - off_d[0], off_ref[base[1] + nxt] - off_d[1]) <= thr

    kb_first = lax.while_loop(lambda kb: jnp.logical_and(kb > 0, needed(kb - 1)), lambda kb: kb - 1, kb_last)
    n_open = kb_last - kb_first

    def qk(kb, slot, first_group=0):
        ksl = pl.ds(pl.multiple_of(kb * tk, tk), tk)
        for c, (hh, g) in enumerate(chains):
            if g >= first_group:
                s_scr[slot][c][...] = _dot(k_ref[0, hh, ksl, :], q_ref[0, hh, :, g * qw:(g + 1) * qw])

    def soft_pv(kb, slot, diag_group=None):
        ksl = pl.ds(pl.multiple_of(kb * tk, tk), tk)
        for c, (hh, g) in enumerate(chains):
            if diag_group is not None and g < diag_group:
                continue
            s = s_scr[slot][c][...]
            if g == diag_group:
                key = lax.broadcasted_iota(jnp.int32, (tk, qw), 0)
                qry = lax.broadcasted_iota(jnp.int32, (tk, qw), 1)
                s = jnp.where(key <= qry, s, NEG_BIG)
            off = off_ref[base[hh] + kb] - off_d[hh]
            m = m_scr[c]
            m_new = jnp.maximum(m, jnp.max(s, axis=0, keepdims=True) - off)
            p = jnp.exp2(s - (m_new + off)).astype(BF16)
            acc_scr[c] = jnp.exp2(m - m_new) * acc_scr[c] + _dot(v_ref[0, hh, :, ksl], p)
            m_scr[c] = m_new

    m_scr[...] = jnp.full(m_scr.shape, NEG_BIG, F32)
    acc_scr[...] = jnp.zeros(acc_scr.shape, F32)
    qk(kb_first, 0)

    def pair(i, _):
        kb = kb_first + 2 * i
        qk(kb + 1, 1)
        soft_pv(kb, 0)
        qk(kb + 2, 0)
        soft_pv(kb + 1, 1)
        return 0

    lax.fori_loop(0, n_open // 2, pair, 0)

    def finish(blocks):
        for j, (kb, diag_group) in enumerate(blocks):
            if j + 1 < len(blocks):
                qk(blocks[j + 1][0], (j + 1) % 2, first_group=blocks[j + 1][1] or 0)
            soft_pv(kb, j % 2, diag_group)

    diagonal = [(kb_last + g, g) for g in range(n_g)]

    @pl.when(n_open % 2 == 1)
    def _():
        finish([(kb_last - 1, None)] + diagonal)

    @pl.when(n_open % 2 == 0)
    def _():
        finish(diagonal)

    acc = [acc_scr[c] for c in range(n_c)]
    o_t = jnp.concatenate([jnp.concatenate([acc[c][:HEAD] / acc[c][HEAD:HEAD + 1] for c, (h2, _) in enumerate(chains)
                                            if h2 == hh], axis=1) for hh in range(2)], axis=0)
    o_ref[0] = o_t.astype(BF16)


def _fox(q_t, k_aug, v_t, off, q_norm, k_norm, *, tq=512, tk=FOX_KBLOCK, qw=MXU_DIM):
    bsz, nh, s, _ = k_aug.shape
    n_kblk = s // tk
    n_chain = 2 * (tq // qw)
    assert tq % qw == 0 and tk == qw
    assert s % tq == 0 and s % tk == 0 and nh % 2 == 0 and off.shape == (bsz * nh * n_kblk,)
    qk_bound = 1.02 * LOG2E * HEAD ** 0.5 * jnp.max(jnp.abs(q_norm)) * jnp.max(jnp.abs(k_norm))
    thr = (2.0 * qk_bound + EXP2_UNDERFLOW).reshape(1).astype(F32)
    return pl.pallas_call(
        functools.partial(_fox_kernel, tq=tq, tk=tk, qw=qw, n_kblk=n_kblk),
        out_shape=jax.ShapeDtypeStruct((bsz, nh * HEAD, s), BF16),
        grid=(bsz, nh // 2, s // tq),
        in_specs=[
            pl.BlockSpec(memory_space=pltpu.SMEM),
            pl.BlockSpec(memory_space=pltpu.SMEM),
            pl.BlockSpec((1, 2, 2 * HEAD, tq), lambda b, hp, i: (b, hp, 0, i)),
            pl.BlockSpec((1, 2, s, 2 * HEAD), lambda b, hp, i: (b, hp, 0, 0)),
            pl.BlockSpec((1, 2, V_ROWS, s), lambda b, hp, i: (b, hp, 0, 0)),
        ],
        out_specs=pl.BlockSpec((1, 2 * HEAD, tq), lambda b, hp, i: (b, hp, i)),
        scratch_shapes=[pltpu.VMEM((tk, qw), F32)] * (2 * n_chain)
        + [pltpu.VMEM((n_chain, 1, qw), F32), pltpu.VMEM((n_chain, V_ROWS, qw), F32)],
        compiler_params=pltpu.CompilerParams(
            dimension_semantics=("parallel", "parallel", "arbitrary"), vmem_limit_bytes=VMEM_LIMIT),
        name="fox",
    )(off, thr, q_t, k_aug, v_t)


def _merge_kernel(x_ref, g_ref, wg_ref, ya_ref, yb_ref, pa_ref, pb_ref, wo_ref, o_ref):
    x = x_ref[...]
    d = x.shape[-1]
    h = _rms(x, g_ref[...]).astype(BF16)
    gates = _sigmoid(_dot(h, wg_ref[...], _NT))
    ya = _dot(ya_ref[...], pa_ref[...])
    yb = _dot(yb_ref[0], pb_ref[...], _TN)
    mix = (gates[:, :d] * ya + gates[:, d:] * yb).astype(BF16)
    o_ref[...] = x + _dot(mix, wo_ref[...])


def _merge(x2d, mix_norm, w_gates, ya, yb, p_a, p_b, w_out, *, tm=512):
    m, d = x2d.shape
    tiles_per_seq = yb.shape[-1] // tm
    assert m % tm == 0 and yb.shape[-1] % tm == 0
    full = lambda t: pl.BlockSpec(t.shape, lambda i: (0, 0))
    tok = lambda n: pl.BlockSpec((tm, n), lambda i: (i, 0))
    yb_spec = pl.BlockSpec((1, yb.shape[1], tm), lambda i: (i // tiles_per_seq, 0, i % tiles_per_seq))
    consts = [mix_norm.reshape(1, d).astype(F32), w_gates.astype(BF16)]
    mats = [p_a.astype(BF16), p_b.astype(BF16), w_out.astype(BF16)]
    return pl.pallas_call(
        _merge_kernel,
        out_shape=jax.ShapeDtypeStruct((m, d), F32),
        grid=(m // tm,),
        in_specs=[tok(d), full(consts[0]), full(consts[1]), tok(D_MIX), yb_spec] + [full(t) for t in mats],
        out_specs=tok(d),
        compiler_params=pltpu.CompilerParams(dimension_semantics=("parallel",), vmem_limit_bytes=VMEM_LIMIT),
        name="merge",
    )(x2d, consts[0], consts[1], ya, yb, *mats)


def kernel(x, ffn1_norm, ffn1_w1, ffn1_w3, ffn1_w2, mix_norm, w_in, shift_mu, rwkv_w0, rwkv_w_up, rwkv_a0, rwkv_a_up, rwkv_g_up, rwkv_k_k, rwkv_k_a, rwkv_r_k, rwkv_lnx_g, rwkv_lnx_b, rwkv_proj, fox_f_bias, fox_q_norm, fox_k_norm, fox_proj, w_out, ffn2_norm, ffn2_w1, ffn2_w3, ffn2_w2, final_norm):
    bsz, s, d = x.shape
    depth = ffn1_norm.shape[0]
    x2d = x.reshape(bsz * s, d)
    for l in range(depth):
        x2d = _ffn(x2d, ffn1_norm[l], ffn1_w1[l], ffn1_w3[l], ffn1_w2[l], final_norm, final_norm=False)
        (r, lw, k, v, kk, b, g, bonus, fq_t, fk_aug, fv_t, off, w_gates) = _in_proj(
            x2d.reshape(bsz, s, d), mix_norm[l], w_in[l], shift_mu[l], rwkv_w0[l], rwkv_w_up[l], rwkv_a0[l],
            rwkv_a_up[l], rwkv_g_up[l], rwkv_k_k[l], rwkv_k_a[l], rwkv_r_k[l], fox_f_bias[l], fox_q_norm[l],
            fox_k_norm[l])
        ya = _rwkv(r, lw, k, v, kk, b, g, bonus, rwkv_lnx_g[l], rwkv_lnx_b[l])
        off = off[:, ::FOX_KBLOCK // (s // off.shape[1]), 0, :N_HEADS]
        off = jnp.transpose(off, (0, 2, 1)).reshape(-1)
        yb = _fox(fq_t, fk_aug, fv_t, off, fox_q_norm[l], fox_k_norm[l])
        x2d = _merge(x2d, mix_norm[l], w_gates, ya.reshape(bsz * s, D_MIX), yb,
                     rwkv_proj[l], fox_proj[l], w_out[l])
        x2d = _ffn(x2d, ffn2_norm[l], ffn2_w1[l], ffn2_w3[l], ffn2_w2[l], final_norm, final_norm=(l == depth - 1))
    return x2d.reshape(bsz, s, d)
```

```python
import functools

import jax
import jax.numpy as jnp
from jax import lax
from jax.experimental import pallas as pl
from jax.experimental.pallas import tpu as pltpu

F32 = jnp.float32
BF16 = jnp.bfloat16

HEAD = 64
N_HEADS = 8
D_MIX = HEAD * N_HEADS
LANE = 128
MXU_DIM = 256
CHUNK = 64
RMS_EPS = 1e-6
LNX_EPS = 64e-5
NEG_BIG = -1e30
EXP2_UNDERFLOW = 160.0
VMEM_LIMIT = 56 * 1024 * 1024

_NT = (((1,), (1,)), ((), ()))
_TN = (((0,), (0,)), ((), ()))


def _dot(a, b, dims=None):
    if dims is None:
        return jnp.dot(a, b, preferred_element_type=F32)
    return lax.dot_general(a, b, dims, preferred_element_type=F32)


def _bdot(a, b, dims=None):
    return _dot(a.astype(BF16), b.astype(BF16), dims)


def _rms(x, g):
    return x * lax.rsqrt(jnp.mean(x * x, axis=-1, keepdims=True) + RMS_EPS) * g


def _softplus(z):
    return jnp.maximum(z, 0.0) + jnp.log(1.0 + jnp.exp(-jnp.abs(z)))


def _sigmoid(z):
    return 1.0 / (1.0 + jnp.exp(-z))


def _head_sum(x, e):
    xb = x.astype(BF16)
    return jnp.concatenate([_dot(xb[:, j:j + MXU_DIM], e) for j in range(0, x.shape[1], MXU_DIM)], axis=1)


def _cumsum_rows(tri, x):
    hi = x.astype(BF16)
    r1 = x - hi.astype(F32)
    mid = r1.astype(BF16)
    lo = (r1 - mid.astype(F32)).astype(BF16)
    return _dot(tri, hi) + _dot(tri, mid) + _dot(tri, lo)


WEIGHT_CHUNKS = 8
PIECE_ROWS = 512
STAGE_ROWS = 640
STAGE_SLOTS = 3


def _load_cast(w_hbm, w_ref, stage, sem):
    rows = w_hbm.shape[0] // WEIGHT_CHUNKS

    def copy(c):
        slot = c % STAGE_SLOTS
        return pltpu.make_async_copy(w_hbm.at[pl.ds(c * rows, rows)], stage.at[slot], sem.at[slot])

    for c in range(STAGE_SLOTS - 1):
        copy(c).start()
    for c in range(WEIGHT_CHUNKS):
        if c + STAGE_SLOTS - 1 < WEIGHT_CHUNKS:
            copy(c + STAGE_SLOTS - 1).start()
        copy(c).wait()
        w_ref[pl.ds(c * rows, rows), :] = stage[c % STAGE_SLOTS].astype(BF16)


def _ffn_kernel(x_ref, g_ref, w1_hbm, w3_hbm, w2_hbm, gf_ref, o_ref, w1_ref, w3_ref, w2_ref, stage_up, stage_down, sem,
                *, final_norm):
    @pl.when(pl.program_id(0) == 0)
    def _():
        _load_cast(w1_hbm, w1_ref, stage_up, sem)
        _load_cast(w3_hbm, w3_ref, stage_up, sem)
        _load_cast(w2_hbm, w2_ref, stage_down, sem)

    x = x_ref[...]
    h = _rms(x, g_ref[...]).astype(BF16)
    a = _dot(h, w1_ref[...])
    b = _dot(h, w3_ref[...])
    u = (a * _sigmoid(a) * b).astype(BF16)
    y = x + 0.5 * _dot(u, w2_ref[...])
    if final_norm:
        y = _rms(y, gf_ref[...])
    o_ref[...] = y


def _resident(shape):
    return pl.BlockSpec(shape, lambda *_: (0,) * len(shape), pipeline_mode=pl.Buffered(1))


def _ffn(x2d, g, w1, w3, w2, gf, *, final_norm, tm=512):
    m, d = x2d.shape
    f_dim = w1.shape[1]
    assert m % tm == 0 and d % WEIGHT_CHUNKS == 0 and f_dim % WEIGHT_CHUNKS == 0
    in_hbm = pl.BlockSpec(memory_space=pl.ANY)
    return pl.pallas_call(
        functools.partial(_ffn_kernel, final_norm=final_norm),
        out_shape=jax.ShapeDtypeStruct((m, d), F32),
        grid=(m // tm,),
        in_specs=[pl.BlockSpec((tm, d), lambda i: (i, 0)), _resident((1, d)), in_hbm, in_hbm, in_hbm, _resident((1, d))],
        out_specs=pl.BlockSpec((tm, d), lambda i: (i, 0)),
        scratch_shapes=[
            pltpu.VMEM((d, f_dim), BF16), pltpu.VMEM((d, f_dim), BF16), pltpu.VMEM((f_dim, d), BF16),
            pltpu.VMEM((STAGE_SLOTS, d // WEIGHT_CHUNKS, f_dim), F32),
            pltpu.VMEM((STAGE_SLOTS, f_dim // WEIGHT_CHUNKS, d), F32),
            pltpu.SemaphoreType.DMA((STAGE_SLOTS,)),
        ],
        compiler_params=pltpu.CompilerParams(dimension_semantics=("arbitrary",), vmem_limit_bytes=VMEM_LIMIT),
        name="ffn",
    )(x2d, g.reshape(1, d), w1, w3, w2, gf.reshape(1, d))


N_SHIFT = 3 * D_MIX + 3 * LANE
N_PROJ = N_SHIFT + 3 * D_MIX + LANE
LOG2E = 1.4426950408889634
V_ROWS = HEAD + 16
FOX_KBLOCK = 256


def _in_proj_kernel(x_ref, g_ref, w_hbm, mu_ref, w0_ref, wup_ref, a0_ref, aup_ref, gup_ref, kk_ref, ka_ref,
                    rk_ref, fb_ref, qn_ref, kn_ref, e_ref,
                    r_out, lw_out, k_out, v_out, kk_out, b_out, g_out, bonus_out, fq_out, fk_out, fv_out, off_out,
                    wg_out, carry_p, carry_c, carry_b, w_ref, stage, sem, *, tm, n_sub, tiles_per_kblock, chunks):
    step = pl.program_id(1)

    @pl.when(jnp.logical_and(pl.program_id(0) == 0, step == 0))
    def _():
        w_ref[...] = jnp.zeros_like(w_ref)

        def copy(i):
            src, n = chunks[i][:2]
            slot = i % STAGE_SLOTS
            return pltpu.make_async_copy(w_hbm.at[pl.ds(src, n)], stage.at[slot, pl.ds(0, n)], sem.at[slot])

        for i in range(min(STAGE_SLOTS - 1, len(chunks))):
            copy(i).start()
        for i, (_, _, stores) in enumerate(chunks):
            if i + STAGE_SLOTS - 1 < len(chunks):
                copy(i + STAGE_SLOTS - 1).start()
            copy(i).wait()
            for at, n, to_gates, dst in stores:
                (wg_out if to_gates else w_ref)[pl.ds(dst, n), :] = stage[i % STAGE_SLOTS, at:at + n].astype(BF16)

    @pl.when(step == 0)
    def _():
        carry_p[...] = jnp.zeros_like(carry_p)
        carry_c[...] = jnp.zeros_like(carry_c)
        carry_b[...] = jnp.zeros_like(carry_b)

    outs = (r_out, lw_out, k_out, v_out, kk_out, b_out, g_out, bonus_out)
    for t in range(n_sub):
        rows = pl.ds(t * tm, tm)
        _in_proj_tile(step * n_sub + t, x_ref.at[0, rows], g_ref, w_ref, mu_ref, w0_ref, wup_ref, a0_ref, aup_ref, gup_ref,
                      kk_ref, ka_ref, rk_ref, fb_ref, qn_ref, kn_ref, e_ref, *[o.at[0, rows] for o in outs],
                      fq_out.at[0, :, rows], fk_out.at[0, :, rows], fv_out.at[0, :, :, rows], off_out.at[0, t],
                      carry_p, carry_c, carry_b, tm=tm, tiles_per_kblock=tiles_per_kblock)


def _in_proj_tile(i, x_ref, g_ref, w_ref, mu_ref, w0_ref, wup_ref, a0_ref, aup_ref, gup_ref, kk_ref, ka_ref,
                  rk_ref, fb_ref, qn_ref, kn_ref, e_ref,
                  r_out, lw_out, k_out, v_out, kk_out, b_out, g_out, bonus_out, fq_out, fk_out, fv_out, off_out,
                  carry_p, carry_c, carry_b, *, tm, tiles_per_kblock):
    h = _rms(x_ref[...], g_ref[...]).astype(BF16)
    p = _dot(h, w_ref[...], _NT)

    ps = p[:, :N_SHIFT]
    row = lax.broadcasted_iota(jnp.int32, (tm, 1), 0)
    prev = jnp.where(row == 0, carry_p[...], pltpu.roll(ps, 1, axis=0))
    carry_p[...] = ps[tm - 1:tm, :]
    sh = ps + (prev - ps) * mu_ref[...]

    e = e_ref[...]
    r = sh[:, 0:D_MIX]
    k = sh[:, D_MIX:2 * D_MIX]
    v = sh[:, 2 * D_MIX:3 * D_MIX]
    w_lo = sh[:, 3 * D_MIX:3 * D_MIX + LANE]
    a_lo = sh[:, 3 * D_MIX + LANE:3 * D_MIX + 2 * LANE]
    g_lo = sh[:, 3 * D_MIX + 2 * LANE:N_SHIFT]

    w_log = -_softplus(-(w0_ref[...] + _bdot(jnp.tanh(w_lo), wup_ref[...]))) - 0.5
    lw = -jnp.exp(w_log)
    a = _sigmoid(a0_ref[...] + _bdot(a_lo, aup_ref[...]))
    g = _bdot(_sigmoid(g_lo), gup_ref[...])
    kk = k * kk_ref[...]
    kk = kk * lax.rsqrt(jnp.maximum(_head_sum(kk * kk, e), 1e-24))
    k_mod = k * (1.0 + (a - 1.0) * ka_ref[...])
    bonus = _head_sum(r * k_mod * rk_ref[...], e) * v

    r_out[...] = r
    lw_out[...] = lw
    k_out[...] = k_mod
    v_out[...] = v.astype(BF16)
    kk_out[...] = kk
    b_out[...] = kk * a
    g_out[...] = g.astype(BF16)
    bonus_out[...] = bonus.astype(BF16)

    fq = p[:, N_SHIFT:N_SHIFT + D_MIX]
    fk = p[:, N_SHIFT + D_MIX:N_SHIFT + 2 * D_MIX]
    fv = p[:, N_SHIFT + 2 * D_MIX:N_SHIFT + 3 * D_MIX]
    f_lo = p[:, N_SHIFT + 3 * D_MIX:N_PROJ]
    inv_head = 1.0 / HEAD
    qn = fq * lax.rsqrt(_head_sum(fq * fq, e) * inv_head + RMS_EPS) * qn_ref[...] * (LOG2E * HEAD ** -0.5)
    kn = fk * lax.rsqrt(_head_sum(fk * fk, e) * inv_head + RMS_EPS) * kn_ref[...]
    z = f_lo + fb_ref[...]
    log_f = jnp.minimum(z, 0.0) - jnp.log(1.0 + jnp.exp(-jnp.abs(z)))
    tri = jnp.where(lax.broadcasted_iota(jnp.int32, (tm, tm), 1) <= lax.broadcasted_iota(jnp.int32, (tm, tm), 0),
                    1.0, 0.0).astype(BF16)
    c_loc = _cumsum_rows(tri, log_f)
    off_out[...] = carry_c[...] * LOG2E
    in_blk = jnp.where(i % tiles_per_kblock == 0, 0.0, carry_b[...])
    carry_c[...] = carry_c[...] + c_loc[tm - 1:tm, :]
    carry_b[...] = in_blk + c_loc[tm - 1:tm, :]
    c2 = (c_loc + in_blk) * LOG2E

    lane = lax.broadcasted_iota(jnp.int32, (tm, HEAD), 1)
    q_tail = jnp.where(lane < 3, -1.0, 0.0)
    v_tail = jnp.where(lax.broadcasted_iota(jnp.int32, (V_ROWS - HEAD, tm), 0) == 0, 1.0, 0.0)
    for pr in range(N_HEADS // 2):
        ps2 = slice(pr * 2 * HEAD, (pr + 1) * 2 * HEAD)
        v_t = fv[:, ps2].T
        for half in range(2):
            hh = 2 * pr + half
            rows = slice(half * HEAD, (half + 1) * HEAD)
            fq_out[hh] = jnp.concatenate([qn[:, hh * HEAD:(hh + 1) * HEAD], q_tail], axis=1).astype(BF16)
            fv_out[hh] = jnp.concatenate([v_t[rows], v_tail], axis=0).astype(BF16)
            c_h = c2[:, hh:hh + 1]
            c_hi = c_h.astype(BF16).astype(F32)
            c_mid = (c_h - c_hi).astype(BF16).astype(F32)
            c_lo = c_h - c_hi - c_mid
            c_cols = jnp.where(lane == 0, c_hi, jnp.where(lane == 1, c_mid, jnp.where(lane == 2, c_lo, 0.0)))
            fk_out[hh] = jnp.concatenate([kn[:, hh * HEAD:(hh + 1) * HEAD], c_cols], axis=1).astype(BF16)


def _pad_cols(w, n):
    return jnp.pad(w, ((0, 0), (0, n - w.shape[1])))


def _pad_rows(w, n):
    return jnp.pad(w, ((0, n - w.shape[0]), (0, 0)))


def _in_proj(x, mix_norm, w_in, shift_mu, w0, w_up, a0, a_up, g_up, k_k, k_a, r_k, f_bias, q_norm, k_norm, *, tm=256, n_sub=2):
    bsz, s, d = x.shape
    lw_, la_, lg_ = w_up.shape[0], a_up.shape[0], g_up.shape[0]
    c0 = 3 * D_MIX
    c1 = c0 + lw_ + la_ + lg_
    c2 = c1 + 3 * D_MIX
    c3 = c2 + N_HEADS

    def seg(t, pad):
        return jnp.concatenate([
            t[..., :c0],
            pad(t[..., c0:c0 + lw_], LANE), pad(t[..., c0 + lw_:c0 + lw_ + la_], LANE), pad(t[..., c0 + lw_ + la_:c1], LANE),
        ], axis=-1)

    n_cols = w_in.shape[1]
    groups = ((0, c0, False, 0), (c0, lw_, False, c0), (c0 + lw_, la_, False, c0 + LANE),
              (c0 + lw_ + la_, lg_, False, c0 + 2 * LANE), (c1, c2 - c1, False, N_SHIFT),
              (c2, c3 - c2, False, N_SHIFT + 3 * D_MIX), (c3, n_cols - c3, True, 0))
    pieces = [(src + o, min(PIECE_ROWS, width - o), to_gates, dst + o)
              for src, width, to_gates, dst in groups for o in range(0, width, PIECE_ROWS)]
    chunks = []
    for src, n, to_gates, dst in pieces:
        if chunks and chunks[-1][0] + chunks[-1][1] == src and chunks[-1][1] + n <= STAGE_ROWS:
            c_src, c_n, stores = chunks[-1]
            chunks[-1] = (c_src, c_n + n, stores + ((c_n, n, to_gates, dst),))
        else:
            chunks.append((src, n, ((0, n, to_gates, dst),)))
    chunks = tuple(chunks)
    mu_cat = seg(shift_mu.reshape(1, -1), _pad_cols)
    row = lambda t: t.reshape(1, -1).astype(F32)
    e = _head_mask()
    tile = lambda t: jnp.tile(t.reshape(1, HEAD), (1, N_HEADS)).astype(F32)
    consts = [
        row(mix_norm), jnp.swapaxes(w_in, 0, 1), mu_cat, row(w0), _pad_rows(w_up, LANE).astype(BF16), row(a0),
        _pad_rows(a_up, LANE).astype(BF16), _pad_rows(g_up, LANE).astype(BF16), row(k_k), row(k_a), row(r_k),
        _pad_cols(row(f_bias), LANE), tile(q_norm), tile(k_norm), e,
    ]
    const_specs = [_resident(c.shape) for c in consts]
    const_specs[1] = pl.BlockSpec(memory_space=pl.ANY)
    assert n_cols - c3 == 2 * d
    tm, tile_rows = tm * n_sub, tm
    tok = lambda: pl.BlockSpec((1, tm, D_MIX), lambda b, i: (b, i, 0))
    f32_tok = jax.ShapeDtypeStruct((bsz, s, D_MIX), F32)
    bf_tok = jax.ShapeDtypeStruct((bsz, s, D_MIX), BF16)
    assert s % tm == 0 and FOX_KBLOCK % tile_rows == 0
    return pl.pallas_call(
        functools.partial(_in_proj_kernel, tm=tile_rows, n_sub=n_sub, tiles_per_kblock=FOX_KBLOCK // tile_rows,
                          chunks=chunks),
        out_shape=[f32_tok, f32_tok, f32_tok, bf_tok, f32_tok, f32_tok, bf_tok, bf_tok] + [
            jax.ShapeDtypeStruct((bsz, N_HEADS, s, 2 * HEAD), BF16),
            jax.ShapeDtypeStruct((bsz, N_HEADS, s, 2 * HEAD), BF16),
            jax.ShapeDtypeStruct((bsz, N_HEADS, V_ROWS, s), BF16),
            jax.ShapeDtypeStruct((bsz, s // tile_rows, 1, LANE), F32),
            jax.ShapeDtypeStruct((2 * d, d), BF16),
        ],
        grid=(bsz, s // tm),
        in_specs=[pl.BlockSpec((1, tm, d), lambda b, i: (b, i, 0))] + const_specs,
        out_specs=[tok() for _ in range(8)] + [
            pl.BlockSpec((1, N_HEADS, tm, 2 * HEAD), lambda b, i: (b, 0, i, 0)),
            pl.BlockSpec((1, N_HEADS, tm, 2 * HEAD), lambda b, i: (b, 0, i, 0)),
            pl.BlockSpec((1, N_HEADS, V_ROWS, tm), lambda b, i: (b, 0, 0, i)),
            pl.BlockSpec((1, n_sub, 1, LANE), lambda b, i: (b, i, 0, 0)),
            pl.BlockSpec((2 * d, d), lambda b, i: (0, 0)),
        ],
        scratch_shapes=[pltpu.VMEM((1, N_SHIFT), F32), pltpu.VMEM((1, LANE), F32), pltpu.VMEM((1, LANE), F32),
                        pltpu.VMEM((N_PROJ, d), BF16), pltpu.VMEM((STAGE_SLOTS, STAGE_ROWS, d), F32),
                        pltpu.SemaphoreType.DMA((STAGE_SLOTS,))],
        compiler_params=pltpu.CompilerParams(
            dimension_semantics=("arbitrary", "arbitrary"), vmem_limit_bytes=VMEM_LIMIT),
        name="in_proj",
    )(x, *consts)


def _rwkv_kernel(r_ref, lw_ref, k_ref, v_ref, kk_ref, b_ref, g_ref, bonus_ref, lng_ref, lnb_ref, e_ref, y_ref, s_scr, *, tb):
    i = pl.program_id(1)

    @pl.when(i == 0)
    def _():
        s_scr[...] = jnp.zeros_like(s_scr)

    c = CHUNK
    pw = 2 * HEAD
    n_chunk = tb // c
    n_pair = N_HEADS // 2
    rr = lax.broadcasted_iota(jnp.int32, (c, pw), 0)
    ll = lax.broadcasted_iota(jnp.int32, (c, pw), 1)
    cc = ll % c
    lo = ll < c
    strict = cc < rr
    incl = cc <= rr
    eye = (cc == rr).astype(F32)
    r2 = lax.broadcasted_iota(jnp.int32, (2 * c, pw), 0)
    l2 = lax.broadcasted_iota(jnp.int32, (2 * c, pw), 1)
    lo2 = l2 < c
    diag2 = (r2 < c) == lo2
    zeros = jnp.zeros((c, pw), F32)

    def level_mask(n):
        return ((rr // (2 * n)) == (cc // (2 * n))) & (((rr // n) % 2) == 1) & (((cc // n) % 2) == 0)

    def bd(x):
        return jnp.where(diag2, jnp.concatenate([x, x], axis=0), 0.0).astype(BF16)

    def bd_sw(x):
        return jnp.where(diag2, 0.0, jnp.concatenate([x, x], axis=0)).astype(BF16)

    slab = min(tb, MXU_DIM)
    rb = lax.broadcasted_iota(jnp.int32, (slab, slab), 0)
    cb = lax.broadcasted_iota(jnp.int32, (slab, slab), 1)
    tri = jnp.where((cb <= rb) & ((rb // c) == (cb // c)), 1.0, 0.0).astype(BF16)
    lw = lw_ref[0]
    cs = jnp.concatenate([_cumsum_rows(tri, lw[j:j + slab]) for j in range(0, tb, slab)], axis=0)
    w_inv = jnp.exp(-cs)
    rt = r_ref[0] * jnp.exp(cs)
    at = -kk_ref[0] * jnp.exp(cs - lw)
    bi = b_ref[0] * w_inv
    ki = k_ref[0] * w_inv
    v = v_ref[0].astype(F32)

    chains = [(j, p) for j in range(n_chunk) for p in range(n_pair)]
    cut = lambda x, j, p: x[j * c:(j + 1) * c, p * pw:(p + 1) * pw]
    w_tot = [jnp.exp(cs[(j + 1) * c - 1:(j + 1) * c, :]) for j in range(n_chunk)]
    a_t = [cut(at, j, p) for j, p in chains]
    r_t = [cut(rt, j, p) for j, p in chains]
    b_i = [cut(bi, j, p) for j, p in chains]
    k_i = [cut(ki, j, p) for j, p in chains]
    v_p = [cut(v, j, p) for j, p in chains]
    n = len(chains)

    a_ab, a_ak, a_rb, a_rk = [], [], [], []
    for q in range(n):
        ar = jnp.concatenate([a_t[q], r_t[q]], axis=0)
        gx = _dot(jnp.where(lo2, ar, 0.0).astype(BF16), jnp.concatenate([b_i[q], k_i[q]], axis=0).astype(BF16), _NT)
        gy = _dot(jnp.where(lo2, 0.0, ar).astype(BF16), jnp.concatenate([k_i[q], b_i[q]], axis=0).astype(BF16), _NT)
        a_ab.append(jnp.where(strict, jnp.where(lo, gx[:c], gy[:c]), 0.0))
        a_ak.append(jnp.where(strict, jnp.where(lo, gy[:c], gx[:c]), 0.0))
        a_rb.append(jnp.where(incl, jnp.where(lo, gx[c:], gy[c:]), 0.0))
        a_rk.append(jnp.where(incl, jnp.where(lo, gy[c:], gx[c:]), 0.0))

    t = [eye + jnp.where(level_mask(1), a_ab[q], 0.0) for q in range(n)]
    for m in (2, 4, 8, 16, 32):
        x = [_dot(jnp.where(level_mask(m), a_ab[q], 0.0).astype(BF16), bd(t[q])) for q in range(n)]
        t = [t[q] + _dot(t[q].astype(BF16), bd(x[q])) for q in range(n)]

    v_sw = [bd_sw(v_p[q]) for q in range(n)]
    av = [_dot(jnp.concatenate([a_ak[q], a_rk[q]], axis=0).astype(BF16), v_sw[q]) for q in range(n)]
    x = [_dot(t[q].astype(BF16), jnp.concatenate([bd(a_t[q]), bd(av[q][:c])], axis=1)) for q in range(n)]
    a_p = [x[q][:, :pw] for q in range(n)]
    u_0 = [x[q][:, pw:] for q in range(n)]
    d1 = [_dot(a_rb[q].astype(BF16), jnp.concatenate([bd(a_p[q]), bd(u_0[q])], axis=1)) for q in range(n)]
    rp = [r_t[q] + d1[q][:, :pw] for q in range(n)]
    y_0 = [d1[q][:, pw:] + av[q][c:] for q in range(n)]
    d2 = []
    for q, (j, p) in enumerate(chains):
        wt = w_tot[j][:, p * pw:(p + 1) * pw]
        lhs = jnp.concatenate([jnp.concatenate([a_p[q], u_0[q]], axis=1),
                               jnp.concatenate([zeros, v_p[q]], axis=1)], axis=0).astype(BF16)
        rhs = jnp.concatenate([b_i[q] * wt, k_i[q] * wt], axis=0).astype(BF16)
        d2.append(_dot(lhs, rhs, _TN))

    s = [s_scr[p] for p in range(n_pair)]
    ys = [None] * n
    for j in range(n_chunk):
        for p in range(n_pair):
            q = j * n_pair + p
            mt = jnp.where(diag2, d2[q][:pw], 0.0).astype(BF16)
            nt = jnp.where(lo, d2[q][pw:pw + c], d2[q][pw + c:])
            ys[q] = y_0[q] + _dot(rp[q].astype(BF16), bd(s[p]), _NT)
            s[p] = s[p] * w_tot[j][:, p * pw:(p + 1) * pw] + _dot(s[p].astype(BF16), mt) + nt
    for p in range(n_pair):
        s_scr[p] = s[p]

    y = jnp.concatenate([jnp.concatenate([ys[j * n_pair + p] for p in range(n_pair)], axis=1) for j in range(n_chunk)], axis=0)
    e = e_ref[...]
    inv_head = 1.0 / HEAD
    yc = y - _head_sum(y, e) * inv_head
    yn = yc * lax.rsqrt(_head_sum(yc * yc, e) * inv_head + LNX_EPS)
    y_ref[0] = ((yn * lng_ref[...] + lnb_ref[...] + bonus_ref[0]) * g_ref[0]).astype(BF16)


def _head_mask():
    head_id = jnp.arange(MXU_DIM) // HEAD
    return (head_id[:, None] == head_id[None, :]).astype(BF16)


def _rwkv(r, lw, k, v, kk, b, g, bonus, lnx_g, lnx_b, *, tb=512):
    bsz, s, _ = r.shape
    assert s % tb == 0 and tb % CHUNK == 0
    tok = pl.BlockSpec((1, tb, D_MIX), lambda bb, i: (bb, i, 0))
    vec = pl.BlockSpec((1, D_MIX), lambda bb, i: (0, 0))
    return pl.pallas_call(
        functools.partial(_rwkv_kernel, tb=tb),
        out_shape=jax.ShapeDtypeStruct((bsz, s, D_MIX), BF16),
        grid=(bsz, s // tb),
        in_specs=[tok] * 8 + [vec, vec, pl.BlockSpec((MXU_DIM, MXU_DIM), lambda bb, i: (0, 0))],
        out_specs=tok,
        scratch_shapes=[pltpu.VMEM((N_HEADS // 2, HEAD, 2 * HEAD), F32)],
        compiler_params=pltpu.CompilerParams(
            dimension_semantics=("parallel", "arbitrary"), vmem_limit_bytes=VMEM_LIMIT),
        name="rwkv",
    )(r, lw, k, v, kk, b, g, bonus, lnx_g.reshape(1, -1).astype(F32), lnx_b.reshape(1, -1).astype(F32), _head_mask())


def _fox_kernel(off_ref, thr_ref, q_ref, k_ref, v_ref, o_ref, *scratch, tq, tk, qw, n_kblk):
    b, hp, qb = pl.program_id(0), pl.program_id(1), pl.program_id(2)
    n_g = tq // qw
    kb_last = (qb * tq) // tk
    chains = [(hh, g) for hh in range(2) for g in range(n_g)]
    n_c = len(chains)
    s_scr = (scratch[:n_c], scratch[n_c:2 * n_c])
    m_scr, acc_scr = scratch[2 * n_c:]
    base = [(b * N_HEADS + 2 * hp + hh) * n_kblk for hh in range(2)]
    off_d = [off_ref[base[hh] + kb_last] for hh in range(2)]

    thr = thr_ref[0]

    def needed(kb):
        nxt = jnp.minimum(kb + 1, kb_last)
        return jnp.minimum(off_ref[base[0] + nxt] - off_d[0], off_ref[base[1] + nxt] - off_d[1]) <= thr

    kb_first = lax.while_loop(lambda kb: jnp.logical_and(kb > 0, needed(kb - 1)), lambda kb: kb - 1, kb_last)
    n_open = kb_last - kb_first

    def qk(kb, slot, first_group=0):
        ksl = pl.ds(pl.multiple_of(kb * tk, tk), tk)
        for c, (hh, g) in enumerate(chains):
            if g >= first_group:
                s_scr[slot][c][...] = _dot(k_ref[0, hh, ksl, :], q_ref[0, hh, g * qw:(g + 1) * qw, :], _NT)

    def soft_pv(kb, slot, diag_group=None):
        ksl = pl.ds(pl.multiple_of(kb * tk, tk), tk)
        for c, (hh, g) in enumerate(chains):
            if diag_group is not None and g < diag_group:
                continue
            s = s_scr[slot][c][...]
            if g == diag_group:
                key = lax.broadcasted_iota(jnp.int32, (tk, qw), 0)
                qry = lax.broadcasted_iota(jnp.int32, (tk, qw), 1)
                s = jnp.where(key <= qry, s, NEG_BIG)
            off = off_ref[base[hh] + kb] - off_d[hh]
            m = m_scr[c]
            m_new = jnp.maximum(m, jnp.max(s, axis=0, keepdims=True) - off)
            p = jnp.exp2(s - (m_new + off)).astype(BF16)
            acc_scr[c] = jnp.exp2(m - m_new) * acc_scr[c] + _dot(v_ref[0, hh, :, ksl], p)
            m_scr[c] = m_new

    m_scr[...] = jnp.full(m_scr.shape, NEG_BIG, F32)
    acc_scr[...] = jnp.zeros(acc_scr.shape, F32)
    qk(kb_first, 0)

    def pair(i, _):
        kb = kb_first + 2 * i
        qk(kb + 1, 1)
        soft_pv(kb, 0)
        qk(kb + 2, 0)
        soft_pv(kb + 1, 1)
        return 0

    lax.fori_loop(0, n_open // 2, pair, 0)

    def finish(blocks):
        for j, (kb, diag_group) in enumerate(blocks):
            if j + 1 < len(blocks):
                qk(blocks[j + 1][0], (j + 1) % 2, first_group=blocks[j + 1][1] or 0)
            soft_pv(kb, j % 2, diag_group)

    diagonal = [(kb_last + g, g) for g in range(n_g)]

    @pl.when(n_open % 2 == 1)
    def _():
        finish([(kb_last - 1, None)] + diagonal)

    @pl.when(n_open % 2 == 0)
    def _():
        finish(diagonal)

    acc = [acc_scr[c] for c in range(n_c)]
    o_t = jnp.concatenate([jnp.concatenate([acc[c][:HEAD] / acc[c][HEAD:HEAD + 1] for c, (h2, _) in enumerate(chains)
                                            if h2 == hh], axis=1) for hh in range(2)], axis=0)
    o_ref[0] = o_t.astype(BF16)


def _fox(q_t, k_aug, v_t, off, q_norm, k_norm, *, tq=512, tk=FOX_KBLOCK, qw=MXU_DIM):
    bsz, nh, s, _ = k_aug.shape
    n_kblk = s // tk
    n_chain = 2 * (tq // qw)
    assert tq % qw == 0 and tk == qw
    assert s % tq == 0 and s % tk == 0 and nh % 2 == 0 and off.shape == (bsz * nh * n_kblk,)
    qk_bound = 1.02 * LOG2E * HEAD ** 0.5 * jnp.max(jnp.abs(q_norm)) * jnp.max(jnp.abs(k_norm))
    thr = (2.0 * qk_bound + EXP2_UNDERFLOW).reshape(1).astype(F32)
    return pl.pallas_call(
        functools.partial(_fox_kernel, tq=tq, tk=tk, qw=qw, n_kblk=n_kblk),
        out_shape=jax.ShapeDtypeStruct((bsz, nh * HEAD, s), BF16),
        grid=(bsz, nh // 2, s // tq),
        in_specs=[
            pl.BlockSpec(memory_space=pltpu.SMEM),
            pl.BlockSpec(memory_space=pltpu.SMEM),
            pl.BlockSpec((1, 2, tq, 2 * HEAD), lambda b, hp, i: (b, hp, i, 0)),
            pl.BlockSpec((1, 2, s, 2 * HEAD), lambda b, hp, i: (b, hp, 0, 0)),
            pl.BlockSpec((1, 2, V_ROWS, s), lambda b, hp, i: (b, hp, 0, 0)),
        ],
        out_specs=pl.BlockSpec((1, 2 * HEAD, tq), lambda b, hp, i: (b, hp, i)),
        scratch_shapes=[pltpu.VMEM((tk, qw), F32)] * (2 * n_chain)
        + [pltpu.VMEM((n_chain, 1, qw), F32), pltpu.VMEM((n_chain, V_ROWS, qw), F32)],
        compiler_params=pltpu.CompilerParams(
            dimension_semantics=("parallel", "parallel", "arbitrary"), vmem_limit_bytes=VMEM_LIMIT),
        name="fox",
    )(off, thr, q_t, k_aug, v_t)


def _merge_kernel(x_ref, g_ref, wg_ref, ya_ref, yb_ref, pa_ref, pb_ref, wo_ref, o_ref):
    x = x_ref[...]
    d = x.shape[-1]
    h = _rms(x, g_ref[...]).astype(BF16)
    gates = _sigmoid(_dot(h, wg_ref[...], _NT))
    ya = _dot(ya_ref[...], pa_ref[...])
    yb = _dot(yb_ref[0], pb_ref[...], _TN)
    mix = (gates[:, :d] * ya + gates[:, d:] * yb).astype(BF16)
    o_ref[...] = x + _dot(mix, wo_ref[...])


def _merge(x2d, mix_norm, w_gates, ya, yb, p_a, p_b, w_out, *, tm=512):
    m, d = x2d.shape
    tiles_per_seq = yb.shape[-1] // tm
    assert m % tm == 0 and yb.shape[-1] % tm == 0
    full = lambda t: pl.BlockSpec(t.shape, lambda i: (0, 0))
    tok = lambda n: pl.BlockSpec((tm, n), lambda i: (i, 0))
    yb_spec = pl.BlockSpec((1, yb.shape[1], tm), lambda i: (i // tiles_per_seq, 0, i % tiles_per_seq))
    consts = [mix_norm.reshape(1, d).astype(F32), w_gates.astype(BF16)]
    mats = [p_a.astype(BF16), p_b.astype(BF16), w_out.astype(BF16)]
    return pl.pallas_call(
        _merge_kernel,
        out_shape=jax.ShapeDtypeStruct((m, d), F32),
        grid=(m // tm,),
        in_specs=[tok(d), full(consts[0]), full(consts[1]), tok(D_MIX), yb_spec] + [full(t) for t in mats],
        out_specs=tok(d),
        compiler_params=pltpu.CompilerParams(dimension_semantics=("parallel",), vmem_limit_bytes=VMEM_LIMIT),
        name="merge",
    )(x2d, consts[0], consts[1], ya, yb, *mats)


def kernel(x, ffn1_norm, ffn1_w1, ffn1_w3, ffn1_w2, mix_norm, w_in, shift_mu, rwkv_w0, rwkv_w_up, rwkv_a0, rwkv_a_up, rwkv_g_up, rwkv_k_k, rwkv_k_a, rwkv_r_k, rwkv_lnx_g, rwkv_lnx_b, rwkv_proj, fox_f_bias, fox_q_norm, fox_k_norm, fox_proj, w_out, ffn2_norm, ffn2_w1, ffn2_w3, ffn2_w2, final_norm):
    bsz, s, d = x.shape
    depth = ffn1_norm.shape[0]
    x2d = x.reshape(bsz * s, d)
    for l in range(depth):
        x2d = _ffn(x2d, ffn1_norm[l], ffn1_w1[l], ffn1_w3[l], ffn1_w2[l], final_norm, final_norm=False)
        (r, lw, k, v, kk, b, g, bonus, fq_t, fk_aug, fv_t, off, w_gates) = _in_proj(
            x2d.reshape(bsz, s, d), mix_norm[l], w_in[l], shift_mu[l], rwkv_w0[l], rwkv_w_up[l], rwkv_a0[l],
            rwkv_a_up[l], rwkv_g_up[l], rwkv_k_k[l], rwkv_k_a[l], rwkv_r_k[l], fox_f_bias[l], fox_q_norm[l],
            fox_k_norm[l])
        ya = _rwkv(r, lw, k, v, kk, b, g, bonus, rwkv_lnx_g[l], rwkv_lnx_b[l])
        off = off[:, ::FOX_KBLOCK // (s // off.shape[1]), 0, :N_HEADS]
        off = jnp.transpose(off, (0, 2, 1)).reshape(-1)
        yb = _fox(fq_t, fk_aug, fv_t, off, fox_q_norm[l], fox_k_norm[l])
        x2d = _merge(x2d, mix_norm[l], w_gates, ya.reshape(bsz * s, D_MIX), yb,
                     rwkv_proj[l], fox_proj[l], w_out[l])
        x2d = _ffn(x2d, ffn2_norm[l], ffn2_w1[l], ffn2_w3[l], ffn2_w2[l], final_norm, final_norm=(l == depth - 1))
    return x2d.reshape(bsz, s, d)
```
